```python
import jax
import jax.numpy as jnp
from jax import lax
import numpy as np


D_MODEL = 1024
BATCH = 8
SEQ = 4096
DEPTH = 1
DEC_BATCH = 128
DEC_SEQ = 4
PAST_LEN = 8192
PAGE_SIZE = 128

ATT_HEADS = 8
ATT_HEAD_DIM = 64
KV_HEADS = 2
ATT_GROUP = ATT_HEADS // KV_HEADS
ATT_WIDTH = ATT_HEADS * ATT_HEAD_DIM
ROPE_THETA = 500000.0
ATT_ROPE_DIMS = ATT_HEAD_DIM // 4
IDX_HEADS = 8
IDX_DIM = 64
IDX_ROPE_DIMS = IDX_DIM // 4
TOPK_MAX = 256
Q_BLOCK = 128
RET_HEADS = 4
RET_HEAD_DIM = 128
RET_WIDTH = RET_HEADS * RET_HEAD_DIM
RET_ROPE_THETA = 10000.0
RET_CHUNK = 128
MIX_WIDTH = ATT_WIDTH + RET_WIDTH
PEER_HEADS = 8
PEER_NKEYS = 128
PEER_EXPERTS = PEER_NKEYS * PEER_NKEYS
PEER_KEY_DIM = 128
PEER_HALF = PEER_KEY_DIM // 2
PEER_TOPK = 16
PEER_BLOCK = 128
NORM_EPS = 1e-6
GN_EPS = 1e-6
IN_SPLITS = (ATT_WIDTH, KV_HEADS * ATT_HEAD_DIM, KV_HEADS * ATT_HEAD_DIM,
             IDX_HEADS * IDX_DIM, IDX_DIM, IDX_HEADS,
             RET_WIDTH, RET_WIDTH, RET_WIDTH, RET_WIDTH)
IN_WIDTH = sum(IN_SPLITS)

kernel_name = 'hymba_dsa_retnet_peer_step'

F32 = jnp.float32


def rms_norm(x, w):
    xf = x.astype(F32)
    y = xf * lax.rsqrt(jnp.mean(xf * xf, axis=-1, keepdims=True) + NORM_EPS)
    return (y * w.astype(F32)).astype(x.dtype)


def rope(x, pos, rot_dims, theta):
    half = rot_dims // 2
    inv = 1.0 / (theta ** (jnp.arange(half, dtype=F32) / half))
    ang = pos.astype(F32)[:, None] * inv[None, :]
    cos = jnp.cos(ang)[:, None, :]
    sin = jnp.sin(ang)[:, None, :]
    xr = x[..., :rot_dims].astype(F32)
    x1, x2 = xr[..., :half], xr[..., half:]
    rot = jnp.concatenate([x1 * cos - x2 * sin, x2 * cos + x1 * sin], axis=-1).astype(x.dtype)
    return jnp.concatenate([rot, x[..., rot_dims:]], axis=-1)


def split_cols(h):
    outs = []
    off = 0
    for n in IN_SPLITS:
        outs.append(h[..., off:off + n])
        off += n
    return outs


def mixer_inputs(xn, pos, w_in_l):
    B, T, _ = xn.shape
    qa, ka, va, qi, ki, wi, qr, kr, vr, gr = split_cols(xn @ w_in_l)
    qa = rope(qa.reshape(B, T, ATT_HEADS, ATT_HEAD_DIM), pos, ATT_ROPE_DIMS, ROPE_THETA)
    ka = rope(ka.reshape(B, T, KV_HEADS, ATT_HEAD_DIM), pos, ATT_ROPE_DIMS, ROPE_THETA)
    va = va.reshape(B, T, KV_HEADS, ATT_HEAD_DIM)
    qi = rope(qi.reshape(B, T, IDX_HEADS, IDX_DIM), pos, IDX_ROPE_DIMS, ROPE_THETA)
    ki = rope(ki.reshape(B, T, 1, IDX_DIM), pos, IDX_ROPE_DIMS, ROPE_THETA)[:, :, 0]
    wi = wi * (IDX_HEADS ** -0.5)
    qr = rope(qr.reshape(B, T, RET_HEADS, RET_HEAD_DIM), pos, RET_HEAD_DIM, RET_ROPE_THETA)
    kr = rope(kr.reshape(B, T, RET_HEADS, RET_HEAD_DIM), pos, RET_HEAD_DIM, RET_ROPE_THETA) * (RET_HEAD_DIM ** -0.5)
    vr = vr.reshape(B, T, RET_HEADS, RET_HEAD_DIM)
    return qa, ka, va, qi, ki, wi, qr, kr, vr, gr


def indexer_scores(qi, wi, ki):
    dots = jnp.einsum('bthd,bld->bthl', qi, ki).astype(F32) * (IDX_DIM ** -0.5)
    return jnp.einsum('bth,bthl->btl', wi.astype(F32), jax.nn.relu(dots))


def sparse_attend(q, k_sel, v_sel, valid):
    B, T = q.shape[:2]
    qg = q.reshape(B, T, KV_HEADS, ATT_GROUP, ATT_HEAD_DIM)
    logits = jnp.einsum('btcgd,btscd->btcgs', qg, k_sel).astype(F32) * (ATT_HEAD_DIM ** -0.5)
    logits = jnp.where(valid[:, :, None, None, :], logits, -jnp.inf)
    p = jax.nn.softmax(logits, axis=-1)
    o = jnp.einsum('btcgs,btscd->btcgd', p.astype(v_sel.dtype), v_sel)
    return o.reshape(B, T, ATT_WIDTH)


def prompt_dsa(qa, ka, va, qi, ki, wi):
    B, S = qa.shape[:2]
    topk = min(TOPK_MAX, S // 4)
    n_blocks = S // Q_BLOCK
    bidx = jnp.arange(B)[:, None, None]
    key_pos = jnp.arange(S)

    def block(i):
        t0 = i * Q_BLOCK
        q_b = lax.dynamic_slice_in_dim(qa, t0, Q_BLOCK, axis=1)
        qi_b = lax.dynamic_slice_in_dim(qi, t0, Q_BLOCK, axis=1)
        wi_b = lax.dynamic_slice_in_dim(wi, t0, Q_BLOCK, axis=1)
        qpos = t0 + jnp.arange(Q_BLOCK)
        sc = indexer_scores(qi_b, wi_b, ki)
        sc = jnp.where((key_pos[None, :] <= qpos[:, None])[None], sc, -jnp.inf)
        _, sel = lax.top_k(sc, topk)
        valid = sel <= qpos[None, :, None]
        return sparse_attend(q_b, ka[bidx, sel], va[bidx, sel], valid)

    out = lax.map(block, jnp.arange(n_blocks))
    return out.transpose(1, 0, 2, 3).reshape(B, S, ATT_WIDTH)


def sample_dsa(qa, ka, va, qi, ki, wi, cache_k_l, cache_v_l, cache_idx_k_l, page_table):
    Bd, T = qa.shape[:2]
    past_len = page_table.shape[1] * PAGE_SIZE
    L = past_len + T
    topk = min(TOPK_MAX, L // 4)
    bidx = jnp.arange(Bd)[:, None, None]
    past_ik = cache_idx_k_l[page_table].reshape(Bd, past_len, IDX_DIM)
    all_ik = jnp.concatenate([past_ik, ki.astype(past_ik.dtype)], axis=1)
    sc = indexer_scores(qi, wi, all_ik)
    key_pos = jnp.arange(L)
    qpos = past_len + jnp.arange(T)
    sc = jnp.where((key_pos[None, :] <= qpos[:, None])[None], sc, -jnp.inf)
    _, sel = lax.top_k(sc, topk)
    valid = sel <= qpos[None, :, None]
    in_past = sel < past_len
    ps = jnp.minimum(sel, past_len - 1)
    phys = page_table[bidx, ps // PAGE_SIZE]
    off = ps % PAGE_SIZE
    nidx = jnp.clip(sel - past_len, 0, T - 1)
    k_sel = jnp.where(in_past[..., None, None], cache_k_l[phys, off], ka[bidx, nidx].astype(cache_k_l.dtype))
    v_sel = jnp.where(in_past[..., None, None], cache_v_l[phys, off], va[bidx, nidx].astype(cache_v_l.dtype))
    return sparse_attend(qa, k_sel.astype(qa.dtype), v_sel.astype(qa.dtype), valid)


def ret_log_decay():
    return jnp.log1p(-(2.0 ** (-5.0 - jnp.arange(RET_HEADS, dtype=F32))))


def retention_chunk(state, q, k, v):
    C = q.shape[1]
    lg = ret_log_decay()
    i = jnp.arange(C, dtype=F32)
    diff = i[:, None] - i[None, :]
    dmask = jnp.where(diff >= 0, jnp.exp(lg[:, None, None] * jnp.maximum(diff, 0.0)), 0.0)
    q_dec = jnp.exp(lg[None, :] * (i[:, None] + 1.0))
    k_dec = jnp.exp(lg[None, :] * (C - 1.0 - i[:, None]))
    chunk_dec = jnp.exp(lg * C)
    qf, kf, vf = q.astype(F32), k.astype(F32), v.astype(F32)
    att = jnp.einsum('bihd,bjhd->bhij', qf, kf) * dmask[None]
    o = (jnp.einsum('bhij,bjhe->bihe', att, vf)
         + jnp.einsum('bihd,bhde->bihe', qf, state) * q_dec[None, :, :, None])
    new_state = (state * chunk_dec[None, :, None, None]
                 + jnp.einsum('bjhd,bjhe->bhde', kf * k_dec[None, :, :, None], vf))
    return new_state, o


def prompt_retention(q, k, v):
    B, S, H, D = q.shape
    nc = S // RET_CHUNK

    def resh(a):
        return a.reshape(B, nc, RET_CHUNK, H, D).swapaxes(0, 1)

    s0 = jnp.zeros((B, H, D, D), F32)
    s_fin, o = lax.scan(lambda s, xs: retention_chunk(s, xs[0], xs[1], xs[2]), s0, (resh(q), resh(k), resh(v)))
    return o.swapaxes(0, 1).reshape(B, S, H, D), s_fin


def retention_out(o, g, gn_w_l):
    B, T = o.shape[:2]
    mu = jnp.mean(o, axis=-1, keepdims=True)
    var = jnp.mean(jnp.square(o - mu), axis=-1, keepdims=True)
    on = ((o - mu) * lax.rsqrt(var + GN_EPS)).reshape(B, T, RET_WIDTH) * gn_w_l.astype(F32)
    return (jax.nn.silu(g.astype(F32)) * on).astype(g.dtype)


def peer_ffn(x, w_q, sub_keys_1, sub_keys_2, u_emb, v_emb):
    N = x.shape[0]
    nb = -(-N // PEER_BLOCK)
    xp = jnp.pad(x, ((0, nb * PEER_BLOCK - N), (0, 0)))

    def block(xb):
        q = (xb @ w_q).reshape(PEER_BLOCK, PEER_HEADS, 2, PEER_HALF)
        s1 = jnp.einsum('thd,kd->thk', q[:, :, 0], sub_keys_1).astype(F32)
        s2 = jnp.einsum('thd,kd->thk', q[:, :, 1], sub_keys_2).astype(F32)
        v1, i1 = lax.top_k(s1, PEER_TOPK)
        v2, i2 = lax.top_k(s2, PEER_TOPK)
        cand = (v1[..., :, None] + v2[..., None, :]).reshape(PEER_BLOCK, PEER_HEADS, PEER_TOPK * PEER_TOPK)
        cid = (i1[..., :, None] * PEER_NKEYS + i2[..., None, :]).reshape(PEER_BLOCK, PEER_HEADS, PEER_TOPK * PEER_TOPK)
        sv, si = lax.top_k(cand, PEER_TOPK)
        eid = jnp.take_along_axis(cid, si, axis=-1)
        gate = jax.nn.softmax(sv, axis=-1)
        hval = jnp.einsum('thkd,td->thk', u_emb[eid], xb).astype(F32)
        coef = (gate * jax.nn.gelu(hval, approximate=False)).astype(xb.dtype)
        return jnp.einsum('thk,thkd->td', coef, v_emb[eid])

    y = lax.map(block, xp.reshape(nb, PEER_BLOCK, x.shape[1]))
    return y.reshape(nb * PEER_BLOCK, x.shape[1])[:N]


def ffn_residual(h, norm_w_l, w_q, sk1, sk2, u_emb, v_emb):
    B, T, D = h.shape
    y = peer_ffn(rms_norm(h, norm_w_l).reshape(B * T, D), w_q, sk1, sk2, u_emb, v_emb)
    return h + y.reshape(B, T, D).astype(h.dtype)


def setup_inputs(seed: int = 0) -> dict:
    key = jax.random.key(seed)
    ks = jax.random.split(key, 20)
    n_pages = PAST_LEN // PAGE_SIZE
    n_used = DEC_BATCH * n_pages
    n_phys = n_used + (n_used + 3) // 4
    perm = jax.random.permutation(ks[0], n_phys)[:n_used]
    page_table = perm.reshape(DEC_BATCH, n_pages).astype(jnp.int32)
    nrm = jax.random.normal
    return {
        'x_prompt': nrm(ks[1], (BATCH, SEQ, D_MODEL), F32),
        'x_sample': nrm(ks[2], (DEC_BATCH, DEC_SEQ, D_MODEL), F32),
        'cache_k': nrm(ks[3], (DEPTH, n_phys, PAGE_SIZE, KV_HEADS, ATT_HEAD_DIM), F32),
        'cache_v': nrm(ks[4], (DEPTH, n_phys, PAGE_SIZE, KV_HEADS, ATT_HEAD_DIM), F32),
        'cache_idx_k': nrm(ks[5], (DEPTH, n_phys, PAGE_SIZE, IDX_DIM), F32),
        'state_ret': nrm(ks[6], (DEPTH, DEC_BATCH, RET_HEADS, RET_HEAD_DIM, RET_HEAD_DIM), F32),
        'page_table': page_table,
        'norm_attn_w': 1.0 + 0.02 * nrm(ks[7], (DEPTH, D_MODEL), F32),
        'w_in': nrm(ks[8], (DEPTH, D_MODEL, IN_WIDTH), F32) * D_MODEL ** -0.5,
        'ret_gn_w': 1.0 + 0.02 * nrm(ks[9], (DEPTH, RET_WIDTH), F32),
        'w_out': nrm(ks[10], (DEPTH, MIX_WIDTH, D_MODEL), F32) * MIX_WIDTH ** -0.5,
        'norm_ffn_w': 1.0 + 0.02 * nrm(ks[11], (DEPTH, D_MODEL), F32),
        'peer_w_q': nrm(ks[12], (DEPTH, D_MODEL, PEER_HEADS * PEER_KEY_DIM), F32) * D_MODEL ** -0.5,
        'peer_sub_keys_1': nrm(ks[13], (DEPTH, PEER_NKEYS, PEER_HALF), F32) * PEER_HALF ** -0.5,
        'peer_sub_keys_2': nrm(ks[14], (DEPTH, PEER_NKEYS, PEER_HALF), F32) * PEER_HALF ** -0.5,
        'peer_u': nrm(ks[15], (DEPTH, PEER_EXPERTS, D_MODEL), F32) * D_MODEL ** -0.5,
        'peer_v': nrm(ks[16], (DEPTH, PEER_EXPERTS, D_MODEL), F32) * PEER_HEADS ** -0.5,
        'final_norm_w': 1.0 + 0.02 * nrm(ks[17], (D_MODEL,), F32),
    }


def reference(x_prompt, x_sample, cache_k, cache_v, cache_idx_k, state_ret, page_table,
              norm_attn_w, w_in, ret_gn_w, w_out, norm_ffn_w, peer_w_q,
              peer_sub_keys_1, peer_sub_keys_2, peer_u, peer_v, final_norm_w):
    hp = x_prompt
    hs = x_sample
    S = x_prompt.shape[1]
    T = x_sample.shape[1]
    pos_p = jnp.arange(S)
    pos_s = page_table.shape[1] * PAGE_SIZE + jnp.arange(T)
    kp_l, vp_l, ikp_l, sp_l = [], [], [], []
    ks_l, vs_l, iks_l, ss_l = [], [], [], []
    for l in range(DEPTH):
        qa, ka, va, qi, ki, wi, qr, kr, vr, gr = mixer_inputs(rms_norm(hp, norm_attn_w[l]), pos_p, w_in[l])
        att = prompt_dsa(qa, ka, va, qi, ki, wi)
        o_r, s_r = prompt_retention(qr, kr, vr)
        mix = jnp.concatenate([att, retention_out(o_r, gr, ret_gn_w[l])], axis=-1)
        hp = hp + mix @ w_out[l]
        hp = ffn_residual(hp, norm_ffn_w[l], peer_w_q[l], peer_sub_keys_1[l], peer_sub_keys_2[l], peer_u[l], peer_v[l])
        kp_l.append(ka)
        vp_l.append(va)
        ikp_l.append(ki)
        sp_l.append(s_r.astype(x_prompt.dtype))
        qa, ka, va, qi, ki, wi, qr, kr, vr, gr = mixer_inputs(rms_norm(hs, norm_attn_w[l]), pos_s, w_in[l])
        att = sample_dsa(qa, ka, va, qi, ki, wi, cache_k[l], cache_v[l], cache_idx_k[l], page_table)
        s_new, o_r = retention_chunk(state_ret[l].astype(F32), qr, kr, vr)
        mix = jnp.concatenate([att, retention_out(o_r, gr, ret_gn_w[l])], axis=-1)
        hs = hs + mix @ w_out[l]
        hs = ffn_residual(hs, norm_ffn_w[l], peer_w_q[l], peer_sub_keys_1[l], peer_sub_keys_2[l], peer_u[l], peer_v[l])
        ks_l.append(ka)
        vs_l.append(va)
        iks_l.append(ki)
        ss_l.append(s_new.astype(state_ret.dtype))
    y_prompt = rms_norm(hp, final_norm_w)
    y_sample = rms_norm(hs, final_norm_w)
    return (y_prompt, y_sample,
            jnp.stack(kp_l), jnp.stack(vp_l), jnp.stack(ikp_l), jnp.stack(sp_l),
            jnp.stack(ks_l), jnp.stack(vs_l), jnp.stack(iks_l), jnp.stack(ss_l))
```

```python
import functools

import jax
import jax.numpy as jnp
import numpy as np
from jax import lax
from jax.experimental import pallas as pl
from jax.experimental.pallas import tpu as pltpu

F32 = jnp.float32
BF16 = jnp.bfloat16

D_MODEL = 1024
PAGE_SIZE = 128
ATT_HEADS = 8
ATT_HEAD_DIM = 64
KV_HEADS = 2
ATT_GROUP = ATT_HEADS // KV_HEADS
ATT_WIDTH = ATT_HEADS * ATT_HEAD_DIM
KV_WIDTH = KV_HEADS * ATT_HEAD_DIM
ROPE_THETA = 500000.0
ATT_ROPE_DIMS = ATT_HEAD_DIM // 4
IDX_HEADS = 8
IDX_DIM = 64
IDX_WIDTH = IDX_HEADS * IDX_DIM
TOPK_MAX = 256
RET_HEADS = 4
RET_HEAD_DIM = 128
RET_WIDTH = RET_HEADS * RET_HEAD_DIM
RET_ROPE_THETA = 10000.0
RET_CHUNK = 128
PEER_HEADS = 8
PEER_NKEYS = 128
PEER_EXPERTS = PEER_NKEYS * PEER_NKEYS
PEER_KEY_DIM = 128
PEER_HALF = PEER_KEY_DIM // 2
PEER_TOPK = 16
NORM_EPS = 1e-6
GN_EPS = 1e-6

LANES = 128
SUBLANES = 8
VMEM_LIMIT_BYTES = 56 * 1024 * 1024

NEG_BIG = -1e30

_IN_SPLITS = (ATT_WIDTH, KV_WIDTH, KV_WIDTH, IDX_WIDTH, IDX_DIM, IDX_HEADS,
              RET_WIDTH, RET_WIDTH, RET_WIDTH, RET_WIDTH)
_KIWI_WIDTH = LANES
_PAD_OFFS = {}
_off = 0
for _name, _w in (("qa", ATT_WIDTH), ("ka", KV_WIDTH), ("va", KV_WIDTH), ("qi", IDX_WIDTH),
                  ("kiwi", _KIWI_WIDTH), ("qr", RET_WIDTH), ("kr", RET_WIDTH),
                  ("vr", RET_WIDTH), ("gr", RET_WIDTH)):
    _PAD_OFFS[_name] = (_off, _w)
    _off += _w
IN_WIDTH_PADDED = _off


def _cparams(sem):
    return pltpu.CompilerParams(dimension_semantics=sem, vmem_limit_bytes=VMEM_LIMIT_BYTES)


def _dot(a, b):
    return jnp.dot(a.astype(BF16), b.astype(BF16), preferred_element_type=F32)


def _dot_nt(a, b):
    return lax.dot_general(a.astype(BF16), b.astype(BF16), (((1,), (1,)), ((), ())),
                           preferred_element_type=F32)


def _pad_w_in(w_in_l):
    cols = []
    off = 0
    parts = []
    for n in _IN_SPLITS:
        parts.append(w_in_l[:, off:off + n])
        off += n
    qa, ka, va, qi, ki, wi, qr, kr, vr, gr = parts
    kiwi = jnp.concatenate(
        [ki, wi, jnp.zeros((w_in_l.shape[0], _KIWI_WIDTH - IDX_DIM - IDX_HEADS), w_in_l.dtype)], axis=1)
    cols = [qa, ka, va, qi, kiwi, qr, kr, vr, gr]
    return jnp.concatenate(cols, axis=1).astype(BF16)


def _rope_tables(pos):
    posf = pos.astype(F32)
    half = ATT_ROPE_DIMS // 2
    inv = 1.0 / (ROPE_THETA ** (jnp.arange(half, dtype=F32) / half))
    ang = posf[:, None] * inv[None, :]
    cos, sin = jnp.cos(ang), jnp.sin(ang)
    P = pos.shape[0]
    one = jnp.ones((P, ATT_HEAD_DIM - ATT_ROPE_DIMS), F32)
    zero = jnp.zeros((P, ATT_HEAD_DIM - ATT_ROPE_DIMS), F32)
    zh = jnp.zeros((P, half), F32)
    c_head = jnp.concatenate([cos, cos, one], axis=1)
    s1_head = jnp.concatenate([-sin, zh, zero], axis=1)
    s2_head = jnp.concatenate([zh, sin, zero], axis=1)
    rep = LANES // ATT_HEAD_DIM
    att = jnp.stack([jnp.tile(c_head, (1, rep)), jnp.tile(s1_head, (1, rep)), jnp.tile(s2_head, (1, rep))])
    halfr = RET_HEAD_DIM // 2
    invr = 1.0 / (RET_ROPE_THETA ** (jnp.arange(halfr, dtype=F32) / halfr))
    angr = posf[:, None] * invr[None, :]
    cr, sr = jnp.cos(angr), jnp.sin(angr)
    ret = jnp.stack([jnp.concatenate([cr, cr], axis=1), jnp.concatenate([-sr, sr], axis=1)])
    return att, ret


def _in_proj_kernel(x_ref, nw_ref, w_ref, ta_ref, tr_ref,
                    qa_ref, ka_ref, va_ref, qi_ref, kiwi_ref, qr_ref, kr_ref, vr_ref, gr_ref):
    x = x_ref[...]
    ms = jnp.mean(x * x, axis=-1, keepdims=True)
    xn = (x * lax.rsqrt(ms + NORM_EPS) * nw_ref[...]).astype(BF16)
    ca, s1a, s2a = ta_ref[0], ta_ref[1], ta_ref[2]
    cr, sr = tr_ref[0], tr_ref[1]

    def proj(name):
        off, w = _PAD_OFFS[name]
        return jnp.dot(xn, w_ref[:, off:off + w], preferred_element_type=F32)

    def rope_att_chunk(vc):
        return (vc * ca + pltpu.roll(vc, LANES - ATT_ROPE_DIMS // 2, 1) * s1a
                + pltpu.roll(vc, ATT_ROPE_DIMS // 2, 1) * s2a)

    def rope_ret_chunk(vc):
        return vc * cr + pltpu.roll(vc, RET_HEAD_DIM // 2, 1) * sr

    def per_chunk(v, fn):
        n = v.shape[1] // LANES
        return jnp.concatenate([fn(v[:, c * LANES:(c + 1) * LANES]) for c in range(n)], axis=1)

    qa_ref[...] = per_chunk(proj("qa"), rope_att_chunk)
    ka_ref[...] = per_chunk(proj("ka"), rope_att_chunk)
    va_ref[...] = proj("va")
    qi_ref[...] = per_chunk(proj("qi"), rope_att_chunk)
    kiwi = proj("kiwi")
    lane = lax.broadcasted_iota(jnp.int32, kiwi.shape, 1)
    kiwi_ref[...] = jnp.where(lane < IDX_DIM, rope_att_chunk(kiwi), kiwi * (IDX_HEADS ** -0.5))
    qr_ref[...] = per_chunk(proj("qr"), rope_ret_chunk)
    kr_ref[...] = per_chunk(proj("kr"), rope_ret_chunk) * (RET_HEAD_DIM ** -0.5)
    vr_ref[...] = proj("vr")
    gr_ref[...] = proj("gr")


def _in_proj(x, norm_w, w_pad, tab_att, tab_ret, tab_block_of_tile, tm):
    N = x.shape[0]
    nt = N // tm
    tab_idx = jnp.asarray(tab_block_of_tile, jnp.int32)
    names = ("qa", "ka", "va", "qi", "kiwi", "qr", "kr", "vr", "gr")
    out_shape = [jax.ShapeDtypeStruct((N, _PAD_OFFS[n][1]), F32) for n in names]
    out_specs = [pl.BlockSpec((tm, _PAD_OFFS[n][1]), lambda i, t: (i, 0)) for n in names]
    grid_spec = pltpu.PrefetchScalarGridSpec(
        num_scalar_prefetch=1,
        grid=(nt,),
        in_specs=[
            pl.BlockSpec((tm, D_MODEL), lambda i, t: (i, 0)),
            pl.BlockSpec((1, D_MODEL), lambda i, t: (0, 0)),
            pl.BlockSpec((D_MODEL, IN_WIDTH_PADDED), lambda i, t: (0, 0)),
            pl.BlockSpec((3, tm, LANES), lambda i, t: (0, t[i], 0)),
            pl.BlockSpec((2, tm, LANES), lambda i, t: (0, t[i], 0)),
        ],
        out_specs=out_specs,
    )

    def body(t_ref, *refs):
        _in_proj_kernel(*refs)

    outs = pl.pallas_call(
        body, grid_spec=grid_spec, out_shape=out_shape, name="in_proj",
        compiler_params=_cparams(("arbitrary",)),
    )(tab_idx, x, norm_w.reshape(1, D_MODEL), w_pad, tab_att, tab_ret)
    return dict(zip(names, outs))


def _ret_constants(c_eff):
    C = RET_CHUNK
    lg = jnp.log1p(-(2.0 ** (-5.0 - jnp.arange(RET_HEADS, dtype=F32))))
    i = jnp.arange(C, dtype=F32)
    diff = i[:, None] - i[None, :]
    dmask = jnp.where(diff >= 0, jnp.exp(lg[:, None, None] * jnp.maximum(diff, 0.0)), 0.0)
    real = (i < c_eff)
    dmask = jnp.where(real[None, :, None] & real[None, None, :], dmask, 0.0)
    q_dec = jnp.exp(lg[:, None] * (i[None, :] + 1.0))
    k_dec = jnp.where(real[None, :], jnp.exp(lg[:, None] * (c_eff - 1.0 - i[None, :])), 0.0)
    chunk_dec = jnp.exp(lg * c_eff)
    bc = lambda a: jnp.broadcast_to(a[:, :, None], (RET_HEADS, C, C))
    cdec = jnp.broadcast_to(chunk_dec[:, None, None], (RET_HEADS, C, C))
    return dmask, bc(q_dec), bc(k_dec), cdec


def _retention_kernel(q_ref, k_ref, v_ref, g_ref, gnw_ref, s0_ref, dm_ref, qd_ref, kd_ref, cd_ref,
                      o_ref, s_out_ref, state_ref, *, rows):
    c = pl.program_id(1)
    nc = pl.num_programs(1)

    @pl.when(c == 0)
    def _():
        state_ref[...] = s0_ref[0]

    def padded(ref):
        v = ref[...]
        if rows < RET_CHUNK:
            v = jnp.concatenate([v, jnp.zeros((RET_CHUNK - rows, v.shape[1]), v.dtype)], axis=0)
        return v

    q, k, v, g = padded(q_ref), padded(k_ref), padded(v_ref), g_ref[...]
    gnw = gnw_ref[...]
    outs = []
    for h in range(RET_HEADS):
        sl = slice(h * RET_HEAD_DIM, (h + 1) * RET_HEAD_DIM)
        qh, kh, vh = q[:, sl], k[:, sl], v[:, sl]
        st = state_ref[h]
        att = _dot_nt(qh, kh) * dm_ref[h]
        o = _dot(att, vh) + _dot(qh, st) * qd_ref[h]
        state_ref[h] = st * cd_ref[h] + _dot((kh * kd_ref[h]).T, vh)
        o = o[:rows]
        mu = jnp.mean(o, axis=-1, keepdims=True)
        var = jnp.mean(jnp.square(o - mu), axis=-1, keepdims=True)
        outs.append((o - mu) * lax.rsqrt(var + GN_EPS) * gnw[:, sl])
    on = jnp.concatenate(outs, axis=1)
    o_ref[...] = g * (1.0 / (1.0 + jnp.exp(-g))) * on

    @pl.when(c == nc - 1)
    def _():
        s_out_ref[0] = state_ref[...]


def _retention(q, k, v, g, gn_w, s0, consts, nb, nc, rows, row0=0):
    blk0 = row0 // rows
    in_row = pl.BlockSpec((rows, RET_WIDTH), lambda b, c: (blk0 + b * nc + c, 0))
    out_row = pl.BlockSpec((rows, RET_WIDTH), lambda b, c: (b * nc + c, 0))
    st_spec = pl.BlockSpec((1, RET_HEADS, RET_HEAD_DIM, RET_HEAD_DIM), lambda b, c: (b, 0, 0, 0))
    const_spec = pl.BlockSpec((RET_HEADS, RET_CHUNK, RET_CHUNK), lambda b, c: (0, 0, 0))
    return pl.pallas_call(
        functools.partial(_retention_kernel, rows=rows),
        grid=(nb, nc),
        in_specs=[in_row, in_row, in_row, in_row,
                  pl.BlockSpec((1, RET_WIDTH), lambda b, c: (0, 0)),
                  st_spec, const_spec, const_spec, const_spec, const_spec],
        out_specs=[out_row, st_spec],
        out_shape=[jax.ShapeDtypeStruct((nb * nc * rows, RET_WIDTH), F32),
                   jax.ShapeDtypeStruct(s0.shape, F32)],
        scratch_shapes=[pltpu.VMEM((RET_HEADS, RET_HEAD_DIM, RET_HEAD_DIM), F32)],
        name="retention",
        compiler_params=_cparams(("arbitrary", "arbitrary")),
    )(q, k, v, g, gn_w.reshape(1, RET_WIDTH), s0, *consts)


_BISECT_PLAIN_ITERS = 26
_BISECT_MAX_ITERS = 400


def _lane_tiles(x):
    return [x[:, j * LANES:(j + 1) * LANES] for j in range(x.shape[1] // LANES)]


def _rowsum_b(x):
    return jnp.broadcast_to(jnp.sum(x, axis=1, keepdims=True), x.shape)


def _rowmax_b(x):
    return jnp.broadcast_to(jnp.max(x, axis=1, keepdims=True), x.shape)


def _rowmin_b(x):
    return jnp.broadcast_to(jnp.min(x, axis=1, keepdims=True), x.shape)


def _threshold_search(sc_ref, nck, kk, lo0, hi0, n_causal, row_active):
    R = lo0.shape[0]
    zeros = jnp.zeros((R, LANES), F32)

    def count_pass(mid, snap):
        def body(kc, carry):
            x = sc_ref[kc]
            cnt, amin, bmax = carry
            for xt in _lane_tiles(x):
                ge = xt >= mid
                cnt = cnt + jnp.where(ge, 1.0, 0.0)
                if snap:
                    amin = jnp.minimum(amin, jnp.where(ge, xt, jnp.inf))
                    bmax = jnp.maximum(bmax, jnp.where(ge, -jnp.inf, xt))
            return cnt, amin, bmax
        cnt, amin, bmax = lax.fori_loop(
            0, nck, body, (zeros, jnp.full((R, LANES), jnp.inf, F32), jnp.full((R, LANES), -jnp.inf, F32)))
        if snap:
            return _rowsum_b(cnt), _rowmin_b(amin), _rowmax_b(bmax)
        return _rowsum_b(cnt), None, None

    def not_done(c_lo, lo, hib):
        pending = row_active & (c_lo != kk) & (lo != hib)
        return jnp.max(jnp.where(pending, 1.0, 0.0)) > 0.0

    def make_step(snap):
        def step(carry):
            it, lo, hi, hib, c_lo, c_hi = carry
            mid = 0.5 * (lo + hi)
            if snap:
                mid = 0.5 * (lo + jnp.minimum(hi, hib))
                mid = jnp.where(mid > lo, mid, jnp.minimum(hi, hib))
            c, amin, bmax = count_pass(mid, snap)
            ge = c >= kk
            if snap:
                lo = jnp.where(ge, amin, lo)
                hib = jnp.where(ge, hib, bmax)
            else:
                lo = jnp.where(ge, mid, lo)
            hi = jnp.where(ge, hi, mid)
            c_lo = jnp.where(ge, c, c_lo)
            c_hi = jnp.where(ge, c_hi, c)
            return it + 1, lo, hi, hib, c_lo, c_hi
        return step

    init = (jnp.int32(0), lo0, hi0, jnp.full((R, LANES), jnp.inf, F32), n_causal, zeros)
    carry = lax.while_loop(
        lambda c: (c[0] < _BISECT_PLAIN_ITERS) & not_done(c[4], c[1], c[3]), make_step(False), init)
    carry = lax.while_loop(
        lambda c: (c[0] < _BISECT_MAX_ITERS) & not_done(c[4], c[1], c[3]), make_step(True), carry)
    _, lo, hi, hib, c_lo, c_hi = carry
    tied = row_active & (c_lo != kk)
    return lo, hi, c_hi, tied


def _select_chunk(x, lo, hi, need, tied_any, tied, prefix0):
    R, CK = x.shape
    rep = CK // LANES
    wide = lambda a: jnp.concatenate([a] * rep, axis=1)

    def plain(_):
        return jnp.where(x >= wide(lo), 1.0, 0.0), prefix0

    def with_ties(_):
        low, hiw = wide(lo), wide(hi)
        band = (x >= low) & (x < hiw)
        bandf = jnp.where(band, 1.0, 0.0)
        r_i = lax.broadcasted_iota(jnp.int32, (CK, CK), 0)
        c_i = lax.broadcasted_iota(jnp.int32, (CK, CK), 1)
        tri = jnp.where(r_i <= c_i, 1.0, 0.0).astype(BF16)
        rank = jnp.dot(bandf.astype(BF16), tri, preferred_element_type=F32) + wide(prefix0)
        take_tie = jnp.where(rank <= wide(need), bandf, 0.0)
        sel_tied = jnp.where(x >= hiw, 1.0, take_tie)
        sel = jnp.where(wide(tied) > 0.0, sel_tied, jnp.where(x >= low, 1.0, 0.0))
        return sel, prefix0 + _rowsum_b(sum(_lane_tiles(bandf)))

    return lax.cond(tied_any, with_ties, plain, 0)


_QB = 128
_CK = 512


def _prompt_dsa_kernel(qi_ref, kiwiq_ref, kiwi_ref, qa_ref, ka_ref, va_ref, o_ref,
                       sc_ref, m_ref, l_ref, acc_ref, *, topk):
    i = pl.program_id(1)
    t0 = i * _QB
    nck = (t0 + _QB + _CK - 1) // _CK
    R = _QB
    qpos = t0 + lax.broadcasted_iota(jnp.int32, (R, LANES), 0)
    lane_c = lax.broadcasted_iota(jnp.int32, (R, _CK), 1)
    qpos_c = t0 + lax.broadcasted_iota(jnp.int32, (R, _CK), 0)
    lane = lax.broadcasted_iota(jnp.int32, (R, LANES), 1)

    qi = qi_ref[...]
    wq = kiwiq_ref[...]
    q_heads = []
    w_heads = []
    for h in range(IDX_HEADS):
        qh = qi[:, h * IDX_DIM:(h + 1) * IDX_DIM] * (IDX_DIM ** -0.5)
        q_heads.append(jnp.concatenate([qh, jnp.zeros((R, LANES - IDX_DIM), F32)], axis=1).astype(BF16))
        w_heads.append(jnp.broadcast_to(wq[:, IDX_DIM + h:IDX_DIM + h + 1], (R, LANES)))

    def score_body(kc, carry):
        mn, mx = carry
        k0 = pl.multiple_of(kc * _CK, _CK)
        kc_rows = kiwi_ref[pl.ds(k0, _CK), :].astype(BF16)
        acc = jnp.zeros((R, _CK), F32)
        for h in range(IDX_HEADS):
            d = lax.dot_general(q_heads[h], kc_rows, (((1,), (1,)), ((), ())), preferred_element_type=F32)
            acc = acc + jnp.concatenate([w_heads[h]] * (_CK // LANES), axis=1) * jnp.maximum(d, 0.0)
        causal = (k0 + lane_c) <= qpos_c
        sc_ref[kc] = jnp.where(causal, acc, -jnp.inf)
        for j in range(_CK // LANES):
            a = acc[:, j * LANES:(j + 1) * LANES]
            cz = causal[:, j * LANES:(j + 1) * LANES]
            mn = jnp.minimum(mn, jnp.where(cz, a, jnp.inf))
            mx = jnp.maximum(mx, jnp.where(cz, a, -jnp.inf))
        return mn, mx

    mn, mx = lax.fori_loop(0, nck, score_body,
                           (jnp.full((R, LANES), jnp.inf, F32), jnp.full((R, LANES), -jnp.inf, F32)))
    mn, mx = _rowmin_b(mn), _rowmax_b(mx)
    n_causal = (qpos + 1).astype(F32)
    kk = jnp.minimum(n_causal, float(topk))
    hi0 = mx + (mx - mn) + 1.0
    row_active = lane >= 0
    lo, hi, c_hi, tied = _threshold_search(sc_ref, nck, kk, mn, hi0, n_causal, row_active)
    need = kk - c_hi
    tiedf = jnp.where(tied, 1.0, 0.0)
    tied_any = jnp.max(tiedf) > 0.0

    qa = qa_ref[...]
    qa_heads = []
    for h in range(ATT_HEADS):
        c = h // ATT_GROUP
        qh = qa[:, h * ATT_HEAD_DIM:(h + 1) * ATT_HEAD_DIM] * (ATT_HEAD_DIM ** -0.5)
        z = jnp.zeros((R, ATT_HEAD_DIM), F32)
        parts = [z] * KV_HEADS
        parts[c] = qh
        qa_heads.append(jnp.concatenate(parts, axis=1).astype(BF16))
    m_ref[...] = jnp.full(m_ref.shape, NEG_BIG, F32)
    l_ref[...] = jnp.zeros(l_ref.shape, F32)
    acc_ref[...] = jnp.zeros(acc_ref.shape, F32)

    def att_body(kc, prefix):
        k0 = pl.multiple_of(kc * _CK, _CK)
        kch = ka_ref[pl.ds(k0, _CK), :].astype(BF16)
        vch = va_ref[pl.ds(k0, _CK), :].astype(BF16)
        self_, prefix = _select_chunk(sc_ref[kc], lo, hi, need, tied_any, tiedf, prefix)
        selb = self_ > 0.0
        for h in range(ATT_HEADS):
            s = lax.dot_general(qa_heads[h], kch, (((1,), (1,)), ((), ())), preferred_element_type=F32)
            s = jnp.where(selb, s, NEG_BIG)
            m_old = m_ref[h]
            m_new = jnp.maximum(m_old, _rowmax_b(functools.reduce(jnp.maximum, _lane_tiles(s))))
            alpha = jnp.exp(m_old - m_new)
            p = jnp.exp(s - jnp.concatenate([m_new] * (_CK // LANES), axis=1))
            l_ref[h] = l_ref[h] * alpha + _rowsum_b(sum(_lane_tiles(p)))
            acc_ref[h] = acc_ref[h] * alpha + jnp.dot(p.astype(BF16), vch, preferred_element_type=F32)
            m_ref[h] = m_new
        return prefix

    lax.fori_loop(0, nck, att_body, jnp.zeros((R, LANES), F32))
    outs = []
    for h in range(ATT_HEADS):
        c = h // ATT_GROUP
        o = acc_ref[h] / l_ref[h]
        outs.append(o[:, c * ATT_HEAD_DIM:(c + 1) * ATT_HEAD_DIM])
    o_ref[...] = jnp.concatenate(outs, axis=1)


def _prompt_dsa(qi, kiwi, qa, ka, va, B, S):
    nq = S // _QB
    topk = min(TOPK_MAX, S // 4)
    qblk = lambda w: pl.BlockSpec((_QB, w), lambda b, i: (b * nq + i, 0))
    allk = lambda w: pl.BlockSpec((S, w), lambda b, i: (b, 0))
    return pl.pallas_call(
        functools.partial(_prompt_dsa_kernel, topk=topk),
        grid=(B, nq),
        in_specs=[qblk(IDX_WIDTH), qblk(_KIWI_WIDTH), allk(_KIWI_WIDTH), qblk(ATT_WIDTH),
                  allk(KV_WIDTH), allk(KV_WIDTH)],
        out_specs=qblk(ATT_WIDTH),
        out_shape=jax.ShapeDtypeStruct((B * S, ATT_WIDTH), F32),
        scratch_shapes=[pltpu.VMEM((S // _CK, _QB, _CK), F32),
                        pltpu.VMEM((ATT_HEADS, _QB, LANES), F32),
                        pltpu.VMEM((ATT_HEADS, _QB, LANES), F32),
                        pltpu.VMEM((ATT_HEADS, _QB, LANES), F32)],
        name="prompt_dsa",
        compiler_params=_cparams(("arbitrary", "arbitrary")),
    )(qi, kiwi, kiwi, qa, ka, va)


SROWS = SUBLANES
_CKS = 640


def _sample_dsa_kernel(pt_ref, qi_ref, kiwi_ref, qa_ref, ka_ref, va_ref, cik_hbm, ck_hbm, cv_hbm,
                       o_ref, ikbuf, kbuf, vbuf, sems, sc_ref, *, n_pages, t_real, topk):
    b = pl.program_id(0)
    nb = pl.num_programs(0)
    slot = b % 2
    past_len = n_pages * PAGE_SIZE
    L = past_len + PAGE_SIZE
    nck = L // _CKS
    R = SROWS

    def page_copies(bb, s, p):
        phys = pt_ref[bb, p]
        rows = pl.ds(p * PAGE_SIZE, PAGE_SIZE)
        return (pltpu.make_async_copy(cik_hbm.at[phys], ikbuf.at[s, rows, :], sems.at[s, 0]),
                pltpu.make_async_copy(ck_hbm.at[phys], kbuf.at[s, rows, :], sems.at[s, 1]),
                pltpu.make_async_copy(cv_hbm.at[phys], vbuf.at[s, rows, :], sems.at[s, 2]))

    def start_fetch(bb, s):
        def body(p, _):
            for cp in page_copies(bb, s, p):
                cp.start()
            return 0
        lax.fori_loop(0, n_pages, body, 0)

    def wait_fetch(bb, s):
        def body(p, _):
            for cp in page_copies(bb, s, p):
                cp.wait()
            return 0
        lax.fori_loop(0, n_pages, body, 0)

    @pl.when(b == 0)
    def _():
        tail = pl.ds(past_len, PAGE_SIZE)
        for s in range(2):
            ikbuf[s, tail, :] = jnp.zeros((PAGE_SIZE, IDX_DIM), F32)
            kbuf[s, tail, :] = jnp.zeros((PAGE_SIZE, KV_WIDTH), F32)
            vbuf[s, tail, :] = jnp.zeros((PAGE_SIZE, KV_WIDTH), F32)
        start_fetch(0, 0)

    @pl.when(b + 1 < nb)
    def _():
        start_fetch(b + 1, 1 - slot)

    kiwi = kiwi_ref[...]
    new_rows = pl.ds(past_len, R)
    ikbuf[slot, new_rows, :] = kiwi[:, :IDX_DIM]
    kbuf[slot, new_rows, :] = ka_ref[...]
    vbuf[slot, new_rows, :] = va_ref[...]
    wait_fetch(b, slot)

    row = lax.broadcasted_iota(jnp.int32, (R, LANES), 0)
    row_c = lax.broadcasted_iota(jnp.int32, (R, _CKS), 0)
    lane_c = lax.broadcasted_iota(jnp.int32, (R, _CKS), 1)
    rep = _CKS // LANES

    qi = qi_ref[...]
    q_all = jnp.concatenate(
        [qi[:, h * IDX_DIM:(h + 1) * IDX_DIM] * (IDX_DIM ** -0.5) for h in range(IDX_HEADS)], axis=0).astype(BF16)
    w_heads = [jnp.broadcast_to(kiwi[:, IDX_DIM + h:IDX_DIM + h + 1], (R, _CKS)) for h in range(IDX_HEADS)]

    def score_body(kc, carry):
        mn, mx = carry
        k0 = pl.multiple_of(kc * _CKS, LANES)
        keys = ikbuf[slot, pl.ds(k0, _CKS), :].astype(BF16)
        d = lax.dot_general(q_all, keys, (((1,), (1,)), ((), ())), preferred_element_type=F32)
        acc = jnp.zeros((R, _CKS), F32)
        for h in range(IDX_HEADS):
            acc = acc + w_heads[h] * jnp.maximum(d[h * R:(h + 1) * R], 0.0)
        causal = (k0 + lane_c) <= (past_len + row_c)
        sc_ref[kc] = jnp.where(causal, acc, -jnp.inf)
        for j in range(rep):
            a = acc[:, j * LANES:(j + 1) * LANES]
            cz = causal[:, j * LANES:(j + 1) * LANES]
            mn = jnp.minimum(mn, jnp.where(cz, a, jnp.inf))
            mx = jnp.maximum(mx, jnp.where(cz, a, -jnp.inf))
        return mn, mx

    mn, mx = lax.fori_loop(0, nck, score_body,
                           (jnp.full((R, LANES), jnp.inf, F32), jnp.full((R, LANES), -jnp.inf, F32)))
    mn, mx = _rowmin_b(mn), _rowmax_b(mx)
    n_causal = (past_len + row + 1).astype(F32)
    kk = jnp.minimum(n_causal, float(topk))
    hi0 = mx + (mx - mn) + 1.0
    row_active = row < t_real
    lo, hi, c_hi, tied = _threshold_search(sc_ref, nck, kk, mn, hi0, n_causal, row_active)
    need = kk - c_hi
    tiedf = jnp.where(tied, 1.0, 0.0)
    tied_any = jnp.max(tiedf) > 0.0

    qa = qa_ref[...]
    q_rows = []
    for h in range(ATT_HEADS):
        c = h // ATT_GROUP
        z = jnp.zeros((R, ATT_HEAD_DIM), F32)
        parts = [z] * KV_HEADS
        parts[c] = qa[:, h * ATT_HEAD_DIM:(h + 1) * ATT_HEAD_DIM] * (ATT_HEAD_DIM ** -0.5)
        q_rows.append(jnp.concatenate(parts, axis=1))
    q_big = jnp.concatenate(q_rows, axis=0).astype(BF16)
    HR = ATT_HEADS * R

    def att_body(kc, carry):
        m_old, l_old, acc, prefix = carry
        k0 = pl.multiple_of(kc * _CKS, LANES)
        kch = kbuf[slot, pl.ds(k0, _CKS), :].astype(BF16)
        vch = vbuf[slot, pl.ds(k0, _CKS), :].astype(BF16)
        self_, prefix = _select_chunk(sc_ref[kc], lo, hi, need, tied_any, tiedf, prefix)
        selb = jnp.concatenate([self_] * ATT_HEADS, axis=0) > 0.0
        s = lax.dot_general(q_big, kch, (((1,), (1,)), ((), ())), preferred_element_type=F32)
        s = jnp.where(selb, s, NEG_BIG)
        m_new = jnp.maximum(m_old, _rowmax_b(functools.reduce(jnp.maximum, _lane_tiles(s))))
        alpha = jnp.exp(m_old - m_new)
        p = jnp.exp(s - jnp.concatenate([m_new] * rep, axis=1))
        l_new = l_old * alpha + _rowsum_b(sum(_lane_tiles(p)))
        acc = acc * alpha + jnp.dot(p.astype(BF16), vch, preferred_element_type=F32)
        return m_new, l_new, acc, prefix

    init = (jnp.full((HR, LANES), NEG_BIG, F32), jnp.zeros((HR, LANES), F32), jnp.zeros((HR, LANES), F32),
            jnp.zeros((R, LANES), F32))
    _, l_fin, acc, _ = lax.fori_loop(0, nck, att_body, init)
    o = acc / l_fin
    outs = []
    for h in range(ATT_HEADS):
        c = h // ATT_GROUP
        outs.append(o[h * R:(h + 1) * R, c * ATT_HEAD_DIM:(c + 1) * ATT_HEAD_DIM])
    o_ref[...] = jnp.concatenate(outs, axis=1)


def _sample_dsa(page_table, qi, kiwi, qa, ka, va, cache_idx_k_l, cache_k_l, cache_v_l, t_real, row0):
    Bd, n_pages = page_table.shape
    past_len = n_pages * PAGE_SIZE
    L = past_len + PAGE_SIZE
    assert L % _CKS == 0 and row0 % SROWS == 0
    topk = min(TOPK_MAX, (past_len + t_real) // 4)
    blk0 = row0 // SROWS
    blk = lambda w: pl.BlockSpec((SROWS, w), lambda b, pt: (blk0 + b, 0))
    any_spec = pl.BlockSpec(memory_space=pl.ANY)
    grid_spec = pltpu.PrefetchScalarGridSpec(
        num_scalar_prefetch=1,
        grid=(Bd,),
        in_specs=[blk(IDX_WIDTH), blk(_KIWI_WIDTH), blk(ATT_WIDTH), blk(KV_WIDTH), blk(KV_WIDTH),
                  any_spec, any_spec, any_spec],
        out_specs=pl.BlockSpec((SROWS, ATT_WIDTH), lambda b, pt: (b, 0)),
        scratch_shapes=[pltpu.VMEM((2, L, IDX_DIM), F32),
                        pltpu.VMEM((2, L, KV_WIDTH), F32),
                        pltpu.VMEM((2, L, KV_WIDTH), F32),
                        pltpu.SemaphoreType.DMA((2, 3)),
                        pltpu.VMEM((L // _CKS, SROWS, _CKS), F32)],
    )
    return pl.pallas_call(
        functools.partial(_sample_dsa_kernel, n_pages=n_pages, t_real=t_real, topk=topk),
        grid_spec=grid_spec,
        out_shape=jax.ShapeDtypeStruct((Bd * SROWS, ATT_WIDTH), F32),
        name="sample_dsa",
        compiler_params=_cparams(("arbitrary",)),
    )(page_table, qi, kiwi, qa, ka, va, cache_idx_k_l, cache_k_l, cache_v_l)


_TMP = 256
PEER_HALF_EXPERTS = PEER_EXPERTS // 2


def _extract_topk(s, pos, n, k, payload=None):
    vals, idxs = [], []
    for _ in range(k):
        m = jnp.max(s, axis=0, keepdims=True)
        p = jnp.min(jnp.where(s == m, pos, n), axis=0, keepdims=True)
        hit = pos == p
        vals.append(m)
        if payload is None:
            idxs.append(p)
        else:
            idxs.append(jnp.max(jnp.where(hit, payload, -1), axis=0, keepdims=True))
        s = jnp.where(hit, -jnp.inf, s)
    return vals, idxs


def _post_mix_kernel(x_ref, attp_ref, atts_ref, retp_ref, rets_ref, wo_ref, nw_ref, wqt_ref, sk1_ref, sk2_ref,
                     h_ref, xn_ref, idx_ref, gate_ref, qt_ref, et_ref, gt_ref, *, n_prompt_tiles):
    T = x_ref.shape[0]
    is_prompt = pl.program_id(0) < n_prompt_tiles
    att = jnp.where(is_prompt, attp_ref[...], atts_ref[...])
    ret = jnp.where(is_prompt, retp_ref[...], rets_ref[...])
    h = x_ref[...] + _dot(att, wo_ref[:ATT_WIDTH, :]) + _dot(ret, wo_ref[ATT_WIDTH:, :])
    h_ref[...] = h
    ms = jnp.mean(h * h, axis=-1, keepdims=True)
    xn = h * lax.rsqrt(ms + NORM_EPS) * nw_ref[...]
    xn_ref[...] = xn
    qt_ref[...] = _dot_nt(wqt_ref[...], xn)

    pos_k = lax.broadcasted_iota(jnp.int32, (PEER_NKEYS, T), 0)
    n_cand = PEER_TOPK * PEER_TOPK
    pos_c = lax.broadcasted_iota(jnp.int32, (n_cand, T), 0)

    def head_body(hd, _):
        q0 = pl.multiple_of(hd * PEER_KEY_DIM, PEER_KEY_DIM)
        q1 = qt_ref[pl.ds(q0, PEER_HALF), :]
        q2 = qt_ref[pl.ds(q0 + PEER_HALF, PEER_HALF), :]
        s1 = _dot(sk1_ref[...], q1)
        s2 = _dot(sk2_ref[...], q2)
        v1, i1 = _extract_topk(s1, pos_k, PEER_NKEYS, PEER_TOPK)
        v2, i2 = _extract_topk(s2, pos_k, PEER_NKEYS, PEER_TOPK)
        v2m = jnp.concatenate(v2, axis=0)
        i2m = jnp.concatenate(i2, axis=0)
        cand = jnp.concatenate([v1[a] + v2m for a in range(PEER_TOPK)], axis=0)
        cid = jnp.concatenate([i1[a] * PEER_NKEYS + i2m for a in range(PEER_TOPK)], axis=0)
        sv, eid = _extract_topk(cand, pos_c, n_cand, PEER_TOPK, payload=cid)
        svm = jnp.concatenate(sv, axis=0)
        g = jnp.exp(svm - sv[0])
        r0 = pl.multiple_of(hd * PEER_TOPK, PEER_TOPK)
        gt_ref[pl.ds(r0, PEER_TOPK), :] = g / jnp.sum(g, axis=0, keepdims=True)
        et_ref[pl.ds(r0, PEER_TOPK), :] = jnp.concatenate(eid, axis=0).astype(F32)
        return 0

    lax.fori_loop(0, PEER_HEADS, head_body, 0)
    gate_ref[...] = gt_ref[...].T
    e = et_ref[...].T.astype(jnp.int32)
    idx_ref[0] = jnp.where(e < PEER_HALF_EXPERTS, e, PEER_HALF_EXPERTS)
    idx_ref[1] = jnp.where(e >= PEER_HALF_EXPERTS, e - PEER_HALF_EXPERTS, PEER_HALF_EXPERTS)


def _post_mix(x, att_p, att_s, ret_p, ret_s, w_out_bf, norm_w, wq_t_bf, sk1_bf, sk2_bf):
    N = x.shape[0]
    nt = N // _TMP
    ntp = att_p.shape[0] // _TMP
    nts = att_s.shape[0] // _TMP
    assert ntp + nts == nt and ntp > 0 and nts > 0
    npair = PEER_HEADS * PEER_TOPK
    row = lambda w: pl.BlockSpec((_TMP, w), lambda i: (i, 0))
    prow = lambda w: pl.BlockSpec((_TMP, w), lambda i: (jnp.minimum(i, ntp - 1), 0))
    srow = lambda w: pl.BlockSpec((_TMP, w), lambda i: (jnp.maximum(i - ntp, 0), 0))
    full = lambda a: pl.BlockSpec(a.shape, lambda i: (0,) * a.ndim)
    nw = norm_w.reshape(1, D_MODEL)
    return pl.pallas_call(
        functools.partial(_post_mix_kernel, n_prompt_tiles=ntp),
        grid=(nt,),
        in_specs=[row(D_MODEL), prow(ATT_WIDTH), srow(ATT_WIDTH), prow(RET_WIDTH), srow(RET_WIDTH),
                  full(w_out_bf), full(nw), full(wq_t_bf), full(sk1_bf), full(sk2_bf)],
        out_specs=[row(D_MODEL), row(D_MODEL),
                   pl.BlockSpec((2, _TMP, npair), lambda i: (0, i, 0)), row(npair)],
        out_shape=[jax.ShapeDtypeStruct((N, D_MODEL), F32), jax.ShapeDtypeStruct((N, D_MODEL), F32),
                   jax.ShapeDtypeStruct((2, N, npair), jnp.int32), jax.ShapeDtypeStruct((N, npair), F32)],
        scratch_shapes=[pltpu.VMEM((PEER_HEADS * PEER_KEY_DIM, _TMP), F32),
                        pltpu.VMEM((npair, _TMP), F32), pltpu.VMEM((npair, _TMP), F32)],
        name="post_mix",
        compiler_params=_cparams(("arbitrary",)),
    )(x, att_p, att_s, ret_p, ret_s, w_out_bf, nw, wq_t_bf, sk1_bf, sk2_bf)


_TBP = 128
_NPAIR = PEER_HEADS * PEER_TOPK
_ROW_TILE = D_MODEL // LANES
assert _ROW_TILE == SUBLANES


def _load_table_half(tbl_hbm, tbuf, sem, half):
    rows = PEER_HALF_EXPERTS * _ROW_TILE
    tbuf[pl.ds(rows, _ROW_TILE), :] = jnp.zeros((_ROW_TILE, LANES), F32)
    cp = pltpu.make_async_copy(tbl_hbm.at[pl.ds(half * rows, rows), :], tbuf.at[pl.ds(0, rows), :], sem)
    cp.start()
    cp.wait()


def _fold_pair(a, b, k, sub):
    m = (sub & k) == 0
    return jnp.where(m, a, pltpu.roll(b, k, 0)) + jnp.where(m, pltpu.roll(a, SUBLANES - k, 0), b)


def _fold8(p, sub):
    a, b, c, d, e, f, g, h = p[0], p[4], p[2], p[6], p[1], p[5], p[3], p[7]
    t1, t2, t3, t4 = (_fold_pair(a, b, 4, sub), _fold_pair(c, d, 4, sub),
                      _fold_pair(e, f, 4, sub), _fold_pair(g, h, 4, sub))
    u1, u2 = _fold_pair(t1, t2, 2, sub), _fold_pair(t3, t4, 2, sub)
    return _fold_pair(u1, u2, 1, sub)


def _peer_u_kernel(idx_ref, x_ref, u_hbm, o_ref, tbuf, sem, z_ref):
    half = pl.program_id(0)

    @pl.when(pl.program_id(1) == 0)
    def _():
        _load_table_half(u_hbm, tbuf, sem, half)

    sub = lax.broadcasted_iota(jnp.int32, (SUBLANES, LANES), 0)
    lane = lax.broadcasted_iota(jnp.int32, (_NPAIR, LANES), 1)
    ones = jnp.ones((LANES, LANES), BF16)

    def tok_body(t, y):
        xt = x_ref[t]

        def grp_body(g, _):
            tiles = []
            for s in range(SUBLANES):
                r = idx_ref[0, t, g * SUBLANES + s]
                tiles.append(tbuf[pl.ds(pl.multiple_of(r * _ROW_TILE, _ROW_TILE), _ROW_TILE), :] * xt)
            z_ref[pl.ds(pl.multiple_of(g * SUBLANES, SUBLANES), SUBLANES), :] = _fold8(tiles, sub)
            return 0

        lax.fori_loop(0, _NPAIR // SUBLANES, grp_body, 0)
        z = z_ref[...]
        zh = z.astype(BF16)
        zl = (z - zh.astype(F32)).astype(BF16)
        hv = (jnp.dot(zh, ones, preferred_element_type=F32)
              + jnp.dot(zl, ones, preferred_element_type=F32))
        return jnp.where(lane == t, hv, y)

    y = lax.fori_loop(0, _TBP, tok_body, jnp.zeros((_NPAIR, LANES), F32))
    o_ref[0] = y.T


def _peer_u(idx, xn, u_tbl):
    N = xn.shape[0]
    nt = N // _TBP
    return pl.pallas_call(
        _peer_u_kernel,
        grid=(2, nt),
        in_specs=[pl.BlockSpec((1, _TBP, _NPAIR), lambda hf, i: (hf, i, 0), memory_space=pltpu.SMEM),
                  pl.BlockSpec((_TBP, _ROW_TILE, LANES), lambda hf, i: (i, 0, 0)),
                  pl.BlockSpec(memory_space=pl.ANY)],
        out_specs=pl.BlockSpec((1, _TBP, _NPAIR), lambda hf, i: (hf, i, 0)),
        out_shape=jax.ShapeDtypeStruct((2, N, _NPAIR), F32),
        scratch_shapes=[pltpu.VMEM(((PEER_HALF_EXPERTS + 1) * _ROW_TILE, LANES), F32),
                        pltpu.SemaphoreType.DMA(()),
                        pltpu.VMEM((_NPAIR, LANES), F32)],
        name="peer_u",
        compiler_params=_cparams(("arbitrary", "arbitrary")),
    )(idx, xn.reshape(N, _ROW_TILE, LANES), u_tbl.reshape(PEER_EXPERTS * _ROW_TILE, LANES))


def _peer_coef_kernel(hp_ref, gate_ref, o_ref):
    hv = hp_ref[0] + hp_ref[1]
    o_ref[...] = gate_ref[...] * (0.5 * hv * (1.0 + lax.erf(hv * (0.5 ** 0.5))))


def _peer_coef(hpart, gate):
    N = gate.shape[0]
    tm = 1024 if N % 1024 == 0 else _TBP
    return pl.pallas_call(
        _peer_coef_kernel,
        grid=(N // tm,),
        in_specs=[pl.BlockSpec((2, tm, _NPAIR), lambda i: (0, i, 0)), pl.BlockSpec((tm, _NPAIR), lambda i: (i, 0))],
        out_specs=pl.BlockSpec((tm, _NPAIR), lambda i: (i, 0)),
        out_shape=jax.ShapeDtypeStruct((N, _NPAIR), F32),
        name="peer_coef",
        compiler_params=_cparams(("arbitrary",)),
    )(hpart, gate)


def _peer_v_kernel(idx_ref, coef_ref, v_hbm, o_ref, tbuf, sem):
    half = pl.program_id(0)

    @pl.when(pl.program_id(1) == 0)
    def _():
        _load_table_half(v_hbm, tbuf, sem, half)

    n_acc = 4

    def tok_body(t, _):
        def grp_body(g, accs):
            accs = list(accs)
            for s in range(SUBLANES):
                j = g * SUBLANES + s
                r = idx_ref[0, t, j]
                row = tbuf[pl.ds(pl.multiple_of(r * _ROW_TILE, _ROW_TILE), _ROW_TILE), :]
                accs[s % n_acc] = accs[s % n_acc] + coef_ref[t, j] * row
            return tuple(accs)

        zero = jnp.zeros((_ROW_TILE, LANES), F32)
        accs = lax.fori_loop(0, _NPAIR // SUBLANES, grp_body, (zero,) * n_acc)
        o_ref[0, t] = (accs[0] + accs[1]) + (accs[2] + accs[3])
        return 0

    lax.fori_loop(0, _TBP, tok_body, 0)


def _peer_v(idx, coef, v_tbl):
    N = coef.shape[0]
    nt = N // _TBP
    out = pl.pallas_call(
        _peer_v_kernel,
        grid=(2, nt),
        in_specs=[pl.BlockSpec((1, _TBP, _NPAIR), lambda hf, i: (hf, i, 0), memory_space=pltpu.SMEM),
                  pl.BlockSpec((_TBP, _NPAIR), lambda hf, i: (i, 0), memory_space=pltpu.SMEM),
                  pl.BlockSpec(memory_space=pl.ANY)],
        out_specs=pl.BlockSpec((1, _TBP, _ROW_TILE, LANES), lambda hf, i: (hf, i, 0, 0)),
        out_shape=jax.ShapeDtypeStruct((2, N, _ROW_TILE, LANES), F32),
        scratch_shapes=[pltpu.VMEM(((PEER_HALF_EXPERTS + 1) * _ROW_TILE, LANES), F32),
                        pltpu.SemaphoreType.DMA(())],
        name="peer_v",
        compiler_params=_cparams(("arbitrary", "arbitrary")),
    )(idx, coef, v_tbl.reshape(PEER_EXPERTS * _ROW_TILE, LANES))
    return out.reshape(2, N, D_MODEL)


def _final_kernel(h_ref, y_ref, w_ref, o_ref, *, normalize):
    h = h_ref[...] + (y_ref[0] + y_ref[1])
    if normalize:
        ms = jnp.mean(h * h, axis=-1, keepdims=True)
        h = h * lax.rsqrt(ms + NORM_EPS) * w_ref[...]
    o_ref[...] = h


def _final(h, ypart, w):
    N = h.shape[0]
    tm = 512 if N % 512 == 0 else _TBP
    normalize = w is not None
    if not normalize:
        w = jnp.ones((D_MODEL,), F32)
    return pl.pallas_call(
        functools.partial(_final_kernel, normalize=normalize),
        grid=(N // tm,),
        in_specs=[pl.BlockSpec((tm, D_MODEL), lambda i: (i, 0)),
                  pl.BlockSpec((2, tm, D_MODEL), lambda i: (0, i, 0)),
                  pl.BlockSpec((1, D_MODEL), lambda i: (0, 0))],
        out_specs=pl.BlockSpec((tm, D_MODEL), lambda i: (i, 0)),
        out_shape=jax.ShapeDtypeStruct((N, D_MODEL), F32),
        name="final_norm",
        compiler_params=_cparams(("arbitrary",)),
    )(h, ypart, w.reshape(1, D_MODEL))


_TM_IN = 256


def kernel(x_prompt, x_sample, cache_k, cache_v, cache_idx_k, state_ret, page_table, norm_attn_w, w_in, ret_gn_w,
           w_out, norm_ffn_w, peer_w_q, peer_sub_keys_1, peer_sub_keys_2, peer_u, peer_v, final_norm_w):
    B, S, D = x_prompt.shape
    Bd, T, _ = x_sample.shape
    depth = w_in.shape[0]
    n_pages = page_table.shape[1]
    past_len = n_pages * PAGE_SIZE
    n_phys = cache_k.shape[1]
    Np, Ns = B * S, Bd * SROWS
    assert D == D_MODEL and T <= SROWS and S % _TM_IN == 0 and Ns % _TM_IN == 0 and S % _CK == 0

    hs_pad = jnp.pad(x_sample, ((0, 0), (0, SROWS - T), (0, 0)))
    h_all = jnp.concatenate([x_prompt.reshape(Np, D), hs_pad.reshape(Ns, D)], axis=0)

    pos_s = past_len + jnp.arange(SROWS)
    pos = jnp.concatenate([jnp.arange(S), jnp.tile(pos_s, _TM_IN // SROWS)])
    tab_att, tab_ret = _rope_tables(pos)
    tiles_per_seq = S // _TM_IN
    tab_blocks = [i % tiles_per_seq for i in range(Np // _TM_IN)] + [tiles_per_seq] * (Ns // _TM_IN)
    ret_consts_p = _ret_constants(RET_CHUNK)
    ret_consts_s = _ret_constants(T)

    outs = {n: [] for n in ("kp", "vp", "ikp", "sp", "ks", "vs", "iks", "ss")}
    for l in range(depth):
        m = _in_proj(h_all, norm_attn_w[l], _pad_w_in(w_in[l]), tab_att, tab_ret, tab_blocks, _TM_IN)
        att_p = _prompt_dsa(m["qi"], m["kiwi"], m["qa"], m["ka"], m["va"], B, S)
        att_s = _sample_dsa(page_table, m["qi"], m["kiwi"], m["qa"], m["ka"], m["va"],
                            cache_idx_k[l], cache_k[l].reshape(n_phys, PAGE_SIZE, KV_WIDTH),
                            cache_v[l].reshape(n_phys, PAGE_SIZE, KV_WIDTH), T, Np)
        ret_p, s_p = _retention(m["qr"], m["kr"], m["vr"], m["gr"], ret_gn_w[l],
                                jnp.zeros((B, RET_HEADS, RET_HEAD_DIM, RET_HEAD_DIM), F32),
                                ret_consts_p, B, S // RET_CHUNK, RET_CHUNK)
        ret_s, s_s = _retention(m["qr"], m["kr"], m["vr"], m["gr"], ret_gn_w[l], state_ret[l].astype(F32),
                                ret_consts_s, Bd, 1, SROWS, row0=Np)
        h_mid, xn, idx, gate = _post_mix(h_all, att_p, att_s, ret_p, ret_s, w_out[l].astype(BF16), norm_ffn_w[l],
                                         peer_w_q[l].T.astype(BF16), peer_sub_keys_1[l].astype(BF16),
                                         peer_sub_keys_2[l].astype(BF16))
        coef = _peer_coef(_peer_u(idx, xn, peer_u[l]), gate)
        ypart = _peer_v(idx, coef, peer_v[l])
        last = l == depth - 1
        h_all = _final(h_mid, ypart, final_norm_w if last else None)

        sample = lambda a, w: a[Np:].reshape(Bd, SROWS, w)[:, :T]
        outs["kp"].append(m["ka"][:Np].reshape(B, S, KV_HEADS, ATT_HEAD_DIM))
        outs["vp"].append(m["va"][:Np].reshape(B, S, KV_HEADS, ATT_HEAD_DIM))
        outs["ikp"].append(m["kiwi"][:Np, :IDX_DIM].reshape(B, S, IDX_DIM))
        outs["sp"].append(s_p.astype(x_prompt.dtype))
        outs["ks"].append(sample(m["ka"], KV_WIDTH).reshape(Bd, T, KV_HEADS, ATT_HEAD_DIM))
        outs["vs"].append(sample(m["va"], KV_WIDTH).reshape(Bd, T, KV_HEADS, ATT_HEAD_DIM))
        outs["iks"].append(sample(m["kiwi"], _KIWI_WIDTH)[:, :, :IDX_DIM])
        outs["ss"].append(s_s.astype(state_ret.dtype))

    y_prompt = h_all[:Np].reshape(B, S, D)
    y_sample = h_all[Np:].reshape(Bd, SROWS, D)[:, :T]
    st = lambda n: jnp.stack(outs[n])
    return (y_prompt, y_sample, st("kp"), st("vp"), st("ikp"), st("sp"),
            st("ks"), st("vs"), st("iks"), st("ss"))
```

```python
import functools

import jax
import jax.numpy as jnp
import numpy as np
from jax import lax
from jax.experimental import pallas as pl
from jax.experimental.pallas import tpu as pltpu

F32 = jnp.float32
BF16 = jnp.bfloat16

D_MODEL = 1024
PAGE_SIZE = 128
ATT_HEADS = 8
ATT_HEAD_DIM = 64
KV_HEADS = 2
ATT_GROUP = ATT_HEADS // KV_HEADS
ATT_WIDTH = ATT_HEADS * ATT_HEAD_DIM
KV_WIDTH = KV_HEADS * ATT_HEAD_DIM
ROPE_THETA = 500000.0
ATT_ROPE_DIMS = ATT_HEAD_DIM // 4
IDX_HEADS = 8
IDX_DIM = 64
IDX_WIDTH = IDX_HEADS * IDX_DIM
TOPK_MAX = 256
RET_HEADS = 4
RET_HEAD_DIM = 128
RET_WIDTH = RET_HEADS * RET_HEAD_DIM
RET_ROPE_THETA = 10000.0
RET_CHUNK = 128
PEER_HEADS = 8
PEER_NKEYS = 128
PEER_EXPERTS = PEER_NKEYS * PEER_NKEYS
PEER_KEY_DIM = 128
PEER_HALF = PEER_KEY_DIM // 2
PEER_TOPK = 16
NORM_EPS = 1e-6
GN_EPS = 1e-6

LANES = 128
SUBLANES = 8
VMEM_LIMIT_BYTES = 56 * 1024 * 1024

NEG_BIG = -1e30

_IN_SPLITS = (ATT_WIDTH, KV_WIDTH, KV_WIDTH, IDX_WIDTH, IDX_DIM, IDX_HEADS,
              RET_WIDTH, RET_WIDTH, RET_WIDTH, RET_WIDTH)
_KIWI_WIDTH = LANES
_PAD_OFFS = {}
_off = 0
for _name, _w in (("qa", ATT_WIDTH), ("ka", KV_WIDTH), ("va", KV_WIDTH), ("qi", IDX_WIDTH),
                  ("kiwi", _KIWI_WIDTH), ("qr", RET_WIDTH), ("kr", RET_WIDTH),
                  ("vr", RET_WIDTH), ("gr", RET_WIDTH)):
    _PAD_OFFS[_name] = (_off, _w)
    _off += _w
IN_WIDTH_PADDED = _off


def _cparams(sem):
    return pltpu.CompilerParams(dimension_semantics=sem, vmem_limit_bytes=VMEM_LIMIT_BYTES)


def _dot(a, b):
    return jnp.dot(a.astype(BF16), b.astype(BF16), preferred_element_type=F32)


def _dot_nt(a, b):
    return lax.dot_general(a.astype(BF16), b.astype(BF16), (((1,), (1,)), ((), ())),
                           preferred_element_type=F32)


def _pad_w_in(w_in_l):
    cols = []
    off = 0
    parts = []
    for n in _IN_SPLITS:
        parts.append(w_in_l[:, off:off + n])
        off += n
    qa, ka, va, qi, ki, wi, qr, kr, vr, gr = parts
    kiwi = jnp.concatenate(
        [ki, wi, jnp.zeros((w_in_l.shape[0], _KIWI_WIDTH - IDX_DIM - IDX_HEADS), w_in_l.dtype)], axis=1)
    cols = [qa, ka, va, qi, kiwi, qr, kr, vr, gr]
    return jnp.concatenate(cols, axis=1).astype(BF16)


def _rope_tables(pos):
    posf = pos.astype(F32)
    half = ATT_ROPE_DIMS // 2
    inv = 1.0 / (ROPE_THETA ** (jnp.arange(half, dtype=F32) / half))
    ang = posf[:, None] * inv[None, :]
    cos, sin = jnp.cos(ang), jnp.sin(ang)
    P = pos.shape[0]
    one = jnp.ones((P, ATT_HEAD_DIM - ATT_ROPE_DIMS), F32)
    zero = jnp.zeros((P, ATT_HEAD_DIM - ATT_ROPE_DIMS), F32)
    zh = jnp.zeros((P, half), F32)
    c_head = jnp.concatenate([cos, cos, one], axis=1)
    s1_head = jnp.concatenate([-sin, zh, zero], axis=1)
    s2_head = jnp.concatenate([zh, sin, zero], axis=1)
    rep = LANES // ATT_HEAD_DIM
    att = jnp.stack([jnp.tile(c_head, (1, rep)), jnp.tile(s1_head, (1, rep)), jnp.tile(s2_head, (1, rep))])
    halfr = RET_HEAD_DIM // 2
    invr = 1.0 / (RET_ROPE_THETA ** (jnp.arange(halfr, dtype=F32) / halfr))
    angr = posf[:, None] * invr[None, :]
    cr, sr = jnp.cos(angr), jnp.sin(angr)
    ret = jnp.stack([jnp.concatenate([cr, cr], axis=1), jnp.concatenate([-sr, sr], axis=1)])
    return att, ret


def _in_proj_kernel(x_ref, nw_ref, w_ref, ta_ref, tr_ref,
                    qa_ref, ka_ref, va_ref, qi_ref, kiwi_ref, qr_ref, kr_ref, vr_ref, gr_ref):
    x = x_ref[...]
    ms = jnp.mean(x * x, axis=-1, keepdims=True)
    xn = (x * lax.rsqrt(ms + NORM_EPS) * nw_ref[...]).astype(BF16)
    ca, s1a, s2a = ta_ref[0], ta_ref[1], ta_ref[2]
    cr, sr = tr_ref[0], tr_ref[1]

    def proj(name):
        off, w = _PAD_OFFS[name]
        return jnp.dot(xn, w_ref[:, off:off + w], preferred_element_type=F32)

    def rope_att_chunk(vc):
        return (vc * ca + pltpu.roll(vc, LANES - ATT_ROPE_DIMS // 2, 1) * s1a
                + pltpu.roll(vc, ATT_ROPE_DIMS // 2, 1) * s2a)

    def rope_ret_chunk(vc):
        return vc * cr + pltpu.roll(vc, RET_HEAD_DIM // 2, 1) * sr

    def per_chunk(v, fn):
        n = v.shape[1] // LANES
        return jnp.concatenate([fn(v[:, c * LANES:(c + 1) * LANES]) for c in range(n)], axis=1)

    qa_ref[...] = per_chunk(proj("qa"), rope_att_chunk)
    ka_ref[...] = per_chunk(proj("ka"), rope_att_chunk)
    va_ref[...] = proj("va")
    qi_ref[...] = per_chunk(proj("qi"), rope_att_chunk)
    kiwi = proj("kiwi")
    lane = lax.broadcasted_iota(jnp.int32, kiwi.shape, 1)
    kiwi_ref[...] = jnp.where(lane < IDX_DIM, rope_att_chunk(kiwi), kiwi * (IDX_HEADS ** -0.5))
    qr_ref[...] = per_chunk(proj("qr"), rope_ret_chunk)
    kr_ref[...] = per_chunk(proj("kr"), rope_ret_chunk) * (RET_HEAD_DIM ** -0.5)
    vr_ref[...] = proj("vr")
    gr_ref[...] = proj("gr")


def _in_proj(x, norm_w, w_pad, tab_att, tab_ret, tab_block_of_tile, tm):
    N = x.shape[0]
    nt = N // tm
    tab_idx = jnp.asarray(tab_block_of_tile, jnp.int32)
    names = ("qa", "ka", "va", "qi", "kiwi", "qr", "kr", "vr", "gr")
    out_shape = [jax.ShapeDtypeStruct((N, _PAD_OFFS[n][1]), F32) for n in names]
    out_specs = [pl.BlockSpec((tm, _PAD_OFFS[n][1]), lambda i, t: (i, 0)) for n in names]
    grid_spec = pltpu.PrefetchScalarGridSpec(
        num_scalar_prefetch=1,
        grid=(nt,),
        in_specs=[
            pl.BlockSpec((tm, D_MODEL), lambda i, t: (i, 0)),
            pl.BlockSpec((1, D_MODEL), lambda i, t: (0, 0)),
            pl.BlockSpec((D_MODEL, IN_WIDTH_PADDED), lambda i, t: (0, 0)),
            pl.BlockSpec((3, tm, LANES), lambda i, t: (0, t[i], 0)),
            pl.BlockSpec((2, tm, LANES), lambda i, t: (0, t[i], 0)),
        ],
        out_specs=out_specs,
    )

    def body(t_ref, *refs):
        _in_proj_kernel(*refs)

    outs = pl.pallas_call(
        body, grid_spec=grid_spec, out_shape=out_shape, name="in_proj",
        compiler_params=_cparams(("arbitrary",)),
    )(tab_idx, x, norm_w.reshape(1, D_MODEL), w_pad, tab_att, tab_ret)
    return dict(zip(names, outs))


def _ret_constants(c_eff):
    C = RET_CHUNK
    lg = jnp.log1p(-(2.0 ** (-5.0 - jnp.arange(RET_HEADS, dtype=F32))))
    i = jnp.arange(C, dtype=F32)
    diff = i[:, None] - i[None, :]
    dmask = jnp.where(diff >= 0, jnp.exp(lg[:, None, None] * jnp.maximum(diff, 0.0)), 0.0)
    real = (i < c_eff)
    dmask = jnp.where(real[None, :, None] & real[None, None, :], dmask, 0.0)
    q_dec = jnp.exp(lg[:, None] * (i[None, :] + 1.0))
    k_dec = jnp.where(real[None, :], jnp.exp(lg[:, None] * (c_eff - 1.0 - i[None, :])), 0.0)
    chunk_dec = jnp.exp(lg * c_eff)
    bc = lambda a: jnp.broadcast_to(a[:, :, None], (RET_HEADS, C, C))
    cdec = jnp.broadcast_to(chunk_dec[:, None, None], (RET_HEADS, C, C))
    return dmask, bc(q_dec), bc(k_dec), cdec


def _retention_kernel(q_ref, k_ref, v_ref, g_ref, gnw_ref, s0_ref, dm_ref, qd_ref, kd_ref, cd_ref,
                      o_ref, s_out_ref, state_ref, *, rows):
    c = pl.program_id(1)
    nc = pl.num_programs(1)

    @pl.when(c == 0)
    def _():
        state_ref[...] = s0_ref[0]

    def padded(ref):
        v = ref[...]
        if rows < RET_CHUNK:
            v = jnp.concatenate([v, jnp.zeros((RET_CHUNK - rows, v.shape[1]), v.dtype)], axis=0)
        return v

    q, k, v, g = padded(q_ref), padded(k_ref), padded(v_ref), g_ref[...]
    gnw = gnw_ref[...]
    outs = []
    for h in range(RET_HEADS):
        sl = slice(h * RET_HEAD_DIM, (h + 1) * RET_HEAD_DIM)
        qh, kh, vh = q[:, sl], k[:, sl], v[:, sl]
        st = state_ref[h]
        att = _dot_nt(qh, kh) * dm_ref[h]
        o = _dot(att, vh) + _dot(qh, st) * qd_ref[h]
        state_ref[h] = st * cd_ref[h] + _dot((kh * kd_ref[h]).T, vh)
        o = o[:rows]
        mu = jnp.mean(o, axis=-1, keepdims=True)
        var = jnp.mean(jnp.square(o - mu), axis=-1, keepdims=True)
        outs.append((o - mu) * lax.rsqrt(var + GN_EPS) * gnw[:, sl])
    on = jnp.concatenate(outs, axis=1)
    o_ref[...] = g * (1.0 / (1.0 + jnp.exp(-g))) * on

    @pl.when(c == nc - 1)
    def _():
        s_out_ref[0] = state_ref[...]


def _retention(q, k, v, g, gn_w, s0, consts, nb, nc, rows, row0=0):
    blk0 = row0 // rows
    in_row = pl.BlockSpec((rows, RET_WIDTH), lambda b, c: (blk0 + b * nc + c, 0))
    out_row = pl.BlockSpec((rows, RET_WIDTH), lambda b, c: (b * nc + c, 0))
    st_spec = pl.BlockSpec((1, RET_HEADS, RET_HEAD_DIM, RET_HEAD_DIM), lambda b, c: (b, 0, 0, 0))
    const_spec = pl.BlockSpec((RET_HEADS, RET_CHUNK, RET_CHUNK), lambda b, c: (0, 0, 0))
    return pl.pallas_call(
        functools.partial(_retention_kernel, rows=rows),
        grid=(nb, nc),
        in_specs=[in_row, in_row, in_row, in_row,
                  pl.BlockSpec((1, RET_WIDTH), lambda b, c: (0, 0)),
                  st_spec, const_spec, const_spec, const_spec, const_spec],
        out_specs=[out_row, st_spec],
        out_shape=[jax.ShapeDtypeStruct((nb * nc * rows, RET_WIDTH), F32),
                   jax.ShapeDtypeStruct(s0.shape, F32)],
        scratch_shapes=[pltpu.VMEM((RET_HEADS, RET_HEAD_DIM, RET_HEAD_DIM), F32)],
        name="retention",
        compiler_params=_cparams(("arbitrary", "arbitrary")),
    )(q, k, v, g, gn_w.reshape(1, RET_WIDTH), s0, *consts)


_BISECT_PLAIN_ITERS = 26
_BISECT_MAX_ITERS = 400


def _lane_tiles(x):
    return [x[:, j * LANES:(j + 1) * LANES] for j in range(x.shape[1] // LANES)]


def _rowsum_b(x):
    return jnp.broadcast_to(jnp.sum(x, axis=1, keepdims=True), x.shape)


def _rowmax_b(x):
    return jnp.broadcast_to(jnp.max(x, axis=1, keepdims=True), x.shape)


def _rowmin_b(x):
    return jnp.broadcast_to(jnp.min(x, axis=1, keepdims=True), x.shape)


def _threshold_search(sc_ref, nck, kk, lo0, hi0, n_causal, row_active):
    R = lo0.shape[0]
    zeros = jnp.zeros((R, LANES), F32)

    def count_pass(mid, snap):
        def body(kc, carry):
            x = sc_ref[kc]
            cnt, amin, bmax = carry
            for xt in _lane_tiles(x):
                ge = xt >= mid
                cnt = cnt + jnp.where(ge, 1.0, 0.0)
                if snap:
                    amin = jnp.minimum(amin, jnp.where(ge, xt, jnp.inf))
                    bmax = jnp.maximum(bmax, jnp.where(ge, -jnp.inf, xt))
            return cnt, amin, bmax
        cnt, amin, bmax = lax.fori_loop(
            0, nck, body, (zeros, jnp.full((R, LANES), jnp.inf, F32), jnp.full((R, LANES), -jnp.inf, F32)))
        if snap:
            return _rowsum_b(cnt), _rowmin_b(amin), _rowmax_b(bmax)
        return _rowsum_b(cnt), None, None

    def not_done(c_lo, lo, hib):
        pending = row_active & (c_lo != kk) & (lo != hib)
        return jnp.max(jnp.where(pending, 1.0, 0.0)) > 0.0

    def make_step(snap):
        def step(carry):
            it, lo, hi, hib, c_lo, c_hi = carry
            mid = 0.5 * (lo + hi)
            if snap:
                mid = 0.5 * (lo + jnp.minimum(hi, hib))
                mid = jnp.where(mid > lo, mid, jnp.minimum(hi, hib))
            c, amin, bmax = count_pass(mid, snap)
            ge = c >= kk
            if snap:
                lo = jnp.where(ge, amin, lo)
                hib = jnp.where(ge, hib, bmax)
            else:
                lo = jnp.where(ge, mid, lo)
            hi = jnp.where(ge, hi, mid)
            c_lo = jnp.where(ge, c, c_lo)
            c_hi = jnp.where(ge, c_hi, c)
            return it + 1, lo, hi, hib, c_lo, c_hi
        return step

    init = (jnp.int32(0), lo0, hi0, jnp.full((R, LANES), jnp.inf, F32), n_causal, zeros)
    carry = lax.while_loop(
        lambda c: (c[0] < _BISECT_PLAIN_ITERS) & not_done(c[4], c[1], c[3]), make_step(False), init)
    carry = lax.while_loop(
        lambda c: (c[0] < _BISECT_MAX_ITERS) & not_done(c[4], c[1], c[3]), make_step(True), carry)
    _, lo, hi, hib, c_lo, c_hi = carry
    tied = row_active & (c_lo != kk)
    return lo, hi, c_hi, tied


def _select_chunk(x, lo, hi, need, tied_any, tied, prefix0):
    R, CK = x.shape
    rep = CK // LANES
    wide = lambda a: jnp.concatenate([a] * rep, axis=1)

    def plain(_):
        return jnp.where(x >= wide(lo), 1.0, 0.0), prefix0

    def with_ties(_):
        low, hiw = wide(lo), wide(hi)
        band = (x >= low) & (x < hiw)
        bandf = jnp.where(band, 1.0, 0.0)
        r_i = lax.broadcasted_iota(jnp.int32, (CK, CK), 0)
        c_i = lax.broadcasted_iota(jnp.int32, (CK, CK), 1)
        tri = jnp.where(r_i <= c_i, 1.0, 0.0).astype(BF16)
        rank = jnp.dot(bandf.astype(BF16), tri, preferred_element_type=F32) + wide(prefix0)
        take_tie = jnp.where(rank <= wide(need), bandf, 0.0)
        sel_tied = jnp.where(x >= hiw, 1.0, take_tie)
        sel = jnp.where(wide(tied) > 0.0, sel_tied, jnp.where(x >= low, 1.0, 0.0))
        return sel, prefix0 + _rowsum_b(sum(_lane_tiles(bandf)))

    return lax.cond(tied_any, with_ties, plain, 0)


_QB = 128
_CK = 512


def _prompt_dsa_kernel(qi_ref, kiwiq_ref, kiwi_ref, qa_ref, ka_ref, va_ref, o_ref,
                       sc_ref, m_ref, l_ref, acc_ref, *, topk):
    i = pl.program_id(1)
    t0 = i * _QB
    nck = (t0 + _QB + _CK - 1) // _CK
    R = _QB
    qpos = t0 + lax.broadcasted_iota(jnp.int32, (R, LANES), 0)
    lane_c = lax.broadcasted_iota(jnp.int32, (R, _CK), 1)
    qpos_c = t0 + lax.broadcasted_iota(jnp.int32, (R, _CK), 0)
    lane = lax.broadcasted_iota(jnp.int32, (R, LANES), 1)

    qi = qi_ref[...]
    wq = kiwiq_ref[...]
    q_heads = []
    w_heads = []
    for h in range(IDX_HEADS):
        qh = qi[:, h * IDX_DIM:(h + 1) * IDX_DIM] * (IDX_DIM ** -0.5)
        q_heads.append(jnp.concatenate([qh, jnp.zeros((R, LANES - IDX_DIM), F32)], axis=1).astype(BF16))
        w_heads.append(jnp.broadcast_to(wq[:, IDX_DIM + h:IDX_DIM + h + 1], (R, LANES)))

    def score_body(kc, carry):
        mn, mx = carry
        k0 = pl.multiple_of(kc * _CK, _CK)
        kc_rows = kiwi_ref[pl.ds(k0, _CK), :].astype(BF16)
        acc = jnp.zeros((R, _CK), F32)
        for h in range(IDX_HEADS):
            d = lax.dot_general(q_heads[h], kc_rows, (((1,), (1,)), ((), ())), preferred_element_type=F32)
            acc = acc + jnp.concatenate([w_heads[h]] * (_CK // LANES), axis=1) * jnp.maximum(d, 0.0)
        causal = (k0 + lane_c) <= qpos_c
        sc_ref[kc] = jnp.where(causal, acc, -jnp.inf)
        for j in range(_CK // LANES):
            a = acc[:, j * LANES:(j + 1) * LANES]
            cz = causal[:, j * LANES:(j + 1) * LANES]
            mn = jnp.minimum(mn, jnp.where(cz, a, jnp.inf))
            mx = jnp.maximum(mx, jnp.where(cz, a, -jnp.inf))
        return mn, mx

    mn, mx = lax.fori_loop(0, nck, score_body,
                           (jnp.full((R, LANES), jnp.inf, F32), jnp.full((R, LANES), -jnp.inf, F32)))
    mn, mx = _rowmin_b(mn), _rowmax_b(mx)
    n_causal = (qpos + 1).astype(F32)
    kk = jnp.minimum(n_causal, float(topk))
    hi0 = mx + (mx - mn) + 1.0
    row_active = lane >= 0
    lo, hi, c_hi, tied = _threshold_search(sc_ref, nck, kk, mn, hi0, n_causal, row_active)
    need = kk - c_hi
    tiedf = jnp.where(tied, 1.0, 0.0)
    tied_any = jnp.max(tiedf) > 0.0

    qa = qa_ref[...]
    qa_heads = []
    for h in range(ATT_HEADS):
        c = h // ATT_GROUP
        qh = qa[:, h * ATT_HEAD_DIM:(h + 1) * ATT_HEAD_DIM] * (ATT_HEAD_DIM ** -0.5)
        z = jnp.zeros((R, ATT_HEAD_DIM), F32)
        parts = [z] * KV_HEADS
        parts[c] = qh
        qa_heads.append(jnp.concatenate(parts, axis=1).astype(BF16))
    m_ref[...] = jnp.full(m_ref.shape, NEG_BIG, F32)
    l_ref[...] = jnp.zeros(l_ref.shape, F32)
    acc_ref[...] = jnp.zeros(acc_ref.shape, F32)

    def att_body(kc, prefix):
        k0 = pl.multiple_of(kc * _CK, _CK)
        kch = ka_ref[pl.ds(k0, _CK), :].astype(BF16)
        vch = va_ref[pl.ds(k0, _CK), :].astype(BF16)
        self_, prefix = _select_chunk(sc_ref[kc], lo, hi, need, tied_any, tiedf, prefix)
        selb = self_ > 0.0
        for h in range(ATT_HEADS):
            s = lax.dot_general(qa_heads[h], kch, (((1,), (1,)), ((), ())), preferred_element_type=F32)
            s = jnp.where(selb, s, NEG_BIG)
            m_old = m_ref[h]
            m_new = jnp.maximum(m_old, _rowmax_b(functools.reduce(jnp.maximum, _lane_tiles(s))))
            alpha = jnp.exp(m_old - m_new)
            p = jnp.exp(s - jnp.concatenate([m_new] * (_CK // LANES), axis=1))
            l_ref[h] = l_ref[h] * alpha + _rowsum_b(sum(_lane_tiles(p)))
            acc_ref[h] = acc_ref[h] * alpha + jnp.dot(p.astype(BF16), vch, preferred_element_type=F32)
            m_ref[h] = m_new
        return prefix

    lax.fori_loop(0, nck, att_body, jnp.zeros((R, LANES), F32))
    outs = []
    for h in range(ATT_HEADS):
        c = h // ATT_GROUP
        o = acc_ref[h] / l_ref[h]
        outs.append(o[:, c * ATT_HEAD_DIM:(c + 1) * ATT_HEAD_DIM])
    o_ref[...] = jnp.concatenate(outs, axis=1)


def _prompt_dsa(qi, kiwi, qa, ka, va, B, S):
    nq = S // _QB
    topk = min(TOPK_MAX, S // 4)
    qblk = lambda w: pl.BlockSpec((_QB, w), lambda b, i: (b * nq + i, 0))
    allk = lambda w: pl.BlockSpec((S, w), lambda b, i: (b, 0))
    return pl.pallas_call(
        functools.partial(_prompt_dsa_kernel, topk=topk),
        grid=(B, nq),
        in_specs=[qblk(IDX_WIDTH), qblk(_KIWI_WIDTH), allk(_KIWI_WIDTH), qblk(ATT_WIDTH),
                  allk(KV_WIDTH), allk(KV_WIDTH)],
        out_specs=qblk(ATT_WIDTH),
        out_shape=jax.ShapeDtypeStruct((B * S, ATT_WIDTH), F32),
        scratch_shapes=[pltpu.VMEM((S // _CK, _QB, _CK), F32),
                        pltpu.VMEM((ATT_HEADS, _QB, LANES), F32),
                        pltpu.VMEM((ATT_HEADS, _QB, LANES), F32),
                        pltpu.VMEM((ATT_HEADS, _QB, LANES), F32)],
        name="prompt_dsa",
        compiler_params=_cparams(("arbitrary", "arbitrary")),
    )(qi, kiwi, kiwi, qa, ka, va)


SROWS = SUBLANES
_CKS = 640


def _sample_dsa_kernel(pt_ref, qi_ref, kiwi_ref, qa_ref, ka_ref, va_ref, cik_hbm, ck_hbm, cv_hbm,
                       o_ref, ikbuf, kbuf, vbuf, sems, sc_ref, *, n_pages, t_real, topk):
    b = pl.program_id(0)
    nb = pl.num_programs(0)
    slot = b % 2
    past_len = n_pages * PAGE_SIZE
    L = past_len + PAGE_SIZE
    nck = L // _CKS
    R = SROWS

    def page_copies(bb, s, p):
        phys = pt_ref[bb, p]
        rows = pl.ds(p * PAGE_SIZE, PAGE_SIZE)
        return (pltpu.make_async_copy(cik_hbm.at[phys], ikbuf.at[s, rows, :], sems.at[s, 0]),
                pltpu.make_async_copy(ck_hbm.at[phys], kbuf.at[s, rows, :], sems.at[s, 1]),
                pltpu.make_async_copy(cv_hbm.at[phys], vbuf.at[s, rows, :], sems.at[s, 2]))

    def start_fetch(bb, s):
        def body(p, _):
            for cp in page_copies(bb, s, p):
                cp.start()
            return 0
        lax.fori_loop(0, n_pages, body, 0)

    def wait_fetch(bb, s):
        def body(p, _):
            for cp in page_copies(bb, s, p):
                cp.wait()
            return 0
        lax.fori_loop(0, n_pages, body, 0)

    @pl.when(b == 0)
    def _():
        tail = pl.ds(past_len, PAGE_SIZE)
        for s in range(2):
            ikbuf[s, tail, :] = jnp.zeros((PAGE_SIZE, IDX_DIM), F32)
            kbuf[s, tail, :] = jnp.zeros((PAGE_SIZE, KV_WIDTH), F32)
            vbuf[s, tail, :] = jnp.zeros((PAGE_SIZE, KV_WIDTH), F32)
        start_fetch(0, 0)

    @pl.when(b + 1 < nb)
    def _():
        start_fetch(b + 1, 1 - slot)

    kiwi = kiwi_ref[...]
    new_rows = pl.ds(past_len, R)
    ikbuf[slot, new_rows, :] = kiwi[:, :IDX_DIM]
    kbuf[slot, new_rows, :] = ka_ref[...]
    vbuf[slot, new_rows, :] = va_ref[...]
    wait_fetch(b, slot)

    row = lax.broadcasted_iota(jnp.int32, (R, LANES), 0)
    row_c = lax.broadcasted_iota(jnp.int32, (R, _CKS), 0)
    lane_c = lax.broadcasted_iota(jnp.int32, (R, _CKS), 1)
    rep = _CKS // LANES

    qi = qi_ref[...]
    q_all = jnp.concatenate(
        [qi[:, h * IDX_DIM:(h + 1) * IDX_DIM] * (IDX_DIM ** -0.5) for h in range(IDX_HEADS)], axis=0).astype(BF16)
    w_heads = [jnp.broadcast_to(kiwi[:, IDX_DIM + h:IDX_DIM + h + 1], (R, _CKS)) for h in range(IDX_HEADS)]

    def score_body(kc, carry):
        mn, mx = carry
        k0 = pl.multiple_of(kc * _CKS, LANES)
        keys = ikbuf[slot, pl.ds(k0, _CKS), :].astype(BF16)
        d = lax.dot_general(q_all, keys, (((1,), (1,)), ((), ())), preferred_element_type=F32)
        acc = jnp.zeros((R, _CKS), F32)
        for h in range(IDX_HEADS):
            acc = acc + w_heads[h] * jnp.maximum(d[h * R:(h + 1) * R], 0.0)
        causal = (k0 + lane_c) <= (past_len + row_c)
        sc_ref[kc] = jnp.where(causal, acc, -jnp.inf)
        for j in range(rep):
            a = acc[:, j * LANES:(j + 1) * LANES]
            cz = causal[:, j * LANES:(j + 1) * LANES]
            mn = jnp.minimum(mn, jnp.where(cz, a, jnp.inf))
            mx = jnp.maximum(mx, jnp.where(cz, a, -jnp.inf))
        return mn, mx

    mn, mx = lax.fori_loop(0, nck, score_body,
                           (jnp.full((R, LANES), jnp.inf, F32), jnp.full((R, LANES), -jnp.inf, F32)))
    mn, mx = _rowmin_b(mn), _rowmax_b(mx)
    n_causal = (past_len + row + 1).astype(F32)
    kk = jnp.minimum(n_causal, float(topk))
    hi0 = mx + (mx - mn) + 1.0
    row_active = row < t_real
    lo, hi, c_hi, tied = _threshold_search(sc_ref, nck, kk, mn, hi0, n_causal, row_active)
    need = kk - c_hi
    tiedf = jnp.where(tied, 1.0, 0.0)
    tied_any = jnp.max(tiedf) > 0.0

    qa = qa_ref[...]
    q_rows = []
    for h in range(ATT_HEADS):
        c = h // ATT_GROUP
        z = jnp.zeros((R, ATT_HEAD_DIM), F32)
        parts = [z] * KV_HEADS
        parts[c] = qa[:, h * ATT_HEAD_DIM:(h + 1) * ATT_HEAD_DIM] * (ATT_HEAD_DIM ** -0.5)
        q_rows.append(jnp.concatenate(parts, axis=1))
    q_big = jnp.concatenate(q_rows, axis=0).astype(BF16)
    HR = ATT_HEADS * R

    def att_body(kc, carry):
        m_old, l_old, acc, prefix = carry
        k0 = pl.multiple_of(kc * _CKS, LANES)
        kch = kbuf[slot, pl.ds(k0, _CKS), :].astype(BF16)
        vch = vbuf[slot, pl.ds(k0, _CKS), :].astype(BF16)
        self_, prefix = _select_chunk(sc_ref[kc], lo, hi, need, tied_any, tiedf, prefix)
        selb = jnp.concatenate([self_] * ATT_HEADS, axis=0) > 0.0
        s = lax.dot_general(q_big, kch, (((1,), (1,)), ((), ())), preferred_element_type=F32)
        s = jnp.where(selb, s, NEG_BIG)
        m_new = jnp.maximum(m_old, _rowmax_b(functools.reduce(jnp.maximum, _lane_tiles(s))))
        alpha = jnp.exp(m_old - m_new)
        p = jnp.exp(s - jnp.concatenate([m_new] * rep, axis=1))
        l_new = l_old * alpha + _rowsum_b(sum(_lane_tiles(p)))
        acc = acc * alpha + jnp.dot(p.astype(BF16), vch, preferred_element_type=F32)
        return m_new, l_new, acc, prefix

    init = (jnp.full((HR, LANES), NEG_BIG, F32), jnp.zeros((HR, LANES), F32), jnp.zeros((HR, LANES), F32),
            jnp.zeros((R, LANES), F32))
    _, l_fin, acc, _ = lax.fori_loop(0, nck, att_body, init)
    o = acc / l_fin
    outs = []
    for h in range(ATT_HEADS):
        c = h // ATT_GROUP
        outs.append(o[h * R:(h + 1) * R, c * ATT_HEAD_DIM:(c + 1) * ATT_HEAD_DIM])
    o_ref[...] = jnp.concatenate(outs, axis=1)


def _sample_dsa(page_table, qi, kiwi, qa, ka, va, cache_idx_k_l, cache_k_l, cache_v_l, t_real, row0):
    Bd, n_pages = page_table.shape
    past_len = n_pages * PAGE_SIZE
    L = past_len + PAGE_SIZE
    assert L % _CKS == 0 and row0 % SROWS == 0
    topk = min(TOPK_MAX, (past_len + t_real) // 4)
    blk0 = row0 // SROWS
    blk = lambda w: pl.BlockSpec((SROWS, w), lambda b, pt: (blk0 + b, 0))
    any_spec = pl.BlockSpec(memory_space=pl.ANY)
    grid_spec = pltpu.PrefetchScalarGridSpec(
        num_scalar_prefetch=1,
        grid=(Bd,),
        in_specs=[blk(IDX_WIDTH), blk(_KIWI_WIDTH), blk(ATT_WIDTH), blk(KV_WIDTH), blk(KV_WIDTH),
                  any_spec, any_spec, any_spec],
        out_specs=pl.BlockSpec((SROWS, ATT_WIDTH), lambda b, pt: (b, 0)),
        scratch_shapes=[pltpu.VMEM((2, L, IDX_DIM), F32),
                        pltpu.VMEM((2, L, KV_WIDTH), F32),
                        pltpu.VMEM((2, L, KV_WIDTH), F32),
                        pltpu.SemaphoreType.DMA((2, 3)),
                        pltpu.VMEM((L // _CKS, SROWS, _CKS), F32)],
    )
    return pl.pallas_call(
        functools.partial(_sample_dsa_kernel, n_pages=n_pages, t_real=t_real, topk=topk),
        grid_spec=grid_spec,
        out_shape=jax.ShapeDtypeStruct((Bd * SROWS, ATT_WIDTH), F32),
        name="sample_dsa",
        compiler_params=_cparams(("arbitrary",)),
    )(page_table, qi, kiwi, qa, ka, va, cache_idx_k_l, cache_k_l, cache_v_l)


_TMP = 256
PEER_HALF_EXPERTS = PEER_EXPERTS // 2


def _extract_topk(s, pos, n, k, payload=None):
    vals, idxs = [], []
    for _ in range(k):
        m = jnp.max(s, axis=0, keepdims=True)
        p = jnp.min(jnp.where(s == m, pos, n), axis=0, keepdims=True)
        hit = pos == p
        vals.append(m)
        if payload is None:
            idxs.append(p)
        else:
            idxs.append(jnp.max(jnp.where(hit, payload, -1), axis=0, keepdims=True))
        s = jnp.where(hit, -jnp.inf, s)
    return vals, idxs


def _post_mix_kernel(x_ref, attp_ref, atts_ref, retp_ref, rets_ref, wo_ref, nw_ref, wqt_ref, sk1_ref, sk2_ref,
                     h_ref, xn_ref, idx_ref, gate_ref, qt_ref, et_ref, gt_ref, *, n_prompt_tiles):
    T = x_ref.shape[0]
    is_prompt = pl.program_id(0) < n_prompt_tiles
    att = jnp.where(is_prompt, attp_ref[...], atts_ref[...])
    ret = jnp.where(is_prompt, retp_ref[...], rets_ref[...])
    h = x_ref[...] + _dot(att, wo_ref[:ATT_WIDTH, :]) + _dot(ret, wo_ref[ATT_WIDTH:, :])
    h_ref[...] = h
    ms = jnp.mean(h * h, axis=-1, keepdims=True)
    xn = h * lax.rsqrt(ms + NORM_EPS) * nw_ref[...]
    xn_ref[...] = xn
    qt_ref[...] = _dot_nt(wqt_ref[...], xn)

    pos_k = lax.broadcasted_iota(jnp.int32, (PEER_NKEYS, T), 0)
    n_cand = PEER_TOPK * PEER_TOPK
    pos_c = lax.broadcasted_iota(jnp.int32, (n_cand, T), 0)

    def head_body(hd, _):
        q0 = pl.multiple_of(hd * PEER_KEY_DIM, PEER_KEY_DIM)
        q1 = qt_ref[pl.ds(q0, PEER_HALF), :]
        q2 = qt_ref[pl.ds(q0 + PEER_HALF, PEER_HALF), :]
        s1 = _dot(sk1_ref[...], q1)
        s2 = _dot(sk2_ref[...], q2)
        v1, i1 = _extract_topk(s1, pos_k, PEER_NKEYS, PEER_TOPK)
        v2, i2 = _extract_topk(s2, pos_k, PEER_NKEYS, PEER_TOPK)
        v2m = jnp.concatenate(v2, axis=0)
        i2m = jnp.concatenate(i2, axis=0)
        cand = jnp.concatenate([v1[a] + v2m for a in range(PEER_TOPK)], axis=0)
        cid = jnp.concatenate([i1[a] * PEER_NKEYS + i2m for a in range(PEER_TOPK)], axis=0)
        sv, eid = _extract_topk(cand, pos_c, n_cand, PEER_TOPK, payload=cid)
        svm = jnp.concatenate(sv, axis=0)
        g = jnp.exp(svm - sv[0])
        r0 = pl.multiple_of(hd * PEER_TOPK, PEER_TOPK)
        gt_ref[pl.ds(r0, PEER_TOPK), :] = g / jnp.sum(g, axis=0, keepdims=True)
        et_ref[pl.ds(r0, PEER_TOPK), :] = jnp.concatenate(eid, axis=0).astype(F32)
        return 0

    lax.fori_loop(0, PEER_HEADS, head_body, 0)
    gate_ref[...] = gt_ref[...].T
    e = et_ref[...].T.astype(jnp.int32)
    idx_ref[0] = jnp.where(e < PEER_HALF_EXPERTS, e, PEER_HALF_EXPERTS) * SUBLANES
    idx_ref[1] = jnp.where(e >= PEER_HALF_EXPERTS, e - PEER_HALF_EXPERTS, PEER_HALF_EXPERTS) * SUBLANES


def _post_mix(x, att_p, att_s, ret_p, ret_s, w_out_bf, norm_w, wq_t_bf, sk1_bf, sk2_bf):
    N = x.shape[0]
    nt = N // _TMP
    ntp = att_p.shape[0] // _TMP
    nts = att_s.shape[0] // _TMP
    assert ntp + nts == nt and ntp > 0 and nts > 0
    npair = PEER_HEADS * PEER_TOPK
    row = lambda w: pl.BlockSpec((_TMP, w), lambda i: (i, 0))
    prow = lambda w: pl.BlockSpec((_TMP, w), lambda i: (jnp.minimum(i, ntp - 1), 0))
    srow = lambda w: pl.BlockSpec((_TMP, w), lambda i: (jnp.maximum(i - ntp, 0), 0))
    full = lambda a: pl.BlockSpec(a.shape, lambda i: (0,) * a.ndim)
    nw = norm_w.reshape(1, D_MODEL)
    return pl.pallas_call(
        functools.partial(_post_mix_kernel, n_prompt_tiles=ntp),
        grid=(nt,),
        in_specs=[row(D_MODEL), prow(ATT_WIDTH), srow(ATT_WIDTH), prow(RET_WIDTH), srow(RET_WIDTH),
                  full(w_out_bf), full(nw), full(wq_t_bf), full(sk1_bf), full(sk2_bf)],
        out_specs=[row(D_MODEL), row(D_MODEL),
                   pl.BlockSpec((2, _TMP, npair), lambda i: (0, i, 0)), row(npair)],
        out_shape=[jax.ShapeDtypeStruct((N, D_MODEL), F32), jax.ShapeDtypeStruct((N, D_MODEL), F32),
                   jax.ShapeDtypeStruct((2, N, npair), jnp.int32), jax.ShapeDtypeStruct((N, npair), F32)],
        scratch_shapes=[pltpu.VMEM((PEER_HEADS * PEER_KEY_DIM, _TMP), F32),
                        pltpu.VMEM((npair, _TMP), F32), pltpu.VMEM((npair, _TMP), F32)],
        name="post_mix",
        compiler_params=_cparams(("arbitrary",)),
    )(x, att_p, att_s, ret_p, ret_s, w_out_bf, nw, wq_t_bf, sk1_bf, sk2_bf)


_TBP = 128
_NPAIR = PEER_HEADS * PEER_TOPK
_ROW_TILE = D_MODEL // LANES
assert _ROW_TILE == SUBLANES


def _load_table_half(tbl_hbm, tbuf, sem, half):
    rows = PEER_HALF_EXPERTS * _ROW_TILE
    tbuf[pl.ds(rows, _ROW_TILE), :] = jnp.zeros((_ROW_TILE, LANES), F32)
    cp = pltpu.make_async_copy(tbl_hbm.at[pl.ds(half * rows, rows), :], tbuf.at[pl.ds(0, rows), :], sem)
    cp.start()
    cp.wait()


def _fold_pair(a, b, k, sub):
    m = (sub & k) == 0
    return jnp.where(m, a, pltpu.roll(b, k, 0)) + jnp.where(m, pltpu.roll(a, SUBLANES - k, 0), b)


def _fold8(p, sub):
    a, b, c, d, e, f, g, h = p[0], p[4], p[2], p[6], p[1], p[5], p[3], p[7]
    t1, t2, t3, t4 = (_fold_pair(a, b, 4, sub), _fold_pair(c, d, 4, sub),
                      _fold_pair(e, f, 4, sub), _fold_pair(g, h, 4, sub))
    u1, u2 = _fold_pair(t1, t2, 2, sub), _fold_pair(t3, t4, 2, sub)
    return _fold_pair(u1, u2, 1, sub)


def _flat_smem_spec(nt, per_half):
    if per_half:
        index = lambda hf, i: (hf * nt + i, 0, 0)
    else:
        index = lambda hf, i: (i, 0, 0)
    return pl.BlockSpec((1, 1, _TBP * _NPAIR), index, memory_space=pltpu.SMEM)


def _table_row(tbuf, idx_ref, k):
    off = pl.multiple_of(idx_ref[0, 0, k], _ROW_TILE)
    return tbuf[pl.ds(off, _ROW_TILE), :]


def _peer_u_kernel(idx_ref, x_ref, u_hbm, o_ref, tbuf, sem, z_ref):
    half = pl.program_id(0)

    @pl.when(pl.program_id(1) == 0)
    def _():
        _load_table_half(u_hbm, tbuf, sem, half)

    sub = lax.broadcasted_iota(jnp.int32, (SUBLANES, LANES), 0)

    def tok_body(t, _):
        xt = x_ref[t]

        def grp_body(g, _):
            kg = pl.multiple_of(t * _NPAIR + g * SUBLANES, SUBLANES)
            tiles = [_table_row(tbuf, idx_ref, kg + s) * xt for s in range(SUBLANES)]
            z_ref[pl.ds(kg, SUBLANES), :] = _fold8(tiles, sub)
            return 0

        lax.fori_loop(0, _NPAIR // SUBLANES, grp_body, 0, unroll=2)
        return 0

    lax.fori_loop(0, _TBP, tok_body, 0)

    ones = jnp.ones((LANES, LANES), BF16)
    eye = (lax.broadcasted_iota(jnp.int32, (_NPAIR, LANES), 0)
           == lax.broadcasted_iota(jnp.int32, (_NPAIR, LANES), 1))

    def chunk_body(c, _):
        r0 = pl.multiple_of(c * SUBLANES * _NPAIR, SUBLANES * _NPAIR)
        z = z_ref[pl.ds(r0, SUBLANES * _NPAIR), :]
        zh = z.astype(BF16)
        zl = (z - zh.astype(F32)).astype(BF16)
        hv = jnp.dot(zh, ones, preferred_element_type=F32) + jnp.dot(zl, ones, preferred_element_type=F32)
        rows = [jnp.sum(jnp.where(eye, hv[tt * _NPAIR:(tt + 1) * _NPAIR], 0.0), axis=0, keepdims=True)
                for tt in range(SUBLANES)]
        o_ref[0, pl.ds(pl.multiple_of(c * SUBLANES, SUBLANES), SUBLANES), :] = jnp.concatenate(rows, axis=0)
        return 0

    lax.fori_loop(0, _TBP // SUBLANES, chunk_body, 0)


def _peer_u(idx, xn, u_tbl):
    N = xn.shape[0]
    nt = N // _TBP
    return pl.pallas_call(
        _peer_u_kernel,
        grid=(2, nt),
        in_specs=[_flat_smem_spec(nt, True),
                  pl.BlockSpec((_TBP, _ROW_TILE, LANES), lambda hf, i: (i, 0, 0)),
                  pl.BlockSpec(memory_space=pl.ANY)],
        out_specs=pl.BlockSpec((1, _TBP, _NPAIR), lambda hf, i: (hf, i, 0)),
        out_shape=jax.ShapeDtypeStruct((2, N, _NPAIR), F32),
        scratch_shapes=[pltpu.VMEM(((PEER_HALF_EXPERTS + 1) * _ROW_TILE, LANES), F32),
                        pltpu.SemaphoreType.DMA(()),
                        pltpu.VMEM((_TBP * _NPAIR, LANES), F32)],
        name="peer_u",
        compiler_params=_cparams(("arbitrary", "arbitrary")),
    )(idx.reshape(2 * nt, 1, _TBP * _NPAIR), xn.reshape(N, _ROW_TILE, LANES),
      u_tbl.reshape(PEER_EXPERTS * _ROW_TILE, LANES))


def _peer_coef_kernel(hp_ref, gate_ref, o_ref):
    hv = hp_ref[0] + hp_ref[1]
    o_ref[...] = gate_ref[...] * (0.5 * hv * (1.0 + lax.erf(hv * (0.5 ** 0.5))))


def _peer_coef(hpart, gate):
    N = gate.shape[0]
    tm = 1024 if N % 1024 == 0 else _TBP
    return pl.pallas_call(
        _peer_coef_kernel,
        grid=(N // tm,),
        in_specs=[pl.BlockSpec((2, tm, _NPAIR), lambda i: (0, i, 0)), pl.BlockSpec((tm, _NPAIR), lambda i: (i, 0))],
        out_specs=pl.BlockSpec((tm, _NPAIR), lambda i: (i, 0)),
        out_shape=jax.ShapeDtypeStruct((N, _NPAIR), F32),
        name="peer_coef",
        compiler_params=_cparams(("arbitrary",)),
    )(hpart, gate)


def _peer_v_kernel(idx_ref, coef_ref, v_hbm, o_ref, tbuf, sem):
    half = pl.program_id(0)

    @pl.when(pl.program_id(1) == 0)
    def _():
        _load_table_half(v_hbm, tbuf, sem, half)

    n_acc = 4

    def tok_body(t, _):
        def grp_body(g, accs):
            kg = pl.multiple_of(t * _NPAIR + g * SUBLANES, SUBLANES)
            accs = list(accs)
            for s in range(SUBLANES):
                accs[s % n_acc] = accs[s % n_acc] + coef_ref[0, 0, kg + s] * _table_row(tbuf, idx_ref, kg + s)
            return tuple(accs)

        zero = jnp.zeros((_ROW_TILE, LANES), F32)
        accs = lax.fori_loop(0, _NPAIR // SUBLANES, grp_body, (zero,) * n_acc, unroll=2)
        o_ref[0, t] = (accs[0] + accs[1]) + (accs[2] + accs[3])
        return 0

    lax.fori_loop(0, _TBP, tok_body, 0)


def _peer_v(idx, coef, v_tbl):
    N = coef.shape[0]
    nt = N // _TBP
    out = pl.pallas_call(
        _peer_v_kernel,
        grid=(2, nt),
        in_specs=[_flat_smem_spec(nt, True), _flat_smem_spec(nt, False),
                  pl.BlockSpec(memory_space=pl.ANY)],
        out_specs=pl.BlockSpec((1, _TBP, _ROW_TILE, LANES), lambda hf, i: (hf, i, 0, 0)),
        out_shape=jax.ShapeDtypeStruct((2, N, _ROW_TILE, LANES), F32),
        scratch_shapes=[pltpu.VMEM(((PEER_HALF_EXPERTS + 1) * _ROW_TILE, LANES), F32),
                        pltpu.SemaphoreType.DMA(())],
        name="peer_v",
        compiler_params=_cparams(("arbitrary", "arbitrary")),
    )(idx.reshape(2 * nt, 1, _TBP * _NPAIR), coef.reshape(nt, 1, _TBP * _NPAIR),
      v_tbl.reshape(PEER_EXPERTS * _ROW_TILE, LANES))
    return out.reshape(2, N, D_MODEL)


def _final_kernel(h_ref, y_ref, w_ref, o_ref, *, normalize):
    h = h_ref[...] + (y_ref[0] + y_ref[1])
    if normalize:
        ms = jnp.mean(h * h, axis=-1, keepdims=True)
        h = h * lax.rsqrt(ms + NORM_EPS) * w_ref[...]
    o_ref[...] = h


def _final(h, ypart, w):
    N = h.shape[0]
    tm = 512 if N % 512 == 0 else _TBP
    normalize = w is not None
    if not normalize:
        w = jnp.ones((D_MODEL,), F32)
    return pl.pallas_call(
        functools.partial(_final_kernel, normalize=normalize),
        grid=(N // tm,),
        in_specs=[pl.BlockSpec((tm, D_MODEL), lambda i: (i, 0)),
                  pl.BlockSpec((2, tm, D_MODEL), lambda i: (0, i, 0)),
                  pl.BlockSpec((1, D_MODEL), lambda i: (0, 0))],
        out_specs=pl.BlockSpec((tm, D_MODEL), lambda i: (i, 0)),
        out_shape=jax.ShapeDtypeStruct((N, D_MODEL), F32),
        name="final_norm",
        compiler_params=_cparams(("arbitrary",)),
    )(h, ypart, w.reshape(1, D_MODEL))


_TM_IN = 256


def kernel(x_prompt, x_sample, cache_k, cache_v, cache_idx_k, state_ret, page_table, norm_attn_w, w_in, ret_gn_w,
           w_out, norm_ffn_w, peer_w_q, peer_sub_keys_1, peer_sub_keys_2, peer_u, peer_v, final_norm_w):
    B, S, D = x_prompt.shape
    Bd, T, _ = x_sample.shape
    depth = w_in.shape[0]
    n_pages = page_table.shape[1]
    past_len = n_pages * PAGE_SIZE
    n_phys = cache_k.shape[1]
    Np, Ns = B * S, Bd * SROWS
    assert D == D_MODEL and T <= SROWS and S % _TM_IN == 0 and Ns % _TM_IN == 0 and S % _CK == 0

    hs_pad = jnp.pad(x_sample, ((0, 0), (0, SROWS - T), (0, 0)))
    h_all = jnp.concatenate([x_prompt.reshape(Np, D), hs_pad.reshape(Ns, D)], axis=0)

    pos_s = past_len + jnp.arange(SROWS)
    pos = jnp.concatenate([jnp.arange(S), jnp.tile(pos_s, _TM_IN // SROWS)])
    tab_att, tab_ret = _rope_tables(pos)
    tiles_per_seq = S // _TM_IN
    tab_blocks = [i % tiles_per_seq for i in range(Np // _TM_IN)] + [tiles_per_seq] * (Ns // _TM_IN)
    ret_consts_p = _ret_constants(RET_CHUNK)
    ret_consts_s = _ret_constants(T)

    outs = {n: [] for n in ("kp", "vp", "ikp", "sp", "ks", "vs", "iks", "ss")}
    for l in range(depth):
        m = _in_proj(h_all, norm_attn_w[l], _pad_w_in(w_in[l]), tab_att, tab_ret, tab_blocks, _TM_IN)
        att_p = _prompt_dsa(m["qi"], m["kiwi"], m["qa"], m["ka"], m["va"], B, S)
        att_s = _sample_dsa(page_table, m["qi"], m["kiwi"], m["qa"], m["ka"], m["va"],
                            cache_idx_k[l], cache_k[l].reshape(n_phys, PAGE_SIZE, KV_WIDTH),
                            cache_v[l].reshape(n_phys, PAGE_SIZE, KV_WIDTH), T, Np)
        ret_p, s_p = _retention(m["qr"], m["kr"], m["vr"], m["gr"], ret_gn_w[l],
                                jnp.zeros((B, RET_HEADS, RET_HEAD_DIM, RET_HEAD_DIM), F32),
                                ret_consts_p, B, S // RET_CHUNK, RET_CHUNK)
        ret_s, s_s = _retention(m["qr"], m["kr"], m["vr"], m["gr"], ret_gn_w[l], state_ret[l].astype(F32),
                                ret_consts_s, Bd, 1, SROWS, row0=Np)
        h_mid, xn, idx, gate = _post_mix(h_all, att_p, att_s, ret_p, ret_s, w_out[l].astype(BF16), norm_ffn_w[l],
                                         peer_w_q[l].T.astype(BF16), peer_sub_keys_1[l].astype(BF16),
                                         peer_sub_keys_2[l].astype(BF16))
        coef = _peer_coef(_peer_u(idx, xn, peer_u[l]), gate)
        ypart = _peer_v(idx, coef, peer_v[l])
        last = l == depth - 1
        h_all = _final(h_mid, ypart, final_norm_w if last else None)

        sample = lambda a, w: a[Np:].reshape(Bd, SROWS, w)[:, :T]
        outs["kp"].append(m["ka"][:Np].reshape(B, S, KV_HEADS, ATT_HEAD_DIM))
        outs["vp"].append(m["va"][:Np].reshape(B, S, KV_HEADS, ATT_HEAD_DIM))
        outs["ikp"].append(m["kiwi"][:Np, :IDX_DIM].reshape(B, S, IDX_DIM))
        outs["sp"].append(s_p.astype(x_prompt.dtype))
        outs["ks"].append(sample(m["ka"], KV_WIDTH).reshape(Bd, T, KV_HEADS, ATT_HEAD_DIM))
        outs["vs"].append(sample(m["va"], KV_WIDTH).reshape(Bd, T, KV_HEADS, ATT_HEAD_DIM))
        outs["iks"].append(sample(m["kiwi"], _KIWI_WIDTH)[:, :, :IDX_DIM])
        outs["ss"].append(s_s.astype(state_ret.dtype))

    y_prompt = h_all[:Np].reshape(B, S, D)
    y_sample = h_all[Np:].reshape(Bd, SROWS, D)[:, :T]
    st = lambda n: jnp.stack(outs[n])
    return (y_prompt, y_sample, st("kp"), st("vp"), st("ikp"), st("sp"),
            st("ks"), st("vs"), st("iks"), st("ss"))
```

```python
import functools

import jax
import jax.numpy as jnp
import numpy as np
from jax import lax
from jax.experimental import pallas as pl
from jax.experimental.pallas import tpu as pltpu

F32 = jnp.float32
BF16 = jnp.bfloat16

D_MODEL = 1024
PAGE_SIZE = 128
ATT_HEADS = 8
ATT_HEAD_DIM = 64
KV_HEADS = 2
ATT_GROUP = ATT_HEADS // KV_HEADS
ATT_WIDTH = ATT_HEADS * ATT_HEAD_DIM
KV_WIDTH = KV_HEADS * ATT_HEAD_DIM
ROPE_THETA = 500000.0
ATT_ROPE_DIMS = ATT_HEAD_DIM // 4
IDX_HEADS = 8
IDX_DIM = 64
IDX_WIDTH = IDX_HEADS * IDX_DIM
TOPK_MAX = 256
RET_HEADS = 4
RET_HEAD_DIM = 128
RET_WIDTH = RET_HEADS * RET_HEAD_DIM
RET_ROPE_THETA = 10000.0
RET_CHUNK = 128
PEER_HEADS = 8
PEER_NKEYS = 128
PEER_EXPERTS = PEER_NKEYS * PEER_NKEYS
PEER_KEY_DIM = 128
PEER_HALF = PEER_KEY_DIM // 2
PEER_TOPK = 16
NORM_EPS = 1e-6
GN_EPS = 1e-6

LANES = 128
SUBLANES = 8
VMEM_LIMIT_BYTES = 56 * 1024 * 1024

NEG_BIG = -1e30

_IN_SPLITS = (ATT_WIDTH, KV_WIDTH, KV_WIDTH, IDX_WIDTH, IDX_DIM, IDX_HEADS,
              RET_WIDTH, RET_WIDTH, RET_WIDTH, RET_WIDTH)
_KIWI_WIDTH = LANES
_PAD_OFFS = {}
_off = 0
for _name, _w in (("qa", ATT_WIDTH), ("ka", KV_WIDTH), ("va", KV_WIDTH), ("qi", IDX_WIDTH),
                  ("kiwi", _KIWI_WIDTH), ("qr", RET_WIDTH), ("kr", RET_WIDTH),
                  ("vr", RET_WIDTH), ("gr", RET_WIDTH)):
    _PAD_OFFS[_name] = (_off, _w)
    _off += _w
IN_WIDTH_PADDED = _off


def _cparams(sem):
    return pltpu.CompilerParams(dimension_semantics=sem, vmem_limit_bytes=VMEM_LIMIT_BYTES)


def _dot(a, b):
    return jnp.dot(a.astype(BF16), b.astype(BF16), preferred_element_type=F32)


def _dot_nt(a, b):
    return lax.dot_general(a.astype(BF16), b.astype(BF16), (((1,), (1,)), ((), ())),
                           preferred_element_type=F32)


def _pad_w_in(w_in_l):
    cols = []
    off = 0
    parts = []
    for n in _IN_SPLITS:
        parts.append(w_in_l[:, off:off + n])
        off += n
    qa, ka, va, qi, ki, wi, qr, kr, vr, gr = parts
    kiwi = jnp.concatenate(
        [ki, wi, jnp.zeros((w_in_l.shape[0], _KIWI_WIDTH - IDX_DIM - IDX_HEADS), w_in_l.dtype)], axis=1)
    cols = [qa, ka, va, qi, kiwi, qr, kr, vr, gr]
    return jnp.concatenate(cols, axis=1).astype(BF16)


def _rope_tables(pos):
    posf = pos.astype(F32)
    half = ATT_ROPE_DIMS // 2
    inv = 1.0 / (ROPE_THETA ** (jnp.arange(half, dtype=F32) / half))
    ang = posf[:, None] * inv[None, :]
    cos, sin = jnp.cos(ang), jnp.sin(ang)
    P = pos.shape[0]
    one = jnp.ones((P, ATT_HEAD_DIM - ATT_ROPE_DIMS), F32)
    zero = jnp.zeros((P, ATT_HEAD_DIM - ATT_ROPE_DIMS), F32)
    zh = jnp.zeros((P, half), F32)
    c_head = jnp.concatenate([cos, cos, one], axis=1)
    s1_head = jnp.concatenate([-sin, zh, zero], axis=1)
    s2_head = jnp.concatenate([zh, sin, zero], axis=1)
    rep = LANES // ATT_HEAD_DIM
    att = jnp.stack([jnp.tile(c_head, (1, rep)), jnp.tile(s1_head, (1, rep)), jnp.tile(s2_head, (1, rep))])
    halfr = RET_HEAD_DIM // 2
    invr = 1.0 / (RET_ROPE_THETA ** (jnp.arange(halfr, dtype=F32) / halfr))
    angr = posf[:, None] * invr[None, :]
    cr, sr = jnp.cos(angr), jnp.sin(angr)
    ret = jnp.stack([jnp.concatenate([cr, cr], axis=1), jnp.concatenate([-sr, sr], axis=1)])
    return att, ret


def _in_proj_kernel(x_ref, nw_ref, w_ref, ta_ref, tr_ref,
                    qa_ref, ka_ref, va_ref, qi_ref, kiwi_ref, qr_ref, kr_ref, vr_ref, gr_ref):
    x = x_ref[...]
    ms = jnp.mean(x * x, axis=-1, keepdims=True)
    xn = (x * lax.rsqrt(ms + NORM_EPS) * nw_ref[...]).astype(BF16)
    ca, s1a, s2a = ta_ref[0], ta_ref[1], ta_ref[2]
    cr, sr = tr_ref[0], tr_ref[1]

    def proj(name):
        off, w = _PAD_OFFS[name]
        return jnp.dot(xn, w_ref[:, off:off + w], preferred_element_type=F32)

    def rope_att_chunk(vc):
        return (vc * ca + pltpu.roll(vc, LANES - ATT_ROPE_DIMS // 2, 1) * s1a
                + pltpu.roll(vc, ATT_ROPE_DIMS // 2, 1) * s2a)

    def rope_ret_chunk(vc):
        return vc * cr + pltpu.roll(vc, RET_HEAD_DIM // 2, 1) * sr

    def per_chunk(v, fn):
        n = v.shape[1] // LANES
        return jnp.concatenate([fn(v[:, c * LANES:(c + 1) * LANES]) for c in range(n)], axis=1)

    qa_ref[...] = per_chunk(proj("qa"), rope_att_chunk)
    ka_ref[...] = per_chunk(proj("ka"), rope_att_chunk)
    va_ref[...] = proj("va")
    qi_ref[...] = per_chunk(proj("qi"), rope_att_chunk)
    kiwi = proj("kiwi")
    lane = lax.broadcasted_iota(jnp.int32, kiwi.shape, 1)
    kiwi_ref[...] = jnp.where(lane < IDX_DIM, rope_att_chunk(kiwi), kiwi * (IDX_HEADS ** -0.5))
    qr_ref[...] = per_chunk(proj("qr"), rope_ret_chunk)
    kr_ref[...] = per_chunk(proj("kr"), rope_ret_chunk) * (RET_HEAD_DIM ** -0.5)
    vr_ref[...] = proj("vr")
    gr_ref[...] = proj("gr")


def _in_proj(x, norm_w, w_pad, tab_att, tab_ret, tab_block_of_tile, tm):
    N = x.shape[0]
    nt = N // tm
    tab_idx = jnp.asarray(tab_block_of_tile, jnp.int32)
    names = ("qa", "ka", "va", "qi", "kiwi", "qr", "kr", "vr", "gr")
    out_shape = [jax.ShapeDtypeStruct((N, _PAD_OFFS[n][1]), F32) for n in names]
    out_specs = [pl.BlockSpec((tm, _PAD_OFFS[n][1]), lambda i, t: (i, 0)) for n in names]
    grid_spec = pltpu.PrefetchScalarGridSpec(
        num_scalar_prefetch=1,
        grid=(nt,),
        in_specs=[
            pl.BlockSpec((tm, D_MODEL), lambda i, t: (i, 0)),
            pl.BlockSpec((1, D_MODEL), lambda i, t: (0, 0)),
            pl.BlockSpec((D_MODEL, IN_WIDTH_PADDED), lambda i, t: (0, 0)),
            pl.BlockSpec((3, tm, LANES), lambda i, t: (0, t[i], 0)),
            pl.BlockSpec((2, tm, LANES), lambda i, t: (0, t[i], 0)),
        ],
        out_specs=out_specs,
    )

    def body(t_ref, *refs):
        _in_proj_kernel(*refs)

    outs = pl.pallas_call(
        body, grid_spec=grid_spec, out_shape=out_shape, name="in_proj",
        compiler_params=_cparams(("arbitrary",)),
    )(tab_idx, x, norm_w.reshape(1, D_MODEL), w_pad, tab_att, tab_ret)
    return dict(zip(names, outs))


def _ret_constants(c_eff):
    C = RET_CHUNK
    lg = jnp.log1p(-(2.0 ** (-5.0 - jnp.arange(RET_HEADS, dtype=F32))))
    i = jnp.arange(C, dtype=F32)
    diff = i[:, None] - i[None, :]
    dmask = jnp.where(diff >= 0, jnp.exp(lg[:, None, None] * jnp.maximum(diff, 0.0)), 0.0)
    real = (i < c_eff)
    dmask = jnp.where(real[None, :, None] & real[None, None, :], dmask, 0.0)
    q_dec = jnp.exp(lg[:, None] * (i[None, :] + 1.0))
    k_dec = jnp.where(real[None, :], jnp.exp(lg[:, None] * (c_eff - 1.0 - i[None, :])), 0.0)
    chunk_dec = jnp.exp(lg * c_eff)
    bc = lambda a: jnp.broadcast_to(a[:, :, None], (RET_HEADS, C, C))
    cdec = jnp.broadcast_to(chunk_dec[:, None, None], (RET_HEADS, C, C))
    return dmask, bc(q_dec), bc(k_dec), cdec


def _retention_kernel(q_ref, k_ref, v_ref, g_ref, gnw_ref, s0_ref, dm_ref, qd_ref, kd_ref, cd_ref,
                      o_ref, s_out_ref, state_ref, *, rows):
    c = pl.program_id(1)
    nc = pl.num_programs(1)

    @pl.when(c == 0)
    def _():
        state_ref[...] = s0_ref[0]

    def padded(ref):
        v = ref[...]
        if rows < RET_CHUNK:
            v = jnp.concatenate([v, jnp.zeros((RET_CHUNK - rows, v.shape[1]), v.dtype)], axis=0)
        return v

    q, k, v, g = padded(q_ref), padded(k_ref), padded(v_ref), g_ref[...]
    gnw = gnw_ref[...]
    outs = []
    for h in range(RET_HEADS):
        sl = slice(h * RET_HEAD_DIM, (h + 1) * RET_HEAD_DIM)
        qh, kh, vh = q[:, sl], k[:, sl], v[:, sl]
        st = state_ref[h]
        att = _dot_nt(qh, kh) * dm_ref[h]
        o = _dot(att, vh) + _dot(qh, st) * qd_ref[h]
        state_ref[h] = st * cd_ref[h] + _dot((kh * kd_ref[h]).T, vh)
        o = o[:rows]
        mu = jnp.mean(o, axis=-1, keepdims=True)
        var = jnp.mean(jnp.square(o - mu), axis=-1, keepdims=True)
        outs.append((o - mu) * lax.rsqrt(var + GN_EPS) * gnw[:, sl])
    on = jnp.concatenate(outs, axis=1)
    o_ref[...] = g * (1.0 / (1.0 + jnp.exp(-g))) * on

    @pl.when(c == nc - 1)
    def _():
        s_out_ref[0] = state_ref[...]


def _retention(q, k, v, g, gn_w, s0, consts, nb, nc, rows, row0=0):
    blk0 = row0 // rows
    in_row = pl.BlockSpec((rows, RET_WIDTH), lambda b, c: (blk0 + b * nc + c, 0))
    out_row = pl.BlockSpec((rows, RET_WIDTH), lambda b, c: (b * nc + c, 0))
    st_spec = pl.BlockSpec((1, RET_HEADS, RET_HEAD_DIM, RET_HEAD_DIM), lambda b, c: (b, 0, 0, 0))
    const_spec = pl.BlockSpec((RET_HEADS, RET_CHUNK, RET_CHUNK), lambda b, c: (0, 0, 0))
    return pl.pallas_call(
        functools.partial(_retention_kernel, rows=rows),
        grid=(nb, nc),
        in_specs=[in_row, in_row, in_row, in_row,
                  pl.BlockSpec((1, RET_WIDTH), lambda b, c: (0, 0)),
                  st_spec, const_spec, const_spec, const_spec, const_spec],
        out_specs=[out_row, st_spec],
        out_shape=[jax.ShapeDtypeStruct((nb * nc * rows, RET_WIDTH), F32),
                   jax.ShapeDtypeStruct(s0.shape, F32)],
        scratch_shapes=[pltpu.VMEM((RET_HEADS, RET_HEAD_DIM, RET_HEAD_DIM), F32)],
        name="retention",
        compiler_params=_cparams(("arbitrary", "arbitrary")),
    )(q, k, v, g, gn_w.reshape(1, RET_WIDTH), s0, *consts)


_BISECT_PLAIN_ITERS = 26
_BISECT_MAX_ITERS = 400


def _lane_tiles(x):
    return [x[:, j * LANES:(j + 1) * LANES] for j in range(x.shape[1] // LANES)]


def _rowsum_b(x):
    return jnp.broadcast_to(jnp.sum(x, axis=1, keepdims=True), x.shape)


def _rowmax_b(x):
    return jnp.broadcast_to(jnp.max(x, axis=1, keepdims=True), x.shape)


def _rowmin_b(x):
    return jnp.broadcast_to(jnp.min(x, axis=1, keepdims=True), x.shape)


def _threshold_search(sc_ref, nck, kk, lo0, hi0, n_causal, row_active):
    R = lo0.shape[0]
    zeros = jnp.zeros((R, LANES), F32)

    def count_pass(mid, snap):
        def body(kc, carry):
            x = sc_ref[kc]
            cnt, amin, bmax = carry
            for xt in _lane_tiles(x):
                ge = xt >= mid
                cnt = cnt + jnp.where(ge, 1.0, 0.0)
                if snap:
                    amin = jnp.minimum(amin, jnp.where(ge, xt, jnp.inf))
                    bmax = jnp.maximum(bmax, jnp.where(ge, -jnp.inf, xt))
            return cnt, amin, bmax
        cnt, amin, bmax = lax.fori_loop(
            0, nck, body, (zeros, jnp.full((R, LANES), jnp.inf, F32), jnp.full((R, LANES), -jnp.inf, F32)))
        if snap:
            return _rowsum_b(cnt), _rowmin_b(amin), _rowmax_b(bmax)
        return _rowsum_b(cnt), None, None

    def not_done(c_lo, lo, hib):
        pending = row_active & (c_lo != kk) & (lo != hib)
        return jnp.max(jnp.where(pending, 1.0, 0.0)) > 0.0

    def make_step(snap):
        def step(carry):
            it, lo, hi, hib, c_lo, c_hi = carry
            mid = 0.5 * (lo + hi)
            if snap:
                mid = 0.5 * (lo + jnp.minimum(hi, hib))
                mid = jnp.where(mid > lo, mid, jnp.minimum(hi, hib))
            c, amin, bmax = count_pass(mid, snap)
            ge = c >= kk
            if snap:
                lo = jnp.where(ge, amin, lo)
                hib = jnp.where(ge, hib, bmax)
            else:
                lo = jnp.where(ge, mid, lo)
            hi = jnp.where(ge, hi, mid)
            c_lo = jnp.where(ge, c, c_lo)
            c_hi = jnp.where(ge, c_hi, c)
            return it + 1, lo, hi, hib, c_lo, c_hi
        return step

    init = (jnp.int32(0), lo0, hi0, jnp.full((R, LANES), jnp.inf, F32), n_causal, zeros)
    carry = lax.while_loop(
        lambda c: (c[0] < _BISECT_PLAIN_ITERS) & not_done(c[4], c[1], c[3]), make_step(False), init)
    carry = lax.while_loop(
        lambda c: (c[0] < _BISECT_MAX_ITERS) & not_done(c[4], c[1], c[3]), make_step(True), carry)
    _, lo, hi, hib, c_lo, c_hi = carry
    tied = row_active & (c_lo != kk)
    return lo, hi, c_hi, tied


def _select_chunk(x, lo, hi, need, tied_any, tied, prefix0):
    R, CK = x.shape
    rep = CK // LANES
    wide = lambda a: jnp.concatenate([a] * rep, axis=1)

    def plain(_):
        return jnp.where(x >= wide(lo), 1.0, 0.0), prefix0

    def with_ties(_):
        low, hiw = wide(lo), wide(hi)
        band = (x >= low) & (x < hiw)
        bandf = jnp.where(band, 1.0, 0.0)
        r_i = lax.broadcasted_iota(jnp.int32, (CK, CK), 0)
        c_i = lax.broadcasted_iota(jnp.int32, (CK, CK), 1)
        tri = jnp.where(r_i <= c_i, 1.0, 0.0).astype(BF16)
        rank = jnp.dot(bandf.astype(BF16), tri, preferred_element_type=F32) + wide(prefix0)
        take_tie = jnp.where(rank <= wide(need), bandf, 0.0)
        sel_tied = jnp.where(x >= hiw, 1.0, take_tie)
        sel = jnp.where(wide(tied) > 0.0, sel_tied, jnp.where(x >= low, 1.0, 0.0))
        return sel, prefix0 + _rowsum_b(sum(_lane_tiles(bandf)))

    return lax.cond(tied_any, with_ties, plain, 0)


_QB = 128
_CK = 512


def _prompt_dsa_kernel(qi_ref, kiwiq_ref, kiwi_ref, qa_ref, ka_ref, va_ref, o_ref,
                       sc_ref, m_ref, l_ref, acc_ref, *, topk):
    i = pl.program_id(1)
    t0 = i * _QB
    nck = (t0 + _QB + _CK - 1) // _CK
    R = _QB
    qpos = t0 + lax.broadcasted_iota(jnp.int32, (R, LANES), 0)
    lane_c = lax.broadcasted_iota(jnp.int32, (R, _CK), 1)
    qpos_c = t0 + lax.broadcasted_iota(jnp.int32, (R, _CK), 0)
    lane = lax.broadcasted_iota(jnp.int32, (R, LANES), 1)

    qi = qi_ref[...]
    wq = kiwiq_ref[...]
    q_heads = []
    w_heads = []
    for h in range(IDX_HEADS):
        qh = qi[:, h * IDX_DIM:(h + 1) * IDX_DIM] * (IDX_DIM ** -0.5)
        q_heads.append(jnp.concatenate([qh, jnp.zeros((R, LANES - IDX_DIM), F32)], axis=1).astype(BF16))
        w_heads.append(jnp.broadcast_to(wq[:, IDX_DIM + h:IDX_DIM + h + 1], (R, LANES)))

    def score_body(kc, carry):
        mn, mx = carry
        k0 = pl.multiple_of(kc * _CK, _CK)
        kc_rows = kiwi_ref[pl.ds(k0, _CK), :].astype(BF16)
        acc = jnp.zeros((R, _CK), F32)
        for h in range(IDX_HEADS):
            d = lax.dot_general(q_heads[h], kc_rows, (((1,), (1,)), ((), ())), preferred_element_type=F32)
            acc = acc + jnp.concatenate([w_heads[h]] * (_CK // LANES), axis=1) * jnp.maximum(d, 0.0)
        causal = (k0 + lane_c) <= qpos_c
        sc_ref[kc] = jnp.where(causal, acc, -jnp.inf)
        for j in range(_CK // LANES):
            a = acc[:, j * LANES:(j + 1) * LANES]
            cz = causal[:, j * LANES:(j + 1) * LANES]
            mn = jnp.minimum(mn, jnp.where(cz, a, jnp.inf))
            mx = jnp.maximum(mx, jnp.where(cz, a, -jnp.inf))
        return mn, mx

    mn, mx = lax.fori_loop(0, nck, score_body,
                           (jnp.full((R, LANES), jnp.inf, F32), jnp.full((R, LANES), -jnp.inf, F32)))
    mn, mx = _rowmin_b(mn), _rowmax_b(mx)
    n_causal = (qpos + 1).astype(F32)
    kk = jnp.minimum(n_causal, float(topk))
    hi0 = mx + (mx - mn) + 1.0
    row_active = lane >= 0
    lo, hi, c_hi, tied = _threshold_search(sc_ref, nck, kk, mn, hi0, n_causal, row_active)
    need = kk - c_hi
    tiedf = jnp.where(tied, 1.0, 0.0)
    tied_any = jnp.max(tiedf) > 0.0

    qa = qa_ref[...]
    qa_heads = []
    for h in range(ATT_HEADS):
        c = h // ATT_GROUP
        qh = qa[:, h * ATT_HEAD_DIM:(h + 1) * ATT_HEAD_DIM] * (ATT_HEAD_DIM ** -0.5)
        z = jnp.zeros((R, ATT_HEAD_DIM), F32)
        parts = [z] * KV_HEADS
        parts[c] = qh
        qa_heads.append(jnp.concatenate(parts, axis=1).astype(BF16))
    m_ref[...] = jnp.full(m_ref.shape, NEG_BIG, F32)
    l_ref[...] = jnp.zeros(l_ref.shape, F32)
    acc_ref[...] = jnp.zeros(acc_ref.shape, F32)

    def att_body(kc, prefix):
        k0 = pl.multiple_of(kc * _CK, _CK)
        kch = ka_ref[pl.ds(k0, _CK), :].astype(BF16)
        vch = va_ref[pl.ds(k0, _CK), :].astype(BF16)
        self_, prefix = _select_chunk(sc_ref[kc], lo, hi, need, tied_any, tiedf, prefix)
        selb = self_ > 0.0
        for h in range(ATT_HEADS):
            s = lax.dot_general(qa_heads[h], kch, (((1,), (1,)), ((), ())), preferred_element_type=F32)
            s = jnp.where(selb, s, NEG_BIG)
            m_old = m_ref[h]
            m_new = jnp.maximum(m_old, _rowmax_b(functools.reduce(jnp.maximum, _lane_tiles(s))))
            alpha = jnp.exp(m_old - m_new)
            p = jnp.exp(s - jnp.concatenate([m_new] * (_CK // LANES), axis=1))
            l_ref[h] = l_ref[h] * alpha + _rowsum_b(sum(_lane_tiles(p)))
            acc_ref[h] = acc_ref[h] * alpha + jnp.dot(p.astype(BF16), vch, preferred_element_type=F32)
            m_ref[h] = m_new
        return prefix

    lax.fori_loop(0, nck, att_body, jnp.zeros((R, LANES), F32))
    outs = []
    for h in range(ATT_HEADS):
        c = h // ATT_GROUP
        o = acc_ref[h] / l_ref[h]
        outs.append(o[:, c * ATT_HEAD_DIM:(c + 1) * ATT_HEAD_DIM])
    o_ref[...] = jnp.concatenate(outs, axis=1)


def _prompt_dsa(qi, kiwi, qa, ka, va, B, S):
    nq = S // _QB
    topk = min(TOPK_MAX, S // 4)
    qblk = lambda w: pl.BlockSpec((_QB, w), lambda b, i: (b * nq + i, 0))
    allk = lambda w: pl.BlockSpec((S, w), lambda b, i: (b, 0))
    return pl.pallas_call(
        functools.partial(_prompt_dsa_kernel, topk=topk),
        grid=(B, nq),
        in_specs=[qblk(IDX_WIDTH), qblk(_KIWI_WIDTH), allk(_KIWI_WIDTH), qblk(ATT_WIDTH),
                  allk(KV_WIDTH), allk(KV_WIDTH)],
        out_specs=qblk(ATT_WIDTH),
        out_shape=jax.ShapeDtypeStruct((B * S, ATT_WIDTH), F32),
        scratch_shapes=[pltpu.VMEM((S // _CK, _QB, _CK), F32),
                        pltpu.VMEM((ATT_HEADS, _QB, LANES), F32),
                        pltpu.VMEM((ATT_HEADS, _QB, LANES), F32),
                        pltpu.VMEM((ATT_HEADS, _QB, LANES), F32)],
        name="prompt_dsa",
        compiler_params=_cparams(("arbitrary", "arbitrary")),
    )(qi, kiwi, kiwi, qa, ka, va)


SROWS = SUBLANES
_CKS = 640


def _sample_dsa_kernel(pt_ref, qi_ref, kiwi_ref, qa_ref, ka_ref, va_ref, cik_hbm, ck_hbm, cv_hbm,
                       o_ref, ikbuf, kbuf, vbuf, sems, sc_ref, *, n_pages, t_real, topk):
    b = pl.program_id(0)
    nb = pl.num_programs(0)
    slot = b % 2
    past_len = n_pages * PAGE_SIZE
    L = past_len + PAGE_SIZE
    nck = L // _CKS
    R = SROWS

    def page_copies(bb, s, p):
        phys = pt_ref[bb, p]
        rows = pl.ds(p * PAGE_SIZE, PAGE_SIZE)
        return (pltpu.make_async_copy(cik_hbm.at[phys], ikbuf.at[s, rows, :], sems.at[s, 0]),
                pltpu.make_async_copy(ck_hbm.at[phys], kbuf.at[s, rows, :], sems.at[s, 1]),
                pltpu.make_async_copy(cv_hbm.at[phys], vbuf.at[s, rows, :], sems.at[s, 2]))

    def start_fetch(bb, s):
        def body(p, _):
            for cp in page_copies(bb, s, p):
                cp.start()
            return 0
        lax.fori_loop(0, n_pages, body, 0)

    def wait_fetch(bb, s):
        def body(p, _):
            for cp in page_copies(bb, s, p):
                cp.wait()
            return 0
        lax.fori_loop(0, n_pages, body, 0)

    @pl.when(b == 0)
    def _():
        tail = pl.ds(past_len, PAGE_SIZE)
        for s in range(2):
            ikbuf[s, tail, :] = jnp.zeros((PAGE_SIZE, IDX_DIM), F32)
            kbuf[s, tail, :] = jnp.zeros((PAGE_SIZE, KV_WIDTH), F32)
            vbuf[s, tail, :] = jnp.zeros((PAGE_SIZE, KV_WIDTH), F32)
        start_fetch(0, 0)

    @pl.when(b + 1 < nb)
    def _():
        start_fetch(b + 1, 1 - slot)

    kiwi = kiwi_ref[...]
    new_rows = pl.ds(past_len, R)
    ikbuf[slot, new_rows, :] = kiwi[:, :IDX_DIM]
    kbuf[slot, new_rows, :] = ka_ref[...]
    vbuf[slot, new_rows, :] = va_ref[...]
    wait_fetch(b, slot)

    row = lax.broadcasted_iota(jnp.int32, (R, LANES), 0)
    row_c = lax.broadcasted_iota(jnp.int32, (R, _CKS), 0)
    lane_c = lax.broadcasted_iota(jnp.int32, (R, _CKS), 1)
    rep = _CKS // LANES

    qi = qi_ref[...]
    q_all = jnp.concatenate(
        [qi[:, h * IDX_DIM:(h + 1) * IDX_DIM] * (IDX_DIM ** -0.5) for h in range(IDX_HEADS)], axis=0).astype(BF16)
    w_heads = [jnp.broadcast_to(kiwi[:, IDX_DIM + h:IDX_DIM + h + 1], (R, _CKS)) for h in range(IDX_HEADS)]

    def score_body(kc, carry):
        mn, mx = carry
        k0 = pl.multiple_of(kc * _CKS, LANES)
        keys = ikbuf[slot, pl.ds(k0, _CKS), :].astype(BF16)
        d = lax.dot_general(q_all, keys, (((1,), (1,)), ((), ())), preferred_element_type=F32)
        acc = jnp.zeros((R, _CKS), F32)
        for h in range(IDX_HEADS):
            acc = acc + w_heads[h] * jnp.maximum(d[h * R:(h + 1) * R], 0.0)
        causal = (k0 + lane_c) <= (past_len + row_c)
        sc_ref[kc] = jnp.where(causal, acc, -jnp.inf)
        for j in range(rep):
            a = acc[:, j * LANES:(j + 1) * LANES]
            cz = causal[:, j * LANES:(j + 1) * LANES]
            mn = jnp.minimum(mn, jnp.where(cz, a, jnp.inf))
            mx = jnp.maximum(mx, jnp.where(cz, a, -jnp.inf))
        return mn, mx

    mn, mx = lax.fori_loop(0, nck, score_body,
                           (jnp.full((R, LANES), jnp.inf, F32), jnp.full((R, LANES), -jnp.inf, F32)))
    mn, mx = _rowmin_b(mn), _rowmax_b(mx)
    n_causal = (past_len + row + 1).astype(F32)
    kk = jnp.minimum(n_causal, float(topk))
    hi0 = mx + (mx - mn) + 1.0
    row_active = row < t_real
    lo, hi, c_hi, tied = _threshold_search(sc_ref, nck, kk, mn, hi0, n_causal, row_active)
    need = kk - c_hi
    tiedf = jnp.where(tied, 1.0, 0.0)
    tied_any = jnp.max(tiedf) > 0.0

    qa = qa_ref[...]
    q_rows = []
    for h in range(ATT_HEADS):
        c = h // ATT_GROUP
        z = jnp.zeros((R, ATT_HEAD_DIM), F32)
        parts = [z] * KV_HEADS
        parts[c] = qa[:, h * ATT_HEAD_DIM:(h + 1) * ATT_HEAD_DIM] * (ATT_HEAD_DIM ** -0.5)
        q_rows.append(jnp.concatenate(parts, axis=1))
    q_big = jnp.concatenate(q_rows, axis=0).astype(BF16)
    HR = ATT_HEADS * R

    def att_body(kc, carry):
        m_old, l_old, acc, prefix = carry
        k0 = pl.multiple_of(kc * _CKS, LANES)
        kch = kbuf[slot, pl.ds(k0, _CKS), :].astype(BF16)
        vch = vbuf[slot, pl.ds(k0, _CKS), :].astype(BF16)
        self_, prefix = _select_chunk(sc_ref[kc], lo, hi, need, tied_any, tiedf, prefix)
        selb = jnp.concatenate([self_] * ATT_HEADS, axis=0) > 0.0
        s = lax.dot_general(q_big, kch, (((1,), (1,)), ((), ())), preferred_element_type=F32)
        s = jnp.where(selb, s, NEG_BIG)
        m_new = jnp.maximum(m_old, _rowmax_b(functools.reduce(jnp.maximum, _lane_tiles(s))))
        alpha = jnp.exp(m_old - m_new)
        p = jnp.exp(s - jnp.concatenate([m_new] * rep, axis=1))
        l_new = l_old * alpha + _rowsum_b(sum(_lane_tiles(p)))
        acc = acc * alpha + jnp.dot(p.astype(BF16), vch, preferred_element_type=F32)
        return m_new, l_new, acc, prefix

    init = (jnp.full((HR, LANES), NEG_BIG, F32), jnp.zeros((HR, LANES), F32), jnp.zeros((HR, LANES), F32),
            jnp.zeros((R, LANES), F32))
    _, l_fin, acc, _ = lax.fori_loop(0, nck, att_body, init)
    o = acc / l_fin
    outs = []
    for h in range(ATT_HEADS):
        c = h // ATT_GROUP
        outs.append(o[h * R:(h + 1) * R, c * ATT_HEAD_DIM:(c + 1) * ATT_HEAD_DIM])
    o_ref[...] = jnp.concatenate(outs, axis=1)


def _sample_dsa(page_table, qi, kiwi, qa, ka, va, cache_idx_k_l, cache_k_l, cache_v_l, t_real, row0):
    Bd, n_pages = page_table.shape
    past_len = n_pages * PAGE_SIZE
    L = past_len + PAGE_SIZE
    assert L % _CKS == 0 and row0 % SROWS == 0
    topk = min(TOPK_MAX, (past_len + t_real) // 4)
    blk0 = row0 // SROWS
    blk = lambda w: pl.BlockSpec((SROWS, w), lambda b, pt: (blk0 + b, 0))
    any_spec = pl.BlockSpec(memory_space=pl.ANY)
    grid_spec = pltpu.PrefetchScalarGridSpec(
        num_scalar_prefetch=1,
        grid=(Bd,),
        in_specs=[blk(IDX_WIDTH), blk(_KIWI_WIDTH), blk(ATT_WIDTH), blk(KV_WIDTH), blk(KV_WIDTH),
                  any_spec, any_spec, any_spec],
        out_specs=pl.BlockSpec((SROWS, ATT_WIDTH), lambda b, pt: (b, 0)),
        scratch_shapes=[pltpu.VMEM((2, L, IDX_DIM), F32),
                        pltpu.VMEM((2, L, KV_WIDTH), F32),
                        pltpu.VMEM((2, L, KV_WIDTH), F32),
                        pltpu.SemaphoreType.DMA((2, 3)),
                        pltpu.VMEM((L // _CKS, SROWS, _CKS), F32)],
    )
    return pl.pallas_call(
        functools.partial(_sample_dsa_kernel, n_pages=n_pages, t_real=t_real, topk=topk),
        grid_spec=grid_spec,
        out_shape=jax.ShapeDtypeStruct((Bd * SROWS, ATT_WIDTH), F32),
        name="sample_dsa",
        compiler_params=_cparams(("arbitrary",)),
    )(page_table, qi, kiwi, qa, ka, va, cache_idx_k_l, cache_k_l, cache_v_l)


_TMP = 256
_PAIR_GROUP = 16
PEER_HALF_EXPERTS = PEER_EXPERTS // 2


def _extract_topk(s, pos, n, k, payload=None):
    vals, idxs = [], []
    for _ in range(k):
        m = jnp.max(s, axis=0, keepdims=True)
        p = jnp.min(jnp.where(s == m, pos, n), axis=0, keepdims=True)
        hit = pos == p
        vals.append(m)
        if payload is None:
            idxs.append(p)
        else:
            idxs.append(jnp.max(jnp.where(hit, payload, -1), axis=0, keepdims=True))
        s = jnp.where(hit, -jnp.inf, s)
    return vals, idxs


def _post_mix_kernel(x_ref, attp_ref, atts_ref, retp_ref, rets_ref, wo_ref, nw_ref, wqt_ref, sk1_ref, sk2_ref,
                     h_ref, xn_ref, idx_ref, gate_ref, ng_ref, qt_ref, et_ref, gt_ref, la_ref, lg_ref,
                     *, n_prompt_tiles):
    T = x_ref.shape[0]
    is_prompt = pl.program_id(0) < n_prompt_tiles
    att = jnp.where(is_prompt, attp_ref[...], atts_ref[...])
    ret = jnp.where(is_prompt, retp_ref[...], rets_ref[...])
    h = x_ref[...] + _dot(att, wo_ref[:ATT_WIDTH, :]) + _dot(ret, wo_ref[ATT_WIDTH:, :])
    h_ref[...] = h
    ms = jnp.mean(h * h, axis=-1, keepdims=True)
    xn = h * lax.rsqrt(ms + NORM_EPS) * nw_ref[...]
    xn_ref[...] = xn
    qt_ref[...] = _dot_nt(wqt_ref[...], xn)

    pos_k = lax.broadcasted_iota(jnp.int32, (PEER_NKEYS, T), 0)
    n_cand = PEER_TOPK * PEER_TOPK
    pos_c = lax.broadcasted_iota(jnp.int32, (n_cand, T), 0)

    def head_body(hd, _):
        q0 = pl.multiple_of(hd * PEER_KEY_DIM, PEER_KEY_DIM)
        q1 = qt_ref[pl.ds(q0, PEER_HALF), :]
        q2 = qt_ref[pl.ds(q0 + PEER_HALF, PEER_HALF), :]
        s1 = _dot(sk1_ref[...], q1)
        s2 = _dot(sk2_ref[...], q2)
        v1, i1 = _extract_topk(s1, pos_k, PEER_NKEYS, PEER_TOPK)
        v2, i2 = _extract_topk(s2, pos_k, PEER_NKEYS, PEER_TOPK)
        v2m = jnp.concatenate(v2, axis=0)
        i2m = jnp.concatenate(i2, axis=0)
        cand = jnp.concatenate([v1[a] + v2m for a in range(PEER_TOPK)], axis=0)
        cid = jnp.concatenate([i1[a] * PEER_NKEYS + i2m for a in range(PEER_TOPK)], axis=0)
        sv, eid = _extract_topk(cand, pos_c, n_cand, PEER_TOPK, payload=cid)
        svm = jnp.concatenate(sv, axis=0)
        g = jnp.exp(svm - sv[0])
        r0 = pl.multiple_of(hd * PEER_TOPK, PEER_TOPK)
        gt_ref[pl.ds(r0, PEER_TOPK), :] = g / jnp.sum(g, axis=0, keepdims=True)
        et_ref[pl.ds(r0, PEER_TOPK), :] = jnp.concatenate(eid, axis=0).astype(F32)
        return 0

    lax.fori_loop(0, PEER_HEADS, head_body, 0)

    e = et_ref[...]
    g = gt_ref[...]
    npair = e.shape[0]
    is0 = e < float(PEER_HALF_EXPERTS)
    r_i = lax.broadcasted_iota(jnp.int32, (npair, npair), 0)
    c_i = lax.broadcasted_iota(jnp.int32, (npair, npair), 1)
    tri = jnp.where(c_i <= r_i, 1.0, 0.0).astype(BF16)
    rank0 = jnp.dot(tri, jnp.where(is0, 1.0, 0.0).astype(BF16), preferred_element_type=F32)
    rowf = lax.broadcasted_iota(jnp.int32, (npair, T), 0).astype(F32)
    place = jnp.where(is0, rank0 - 1.0, (npair - 1.0) - (rowf - rank0))
    off = jnp.where(is0, e, e - float(PEER_HALF_EXPERTS)) * float(SUBLANES)
    for p in range(npair):
        m = place == float(p)
        la_ref[p:p + 1, :] = jnp.sum(jnp.where(m, off, 0.0), axis=0, keepdims=True)
        lg_ref[p:p + 1, :] = jnp.sum(jnp.where(m, g, 0.0), axis=0, keepdims=True)
    n0 = rank0[npair - 1:npair, :]
    in0 = rowf < n0
    offp, gp = la_ref[...], lg_ref[...]
    dummy = float(PEER_HALF_EXPERTS * SUBLANES)
    idx_ref[0] = jnp.where(in0, offp, dummy).T.astype(jnp.int32)
    idx_ref[1] = jnp.where(in0, dummy, offp).T.astype(jnp.int32)
    gate_ref[0] = jnp.where(in0, gp, 0.0).T
    gate_ref[1] = jnp.where(in0, 0.0, gp).T
    inv_grp = 1.0 / _PAIR_GROUP
    ng_ref[0:1, :] = jnp.floor((n0 + (_PAIR_GROUP - 1.0)) * inv_grp).astype(jnp.int32)
    ng_ref[1:2, :] = jnp.floor(((npair - n0) + (_PAIR_GROUP - 1.0)) * inv_grp).astype(jnp.int32)


def _post_mix(x, att_p, att_s, ret_p, ret_s, w_out_bf, norm_w, wq_t_bf, sk1_bf, sk2_bf):
    N = x.shape[0]
    nt = N // _TMP
    ntp = att_p.shape[0] // _TMP
    nts = att_s.shape[0] // _TMP
    assert ntp + nts == nt and ntp > 0 and nts > 0
    npair = PEER_HEADS * PEER_TOPK
    row = lambda w: pl.BlockSpec((_TMP, w), lambda i: (i, 0))
    prow = lambda w: pl.BlockSpec((_TMP, w), lambda i: (jnp.minimum(i, ntp - 1), 0))
    srow = lambda w: pl.BlockSpec((_TMP, w), lambda i: (jnp.maximum(i - ntp, 0), 0))
    full = lambda a: pl.BlockSpec(a.shape, lambda i: (0,) * a.ndim)
    nw = norm_w.reshape(1, D_MODEL)
    return pl.pallas_call(
        functools.partial(_post_mix_kernel, n_prompt_tiles=ntp),
        grid=(nt,),
        in_specs=[row(D_MODEL), prow(ATT_WIDTH), srow(ATT_WIDTH), prow(RET_WIDTH), srow(RET_WIDTH),
                  full(w_out_bf), full(nw), full(wq_t_bf), full(sk1_bf), full(sk2_bf)],
        out_specs=[row(D_MODEL), row(D_MODEL),
                   pl.BlockSpec((2, _TMP, npair), lambda i: (0, i, 0)),
                   pl.BlockSpec((2, _TMP, npair), lambda i: (0, i, 0)),
                   pl.BlockSpec((2, _TMP), lambda i: (0, i))],
        out_shape=[jax.ShapeDtypeStruct((N, D_MODEL), F32), jax.ShapeDtypeStruct((N, D_MODEL), F32),
                   jax.ShapeDtypeStruct((2, N, npair), jnp.int32), jax.ShapeDtypeStruct((2, N, npair), F32),
                   jax.ShapeDtypeStruct((2, N), jnp.int32)],
        scratch_shapes=[pltpu.VMEM((PEER_HEADS * PEER_KEY_DIM, _TMP), F32),
                        pltpu.VMEM((npair, _TMP), F32), pltpu.VMEM((npair, _TMP), F32),
                        pltpu.VMEM((npair, _TMP), F32), pltpu.VMEM((npair, _TMP), F32)],
        name="post_mix",
        compiler_params=_cparams(("arbitrary",)),
    )(x, att_p, att_s, ret_p, ret_s, w_out_bf, nw, wq_t_bf, sk1_bf, sk2_bf)


_TBP = 128
_NPAIR = PEER_HEADS * PEER_TOPK
_ROW_TILE = D_MODEL // LANES
assert _ROW_TILE == SUBLANES


def _load_table_half(tbl_hbm, tbuf, sem, half):
    rows = PEER_HALF_EXPERTS * _ROW_TILE
    tbuf[pl.ds(rows, _ROW_TILE), :] = jnp.zeros((_ROW_TILE, LANES), F32)
    cp = pltpu.make_async_copy(tbl_hbm.at[pl.ds(half * rows, rows), :], tbuf.at[pl.ds(0, rows), :], sem)
    cp.start()
    cp.wait()


def _fold_pair(a, b, k, sub):
    m = (sub & k) == 0
    return jnp.where(m, a, pltpu.roll(b, k, 0)) + jnp.where(m, pltpu.roll(a, SUBLANES - k, 0), b)


def _fold8(p, sub):
    a, b, c, d, e, f, g, h = p[0], p[4], p[2], p[6], p[1], p[5], p[3], p[7]
    t1, t2, t3, t4 = (_fold_pair(a, b, 4, sub), _fold_pair(c, d, 4, sub),
                      _fold_pair(e, f, 4, sub), _fold_pair(g, h, 4, sub))
    u1, u2 = _fold_pair(t1, t2, 2, sub), _fold_pair(t3, t4, 2, sub)
    return _fold_pair(u1, u2, 1, sub)


def _flat_smem_spec(nt, per_token):
    return pl.BlockSpec((1, 1, _TBP * per_token), lambda hf, i: (hf * nt + i, 0, 0), memory_space=pltpu.SMEM)


def _flat_blocks(a, nt):
    return a.reshape(2 * nt, 1, -1)


def _table_row(tbuf, idx_ref, k):
    off = pl.multiple_of(idx_ref[0, 0, k], _ROW_TILE)
    return tbuf[pl.ds(off, _ROW_TILE), :]


def _group_range(half, ng):
    n_groups = _NPAIR // _PAIR_GROUP
    g0 = jnp.where(half == 0, 0, n_groups - ng)
    return g0, g0 + ng


def _load_pair_group(tbuf, idx_ref, t, g):
    g = jnp.minimum(g, _NPAIR // _PAIR_GROUP - 1)
    kg = pl.multiple_of(t * _NPAIR + g * _PAIR_GROUP, _PAIR_GROUP)
    return tuple(_table_row(tbuf, idx_ref, kg + s) for s in range(_PAIR_GROUP))


def _peer_u_kernel(idx_ref, ng_ref, x_ref, u_hbm, o_ref, tbuf, sem, z_ref):
    half = pl.program_id(0)

    @pl.when(pl.program_id(1) == 0)
    def _():
        _load_table_half(u_hbm, tbuf, sem, half)

    @pl.when((half == 0) & (pl.program_id(1) == 0))
    def _():
        z_ref[...] = jnp.zeros(z_ref.shape, F32)

    sub = lax.broadcasted_iota(jnp.int32, (SUBLANES, LANES), 0)

    def tok_body(t, _):
        xt = x_ref[t]

        def grp_body(g, tiles):
            nxt = _load_pair_group(tbuf, idx_ref, t, g + 1)
            kg = pl.multiple_of(t * _NPAIR + g * _PAIR_GROUP, _PAIR_GROUP)
            prods = [tl * xt for tl in tiles]
            for q in range(_PAIR_GROUP // SUBLANES):
                z_ref[pl.ds(kg + q * SUBLANES, SUBLANES), :] = _fold8(prods[q * SUBLANES:(q + 1) * SUBLANES], sub)
            return nxt

        g0, g1 = _group_range(half, ng_ref[0, 0, t])
        lax.fori_loop(g0, g1, grp_body, _load_pair_group(tbuf, idx_ref, t, g0))
        return 0

    lax.fori_loop(0, _TBP, tok_body, 0)

    ones = jnp.ones((LANES, LANES), BF16)
    eye = (lax.broadcasted_iota(jnp.int32, (_NPAIR, LANES), 0)
           == lax.broadcasted_iota(jnp.int32, (_NPAIR, LANES), 1))

    def chunk_body(c, _):
        r0 = pl.multiple_of(c * SUBLANES * _NPAIR, SUBLANES * _NPAIR)
        z = z_ref[pl.ds(r0, SUBLANES * _NPAIR), :]
        zh = z.astype(BF16)
        zl = (z - zh.astype(F32)).astype(BF16)
        hv = jnp.dot(zh, ones, preferred_element_type=F32) + jnp.dot(zl, ones, preferred_element_type=F32)
        rows = [jnp.sum(jnp.where(eye, hv[tt * _NPAIR:(tt + 1) * _NPAIR], 0.0), axis=0, keepdims=True)
                for tt in range(SUBLANES)]
        o_ref[0, pl.ds(pl.multiple_of(c * SUBLANES, SUBLANES), SUBLANES), :] = jnp.concatenate(rows, axis=0)
        return 0

    lax.fori_loop(0, _TBP // SUBLANES, chunk_body, 0)


def _peer_u(idx, ngrp, xn, u_tbl):
    N = xn.shape[0]
    nt = N // _TBP
    return pl.pallas_call(
        _peer_u_kernel,
        grid=(2, nt),
        in_specs=[_flat_smem_spec(nt, _NPAIR), _flat_smem_spec(nt, 1),
                  pl.BlockSpec((_TBP, _ROW_TILE, LANES), lambda hf, i: (i, 0, 0)),
                  pl.BlockSpec(memory_space=pl.ANY)],
        out_specs=pl.BlockSpec((1, _TBP, _NPAIR), lambda hf, i: (hf, i, 0)),
        out_shape=jax.ShapeDtypeStruct((2, N, _NPAIR), F32),
        scratch_shapes=[pltpu.VMEM(((PEER_HALF_EXPERTS + 1) * _ROW_TILE, LANES), F32),
                        pltpu.SemaphoreType.DMA(()),
                        pltpu.VMEM((_TBP * _NPAIR, LANES), F32)],
        name="peer_u",
        compiler_params=_cparams(("arbitrary", "arbitrary")),
    )(_flat_blocks(idx, nt), _flat_blocks(ngrp, nt), xn.reshape(N, _ROW_TILE, LANES),
      u_tbl.reshape(PEER_EXPERTS * _ROW_TILE, LANES))


def _peer_coef_kernel(hv_ref, gate_ref, o_ref):
    hv = hv_ref[...]
    o_ref[...] = gate_ref[...] * (0.5 * hv * (1.0 + lax.erf(hv * (0.5 ** 0.5))))


def _peer_coef(hval, gate):
    N = gate.shape[1]
    tm = 1024 if N % 1024 == 0 else _TBP
    spec = pl.BlockSpec((2, tm, _NPAIR), lambda i: (0, i, 0))
    return pl.pallas_call(
        _peer_coef_kernel,
        grid=(N // tm,),
        in_specs=[spec, spec],
        out_specs=spec,
        out_shape=jax.ShapeDtypeStruct((2, N, _NPAIR), F32),
        name="peer_coef",
        compiler_params=_cparams(("arbitrary",)),
    )(hval, gate)


def _peer_v_kernel(idx_ref, ng_ref, coef_ref, v_hbm, o_ref, tbuf, sem):
    half = pl.program_id(0)

    @pl.when(pl.program_id(1) == 0)
    def _():
        _load_table_half(v_hbm, tbuf, sem, half)

    n_acc = 4

    def weighted_group(t, g):
        g = jnp.minimum(g, _NPAIR // _PAIR_GROUP - 1)
        kg = pl.multiple_of(t * _NPAIR + g * _PAIR_GROUP, _PAIR_GROUP)
        return tuple(coef_ref[0, 0, kg + s] * _table_row(tbuf, idx_ref, kg + s) for s in range(_PAIR_GROUP))

    def tok_body(t, _):
        def grp_body(g, carry):
            prods, accs = carry
            nxt = weighted_group(t, g + 1)
            accs = list(accs)
            for s in range(_PAIR_GROUP):
                accs[s % n_acc] = accs[s % n_acc] + prods[s]
            return nxt, tuple(accs)

        zero = jnp.zeros((_ROW_TILE, LANES), F32)
        g0, g1 = _group_range(half, ng_ref[0, 0, t])
        _, accs = lax.fori_loop(g0, g1, grp_body, (weighted_group(t, g0), (zero,) * n_acc))
        o_ref[0, t] = (accs[0] + accs[1]) + (accs[2] + accs[3])
        return 0

    lax.fori_loop(0, _TBP, tok_body, 0)


def _peer_v(idx, ngrp, coef, v_tbl):
    N = coef.shape[1]
    nt = N // _TBP
    out = pl.pallas_call(
        _peer_v_kernel,
        grid=(2, nt),
        in_specs=[_flat_smem_spec(nt, _NPAIR), _flat_smem_spec(nt, 1), _flat_smem_spec(nt, _NPAIR),
                  pl.BlockSpec(memory_space=pl.ANY)],
        out_specs=pl.BlockSpec((1, _TBP, _ROW_TILE, LANES), lambda hf, i: (hf, i, 0, 0)),
        out_shape=jax.ShapeDtypeStruct((2, N, _ROW_TILE, LANES), F32),
        scratch_shapes=[pltpu.VMEM(((PEER_HALF_EXPERTS + 1) * _ROW_TILE, LANES), F32),
                        pltpu.SemaphoreType.DMA(())],
        name="peer_v",
        compiler_params=_cparams(("arbitrary", "arbitrary")),
    )(_flat_blocks(idx, nt), _flat_blocks(ngrp, nt), _flat_blocks(coef, nt),
      v_tbl.reshape(PEER_EXPERTS * _ROW_TILE, LANES))
    return out.reshape(2, N, D_MODEL)


def _final_kernel(h_ref, y_ref, w_ref, o_ref, *, normalize):
    h = h_ref[...] + (y_ref[0] + y_ref[1])
    if normalize:
        ms = jnp.mean(h * h, axis=-1, keepdims=True)
        h = h * lax.rsqrt(ms + NORM_EPS) * w_ref[...]
    o_ref[...] = h


def _final(h, ypart, w):
    N = h.shape[0]
    tm = 512 if N % 512 == 0 else _TBP
    normalize = w is not None
    if not normalize:
        w = jnp.ones((D_MODEL,), F32)
    return pl.pallas_call(
        functools.partial(_final_kernel, normalize=normalize),
        grid=(N // tm,),
        in_specs=[pl.BlockSpec((tm, D_MODEL), lambda i: (i, 0)),
                  pl.BlockSpec((2, tm, D_MODEL), lambda i: (0, i, 0)),
                  pl.BlockSpec((1, D_MODEL), lambda i: (0, 0))],
        out_specs=pl.BlockSpec((tm, D_MODEL), lambda i: (i, 0)),
        out_shape=jax.ShapeDtypeStruct((N, D_MODEL), F32),
        name="final_norm",
        compiler_params=_cparams(("arbitrary",)),
    )(h, ypart, w.reshape(1, D_MODEL))


_TM_IN = 256


def kernel(x_prompt, x_sample, cache_k, cache_v, cache_idx_k, state_ret, page_table, norm_attn_w, w_in, ret_gn_w,
           w_out, norm_ffn_w, peer_w_q, peer_sub_keys_1, peer_sub_keys_2, peer_u, peer_v, final_norm_w):
    B, S, D = x_prompt.shape
    Bd, T, _ = x_sample.shape
    depth = w_in.shape[0]
    n_pages = page_table.shape[1]
    past_len = n_pages * PAGE_SIZE
    n_phys = cache_k.shape[1]
    Np, Ns = B * S, Bd * SROWS
    assert D == D_MODEL and T <= SROWS and S % _TM_IN == 0 and Ns % _TM_IN == 0 and S % _CK == 0

    hs_pad = jnp.pad(x_sample, ((0, 0), (0, SROWS - T), (0, 0)))
    h_all = jnp.concatenate([x_prompt.reshape(Np, D), hs_pad.reshape(Ns, D)], axis=0)

    pos_s = past_len + jnp.arange(SROWS)
    pos = jnp.concatenate([jnp.arange(S), jnp.tile(pos_s, _TM_IN // SROWS)])
    tab_att, tab_ret = _rope_tables(pos)
    tiles_per_seq = S // _TM_IN
    tab_blocks = [i % tiles_per_seq for i in range(Np // _TM_IN)] + [tiles_per_seq] * (Ns // _TM_IN)
    ret_consts_p = _ret_constants(RET_CHUNK)
    ret_consts_s = _ret_constants(T)

    outs = {n: [] for n in ("kp", "vp", "ikp", "sp", "ks", "vs", "iks", "ss")}
    for l in range(depth):
        m = _in_proj(h_all, norm_attn_w[l], _pad_w_in(w_in[l]), tab_att, tab_ret, tab_blocks, _TM_IN)
        att_p = _prompt_dsa(m["qi"], m["kiwi"], m["qa"], m["ka"], m["va"], B, S)
        att_s = _sample_dsa(page_table, m["qi"], m["kiwi"], m["qa"], m["ka"], m["va"],
                            cache_idx_k[l], cache_k[l].reshape(n_phys, PAGE_SIZE, KV_WIDTH),
                            cache_v[l].reshape(n_phys, PAGE_SIZE, KV_WIDTH), T, Np)
        ret_p, s_p = _retention(m["qr"], m["kr"], m["vr"], m["gr"], ret_gn_w[l],
                                jnp.zeros((B, RET_HEADS, RET_HEAD_DIM, RET_HEAD_DIM), F32),
                                ret_consts_p, B, S // RET_CHUNK, RET_CHUNK)
        ret_s, s_s = _retention(m["qr"], m["kr"], m["vr"], m["gr"], ret_gn_w[l], state_ret[l].astype(F32),
                                ret_consts_s, Bd, 1, SROWS, row0=Np)
        h_mid, xn, idx, gate, ngrp = _post_mix(
            h_all, att_p, att_s, ret_p, ret_s, w_out[l].astype(BF16), norm_ffn_w[l],
            peer_w_q[l].T.astype(BF16), peer_sub_keys_1[l].astype(BF16), peer_sub_keys_2[l].astype(BF16))
        coef = _peer_coef(_peer_u(idx, ngrp, xn, peer_u[l]), gate)
        ypart = _peer_v(idx, ngrp, coef, peer_v[l])
        last = l == depth - 1
        h_all = _final(h_mid, ypart, final_norm_w if last else None)

        sample = lambda a, w: a[Np:].reshape(Bd, SROWS, w)[:, :T]
        outs["kp"].append(m["ka"][:Np].reshape(B, S, KV_HEADS, ATT_HEAD_DIM))
        outs["vp"].append(m["va"][:Np].reshape(B, S, KV_HEADS, ATT_HEAD_DIM))
        outs["ikp"].append(m["kiwi"][:Np, :IDX_DIM].reshape(B, S, IDX_DIM))
        outs["sp"].append(s_p.astype(x_prompt.dtype))
        outs["ks"].append(sample(m["ka"], KV_WIDTH).reshape(Bd, T, KV_HEADS, ATT_HEAD_DIM))
        outs["vs"].append(sample(m["va"], KV_WIDTH).reshape(Bd, T, KV_HEADS, ATT_HEAD_DIM))
        outs["iks"].append(sample(m["kiwi"], _KIWI_WIDTH)[:, :, :IDX_DIM])
        outs["ss"].append(s_s.astype(state_ret.dtype))

    y_prompt = h_all[:Np].reshape(B, S, D)
    y_sample = h_all[Np:].reshape(Bd, SROWS, D)[:, :T]
    st = lambda n: jnp.stack(outs[n])
    return (y_prompt, y_sample, st("kp"), st("vp"), st("ikp"), st("sp"),
            st("ks"), st("vs"), st("iks"), st("ss"))
```

```python
import functools

import jax
import jax.numpy as jnp
import numpy as np
from jax import lax
from jax.experimental import pallas as pl
from jax.experimental.pallas import tpu as pltpu

F32 = jnp.float32
BF16 = jnp.bfloat16

D_MODEL = 1024
PAGE_SIZE = 128
ATT_HEADS = 8
ATT_HEAD_DIM = 64
KV_HEADS = 2
ATT_GROUP = ATT_HEADS // KV_HEADS
ATT_WIDTH = ATT_HEADS * ATT_HEAD_DIM
KV_WIDTH = KV_HEADS * ATT_HEAD_DIM
ROPE_THETA = 500000.0
ATT_ROPE_DIMS = ATT_HEAD_DIM // 4
IDX_HEADS = 8
IDX_DIM = 64
IDX_WIDTH = IDX_HEADS * IDX_DIM
TOPK_MAX = 256
RET_HEADS = 4
RET_HEAD_DIM = 128
RET_WIDTH = RET_HEADS * RET_HEAD_DIM
RET_ROPE_THETA = 10000.0
RET_CHUNK = 128
PEER_HEADS = 8
PEER_NKEYS = 128
PEER_EXPERTS = PEER_NKEYS * PEER_NKEYS
PEER_KEY_DIM = 128
PEER_HALF = PEER_KEY_DIM // 2
PEER_TOPK = 16
NORM_EPS = 1e-6
GN_EPS = 1e-6

LANES = 128
SUBLANES = 8
VMEM_LIMIT_BYTES = 56 * 1024 * 1024

NEG_BIG = -1e30

_IN_SPLITS = (ATT_WIDTH, KV_WIDTH, KV_WIDTH, IDX_WIDTH, IDX_DIM, IDX_HEADS,
              RET_WIDTH, RET_WIDTH, RET_WIDTH, RET_WIDTH)
_KIWI_WIDTH = LANES
_PAD_OFFS = {}
_off = 0
for _name, _w in (("qa", ATT_WIDTH), ("ka", KV_WIDTH), ("va", KV_WIDTH), ("qi", IDX_WIDTH),
                  ("kiwi", _KIWI_WIDTH), ("qr", RET_WIDTH), ("kr", RET_WIDTH),
                  ("vr", RET_WIDTH), ("gr", RET_WIDTH)):
    _PAD_OFFS[_name] = (_off, _w)
    _off += _w
IN_WIDTH_PADDED = _off


def _cparams(sem):
    return pltpu.CompilerParams(dimension_semantics=sem, vmem_limit_bytes=VMEM_LIMIT_BYTES)


def _dot(a, b):
    return jnp.dot(a.astype(BF16), b.astype(BF16), preferred_element_type=F32)


def _dot_nt(a, b):
    return lax.dot_general(a.astype(BF16), b.astype(BF16), (((1,), (1,)), ((), ())),
                           preferred_element_type=F32)


def _pad_w_in(w_in_l):
    cols = []
    off = 0
    parts = []
    for n in _IN_SPLITS:
        parts.append(w_in_l[:, off:off + n])
        off += n
    qa, ka, va, qi, ki, wi, qr, kr, vr, gr = parts
    kiwi = jnp.concatenate(
        [ki, wi, jnp.zeros((w_in_l.shape[0], _KIWI_WIDTH - IDX_DIM - IDX_HEADS), w_in_l.dtype)], axis=1)
    cols = [qa, ka, va, qi, kiwi, qr, kr, vr, gr]
    return jnp.concatenate(cols, axis=1).astype(BF16)


def _rope_tables(pos):
    posf = pos.astype(F32)
    half = ATT_ROPE_DIMS // 2
    inv = 1.0 / (ROPE_THETA ** (jnp.arange(half, dtype=F32) / half))
    ang = posf[:, None] * inv[None, :]
    cos, sin = jnp.cos(ang), jnp.sin(ang)
    P = pos.shape[0]
    one = jnp.ones((P, ATT_HEAD_DIM - ATT_ROPE_DIMS), F32)
    zero = jnp.zeros((P, ATT_HEAD_DIM - ATT_ROPE_DIMS), F32)
    zh = jnp.zeros((P, half), F32)
    c_head = jnp.concatenate([cos, cos, one], axis=1)
    s1_head = jnp.concatenate([-sin, zh, zero], axis=1)
    s2_head = jnp.concatenate([zh, sin, zero], axis=1)
    rep = LANES // ATT_HEAD_DIM
    att = jnp.stack([jnp.tile(c_head, (1, rep)), jnp.tile(s1_head, (1, rep)), jnp.tile(s2_head, (1, rep))])
    halfr = RET_HEAD_DIM // 2
    invr = 1.0 / (RET_ROPE_THETA ** (jnp.arange(halfr, dtype=F32) / halfr))
    angr = posf[:, None] * invr[None, :]
    cr, sr = jnp.cos(angr), jnp.sin(angr)
    ret = jnp.stack([jnp.concatenate([cr, cr], axis=1), jnp.concatenate([-sr, sr], axis=1)])
    return att, ret


def _in_proj_kernel(x_ref, nw_ref, w_ref, ta_ref, tr_ref,
                    qa_ref, ka_ref, va_ref, qi_ref, kiwi_ref, qr_ref, kr_ref, vr_ref, gr_ref):
    x = x_ref[...]
    ms = jnp.mean(x * x, axis=-1, keepdims=True)
    xn = (x * lax.rsqrt(ms + NORM_EPS) * nw_ref[...]).astype(BF16)
    ca, s1a, s2a = ta_ref[0], ta_ref[1], ta_ref[2]
    cr, sr = tr_ref[0], tr_ref[1]

    def proj(name):
        off, w = _PAD_OFFS[name]
        return jnp.dot(xn, w_ref[:, off:off + w], preferred_element_type=F32)

    def rope_att_chunk(vc):
        return (vc * ca + pltpu.roll(vc, LANES - ATT_ROPE_DIMS // 2, 1) * s1a
                + pltpu.roll(vc, ATT_ROPE_DIMS // 2, 1) * s2a)

    def rope_ret_chunk(vc):
        return vc * cr + pltpu.roll(vc, RET_HEAD_DIM // 2, 1) * sr

    def per_chunk(v, fn):
        n = v.shape[1] // LANES
        return jnp.concatenate([fn(v[:, c * LANES:(c + 1) * LANES]) for c in range(n)], axis=1)

    qa_ref[...] = per_chunk(proj("qa"), rope_att_chunk)
    ka_ref[...] = per_chunk(proj("ka"), rope_att_chunk)
    va_ref[...] = proj("va")
    qi_ref[...] = per_chunk(proj("qi"), rope_att_chunk)
    kiwi = proj("kiwi")
    lane = lax.broadcasted_iota(jnp.int32, kiwi.shape, 1)
    kiwi_ref[...] = jnp.where(lane < IDX_DIM, rope_att_chunk(kiwi), kiwi * (IDX_HEADS ** -0.5))
    qr_ref[...] = per_chunk(proj("qr"), rope_ret_chunk)
    kr_ref[...] = per_chunk(proj("kr"), rope_ret_chunk) * (RET_HEAD_DIM ** -0.5)
    vr_ref[...] = proj("vr")
    gr_ref[...] = proj("gr")


def _in_proj(x, norm_w, w_pad, tab_att, tab_ret, tab_block_of_tile, tm):
    N = x.shape[0]
    nt = N // tm
    tab_idx = jnp.asarray(tab_block_of_tile, jnp.int32)
    names = ("qa", "ka", "va", "qi", "kiwi", "qr", "kr", "vr", "gr")
    out_shape = [jax.ShapeDtypeStruct((N, _PAD_OFFS[n][1]), F32) for n in names]
    out_specs = [pl.BlockSpec((tm, _PAD_OFFS[n][1]), lambda i, t: (i, 0)) for n in names]
    grid_spec = pltpu.PrefetchScalarGridSpec(
        num_scalar_prefetch=1,
        grid=(nt,),
        in_specs=[
            pl.BlockSpec((tm, D_MODEL), lambda i, t: (i, 0)),
            pl.BlockSpec((1, D_MODEL), lambda i, t: (0, 0)),
            pl.BlockSpec((D_MODEL, IN_WIDTH_PADDED), lambda i, t: (0, 0)),
            pl.BlockSpec((3, tm, LANES), lambda i, t: (0, t[i], 0)),
            pl.BlockSpec((2, tm, LANES), lambda i, t: (0, t[i], 0)),
        ],
        out_specs=out_specs,
    )

    def body(t_ref, *refs):
        _in_proj_kernel(*refs)

    outs = pl.pallas_call(
        body, grid_spec=grid_spec, out_shape=out_shape, name="in_proj",
        compiler_params=_cparams(("arbitrary",)),
    )(tab_idx, x, norm_w.reshape(1, D_MODEL), w_pad, tab_att, tab_ret)
    return dict(zip(names, outs))


def _ret_constants(c_eff):
    C = RET_CHUNK
    lg = jnp.log1p(-(2.0 ** (-5.0 - jnp.arange(RET_HEADS, dtype=F32))))
    i = jnp.arange(C, dtype=F32)
    diff = i[:, None] - i[None, :]
    dmask = jnp.where(diff >= 0, jnp.exp(lg[:, None, None] * jnp.maximum(diff, 0.0)), 0.0)
    real = (i < c_eff)
    dmask = jnp.where(real[None, :, None] & real[None, None, :], dmask, 0.0)
    q_dec = jnp.exp(lg[:, None] * (i[None, :] + 1.0))
    k_dec = jnp.where(real[None, :], jnp.exp(lg[:, None] * (c_eff - 1.0 - i[None, :])), 0.0)
    chunk_dec = jnp.exp(lg * c_eff)
    bc = lambda a: jnp.broadcast_to(a[:, :, None], (RET_HEADS, C, C))
    cdec = jnp.broadcast_to(chunk_dec[:, None, None], (RET_HEADS, C, C))
    return dmask, bc(q_dec), bc(k_dec), cdec


def _retention_kernel(q_ref, k_ref, v_ref, g_ref, gnw_ref, s0_ref, dm_ref, qd_ref, kd_ref, cd_ref,
                      o_ref, s_out_ref, state_ref, *, rows):
    c = pl.program_id(1)
    nc = pl.num_programs(1)

    @pl.when(c == 0)
    def _():
        state_ref[...] = s0_ref[0]

    def padded(ref):
        v = ref[...]
        if rows < RET_CHUNK:
            v = jnp.concatenate([v, jnp.zeros((RET_CHUNK - rows, v.shape[1]), v.dtype)], axis=0)
        return v

    q, k, v, g = padded(q_ref), padded(k_ref), padded(v_ref), g_ref[...]
    gnw = gnw_ref[...]
    outs = []
    for h in range(RET_HEADS):
        sl = slice(h * RET_HEAD_DIM, (h + 1) * RET_HEAD_DIM)
        qh, kh, vh = q[:, sl], k[:, sl], v[:, sl]
        st = state_ref[h]
        att = _dot_nt(qh, kh) * dm_ref[h]
        o = _dot(att, vh) + _dot(qh, st) * qd_ref[h]
        state_ref[h] = st * cd_ref[h] + _dot((kh * kd_ref[h]).T, vh)
        o = o[:rows]
        mu = jnp.mean(o, axis=-1, keepdims=True)
        var = jnp.mean(jnp.square(o - mu), axis=-1, keepdims=True)
        outs.append((o - mu) * lax.rsqrt(var + GN_EPS) * gnw[:, sl])
    on = jnp.concatenate(outs, axis=1)
    o_ref[...] = g * (1.0 / (1.0 + jnp.exp(-g))) * on

    @pl.when(c == nc - 1)
    def _():
        s_out_ref[0] = state_ref[...]


def _retention(q, k, v, g, gn_w, s0, consts, nb, nc, rows, row0=0):
    blk0 = row0 // rows
    in_row = pl.BlockSpec((rows, RET_WIDTH), lambda b, c: (blk0 + b * nc + c, 0))
    out_row = pl.BlockSpec((rows, RET_WIDTH), lambda b, c: (b * nc + c, 0))
    st_spec = pl.BlockSpec((1, RET_HEADS, RET_HEAD_DIM, RET_HEAD_DIM), lambda b, c: (b, 0, 0, 0))
    const_spec = pl.BlockSpec((RET_HEADS, RET_CHUNK, RET_CHUNK), lambda b, c: (0, 0, 0))
    return pl.pallas_call(
        functools.partial(_retention_kernel, rows=rows),
        grid=(nb, nc),
        in_specs=[in_row, in_row, in_row, in_row,
                  pl.BlockSpec((1, RET_WIDTH), lambda b, c: (0, 0)),
                  st_spec, const_spec, const_spec, const_spec, const_spec],
        out_specs=[out_row, st_spec],
        out_shape=[jax.ShapeDtypeStruct((nb * nc * rows, RET_WIDTH), F32),
                   jax.ShapeDtypeStruct(s0.shape, F32)],
        scratch_shapes=[pltpu.VMEM((RET_HEADS, RET_HEAD_DIM, RET_HEAD_DIM), F32)],
        name="retention",
        compiler_params=_cparams(("arbitrary", "arbitrary")),
    )(q, k, v, g, gn_w.reshape(1, RET_WIDTH), s0, *consts)


_BISECT_PLAIN_ITERS = 26
_BISECT_MAX_ITERS = 400


def _lane_tiles(x):
    return [x[:, j * LANES:(j + 1) * LANES] for j in range(x.shape[1] // LANES)]


def _rowsum_b(x):
    return jnp.broadcast_to(jnp.sum(x, axis=1, keepdims=True), x.shape)


def _rowmax_b(x):
    return jnp.broadcast_to(jnp.max(x, axis=1, keepdims=True), x.shape)


def _rowmin_b(x):
    return jnp.broadcast_to(jnp.min(x, axis=1, keepdims=True), x.shape)


def _threshold_search(sc_ref, nck, kk, lo0, hi0, n_causal, row_active):
    R = lo0.shape[0]
    zeros = jnp.zeros((R, LANES), F32)

    def count_pass(mid, snap):
        def body(kc, carry):
            x = sc_ref[kc]
            cnt, amin, bmax = carry
            for xt in _lane_tiles(x):
                ge = xt >= mid
                cnt = cnt + jnp.where(ge, 1.0, 0.0)
                if snap:
                    amin = jnp.minimum(amin, jnp.where(ge, xt, jnp.inf))
                    bmax = jnp.maximum(bmax, jnp.where(ge, -jnp.inf, xt))
            return cnt, amin, bmax
        cnt, amin, bmax = lax.fori_loop(
            0, nck, body, (zeros, jnp.full((R, LANES), jnp.inf, F32), jnp.full((R, LANES), -jnp.inf, F32)))
        if snap:
            return _rowsum_b(cnt), _rowmin_b(amin), _rowmax_b(bmax)
        return _rowsum_b(cnt), None, None

    def not_done(c_lo, lo, hib):
        pending = row_active & (c_lo != kk) & (lo != hib)
        return jnp.max(jnp.where(pending, 1.0, 0.0)) > 0.0

    def make_step(snap):
        def step(carry):
            it, lo, hi, hib, c_lo, c_hi = carry
            mid = 0.5 * (lo + hi)
            if snap:
                mid = 0.5 * (lo + jnp.minimum(hi, hib))
                mid = jnp.where(mid > lo, mid, jnp.minimum(hi, hib))
            c, amin, bmax = count_pass(mid, snap)
            ge = c >= kk
            if snap:
                lo = jnp.where(ge, amin, lo)
                hib = jnp.where(ge, hib, bmax)
            else:
                lo = jnp.where(ge, mid, lo)
            hi = jnp.where(ge, hi, mid)
            c_lo = jnp.where(ge, c, c_lo)
            c_hi = jnp.where(ge, c_hi, c)
            return it + 1, lo, hi, hib, c_lo, c_hi
        return step

    init = (jnp.int32(0), lo0, hi0, jnp.full((R, LANES), jnp.inf, F32), n_causal, zeros)
    carry = lax.while_loop(
        lambda c: (c[0] < _BISECT_PLAIN_ITERS) & not_done(c[4], c[1], c[3]), make_step(False), init)
    carry = lax.while_loop(
        lambda c: (c[0] < _BISECT_MAX_ITERS) & not_done(c[4], c[1], c[3]), make_step(True), carry)
    _, lo, hi, hib, c_lo, c_hi = carry
    tied = row_active & (c_lo != kk)
    return lo, hi, c_hi, tied


def _select_chunk(x, lo, hi, need, tied_any, tied, prefix0):
    R, CK = x.shape
    rep = CK // LANES
    wide = lambda a: jnp.concatenate([a] * rep, axis=1)

    def plain(_):
        return jnp.where(x >= wide(lo), 1.0, 0.0), prefix0

    def with_ties(_):
        low, hiw = wide(lo), wide(hi)
        band = (x >= low) & (x < hiw)
        bandf = jnp.where(band, 1.0, 0.0)
        r_i = lax.broadcasted_iota(jnp.int32, (CK, CK), 0)
        c_i = lax.broadcasted_iota(jnp.int32, (CK, CK), 1)
        tri = jnp.where(r_i <= c_i, 1.0, 0.0).astype(BF16)
        rank = jnp.dot(bandf.astype(BF16), tri, preferred_element_type=F32) + wide(prefix0)
        take_tie = jnp.where(rank <= wide(need), bandf, 0.0)
        sel_tied = jnp.where(x >= hiw, 1.0, take_tie)
        sel = jnp.where(wide(tied) > 0.0, sel_tied, jnp.where(x >= low, 1.0, 0.0))
        return sel, prefix0 + _rowsum_b(sum(_lane_tiles(bandf)))

    return lax.cond(tied_any, with_ties, plain, 0)


_QB = 128
_CK = 512


def _prompt_dsa_kernel(qi_ref, kiwiq_ref, kiwi_ref, qa_ref, ka_ref, va_ref, o_ref,
                       sc_ref, m_ref, l_ref, acc_ref, *, topk):
    i = pl.program_id(1)
    t0 = i * _QB
    nck = (t0 + _QB + _CK - 1) // _CK
    R = _QB
    qpos = t0 + lax.broadcasted_iota(jnp.int32, (R, LANES), 0)
    lane_c = lax.broadcasted_iota(jnp.int32, (R, _CK), 1)
    qpos_c = t0 + lax.broadcasted_iota(jnp.int32, (R, _CK), 0)
    lane = lax.broadcasted_iota(jnp.int32, (R, LANES), 1)

    qi = qi_ref[...]
    wq = kiwiq_ref[...]
    q_heads = []
    w_heads = []
    for h in range(IDX_HEADS):
        qh = qi[:, h * IDX_DIM:(h + 1) * IDX_DIM] * (IDX_DIM ** -0.5)
        q_heads.append(jnp.concatenate([qh, jnp.zeros((R, LANES - IDX_DIM), F32)], axis=1).astype(BF16))
        w_heads.append(jnp.broadcast_to(wq[:, IDX_DIM + h:IDX_DIM + h + 1], (R, LANES)))

    def score_body(kc, carry):
        mn, mx = carry
        k0 = pl.multiple_of(kc * _CK, _CK)
        kc_rows = kiwi_ref[pl.ds(k0, _CK), :].astype(BF16)
        acc = jnp.zeros((R, _CK), F32)
        for h in range(IDX_HEADS):
            d = lax.dot_general(q_heads[h], kc_rows, (((1,), (1,)), ((), ())), preferred_element_type=F32)
            acc = acc + jnp.concatenate([w_heads[h]] * (_CK // LANES), axis=1) * jnp.maximum(d, 0.0)
        causal = (k0 + lane_c) <= qpos_c
        sc_ref[kc] = jnp.where(causal, acc, -jnp.inf)
        for j in range(_CK // LANES):
            a = acc[:, j * LANES:(j + 1) * LANES]
            cz = causal[:, j * LANES:(j + 1) * LANES]
            mn = jnp.minimum(mn, jnp.where(cz, a, jnp.inf))
            mx = jnp.maximum(mx, jnp.where(cz, a, -jnp.inf))
        return mn, mx

    mn, mx = lax.fori_loop(0, nck, score_body,
                           (jnp.full((R, LANES), jnp.inf, F32), jnp.full((R, LANES), -jnp.inf, F32)))
    mn, mx = _rowmin_b(mn), _rowmax_b(mx)
    n_causal = (qpos + 1).astype(F32)
    kk = jnp.minimum(n_causal, float(topk))
    hi0 = mx + (mx - mn) + 1.0
    row_active = lane >= 0
    lo, hi, c_hi, tied = _threshold_search(sc_ref, nck, kk, mn, hi0, n_causal, row_active)
    need = kk - c_hi
    tiedf = jnp.where(tied, 1.0, 0.0)
    tied_any = jnp.max(tiedf) > 0.0

    qa = qa_ref[...]
    qa_heads = []
    for h in range(ATT_HEADS):
        c = h // ATT_GROUP
        qh = qa[:, h * ATT_HEAD_DIM:(h + 1) * ATT_HEAD_DIM] * (ATT_HEAD_DIM ** -0.5)
        z = jnp.zeros((R, ATT_HEAD_DIM), F32)
        parts = [z] * KV_HEADS
        parts[c] = qh
        qa_heads.append(jnp.concatenate(parts, axis=1).astype(BF16))
    m_ref[...] = jnp.full(m_ref.shape, NEG_BIG, F32)
    l_ref[...] = jnp.zeros(l_ref.shape, F32)
    acc_ref[...] = jnp.zeros(acc_ref.shape, F32)

    def att_body(kc, prefix):
        k0 = pl.multiple_of(kc * _CK, _CK)
        kch = ka_ref[pl.ds(k0, _CK), :].astype(BF16)
        vch = va_ref[pl.ds(k0, _CK), :].astype(BF16)
        self_, prefix = _select_chunk(sc_ref[kc], lo, hi, need, tied_any, tiedf, prefix)
        selb = self_ > 0.0
        for h in range(ATT_HEADS):
            s = lax.dot_general(qa_heads[h], kch, (((1,), (1,)), ((), ())), preferred_element_type=F32)
            s = jnp.where(selb, s, NEG_BIG)
            m_old = m_ref[h]
            m_new = jnp.maximum(m_old, _rowmax_b(functools.reduce(jnp.maximum, _lane_tiles(s))))
            alpha = jnp.exp(m_old - m_new)
            p = jnp.exp(s - jnp.concatenate([m_new] * (_CK // LANES), axis=1))
            l_ref[h] = l_ref[h] * alpha + _rowsum_b(sum(_lane_tiles(p)))
            acc_ref[h] = acc_ref[h] * alpha + jnp.dot(p.astype(BF16), vch, preferred_element_type=F32)
            m_ref[h] = m_new
        return prefix

    lax.fori_loop(0, nck, att_body, jnp.zeros((R, LANES), F32))
    outs = []
    for h in range(ATT_HEADS):
        c = h // ATT_GROUP
        o = acc_ref[h] / l_ref[h]
        outs.append(o[:, c * ATT_HEAD_DIM:(c + 1) * ATT_HEAD_DIM])
    o_ref[...] = jnp.concatenate(outs, axis=1)


def _prompt_dsa(qi, kiwi, qa, ka, va, B, S):
    nq = S // _QB
    topk = min(TOPK_MAX, S // 4)
    qblk = lambda w: pl.BlockSpec((_QB, w), lambda b, i: (b * nq + i, 0))
    allk = lambda w: pl.BlockSpec((S, w), lambda b, i: (b, 0))
    return pl.pallas_call(
        functools.partial(_prompt_dsa_kernel, topk=topk),
        grid=(B, nq),
        in_specs=[qblk(IDX_WIDTH), qblk(_KIWI_WIDTH), allk(_KIWI_WIDTH), qblk(ATT_WIDTH),
                  allk(KV_WIDTH), allk(KV_WIDTH)],
        out_specs=qblk(ATT_WIDTH),
        out_shape=jax.ShapeDtypeStruct((B * S, ATT_WIDTH), F32),
        scratch_shapes=[pltpu.VMEM((S // _CK, _QB, _CK), F32),
                        pltpu.VMEM((ATT_HEADS, _QB, LANES), F32),
                        pltpu.VMEM((ATT_HEADS, _QB, LANES), F32),
                        pltpu.VMEM((ATT_HEADS, _QB, LANES), F32)],
        name="prompt_dsa",
        compiler_params=_cparams(("arbitrary", "arbitrary")),
    )(qi, kiwi, kiwi, qa, ka, va)


SROWS = SUBLANES
_CKS = 640


def _sample_dsa_kernel(pt_ref, qi_ref, kiwi_ref, qa_ref, ka_ref, va_ref, cik_hbm, ck_hbm, cv_hbm,
                       o_ref, ikbuf, kbuf, vbuf, sems, sc_ref, *, n_pages, t_real, topk):
    b = pl.program_id(0)
    nb = pl.num_programs(0)
    slot = b % 2
    past_len = n_pages * PAGE_SIZE
    L = past_len + PAGE_SIZE
    nck = L // _CKS
    R = SROWS

    def page_copies(bb, s, p):
        phys = pt_ref[bb, p]
        rows = pl.ds(p * PAGE_SIZE, PAGE_SIZE)
        return (pltpu.make_async_copy(cik_hbm.at[phys], ikbuf.at[s, rows, :], sems.at[s, 0]),
                pltpu.make_async_copy(ck_hbm.at[phys], kbuf.at[s, rows, :], sems.at[s, 1]),
                pltpu.make_async_copy(cv_hbm.at[phys], vbuf.at[s, rows, :], sems.at[s, 2]))

    def start_fetch(bb, s):
        def body(p, _):
            for cp in page_copies(bb, s, p):
                cp.start()
            return 0
        lax.fori_loop(0, n_pages, body, 0)

    def wait_fetch(bb, s):
        def body(p, _):
            for cp in page_copies(bb, s, p):
                cp.wait()
            return 0
        lax.fori_loop(0, n_pages, body, 0)

    @pl.when(b == 0)
    def _():
        tail = pl.ds(past_len, PAGE_SIZE)
        for s in range(2):
            ikbuf[s, tail, :] = jnp.zeros((PAGE_SIZE, IDX_DIM), F32)
            kbuf[s, tail, :] = jnp.zeros((PAGE_SIZE, KV_WIDTH), F32)
            vbuf[s, tail, :] = jnp.zeros((PAGE_SIZE, KV_WIDTH), F32)
        start_fetch(0, 0)

    @pl.when(b + 1 < nb)
    def _():
        start_fetch(b + 1, 1 - slot)

    kiwi = kiwi_ref[...]
    new_rows = pl.ds(past_len, R)
    ikbuf[slot, new_rows, :] = kiwi[:, :IDX_DIM]
    kbuf[slot, new_rows, :] = ka_ref[...]
    vbuf[slot, new_rows, :] = va_ref[...]
    wait_fetch(b, slot)

    row = lax.broadcasted_iota(jnp.int32, (R, LANES), 0)
    row_c = lax.broadcasted_iota(jnp.int32, (R, _CKS), 0)
    lane_c = lax.broadcasted_iota(jnp.int32, (R, _CKS), 1)
    rep = _CKS // LANES

    qi = qi_ref[...]
    q_all = jnp.concatenate(
        [qi[:, h * IDX_DIM:(h + 1) * IDX_DIM] * (IDX_DIM ** -0.5) for h in range(IDX_HEADS)], axis=0).astype(BF16)
    w_heads = [jnp.broadcast_to(kiwi[:, IDX_DIM + h:IDX_DIM + h + 1], (R, _CKS)) for h in range(IDX_HEADS)]

    def score_body(kc, carry):
        mn, mx = carry
        k0 = pl.multiple_of(kc * _CKS, LANES)
        keys = ikbuf[slot, pl.ds(k0, _CKS), :].astype(BF16)
        d = lax.dot_general(q_all, keys, (((1,), (1,)), ((), ())), preferred_element_type=F32)
        acc = jnp.zeros((R, _CKS), F32)
        for h in range(IDX_HEADS):
            acc = acc + w_heads[h] * jnp.maximum(d[h * R:(h + 1) * R], 0.0)
        causal = (k0 + lane_c) <= (past_len + row_c)
        sc_ref[kc] = jnp.where(causal, acc, -jnp.inf)
        for j in range(rep):
            a = acc[:, j * LANES:(j + 1) * LANES]
            cz = causal[:, j * LANES:(j + 1) * LANES]
            mn = jnp.minimum(mn, jnp.where(cz, a, jnp.inf))
            mx = jnp.maximum(mx, jnp.where(cz, a, -jnp.inf))
        return mn, mx

    mn, mx = lax.fori_loop(0, nck, score_body,
                           (jnp.full((R, LANES), jnp.inf, F32), jnp.full((R, LANES), -jnp.inf, F32)))
    mn, mx = _rowmin_b(mn), _rowmax_b(mx)
    n_causal = (past_len + row + 1).astype(F32)
    kk = jnp.minimum(n_causal, float(topk))
    hi0 = mx + (mx - mn) + 1.0
    row_active = row < t_real
    lo, hi, c_hi, tied = _threshold_search(sc_ref, nck, kk, mn, hi0, n_causal, row_active)
    need = kk - c_hi
    tiedf = jnp.where(tied, 1.0, 0.0)
    tied_any = jnp.max(tiedf) > 0.0

    qa = qa_ref[...]
    q_rows = []
    for h in range(ATT_HEADS):
        c = h // ATT_GROUP
        z = jnp.zeros((R, ATT_HEAD_DIM), F32)
        parts = [z] * KV_HEADS
        parts[c] = qa[:, h * ATT_HEAD_DIM:(h + 1) * ATT_HEAD_DIM] * (ATT_HEAD_DIM ** -0.5)
        q_rows.append(jnp.concatenate(parts, axis=1))
    q_big = jnp.concatenate(q_rows, axis=0).astype(BF16)
    HR = ATT_HEADS * R

    def att_body(kc, carry):
        m_old, l_old, acc, prefix = carry
        k0 = pl.multiple_of(kc * _CKS, LANES)
        kch = kbuf[slot, pl.ds(k0, _CKS), :].astype(BF16)
        vch = vbuf[slot, pl.ds(k0, _CKS), :].astype(BF16)
        self_, prefix = _select_chunk(sc_ref[kc], lo, hi, need, tied_any, tiedf, prefix)
        selb = jnp.concatenate([self_] * ATT_HEADS, axis=0) > 0.0
        s = lax.dot_general(q_big, kch, (((1,), (1,)), ((), ())), preferred_element_type=F32)
        s = jnp.where(selb, s, NEG_BIG)
        m_new = jnp.maximum(m_old, _rowmax_b(functools.reduce(jnp.maximum, _lane_tiles(s))))
        alpha = jnp.exp(m_old - m_new)
        p = jnp.exp(s - jnp.concatenate([m_new] * rep, axis=1))
        l_new = l_old * alpha + _rowsum_b(sum(_lane_tiles(p)))
        acc = acc * alpha + jnp.dot(p.astype(BF16), vch, preferred_element_type=F32)
        return m_new, l_new, acc, prefix

    init = (jnp.full((HR, LANES), NEG_BIG, F32), jnp.zeros((HR, LANES), F32), jnp.zeros((HR, LANES), F32),
            jnp.zeros((R, LANES), F32))
    _, l_fin, acc, _ = lax.fori_loop(0, nck, att_body, init)
    o = acc / l_fin
    outs = []
    for h in range(ATT_HEADS):
        c = h // ATT_GROUP
        outs.append(o[h * R:(h + 1) * R, c * ATT_HEAD_DIM:(c + 1) * ATT_HEAD_DIM])
    o_ref[...] = jnp.concatenate(outs, axis=1)


def _sample_dsa(page_table, qi, kiwi, qa, ka, va, cache_idx_k_l, cache_k_l, cache_v_l, t_real, row0):
    Bd, n_pages = page_table.shape
    past_len = n_pages * PAGE_SIZE
    L = past_len + PAGE_SIZE
    assert L % _CKS == 0 and row0 % SROWS == 0
    topk = min(TOPK_MAX, (past_len + t_real) // 4)
    blk0 = row0 // SROWS
    blk = lambda w: pl.BlockSpec((SROWS, w), lambda b, pt: (blk0 + b, 0))
    any_spec = pl.BlockSpec(memory_space=pl.ANY)
    grid_spec = pltpu.PrefetchScalarGridSpec(
        num_scalar_prefetch=1,
        grid=(Bd,),
        in_specs=[blk(IDX_WIDTH), blk(_KIWI_WIDTH), blk(ATT_WIDTH), blk(KV_WIDTH), blk(KV_WIDTH),
                  any_spec, any_spec, any_spec],
        out_specs=pl.BlockSpec((SROWS, ATT_WIDTH), lambda b, pt: (b, 0)),
        scratch_shapes=[pltpu.VMEM((2, L, IDX_DIM), F32),
                        pltpu.VMEM((2, L, KV_WIDTH), F32),
                        pltpu.VMEM((2, L, KV_WIDTH), F32),
                        pltpu.SemaphoreType.DMA((2, 3)),
                        pltpu.VMEM((L // _CKS, SROWS, _CKS), F32)],
    )
    return pl.pallas_call(
        functools.partial(_sample_dsa_kernel, n_pages=n_pages, t_real=t_real, topk=topk),
        grid_spec=grid_spec,
        out_shape=jax.ShapeDtypeStruct((Bd * SROWS, ATT_WIDTH), F32),
        name="sample_dsa",
        compiler_params=_cparams(("arbitrary",)),
    )(page_table, qi, kiwi, qa, ka, va, cache_idx_k_l, cache_k_l, cache_v_l)


_TMP = 256
_PAIR_GROUP = 16
PEER_HALF_EXPERTS = PEER_EXPERTS // 2


def _extract_topk(s, pos, n, k, payload=None):
    vals, idxs = [], []
    for _ in range(k):
        m = jnp.max(s, axis=0, keepdims=True)
        p = jnp.min(jnp.where(s == m, pos, n), axis=0, keepdims=True)
        hit = pos == p
        vals.append(m)
        if payload is None:
            idxs.append(p)
        else:
            idxs.append(jnp.max(jnp.where(hit, payload, -1), axis=0, keepdims=True))
        s = jnp.where(hit, -jnp.inf, s)
    return vals, idxs


def _post_mix_kernel(x_ref, attp_ref, atts_ref, retp_ref, rets_ref, wo_ref, nw_ref, wqt_ref, sk1_ref, sk2_ref,
                     h_ref, xn_ref, idx_ref, gate_ref, ng_ref, qt_ref, et_ref, gt_ref, la_ref, lg_ref,
                     *, n_prompt_tiles):
    T = x_ref.shape[0]
    is_prompt = pl.program_id(0) < n_prompt_tiles
    att = jnp.where(is_prompt, attp_ref[...], atts_ref[...])
    ret = jnp.where(is_prompt, retp_ref[...], rets_ref[...])
    h = x_ref[...] + _dot(att, wo_ref[:ATT_WIDTH, :]) + _dot(ret, wo_ref[ATT_WIDTH:, :])
    h_ref[...] = h
    ms = jnp.mean(h * h, axis=-1, keepdims=True)
    xn = h * lax.rsqrt(ms + NORM_EPS) * nw_ref[...]
    for c in range(D_MODEL // LANES):
        xn_ref[pl.ds(c, T, stride=SUBLANES), :] = xn[:, c * LANES:(c + 1) * LANES]
    qt_ref[...] = _dot_nt(wqt_ref[...], xn)

    pos_k = lax.broadcasted_iota(jnp.int32, (PEER_NKEYS, T), 0)
    n_b = [PEER_TOPK // (a + 1) for a in range(PEER_TOPK)]
    n_cand = -(-sum(n_b) // SUBLANES) * SUBLANES
    pos_c = lax.broadcasted_iota(jnp.int32, (n_cand, T), 0)

    def head_body(hd, _):
        q0 = pl.multiple_of(hd * PEER_KEY_DIM, PEER_KEY_DIM)
        q1 = qt_ref[pl.ds(q0, PEER_HALF), :]
        q2 = qt_ref[pl.ds(q0 + PEER_HALF, PEER_HALF), :]
        s1 = _dot(sk1_ref[...], q1)
        s2 = _dot(sk2_ref[...], q2)
        v1, i1 = _extract_topk(s1, pos_k, PEER_NKEYS, PEER_TOPK)
        v2, i2 = _extract_topk(s2, pos_k, PEER_NKEYS, PEER_TOPK)
        v2m = jnp.concatenate(v2, axis=0)
        i2m = jnp.concatenate(i2, axis=0)
        n_fill = n_cand - sum(n_b)
        cand = jnp.concatenate([v1[a] + v2m[:n_b[a]] for a in range(PEER_TOPK)]
                               + [jnp.full((n_fill, T), -jnp.inf, F32)], axis=0)
        cid = jnp.concatenate([i1[a] * PEER_NKEYS + i2m[:n_b[a]] for a in range(PEER_TOPK)]
                              + [jnp.full((n_fill, T), -1, jnp.int32)], axis=0)
        sv, eid = _extract_topk(cand, pos_c, n_cand, PEER_TOPK, payload=cid)
        svm = jnp.concatenate(sv, axis=0)
        g = jnp.exp(svm - sv[0])
        r0 = pl.multiple_of(hd * PEER_TOPK, PEER_TOPK)
        gt_ref[pl.ds(r0, PEER_TOPK), :] = g / jnp.sum(g, axis=0, keepdims=True)
        et_ref[pl.ds(r0, PEER_TOPK), :] = jnp.concatenate(eid, axis=0).astype(F32)
        return 0

    lax.fori_loop(0, PEER_HEADS, head_body, 0)

    e = et_ref[...]
    g = gt_ref[...]
    npair = e.shape[0]
    is0 = e < float(PEER_HALF_EXPERTS)
    r_i = lax.broadcasted_iota(jnp.int32, (npair, npair), 0)
    c_i = lax.broadcasted_iota(jnp.int32, (npair, npair), 1)
    tri = jnp.where(c_i <= r_i, 1.0, 0.0).astype(BF16)
    rank0 = jnp.dot(tri, jnp.where(is0, 1.0, 0.0).astype(BF16), preferred_element_type=F32)
    rowf = lax.broadcasted_iota(jnp.int32, (npair, T), 0).astype(F32)
    place = jnp.where(is0, rank0 - 1.0, (npair - 1.0) - (rowf - rank0))
    off = jnp.where(is0, e, e - float(PEER_HALF_EXPERTS)) * float(SUBLANES)
    for p in range(npair):
        m = place == float(p)
        la_ref[p:p + 1, :] = jnp.sum(jnp.where(m, off, 0.0), axis=0, keepdims=True)
        lg_ref[p:p + 1, :] = jnp.sum(jnp.where(m, g, 0.0), axis=0, keepdims=True)
    n0 = rank0[npair - 1:npair, :]
    in0 = rowf < n0
    offp, gp = la_ref[...], lg_ref[...]
    dummy = float(PEER_HALF_EXPERTS * SUBLANES)
    idx_ref[0] = jnp.where(in0, offp, dummy).T.astype(jnp.int32)
    idx_ref[1] = jnp.where(in0, dummy, offp).T.astype(jnp.int32)
    gate_ref[0] = jnp.where(in0, gp, 0.0).T
    gate_ref[1] = jnp.where(in0, 0.0, gp).T
    inv_grp = 1.0 / _PAIR_GROUP
    ng_ref[0:1, :] = jnp.floor((n0 + (_PAIR_GROUP - 1.0)) * inv_grp).astype(jnp.int32)
    ng_ref[1:2, :] = jnp.floor(((npair - n0) + (_PAIR_GROUP - 1.0)) * inv_grp).astype(jnp.int32)


def _post_mix(x, att_p, att_s, ret_p, ret_s, w_out_bf, norm_w, wq_t_bf, sk1_bf, sk2_bf):
    N = x.shape[0]
    nt = N // _TMP
    ntp = att_p.shape[0] // _TMP
    nts = att_s.shape[0] // _TMP
    assert ntp + nts == nt and ntp > 0 and nts > 0
    npair = PEER_HEADS * PEER_TOPK
    row = lambda w: pl.BlockSpec((_TMP, w), lambda i: (i, 0))
    prow = lambda w: pl.BlockSpec((_TMP, w), lambda i: (jnp.minimum(i, ntp - 1), 0))
    srow = lambda w: pl.BlockSpec((_TMP, w), lambda i: (jnp.maximum(i - ntp, 0), 0))
    full = lambda a: pl.BlockSpec(a.shape, lambda i: (0,) * a.ndim)
    nw = norm_w.reshape(1, D_MODEL)
    return pl.pallas_call(
        functools.partial(_post_mix_kernel, n_prompt_tiles=ntp),
        grid=(nt,),
        in_specs=[row(D_MODEL), prow(ATT_WIDTH), srow(ATT_WIDTH), prow(RET_WIDTH), srow(RET_WIDTH),
                  full(w_out_bf), full(nw), full(wq_t_bf), full(sk1_bf), full(sk2_bf)],
        out_specs=[row(D_MODEL), pl.BlockSpec((_TMP * SUBLANES, LANES), lambda i: (i, 0)),
                   pl.BlockSpec((2, _TMP, npair), lambda i: (0, i, 0)),
                   pl.BlockSpec((2, _TMP, npair), lambda i: (0, i, 0)),
                   pl.BlockSpec((2, _TMP), lambda i: (0, i))],
        out_shape=[jax.ShapeDtypeStruct((N, D_MODEL), F32), jax.ShapeDtypeStruct((N * SUBLANES, LANES), F32),
                   jax.ShapeDtypeStruct((2, N, npair), jnp.int32), jax.ShapeDtypeStruct((2, N, npair), F32),
                   jax.ShapeDtypeStruct((2, N), jnp.int32)],
        scratch_shapes=[pltpu.VMEM((PEER_HEADS * PEER_KEY_DIM, _TMP), F32),
                        pltpu.VMEM((npair, _TMP), F32), pltpu.VMEM((npair, _TMP), F32),
                        pltpu.VMEM((npair, _TMP), F32), pltpu.VMEM((npair, _TMP), F32)],
        name="post_mix",
        compiler_params=_cparams(("arbitrary",)),
    )(x, att_p, att_s, ret_p, ret_s, w_out_bf, nw, wq_t_bf, sk1_bf, sk2_bf)


_TBP = 128
_NPAIR = PEER_HEADS * PEER_TOPK
_ROW_TILE = D_MODEL // LANES
assert _ROW_TILE == SUBLANES


def _rows_to_tiles_kernel(t_ref, o_ref):
    n = t_ref.shape[0]
    for c in range(_ROW_TILE):
        o_ref[pl.ds(c, n, stride=_ROW_TILE), :] = t_ref[:, c * LANES:(c + 1) * LANES]


def _rows_to_tiles(tbl):
    n_exp = tbl.shape[0]
    blk = 512
    return pl.pallas_call(
        _rows_to_tiles_kernel,
        grid=(n_exp // blk,),
        in_specs=[pl.BlockSpec((blk, D_MODEL), lambda i: (i, 0))],
        out_specs=pl.BlockSpec((blk * _ROW_TILE, LANES), lambda i: (i, 0)),
        out_shape=jax.ShapeDtypeStruct((n_exp * _ROW_TILE, LANES), tbl.dtype),
        name="rows_to_tiles",
        compiler_params=_cparams(("arbitrary",)),
    )(tbl)


def _load_table_half(tbl_hbm, tbuf, sem, half):
    rows = PEER_HALF_EXPERTS * _ROW_TILE
    tbuf[pl.ds(rows, _ROW_TILE), :] = jnp.zeros((_ROW_TILE, LANES), F32)
    cp = pltpu.make_async_copy(tbl_hbm.at[pl.ds(half * rows, rows), :], tbuf.at[pl.ds(0, rows), :], sem)
    cp.start()
    cp.wait()


def _fold_pair(a, b, k, sub):
    m = (sub & k) == 0
    return jnp.where(m, a, pltpu.roll(b, k, 0)) + jnp.where(m, pltpu.roll(a, SUBLANES - k, 0), b)


def _fold8(p, sub):
    a, b, c, d, e, f, g, h = p[0], p[4], p[2], p[6], p[1], p[5], p[3], p[7]
    t1, t2, t3, t4 = (_fold_pair(a, b, 4, sub), _fold_pair(c, d, 4, sub),
                      _fold_pair(e, f, 4, sub), _fold_pair(g, h, 4, sub))
    u1, u2 = _fold_pair(t1, t2, 2, sub), _fold_pair(t3, t4, 2, sub)
    return _fold_pair(u1, u2, 1, sub)


def _flat_smem_spec(nt, per_token):
    return pl.BlockSpec((1, 1, _TBP * per_token), lambda hf, i: (hf * nt + i, 0, 0), memory_space=pltpu.SMEM)


def _flat_blocks(a, nt):
    return a.reshape(2 * nt, 1, -1)


def _table_row(tbuf, idx_ref, k):
    off = pl.multiple_of(idx_ref[0, 0, k], _ROW_TILE)
    return tbuf[pl.ds(off, _ROW_TILE), :]


def _group_range(half, ng):
    n_groups = _NPAIR // _PAIR_GROUP
    g0 = jnp.where(half == 0, 0, n_groups - ng)
    return g0, g0 + ng


def _load_pair_group(tbuf, idx_ref, t, g):
    g = jnp.minimum(g, _NPAIR // _PAIR_GROUP - 1)
    kg = pl.multiple_of(t * _NPAIR + g * _PAIR_GROUP, _PAIR_GROUP)
    return tuple(_table_row(tbuf, idx_ref, kg + s) for s in range(_PAIR_GROUP))


def _peer_u_kernel(idx_ref, ng_ref, x_ref, u_hbm, o_ref, tbuf, sem, z_ref):
    half = pl.program_id(0)

    @pl.when(pl.program_id(1) == 0)
    def _():
        _load_table_half(u_hbm, tbuf, sem, half)

    @pl.when((half == 0) & (pl.program_id(1) == 0))
    def _():
        z_ref[...] = jnp.zeros(z_ref.shape, F32)

    sub = lax.broadcasted_iota(jnp.int32, (SUBLANES, LANES), 0)

    def tok_body(t, _):
        xt = x_ref[t]

        def grp_body(g, tiles):
            nxt = _load_pair_group(tbuf, idx_ref, t, g + 1)
            kg = pl.multiple_of(t * _NPAIR + g * _PAIR_GROUP, _PAIR_GROUP)
            prods = [tl * xt for tl in tiles]
            for q in range(_PAIR_GROUP // SUBLANES):
                z_ref[pl.ds(kg + q * SUBLANES, SUBLANES), :] = _fold8(prods[q * SUBLANES:(q + 1) * SUBLANES], sub)
            return nxt

        g0, g1 = _group_range(half, ng_ref[0, 0, t])
        lax.fori_loop(g0, g1, grp_body, _load_pair_group(tbuf, idx_ref, t, g0))
        return 0

    lax.fori_loop(0, _TBP, tok_body, 0)

    ones = jnp.ones((LANES, LANES), BF16)
    eye = (lax.broadcasted_iota(jnp.int32, (_NPAIR, LANES), 0)
           == lax.broadcasted_iota(jnp.int32, (_NPAIR, LANES), 1))

    def chunk_body(c, _):
        r0 = pl.multiple_of(c * SUBLANES * _NPAIR, SUBLANES * _NPAIR)
        z = z_ref[pl.ds(r0, SUBLANES * _NPAIR), :]
        zh = z.astype(BF16)
        zl = (z - zh.astype(F32)).astype(BF16)
        hv = jnp.dot(zh, ones, preferred_element_type=F32) + jnp.dot(zl, ones, preferred_element_type=F32)
        rows = [jnp.sum(jnp.where(eye, hv[tt * _NPAIR:(tt + 1) * _NPAIR], 0.0), axis=0, keepdims=True)
                for tt in range(SUBLANES)]
        o_ref[0, pl.ds(pl.multiple_of(c * SUBLANES, SUBLANES), SUBLANES), :] = jnp.concatenate(rows, axis=0)
        return 0

    lax.fori_loop(0, _TBP // SUBLANES, chunk_body, 0)


def _peer_u(idx, ngrp, xn_tiles, u_tbl):
    N = xn_tiles.shape[0] // _ROW_TILE
    nt = N // _TBP
    return pl.pallas_call(
        _peer_u_kernel,
        grid=(2, nt),
        in_specs=[_flat_smem_spec(nt, _NPAIR), _flat_smem_spec(nt, 1),
                  pl.BlockSpec((_TBP, _ROW_TILE, LANES), lambda hf, i: (i, 0, 0)),
                  pl.BlockSpec(memory_space=pl.ANY)],
        out_specs=pl.BlockSpec((1, _TBP, _NPAIR), lambda hf, i: (hf, i, 0)),
        out_shape=jax.ShapeDtypeStruct((2, N, _NPAIR), F32),
        scratch_shapes=[pltpu.VMEM(((PEER_HALF_EXPERTS + 1) * _ROW_TILE, LANES), F32),
                        pltpu.SemaphoreType.DMA(()),
                        pltpu.VMEM((_TBP * _NPAIR, LANES), F32)],
        name="peer_u",
        compiler_params=_cparams(("arbitrary", "arbitrary")),
    )(_flat_blocks(idx, nt), _flat_blocks(ngrp, nt), xn_tiles.reshape(N, _ROW_TILE, LANES),
      _rows_to_tiles(u_tbl))


def _peer_coef_kernel(hv_ref, gate_ref, o_ref):
    hv = hv_ref[...]
    o_ref[...] = gate_ref[...] * (0.5 * hv * (1.0 + lax.erf(hv * (0.5 ** 0.5))))


def _peer_coef(hval, gate):
    N = gate.shape[1]
    tm = 1024 if N % 1024 == 0 else _TBP
    spec = pl.BlockSpec((2, tm, _NPAIR), lambda i: (0, i, 0))
    return pl.pallas_call(
        _peer_coef_kernel,
        grid=(N // tm,),
        in_specs=[spec, spec],
        out_specs=spec,
        out_shape=jax.ShapeDtypeStruct((2, N, _NPAIR), F32),
        name="peer_coef",
        compiler_params=_cparams(("arbitrary",)),
    )(hval, gate)


def _peer_v_kernel(idx_ref, ng_ref, coef_ref, v_hbm, o_ref, tbuf, sem):
    half = pl.program_id(0)

    @pl.when(pl.program_id(1) == 0)
    def _():
        _load_table_half(v_hbm, tbuf, sem, half)

    n_acc = 4

    def weighted_group(t, g):
        g = jnp.minimum(g, _NPAIR // _PAIR_GROUP - 1)
        kg = pl.multiple_of(t * _NPAIR + g * _PAIR_GROUP, _PAIR_GROUP)
        return tuple(coef_ref[0, 0, kg + s] * _table_row(tbuf, idx_ref, kg + s) for s in range(_PAIR_GROUP))

    def tok_body(t, _):
        def grp_body(g, carry):
            prods, accs = carry
            nxt = weighted_group(t, g + 1)
            accs = list(accs)
            for s in range(_PAIR_GROUP):
                accs[s % n_acc] = accs[s % n_acc] + prods[s]
            return nxt, tuple(accs)

        zero = jnp.zeros((_ROW_TILE, LANES), F32)
        g0, g1 = _group_range(half, ng_ref[0, 0, t])
        _, accs = lax.fori_loop(g0, g1, grp_body, (weighted_group(t, g0), (zero,) * n_acc))
        o_ref[0, t] = (accs[0] + accs[1]) + (accs[2] + accs[3])
        return 0

    lax.fori_loop(0, _TBP, tok_body, 0)


def _peer_v(idx, ngrp, coef, v_tbl):
    N = coef.shape[1]
    nt = N // _TBP
    out = pl.pallas_call(
        _peer_v_kernel,
        grid=(2, nt),
        in_specs=[_flat_smem_spec(nt, _NPAIR), _flat_smem_spec(nt, 1), _flat_smem_spec(nt, _NPAIR),
                  pl.BlockSpec(memory_space=pl.ANY)],
        out_specs=pl.BlockSpec((1, _TBP, _ROW_TILE, LANES), lambda hf, i: (hf, i, 0, 0)),
        out_shape=jax.ShapeDtypeStruct((2, N, _ROW_TILE, LANES), F32),
        scratch_shapes=[pltpu.VMEM(((PEER_HALF_EXPERTS + 1) * _ROW_TILE, LANES), F32),
                        pltpu.SemaphoreType.DMA(())],
        name="peer_v",
        compiler_params=_cparams(("arbitrary", "arbitrary")),
    )(_flat_blocks(idx, nt), _flat_blocks(ngrp, nt), _flat_blocks(coef, nt), _rows_to_tiles(v_tbl))
    return out.reshape(2, N * _ROW_TILE, LANES)


def _final_kernel(h_ref, y_ref, w_ref, o_ref, *, normalize):
    T = h_ref.shape[0]
    y = jnp.concatenate(
        [y_ref[0, pl.ds(c, T, stride=SUBLANES), :] + y_ref[1, pl.ds(c, T, stride=SUBLANES), :]
         for c in range(D_MODEL // LANES)], axis=1)
    h = h_ref[...] + y
    if normalize:
        ms = jnp.mean(h * h, axis=-1, keepdims=True)
        h = h * lax.rsqrt(ms + NORM_EPS) * w_ref[...]
    o_ref[...] = h


def _final(h, ypart, w):
    N = h.shape[0]
    tm = 512 if N % 512 == 0 else _TBP
    normalize = w is not None
    if not normalize:
        w = jnp.ones((D_MODEL,), F32)
    return pl.pallas_call(
        functools.partial(_final_kernel, normalize=normalize),
        grid=(N // tm,),
        in_specs=[pl.BlockSpec((tm, D_MODEL), lambda i: (i, 0)),
                  pl.BlockSpec((2, tm * SUBLANES, LANES), lambda i: (0, i, 0)),
                  pl.BlockSpec((1, D_MODEL), lambda i: (0, 0))],
        out_specs=pl.BlockSpec((tm, D_MODEL), lambda i: (i, 0)),
        out_shape=jax.ShapeDtypeStruct((N, D_MODEL), F32),
        name="final_norm",
        compiler_params=_cparams(("arbitrary",)),
    )(h, ypart, w.reshape(1, D_MODEL))


_TM_IN = 256


def kernel(x_prompt, x_sample, cache_k, cache_v, cache_idx_k, state_ret, page_table, norm_attn_w, w_in, ret_gn_w,
           w_out, norm_ffn_w, peer_w_q, peer_sub_keys_1, peer_sub_keys_2, peer_u, peer_v, final_norm_w):
    B, S, D = x_prompt.shape
    Bd, T, _ = x_sample.shape
    depth = w_in.shape[0]
    n_pages = page_table.shape[1]
    past_len = n_pages * PAGE_SIZE
    n_phys = cache_k.shape[1]
    Np, Ns = B * S, Bd * SROWS
    assert D == D_MODEL and T <= SROWS and S % _TM_IN == 0 and Ns % _TM_IN == 0 and S % _CK == 0

    hs_pad = jnp.pad(x_sample, ((0, 0), (0, SROWS - T), (0, 0)))
    h_all = jnp.concatenate([x_prompt.reshape(Np, D), hs_pad.reshape(Ns, D)], axis=0)

    pos_s = past_len + jnp.arange(SROWS)
    pos = jnp.concatenate([jnp.arange(S), jnp.tile(pos_s, _TM_IN // SROWS)])
    tab_att, tab_ret = _rope_tables(pos)
    tiles_per_seq = S // _TM_IN
    tab_blocks = [i % tiles_per_seq for i in range(Np // _TM_IN)] + [tiles_per_seq] * (Ns // _TM_IN)
    ret_consts_p = _ret_constants(RET_CHUNK)
    ret_consts_s = _ret_constants(T)

    outs = {n: [] for n in ("kp", "vp", "ikp", "sp", "ks", "vs", "iks", "ss")}
    for l in range(depth):
        m = _in_proj(h_all, norm_attn_w[l], _pad_w_in(w_in[l]), tab_att, tab_ret, tab_blocks, _TM_IN)
        att_p = _prompt_dsa(m["qi"], m["kiwi"], m["qa"], m["ka"], m["va"], B, S)
        att_s = _sample_dsa(page_table, m["qi"], m["kiwi"], m["qa"], m["ka"], m["va"],
                            cache_idx_k[l], cache_k[l].reshape(n_phys, PAGE_SIZE, KV_WIDTH),
                            cache_v[l].reshape(n_phys, PAGE_SIZE, KV_WIDTH), T, Np)
        ret_p, s_p = _retention(m["qr"], m["kr"], m["vr"], m["gr"], ret_gn_w[l],
                                jnp.zeros((B, RET_HEADS, RET_HEAD_DIM, RET_HEAD_DIM), F32),
                                ret_consts_p, B, S // RET_CHUNK, RET_CHUNK)
        ret_s, s_s = _retention(m["qr"], m["kr"], m["vr"], m["gr"], ret_gn_w[l], state_ret[l].astype(F32),
                                ret_consts_s, Bd, 1, SROWS, row0=Np)
        h_mid, xn, idx, gate, ngrp = _post_mix(
            h_all, att_p, att_s, ret_p, ret_s, w_out[l].astype(BF16), norm_ffn_w[l],
            peer_w_q[l].T.astype(BF16), peer_sub_keys_1[l].astype(BF16), peer_sub_keys_2[l].astype(BF16))
        coef = _peer_coef(_peer_u(idx, ngrp, xn, peer_u[l]), gate)
        ypart = _peer_v(idx, ngrp, coef, peer_v[l])
        last = l == depth - 1
        h_all = _final(h_mid, ypart, final_norm_w if last else None)

        sample = lambda a, w: a[Np:].reshape(Bd, SROWS, w)[:, :T]
        outs["kp"].append(m["ka"][:Np].reshape(B, S, KV_HEADS, ATT_HEAD_DIM))
        outs["vp"].append(m["va"][:Np].reshape(B, S, KV_HEADS, ATT_HEAD_DIM))
        outs["ikp"].append(m["kiwi"][:Np, :IDX_DIM].reshape(B, S, IDX_DIM))
        outs["sp"].append(s_p.astype(x_prompt.dtype))
        outs["ks"].append(sample(m["ka"], KV_WIDTH).reshape(Bd, T, KV_HEADS, ATT_HEAD_DIM))
        outs["vs"].append(sample(m["va"], KV_WIDTH).reshape(Bd, T, KV_HEADS, ATT_HEAD_DIM))
        outs["iks"].append(sample(m["kiwi"], _KIWI_WIDTH)[:, :, :IDX_DIM])
        outs["ss"].append(s_s.astype(state_ret.dtype))

    y_prompt = h_all[:Np].reshape(B, S, D)
    y_sample = h_all[Np:].reshape(Bd, SROWS, D)[:, :T]
    st = lambda n: jnp.stack(outs[n])
    return (y_prompt, y_sample, st("kp"), st("vp"), st("ikp"), st("sp"),
            st("ks"), st("vs"), st("iks"), st("ss"))
```

```python
import functools

import jax
import jax.numpy as jnp
import numpy as np
from jax import lax
from jax.experimental import pallas as pl
from jax.experimental.pallas import tpu as pltpu

F32 = jnp.float32
BF16 = jnp.bfloat16

D_MODEL = 1024
PAGE_SIZE = 128
ATT_HEADS = 8
ATT_HEAD_DIM = 64
KV_HEADS = 2
ATT_GROUP = ATT_HEADS // KV_HEADS
ATT_WIDTH = ATT_HEADS * ATT_HEAD_DIM
KV_WIDTH = KV_HEADS * ATT_HEAD_DIM
ROPE_THETA = 500000.0
ATT_ROPE_DIMS = ATT_HEAD_DIM // 4
IDX_HEADS = 8
IDX_DIM = 64
IDX_WIDTH = IDX_HEADS * IDX_DIM
TOPK_MAX = 256
RET_HEADS = 4
RET_HEAD_DIM = 128
RET_WIDTH = RET_HEADS * RET_HEAD_DIM
RET_ROPE_THETA = 10000.0
RET_CHUNK = 128
PEER_HEADS = 8
PEER_NKEYS = 128
PEER_EXPERTS = PEER_NKEYS * PEER_NKEYS
PEER_KEY_DIM = 128
PEER_HALF = PEER_KEY_DIM // 2
PEER_TOPK = 16
NORM_EPS = 1e-6
GN_EPS = 1e-6

LANES = 128
SUBLANES = 8
VMEM_LIMIT_BYTES = 56 * 1024 * 1024

NEG_BIG = -1e30

_IN_SPLITS = (ATT_WIDTH, KV_WIDTH, KV_WIDTH, IDX_WIDTH, IDX_DIM, IDX_HEADS,
              RET_WIDTH, RET_WIDTH, RET_WIDTH, RET_WIDTH)
_KIWI_WIDTH = LANES
_PAD_OFFS = {}
_off = 0
for _name, _w in (("qa", ATT_WIDTH), ("ka", KV_WIDTH), ("va", KV_WIDTH), ("qi", IDX_WIDTH),
                  ("kiwi", _KIWI_WIDTH), ("qr", RET_WIDTH), ("kr", RET_WIDTH),
                  ("vr", RET_WIDTH), ("gr", RET_WIDTH)):
    _PAD_OFFS[_name] = (_off, _w)
    _off += _w
IN_WIDTH_PADDED = _off


def _cparams(sem):
    return pltpu.CompilerParams(dimension_semantics=sem, vmem_limit_bytes=VMEM_LIMIT_BYTES)


def _dot(a, b):
    return jnp.dot(a.astype(BF16), b.astype(BF16), preferred_element_type=F32)


def _dot_nt(a, b):
    return lax.dot_general(a.astype(BF16), b.astype(BF16), (((1,), (1,)), ((), ())),
                           preferred_element_type=F32)


def _pad_w_in(w_in_l):
    cols = []
    off = 0
    parts = []
    for n in _IN_SPLITS:
        parts.append(w_in_l[:, off:off + n])
        off += n
    qa, ka, va, qi, ki, wi, qr, kr, vr, gr = parts
    kiwi = jnp.concatenate(
        [ki, wi, jnp.zeros((w_in_l.shape[0], _KIWI_WIDTH - IDX_DIM - IDX_HEADS), w_in_l.dtype)], axis=1)
    cols = [qa, ka, va, qi, kiwi, qr, kr, vr, gr]
    return jnp.concatenate(cols, axis=1).astype(BF16)


def _rope_tables(pos):
    posf = pos.astype(F32)
    half = ATT_ROPE_DIMS // 2
    inv = 1.0 / (ROPE_THETA ** (jnp.arange(half, dtype=F32) / half))
    ang = posf[:, None] * inv[None, :]
    cos, sin = jnp.cos(ang), jnp.sin(ang)
    P = pos.shape[0]
    one = jnp.ones((P, ATT_HEAD_DIM - ATT_ROPE_DIMS), F32)
    zero = jnp.zeros((P, ATT_HEAD_DIM - ATT_ROPE_DIMS), F32)
    zh = jnp.zeros((P, half), F32)
    c_head = jnp.concatenate([cos, cos, one], axis=1)
    s1_head = jnp.concatenate([-sin, zh, zero], axis=1)
    s2_head = jnp.concatenate([zh, sin, zero], axis=1)
    rep = LANES // ATT_HEAD_DIM
    att = jnp.stack([jnp.tile(c_head, (1, rep)), jnp.tile(s1_head, (1, rep)), jnp.tile(s2_head, (1, rep))])
    halfr = RET_HEAD_DIM // 2
    invr = 1.0 / (RET_ROPE_THETA ** (jnp.arange(halfr, dtype=F32) / halfr))
    angr = posf[:, None] * invr[None, :]
    cr, sr = jnp.cos(angr), jnp.sin(angr)
    ret = jnp.stack([jnp.concatenate([cr, cr], axis=1), jnp.concatenate([-sr, sr], axis=1)])
    return att, ret


def _in_proj_kernel(x_ref, nw_ref, w_ref, ta_ref, tr_ref,
                    qa_ref, ka_ref, va_ref, qi_ref, kiwi_ref, qr_ref, kr_ref, vr_ref, gr_ref):
    x = x_ref[...]
    ms = jnp.mean(x * x, axis=-1, keepdims=True)
    xn = (x * lax.rsqrt(ms + NORM_EPS) * nw_ref[...]).astype(BF16)
    ca, s1a, s2a = ta_ref[0], ta_ref[1], ta_ref[2]
    cr, sr = tr_ref[0], tr_ref[1]

    def proj(name):
        off, w = _PAD_OFFS[name]
        return jnp.dot(xn, w_ref[:, off:off + w], preferred_element_type=F32)

    def rope_att_chunk(vc):
        return (vc * ca + pltpu.roll(vc, LANES - ATT_ROPE_DIMS // 2, 1) * s1a
                + pltpu.roll(vc, ATT_ROPE_DIMS // 2, 1) * s2a)

    def rope_ret_chunk(vc):
        return vc * cr + pltpu.roll(vc, RET_HEAD_DIM // 2, 1) * sr

    def per_chunk(v, fn):
        n = v.shape[1] // LANES
        return jnp.concatenate([fn(v[:, c * LANES:(c + 1) * LANES]) for c in range(n)], axis=1)

    qa_ref[...] = per_chunk(proj("qa"), rope_att_chunk)
    ka_ref[...] = per_chunk(proj("ka"), rope_att_chunk)
    va_ref[...] = proj("va")
    qi_ref[...] = per_chunk(proj("qi"), rope_att_chunk)
    kiwi = proj("kiwi")
    lane = lax.broadcasted_iota(jnp.int32, kiwi.shape, 1)
    kiwi_ref[...] = jnp.where(lane < IDX_DIM, rope_att_chunk(kiwi), kiwi * (IDX_HEADS ** -0.5))
    qr_ref[...] = per_chunk(proj("qr"), rope_ret_chunk)
    kr_ref[...] = per_chunk(proj("kr"), rope_ret_chunk) * (RET_HEAD_DIM ** -0.5)
    vr_ref[...] = proj("vr")
    gr_ref[...] = proj("gr")


def _in_proj(x, norm_w, w_pad, tab_att, tab_ret, tab_block_of_tile, tm):
    N = x.shape[0]
    nt = N // tm
    tab_idx = jnp.asarray(tab_block_of_tile, jnp.int32)
    names = ("qa", "ka", "va", "qi", "kiwi", "qr", "kr", "vr", "gr")
    out_shape = [jax.ShapeDtypeStruct((N, _PAD_OFFS[n][1]), F32) for n in names]
    out_specs = [pl.BlockSpec((tm, _PAD_OFFS[n][1]), lambda i, t: (i, 0)) for n in names]
    grid_spec = pltpu.PrefetchScalarGridSpec(
        num_scalar_prefetch=1,
        grid=(nt,),
        in_specs=[
            pl.BlockSpec((tm, D_MODEL), lambda i, t: (i, 0)),
            pl.BlockSpec((1, D_MODEL), lambda i, t: (0, 0)),
            pl.BlockSpec((D_MODEL, IN_WIDTH_PADDED), lambda i, t: (0, 0)),
            pl.BlockSpec((3, tm, LANES), lambda i, t: (0, t[i], 0)),
            pl.BlockSpec((2, tm, LANES), lambda i, t: (0, t[i], 0)),
        ],
        out_specs=out_specs,
    )

    def body(t_ref, *refs):
        _in_proj_kernel(*refs)

    outs = pl.pallas_call(
        body, grid_spec=grid_spec, out_shape=out_shape, name="in_proj",
        compiler_params=_cparams(("arbitrary",)),
    )(tab_idx, x, norm_w.reshape(1, D_MODEL), w_pad, tab_att, tab_ret)
    return dict(zip(names, outs))


def _ret_constants(c_eff):
    C = RET_CHUNK
    lg = jnp.log1p(-(2.0 ** (-5.0 - jnp.arange(RET_HEADS, dtype=F32))))
    i = jnp.arange(C, dtype=F32)
    diff = i[:, None] - i[None, :]
    dmask = jnp.where(diff >= 0, jnp.exp(lg[:, None, None] * jnp.maximum(diff, 0.0)), 0.0)
    real = (i < c_eff)
    dmask = jnp.where(real[None, :, None] & real[None, None, :], dmask, 0.0)
    q_dec = jnp.exp(lg[:, None] * (i[None, :] + 1.0))
    k_dec = jnp.where(real[None, :], jnp.exp(lg[:, None] * (c_eff - 1.0 - i[None, :])), 0.0)
    chunk_dec = jnp.exp(lg * c_eff)
    bc = lambda a: jnp.broadcast_to(a[:, :, None], (RET_HEADS, C, C))
    cdec = jnp.broadcast_to(chunk_dec[:, None, None], (RET_HEADS, C, C))
    return dmask, bc(q_dec), bc(k_dec), cdec


def _retention_kernel(q_ref, k_ref, v_ref, g_ref, gnw_ref, s0_ref, dm_ref, qd_ref, kd_ref, cd_ref,
                      o_ref, s_out_ref, state_ref, *, rows):
    c = pl.program_id(1)
    nc = pl.num_programs(1)

    @pl.when(c == 0)
    def _():
        state_ref[...] = s0_ref[0]

    def padded(ref):
        v = ref[...]
        if rows < RET_CHUNK:
            v = jnp.concatenate([v, jnp.zeros((RET_CHUNK - rows, v.shape[1]), v.dtype)], axis=0)
        return v

    q, k, v, g = padded(q_ref), padded(k_ref), padded(v_ref), g_ref[...]
    gnw = gnw_ref[...]
    outs = []
    for h in range(RET_HEADS):
        sl = slice(h * RET_HEAD_DIM, (h + 1) * RET_HEAD_DIM)
        qh, kh, vh = q[:, sl], k[:, sl], v[:, sl]
        st = state_ref[h]
        att = _dot_nt(qh, kh) * dm_ref[h]
        o = _dot(att, vh) + _dot(qh, st) * qd_ref[h]
        state_ref[h] = st * cd_ref[h] + _dot((kh * kd_ref[h]).T, vh)
        o = o[:rows]
        mu = jnp.mean(o, axis=-1, keepdims=True)
        var = jnp.mean(jnp.square(o - mu), axis=-1, keepdims=True)
        outs.append((o - mu) * lax.rsqrt(var + GN_EPS) * gnw[:, sl])
    on = jnp.concatenate(outs, axis=1)
    o_ref[...] = g * (1.0 / (1.0 + jnp.exp(-g))) * on

    @pl.when(c == nc - 1)
    def _():
        s_out_ref[0] = state_ref[...]


def _retention(q, k, v, g, gn_w, s0, consts, nb, nc, rows, row0=0):
    blk0 = row0 // rows
    in_row = pl.BlockSpec((rows, RET_WIDTH), lambda b, c: (blk0 + b * nc + c, 0))
    out_row = pl.BlockSpec((rows, RET_WIDTH), lambda b, c: (b * nc + c, 0))
    st_spec = pl.BlockSpec((1, RET_HEADS, RET_HEAD_DIM, RET_HEAD_DIM), lambda b, c: (b, 0, 0, 0))
    const_spec = pl.BlockSpec((RET_HEADS, RET_CHUNK, RET_CHUNK), lambda b, c: (0, 0, 0))
    return pl.pallas_call(
        functools.partial(_retention_kernel, rows=rows),
        grid=(nb, nc),
        in_specs=[in_row, in_row, in_row, in_row,
                  pl.BlockSpec((1, RET_WIDTH), lambda b, c: (0, 0)),
                  st_spec, const_spec, const_spec, const_spec, const_spec],
        out_specs=[out_row, st_spec],
        out_shape=[jax.ShapeDtypeStruct((nb * nc * rows, RET_WIDTH), F32),
                   jax.ShapeDtypeStruct(s0.shape, F32)],
        scratch_shapes=[pltpu.VMEM((RET_HEADS, RET_HEAD_DIM, RET_HEAD_DIM), F32)],
        name="retention",
        compiler_params=_cparams(("arbitrary", "arbitrary")),
    )(q, k, v, g, gn_w.reshape(1, RET_WIDTH), s0, *consts)


_BISECT_PLAIN_ITERS = 26
_BISECT_MAX_ITERS = 400


def _lane_tiles(x):
    return [x[:, j * LANES:(j + 1) * LANES] for j in range(x.shape[1] // LANES)]


def _rowsum_b(x):
    return jnp.broadcast_to(jnp.sum(x, axis=1, keepdims=True), x.shape)


def _rowmax_b(x):
    return jnp.broadcast_to(jnp.max(x, axis=1, keepdims=True), x.shape)


def _rowmin_b(x):
    return jnp.broadcast_to(jnp.min(x, axis=1, keepdims=True), x.shape)


def _threshold_search(sc_ref, nck, kk, lo0, hi0, n_causal, row_active):
    R = lo0.shape[0]
    zeros = jnp.zeros((R, LANES), F32)

    def count_pass(mid, snap):
        def body(kc, carry):
            x = sc_ref[kc]
            cnt, amin, bmax = carry
            for xt in _lane_tiles(x):
                ge = xt >= mid
                cnt = cnt + jnp.where(ge, 1.0, 0.0)
                if snap:
                    amin = jnp.minimum(amin, jnp.where(ge, xt, jnp.inf))
                    bmax = jnp.maximum(bmax, jnp.where(ge, -jnp.inf, xt))
            return cnt, amin, bmax
        cnt, amin, bmax = lax.fori_loop(
            0, nck, body, (zeros, jnp.full((R, LANES), jnp.inf, F32), jnp.full((R, LANES), -jnp.inf, F32)))
        if snap:
            return _rowsum_b(cnt), _rowmin_b(amin), _rowmax_b(bmax)
        return _rowsum_b(cnt), None, None

    def not_done(c_lo, lo, hib):
        pending = row_active & (c_lo != kk) & (lo != hib)
        return jnp.max(jnp.where(pending, 1.0, 0.0)) > 0.0

    def make_step(snap):
        def step(carry):
            it, lo, hi, hib, c_lo, c_hi = carry
            mid = 0.5 * (lo + hi)
            if snap:
                mid = 0.5 * (lo + jnp.minimum(hi, hib))
                mid = jnp.where(mid > lo, mid, jnp.minimum(hi, hib))
            c, amin, bmax = count_pass(mid, snap)
            ge = c >= kk
            if snap:
                lo = jnp.where(ge, amin, lo)
                hib = jnp.where(ge, hib, bmax)
            else:
                lo = jnp.where(ge, mid, lo)
            hi = jnp.where(ge, hi, mid)
            c_lo = jnp.where(ge, c, c_lo)
            c_hi = jnp.where(ge, c_hi, c)
            return it + 1, lo, hi, hib, c_lo, c_hi
        return step

    init = (jnp.int32(0), lo0, hi0, jnp.full((R, LANES), jnp.inf, F32), n_causal, zeros)
    carry = lax.while_loop(
        lambda c: (c[0] < _BISECT_PLAIN_ITERS) & not_done(c[4], c[1], c[3]), make_step(False), init)
    carry = lax.while_loop(
        lambda c: (c[0] < _BISECT_MAX_ITERS) & not_done(c[4], c[1], c[3]), make_step(True), carry)
    _, lo, hi, hib, c_lo, c_hi = carry
    tied = row_active & (c_lo != kk)
    return lo, hi, c_hi, tied


def _select_chunk(x, lo, hi, need, tied_any, tied, prefix0):
    R, CK = x.shape
    rep = CK // LANES
    wide = lambda a: jnp.concatenate([a] * rep, axis=1)

    def plain(_):
        return jnp.where(x >= wide(lo), 1.0, 0.0), prefix0

    def with_ties(_):
        low, hiw = wide(lo), wide(hi)
        band = (x >= low) & (x < hiw)
        bandf = jnp.where(band, 1.0, 0.0)
        r_i = lax.broadcasted_iota(jnp.int32, (CK, CK), 0)
        c_i = lax.broadcasted_iota(jnp.int32, (CK, CK), 1)
        tri = jnp.where(r_i <= c_i, 1.0, 0.0).astype(BF16)
        rank = jnp.dot(bandf.astype(BF16), tri, preferred_element_type=F32) + wide(prefix0)
        take_tie = jnp.where(rank <= wide(need), bandf, 0.0)
        sel_tied = jnp.where(x >= hiw, 1.0, take_tie)
        sel = jnp.where(wide(tied) > 0.0, sel_tied, jnp.where(x >= low, 1.0, 0.0))
        return sel, prefix0 + _rowsum_b(sum(_lane_tiles(bandf)))

    return lax.cond(tied_any, with_ties, plain, 0)


_QB = 128
_CK = 512


def _prompt_dsa_kernel(qi_ref, kiwiq_ref, kiwi_ref, qa_ref, ka_ref, va_ref, o_ref,
                       sc_ref, m_ref, l_ref, acc_ref, *, topk):
    i = pl.program_id(1)
    t0 = i * _QB
    nck = (t0 + _QB + _CK - 1) // _CK
    R = _QB
    qpos = t0 + lax.broadcasted_iota(jnp.int32, (R, LANES), 0)
    lane_c = lax.broadcasted_iota(jnp.int32, (R, _CK), 1)
    qpos_c = t0 + lax.broadcasted_iota(jnp.int32, (R, _CK), 0)
    lane = lax.broadcasted_iota(jnp.int32, (R, LANES), 1)

    qi = qi_ref[...]
    wq = kiwiq_ref[...]
    q_heads = []
    w_heads = []
    for h in range(IDX_HEADS):
        qh = qi[:, h * IDX_DIM:(h + 1) * IDX_DIM] * (IDX_DIM ** -0.5)
        q_heads.append(jnp.concatenate([qh, jnp.zeros((R, LANES - IDX_DIM), F32)], axis=1).astype(BF16))
        w_heads.append(jnp.broadcast_to(wq[:, IDX_DIM + h:IDX_DIM + h + 1], (R, LANES)))

    def score_body(kc, carry):
        mn, mx = carry
        k0 = pl.multiple_of(kc * _CK, _CK)
        kc_rows = kiwi_ref[pl.ds(k0, _CK), :].astype(BF16)
        acc = jnp.zeros((R, _CK), F32)
        for h in range(IDX_HEADS):
            d = lax.dot_general(q_heads[h], kc_rows, (((1,), (1,)), ((), ())), preferred_element_type=F32)
            acc = acc + jnp.concatenate([w_heads[h]] * (_CK // LANES), axis=1) * jnp.maximum(d, 0.0)
        causal = (k0 + lane_c) <= qpos_c
        sc_ref[kc] = jnp.where(causal, acc, -jnp.inf)
        for j in range(_CK // LANES):
            a = acc[:, j * LANES:(j + 1) * LANES]
            cz = causal[:, j * LANES:(j + 1) * LANES]
            mn = jnp.minimum(mn, jnp.where(cz, a, jnp.inf))
            mx = jnp.maximum(mx, jnp.where(cz, a, -jnp.inf))
        return mn, mx

    mn, mx = lax.fori_loop(0, nck, score_body,
                           (jnp.full((R, LANES), jnp.inf, F32), jnp.full((R, LANES), -jnp.inf, F32)))
    mn, mx = _rowmin_b(mn), _rowmax_b(mx)
    n_causal = (qpos + 1).astype(F32)
    kk = jnp.minimum(n_causal, float(topk))
    hi0 = mx + (mx - mn) + 1.0
    row_active = lane >= 0
    lo, hi, c_hi, tied = _threshold_search(sc_ref, nck, kk, mn, hi0, n_causal, row_active)
    need = kk - c_hi
    tiedf = jnp.where(tied, 1.0, 0.0)
    tied_any = jnp.max(tiedf) > 0.0

    qa = qa_ref[...]
    qa_heads = []
    for h in range(ATT_HEADS):
        c = h // ATT_GROUP
        qh = qa[:, h * ATT_HEAD_DIM:(h + 1) * ATT_HEAD_DIM] * (ATT_HEAD_DIM ** -0.5)
        z = jnp.zeros((R, ATT_HEAD_DIM), F32)
        parts = [z] * KV_HEADS
        parts[c] = qh
        qa_heads.append(jnp.concatenate(parts, axis=1).astype(BF16))
    m_ref[...] = jnp.full(m_ref.shape, NEG_BIG, F32)
    l_ref[...] = jnp.zeros(l_ref.shape, F32)
    acc_ref[...] = jnp.zeros(acc_ref.shape, F32)

    def att_body(kc, prefix):
        k0 = pl.multiple_of(kc * _CK, _CK)
        kch = ka_ref[pl.ds(k0, _CK), :].astype(BF16)
        vch = va_ref[pl.ds(k0, _CK), :].astype(BF16)
        self_, prefix = _select_chunk(sc_ref[kc], lo, hi, need, tied_any, tiedf, prefix)
        selb = self_ > 0.0
        for h in range(ATT_HEADS):
            s = lax.dot_general(qa_heads[h], kch, (((1,), (1,)), ((), ())), preferred_element_type=F32)
            s = jnp.where(selb, s, NEG_BIG)
            m_old = m_ref[h]
            m_new = jnp.maximum(m_old, _rowmax_b(functools.reduce(jnp.maximum, _lane_tiles(s))))
            alpha = jnp.exp(m_old - m_new)
            p = jnp.exp(s - jnp.concatenate([m_new] * (_CK // LANES), axis=1))
            l_ref[h] = l_ref[h] * alpha + _rowsum_b(sum(_lane_tiles(p)))
            acc_ref[h] = acc_ref[h] * alpha + jnp.dot(p.astype(BF16), vch, preferred_element_type=F32)
            m_ref[h] = m_new
        return prefix

    lax.fori_loop(0, nck, att_body, jnp.zeros((R, LANES), F32))
    outs = []
    for h in range(ATT_HEADS):
        c = h // ATT_GROUP
        o = acc_ref[h] / l_ref[h]
        outs.append(o[:, c * ATT_HEAD_DIM:(c + 1) * ATT_HEAD_DIM])
    o_ref[...] = jnp.concatenate(outs, axis=1)


def _prompt_dsa(qi, kiwi, qa, ka, va, B, S):
    nq = S // _QB
    topk = min(TOPK_MAX, S // 4)
    qblk = lambda w: pl.BlockSpec((_QB, w), lambda b, i: (b * nq + i, 0))
    allk = lambda w: pl.BlockSpec((S, w), lambda b, i: (b, 0))
    return pl.pallas_call(
        functools.partial(_prompt_dsa_kernel, topk=topk),
        grid=(B, nq),
        in_specs=[qblk(IDX_WIDTH), qblk(_KIWI_WIDTH), allk(_KIWI_WIDTH), qblk(ATT_WIDTH),
                  allk(KV_WIDTH), allk(KV_WIDTH)],
        out_specs=qblk(ATT_WIDTH),
        out_shape=jax.ShapeDtypeStruct((B * S, ATT_WIDTH), F32),
        scratch_shapes=[pltpu.VMEM((S // _CK, _QB, _CK), F32),
                        pltpu.VMEM((ATT_HEADS, _QB, LANES), F32),
                        pltpu.VMEM((ATT_HEADS, _QB, LANES), F32),
                        pltpu.VMEM((ATT_HEADS, _QB, LANES), F32)],
        name="prompt_dsa",
        compiler_params=_cparams(("arbitrary", "arbitrary")),
    )(qi, kiwi, kiwi, qa, ka, va)


SROWS = SUBLANES
_CKS = 640


def _sample_dsa_kernel(pt_ref, qi_ref, kiwi_ref, qa_ref, ka_ref, va_ref, cik_hbm, ck_hbm, cv_hbm,
                       o_ref, ikbuf, kbuf, vbuf, sems, sc_ref, *, n_pages, t_real, topk):
    b = pl.program_id(0)
    nb = pl.num_programs(0)
    slot = b % 2
    past_len = n_pages * PAGE_SIZE
    L = past_len + PAGE_SIZE
    nck = L // _CKS
    R = SROWS

    def page_copies(bb, s, p):
        phys = pt_ref[bb, p]
        cols = pl.ds(pl.multiple_of(p * PAGE_SIZE, PAGE_SIZE), PAGE_SIZE)
        return (pltpu.make_async_copy(cik_hbm.at[phys], ikbuf.at[s, :, cols], sems.at[s, 0]),
                pltpu.make_async_copy(ck_hbm.at[phys], kbuf.at[s, :, cols], sems.at[s, 1]),
                pltpu.make_async_copy(cv_hbm.at[phys], vbuf.at[s, :, cols], sems.at[s, 2]))

    def start_fetch(bb, s):
        def body(p, _):
            for cp in page_copies(bb, s, p):
                cp.start()
            return 0
        lax.fori_loop(0, n_pages, body, 0)

    def wait_fetch(bb, s):
        def body(p, _):
            for cp in page_copies(bb, s, p):
                cp.wait()
            return 0
        lax.fori_loop(0, n_pages, body, 0)

    @pl.when(b == 0)
    def _():
        tail = pl.ds(past_len, PAGE_SIZE)
        for s in range(2):
            ikbuf[s, :, tail] = jnp.zeros((IDX_DIM, PAGE_SIZE), F32)
            kbuf[s, :, tail] = jnp.zeros((KV_WIDTH, PAGE_SIZE), F32)
            vbuf[s, :, tail] = jnp.zeros((KV_WIDTH, PAGE_SIZE), F32)
        start_fetch(0, 0)

    @pl.when(b + 1 < nb)
    def _():
        start_fetch(b + 1, 1 - slot)

    kiwi = kiwi_ref[...]

    def as_columns(rows):
        padded = jnp.concatenate([rows, jnp.zeros((LANES - R, LANES), F32)], axis=0)
        return padded.T[:, :R]

    new_cols = pl.ds(past_len, R)
    ikbuf[slot, :, new_cols] = as_columns(kiwi)[:IDX_DIM]
    kbuf[slot, :, new_cols] = as_columns(ka_ref[...])
    vbuf[slot, :, new_cols] = as_columns(va_ref[...])
    wait_fetch(b, slot)

    row = lax.broadcasted_iota(jnp.int32, (R, LANES), 0)
    row_c = lax.broadcasted_iota(jnp.int32, (R, _CKS), 0)
    lane_c = lax.broadcasted_iota(jnp.int32, (R, _CKS), 1)
    rep = _CKS // LANES

    qi = qi_ref[...]
    q_all = jnp.concatenate(
        [qi[:, h * IDX_DIM:(h + 1) * IDX_DIM] * (IDX_DIM ** -0.5) for h in range(IDX_HEADS)], axis=0).astype(BF16)
    w_heads = [jnp.broadcast_to(kiwi[:, IDX_DIM + h:IDX_DIM + h + 1], (R, _CKS)) for h in range(IDX_HEADS)]

    def score_body(kc, carry):
        mn, mx = carry
        k0 = pl.multiple_of(kc * _CKS, LANES)
        keys_t = ikbuf[slot, :, pl.ds(k0, _CKS)].astype(BF16)
        d = jnp.dot(q_all, keys_t, preferred_element_type=F32)
        acc = jnp.zeros((R, _CKS), F32)
        for h in range(IDX_HEADS):
            acc = acc + w_heads[h] * jnp.maximum(d[h * R:(h + 1) * R], 0.0)
        causal = (k0 + lane_c) <= (past_len + row_c)
        sc_ref[kc] = jnp.where(causal, acc, -jnp.inf)
        for j in range(rep):
            a = acc[:, j * LANES:(j + 1) * LANES]
            cz = causal[:, j * LANES:(j + 1) * LANES]
            mn = jnp.minimum(mn, jnp.where(cz, a, jnp.inf))
            mx = jnp.maximum(mx, jnp.where(cz, a, -jnp.inf))
        return mn, mx

    mn, mx = lax.fori_loop(0, nck, score_body,
                           (jnp.full((R, LANES), jnp.inf, F32), jnp.full((R, LANES), -jnp.inf, F32)))
    mn, mx = _rowmin_b(mn), _rowmax_b(mx)
    n_causal = (past_len + row + 1).astype(F32)
    kk = jnp.minimum(n_causal, float(topk))
    hi0 = mx + (mx - mn) + 1.0
    row_active = row < t_real
    lo, hi, c_hi, tied = _threshold_search(sc_ref, nck, kk, mn, hi0, n_causal, row_active)
    need = kk - c_hi
    tiedf = jnp.where(tied, 1.0, 0.0)
    tied_any = jnp.max(tiedf) > 0.0

    qa = qa_ref[...]
    q_rows = []
    for h in range(ATT_HEADS):
        c = h // ATT_GROUP
        z = jnp.zeros((R, ATT_HEAD_DIM), F32)
        parts = [z] * KV_HEADS
        parts[c] = qa[:, h * ATT_HEAD_DIM:(h + 1) * ATT_HEAD_DIM] * (ATT_HEAD_DIM ** -0.5)
        q_rows.append(jnp.concatenate(parts, axis=1))
    q_big = jnp.concatenate(q_rows, axis=0).astype(BF16)
    HR = ATT_HEADS * R

    def att_body(kc, carry):
        m_old, l_old, acc, prefix = carry
        k0 = pl.multiple_of(kc * _CKS, LANES)
        kch_t = kbuf[slot, :, pl.ds(k0, _CKS)].astype(BF16)
        vch_t = vbuf[slot, :, pl.ds(k0, _CKS)].astype(BF16)
        self_, prefix = _select_chunk(sc_ref[kc], lo, hi, need, tied_any, tiedf, prefix)
        selb = jnp.concatenate([self_] * ATT_HEADS, axis=0) > 0.0
        s = jnp.dot(q_big, kch_t, preferred_element_type=F32)
        s = jnp.where(selb, s, NEG_BIG)
        m_new = jnp.maximum(m_old, _rowmax_b(functools.reduce(jnp.maximum, _lane_tiles(s))))
        alpha = jnp.exp(m_old - m_new)
        p = jnp.exp(s - jnp.concatenate([m_new] * rep, axis=1))
        l_new = l_old * alpha + _rowsum_b(sum(_lane_tiles(p)))
        acc = acc * alpha + _dot_nt(p, vch_t)
        return m_new, l_new, acc, prefix

    init = (jnp.full((HR, LANES), NEG_BIG, F32), jnp.zeros((HR, LANES), F32), jnp.zeros((HR, LANES), F32),
            jnp.zeros((R, LANES), F32))
    _, l_fin, acc, _ = lax.fori_loop(0, nck, att_body, init)
    o = acc / l_fin
    outs = []
    for h in range(ATT_HEADS):
        c = h // ATT_GROUP
        outs.append(o[h * R:(h + 1) * R, c * ATT_HEAD_DIM:(c + 1) * ATT_HEAD_DIM])
    o_ref[...] = jnp.concatenate(outs, axis=1)


def _sample_dsa(page_table, qi, kiwi, qa, ka, va, cache_idx_k_l, cache_k_l, cache_v_l, t_real, row0):
    Bd, n_pages = page_table.shape
    past_len = n_pages * PAGE_SIZE
    L = past_len + PAGE_SIZE
    assert L % _CKS == 0 and row0 % SROWS == 0
    topk = min(TOPK_MAX, (past_len + t_real) // 4)
    blk0 = row0 // SROWS
    blk = lambda w: pl.BlockSpec((SROWS, w), lambda b, pt: (blk0 + b, 0))
    any_spec = pl.BlockSpec(memory_space=pl.ANY)
    grid_spec = pltpu.PrefetchScalarGridSpec(
        num_scalar_prefetch=1,
        grid=(Bd,),
        in_specs=[blk(IDX_WIDTH), blk(_KIWI_WIDTH), blk(ATT_WIDTH), blk(KV_WIDTH), blk(KV_WIDTH),
                  any_spec, any_spec, any_spec],
        out_specs=pl.BlockSpec((SROWS, ATT_WIDTH), lambda b, pt: (b, 0)),
        scratch_shapes=[pltpu.VMEM((2, IDX_DIM, L), F32),
                        pltpu.VMEM((2, KV_WIDTH, L), F32),
                        pltpu.VMEM((2, KV_WIDTH, L), F32),
                        pltpu.SemaphoreType.DMA((2, 3)),
                        pltpu.VMEM((L // _CKS, SROWS, _CKS), F32)],
    )
    return pl.pallas_call(
        functools.partial(_sample_dsa_kernel, n_pages=n_pages, t_real=t_real, topk=topk),
        grid_spec=grid_spec,
        out_shape=jax.ShapeDtypeStruct((Bd * SROWS, ATT_WIDTH), F32),
        name="sample_dsa",
        compiler_params=_cparams(("arbitrary",)),
    )(page_table, qi, kiwi, qa, ka, va, cache_idx_k_l, cache_k_l, cache_v_l)


_TMP = 256
_PAIR_GROUP = 16
PEER_HALF_EXPERTS = PEER_EXPERTS // 2


def _extract_topk(s, pos, n, k, payload=None):
    vals, idxs = [], []
    for _ in range(k):
        m = jnp.max(s, axis=0, keepdims=True)
        p = jnp.min(jnp.where(s == m, pos, n), axis=0, keepdims=True)
        hit = pos == p
        vals.append(m)
        if payload is None:
            idxs.append(p)
        else:
            idxs.append(jnp.max(jnp.where(hit, payload, -1), axis=0, keepdims=True))
        s = jnp.where(hit, -jnp.inf, s)
    return vals, idxs


def _post_mix_kernel(x_ref, attp_ref, atts_ref, retp_ref, rets_ref, wo_ref, nw_ref, wqt_ref, sk1_ref, sk2_ref,
                     h_ref, xn_ref, idx_ref, gate_ref, ng_ref, qt_ref, et_ref, gt_ref, la_ref, lg_ref,
                     *, n_prompt_tiles):
    T = x_ref.shape[0]
    is_prompt = pl.program_id(0) < n_prompt_tiles
    att = jnp.where(is_prompt, attp_ref[...], atts_ref[...])
    ret = jnp.where(is_prompt, retp_ref[...], rets_ref[...])
    h = x_ref[...] + _dot(att, wo_ref[:ATT_WIDTH, :]) + _dot(ret, wo_ref[ATT_WIDTH:, :])
    h_ref[...] = h
    ms = jnp.mean(h * h, axis=-1, keepdims=True)
    xn = h * lax.rsqrt(ms + NORM_EPS) * nw_ref[...]
    for c in range(D_MODEL // LANES):
        xn_ref[pl.ds(c, T, stride=SUBLANES), :] = xn[:, c * LANES:(c + 1) * LANES]
    qt_ref[...] = _dot_nt(wqt_ref[...], xn)

    pos_k = lax.broadcasted_iota(jnp.int32, (PEER_NKEYS, T), 0)
    n_b = [PEER_TOPK // (a + 1) for a in range(PEER_TOPK)]
    n_cand = -(-sum(n_b) // SUBLANES) * SUBLANES
    pos_c = lax.broadcasted_iota(jnp.int32, (n_cand, T), 0)

    def head_body(hd, _):
        q0 = pl.multiple_of(hd * PEER_KEY_DIM, PEER_KEY_DIM)
        q1 = qt_ref[pl.ds(q0, PEER_HALF), :]
        q2 = qt_ref[pl.ds(q0 + PEER_HALF, PEER_HALF), :]
        s1 = _dot(sk1_ref[...], q1)
        s2 = _dot(sk2_ref[...], q2)
        v1, i1 = _extract_topk(s1, pos_k, PEER_NKEYS, PEER_TOPK)
        v2, i2 = _extract_topk(s2, pos_k, PEER_NKEYS, PEER_TOPK)
        v2m = jnp.concatenate(v2, axis=0)
        i2m = jnp.concatenate(i2, axis=0)
        n_fill = n_cand - sum(n_b)
        cand = jnp.concatenate([v1[a] + v2m[:n_b[a]] for a in range(PEER_TOPK)]
                               + [jnp.full((n_fill, T), -jnp.inf, F32)], axis=0)
        cid = jnp.concatenate([i1[a] * PEER_NKEYS + i2m[:n_b[a]] for a in range(PEER_TOPK)]
                              + [jnp.full((n_fill, T), -1, jnp.int32)], axis=0)
        sv, eid = _extract_topk(cand, pos_c, n_cand, PEER_TOPK, payload=cid)
        svm = jnp.concatenate(sv, axis=0)
        g = jnp.exp(svm - sv[0])
        r0 = pl.multiple_of(hd * PEER_TOPK, PEER_TOPK)
        gt_ref[pl.ds(r0, PEER_TOPK), :] = g / jnp.sum(g, axis=0, keepdims=True)
        et_ref[pl.ds(r0, PEER_TOPK), :] = jnp.concatenate(eid, axis=0).astype(F32)
        return 0

    lax.fori_loop(0, PEER_HEADS, head_body, 0)

    e = et_ref[...]
    g = gt_ref[...]
    npair = e.shape[0]
    is0 = e < float(PEER_HALF_EXPERTS)
    r_i = lax.broadcasted_iota(jnp.int32, (npair, npair), 0)
    c_i = lax.broadcasted_iota(jnp.int32, (npair, npair), 1)
    tri = jnp.where(c_i <= r_i, 1.0, 0.0).astype(BF16)
    rank0 = jnp.dot(tri, jnp.where(is0, 1.0, 0.0).astype(BF16), preferred_element_type=F32)
    rowf = lax.broadcasted_iota(jnp.int32, (npair, T), 0).astype(F32)
    place = jnp.where(is0, rank0 - 1.0, (npair - 1.0) - (rowf - rank0))
    off = jnp.where(is0, e, e - float(PEER_HALF_EXPERTS)) * float(SUBLANES)
    for p in range(npair):
        m = place == float(p)
        la_ref[p:p + 1, :] = jnp.sum(jnp.where(m, off, 0.0), axis=0, keepdims=True)
        lg_ref[p:p + 1, :] = jnp.sum(jnp.where(m, g, 0.0), axis=0, keepdims=True)
    n0 = rank0[npair - 1:npair, :]
    in0 = rowf < n0
    offp, gp = la_ref[...], lg_ref[...]
    dummy = float(PEER_HALF_EXPERTS * SUBLANES)
    idx_ref[0] = jnp.where(in0, offp, dummy).T.astype(jnp.int32)
    idx_ref[1] = jnp.where(in0, dummy, offp).T.astype(jnp.int32)
    gate_ref[0] = jnp.where(in0, gp, 0.0).T
    gate_ref[1] = jnp.where(in0, 0.0, gp).T
    inv_grp = 1.0 / _PAIR_GROUP
    ng_ref[0:1, :] = jnp.floor((n0 + (_PAIR_GROUP - 1.0)) * inv_grp).astype(jnp.int32)
    ng_ref[1:2, :] = jnp.floor(((npair - n0) + (_PAIR_GROUP - 1.0)) * inv_grp).astype(jnp.int32)


def _post_mix(x, att_p, att_s, ret_p, ret_s, w_out_bf, norm_w, wq_t_bf, sk1_bf, sk2_bf):
    N = x.shape[0]
    nt = N // _TMP
    ntp = att_p.shape[0] // _TMP
    nts = att_s.shape[0] // _TMP
    assert ntp + nts == nt and ntp > 0 and nts > 0
    npair = PEER_HEADS * PEER_TOPK
    row = lambda w: pl.BlockSpec((_TMP, w), lambda i: (i, 0))
    prow = lambda w: pl.BlockSpec((_TMP, w), lambda i: (jnp.minimum(i, ntp - 1), 0))
    srow = lambda w: pl.BlockSpec((_TMP, w), lambda i: (jnp.maximum(i - ntp, 0), 0))
    full = lambda a: pl.BlockSpec(a.shape, lambda i: (0,) * a.ndim)
    nw = norm_w.reshape(1, D_MODEL)
    return pl.pallas_call(
        functools.partial(_post_mix_kernel, n_prompt_tiles=ntp),
        grid=(nt,),
        in_specs=[row(D_MODEL), prow(ATT_WIDTH), srow(ATT_WIDTH), prow(RET_WIDTH), srow(RET_WIDTH),
                  full(w_out_bf), full(nw), full(wq_t_bf), full(sk1_bf), full(sk2_bf)],
        out_specs=[row(D_MODEL), pl.BlockSpec((_TMP * SUBLANES, LANES), lambda i: (i, 0)),
                   pl.BlockSpec((2, _TMP, npair), lambda i: (0, i, 0)),
                   pl.BlockSpec((2, _TMP, npair), lambda i: (0, i, 0)),
                   pl.BlockSpec((2, _TMP), lambda i: (0, i))],
        out_shape=[jax.ShapeDtypeStruct((N, D_MODEL), F32), jax.ShapeDtypeStruct((N * SUBLANES, LANES), F32),
                   jax.ShapeDtypeStruct((2, N, npair), jnp.int32), jax.ShapeDtypeStruct((2, N, npair), F32),
                   jax.ShapeDtypeStruct((2, N), jnp.int32)],
        scratch_shapes=[pltpu.VMEM((PEER_HEADS * PEER_KEY_DIM, _TMP), F32),
                        pltpu.VMEM((npair, _TMP), F32), pltpu.VMEM((npair, _TMP), F32),
                        pltpu.VMEM((npair, _TMP), F32), pltpu.VMEM((npair, _TMP), F32)],
        name="post_mix",
        compiler_params=_cparams(("arbitrary",)),
    )(x, att_p, att_s, ret_p, ret_s, w_out_bf, nw, wq_t_bf, sk1_bf, sk2_bf)


_TBP = 128
_NPAIR = PEER_HEADS * PEER_TOPK
_ROW_TILE = D_MODEL // LANES
assert _ROW_TILE == SUBLANES


def _rows_to_tiles_kernel(t_ref, o_ref):
    n = t_ref.shape[0]
    for c in range(_ROW_TILE):
        o_ref[pl.ds(c, n, stride=_ROW_TILE), :] = t_ref[:, c * LANES:(c + 1) * LANES]


def _rows_to_tiles(tbl):
    n_exp = tbl.shape[0]
    blk = 512
    return pl.pallas_call(
        _rows_to_tiles_kernel,
        grid=(n_exp // blk,),
        in_specs=[pl.BlockSpec((blk, D_MODEL), lambda i: (i, 0))],
        out_specs=pl.BlockSpec((blk * _ROW_TILE, LANES), lambda i: (i, 0)),
        out_shape=jax.ShapeDtypeStruct((n_exp * _ROW_TILE, LANES), tbl.dtype),
        name="rows_to_tiles",
        compiler_params=_cparams(("arbitrary",)),
    )(tbl)


def _load_table_half(tbl_hbm, tbuf, sem, half):
    rows = PEER_HALF_EXPERTS * _ROW_TILE
    tbuf[pl.ds(rows, _ROW_TILE), :] = jnp.zeros((_ROW_TILE, LANES), F32)
    cp = pltpu.make_async_copy(tbl_hbm.at[pl.ds(half * rows, rows), :], tbuf.at[pl.ds(0, rows), :], sem)
    cp.start()
    cp.wait()


def _fold_pair(a, b, k, sub):
    m = (sub & k) == 0
    return jnp.where(m, a, pltpu.roll(b, k, 0)) + jnp.where(m, pltpu.roll(a, SUBLANES - k, 0), b)


def _fold8(p, sub):
    a, b, c, d, e, f, g, h = p[0], p[4], p[2], p[6], p[1], p[5], p[3], p[7]
    t1, t2, t3, t4 = (_fold_pair(a, b, 4, sub), _fold_pair(c, d, 4, sub),
                      _fold_pair(e, f, 4, sub), _fold_pair(g, h, 4, sub))
    u1, u2 = _fold_pair(t1, t2, 2, sub), _fold_pair(t3, t4, 2, sub)
    return _fold_pair(u1, u2, 1, sub)


def _flat_smem_spec(nt, per_token):
    return pl.BlockSpec((1, 1, _TBP * per_token), lambda hf, i: (hf * nt + i, 0, 0), memory_space=pltpu.SMEM)


def _flat_blocks(a, nt):
    return a.reshape(2 * nt, 1, -1)


def _table_row(tbuf, idx_ref, k):
    off = pl.multiple_of(idx_ref[0, 0, k], _ROW_TILE)
    return tbuf[pl.ds(off, _ROW_TILE), :]


def _group_range(half, ng):
    n_groups = _NPAIR // _PAIR_GROUP
    g0 = jnp.where(half == 0, 0, n_groups - ng)
    return g0, g0 + ng


def _load_pair_group(tbuf, idx_ref, t, g):
    g = jnp.minimum(g, _NPAIR // _PAIR_GROUP - 1)
    kg = pl.multiple_of(t * _NPAIR + g * _PAIR_GROUP, _PAIR_GROUP)
    return tuple(_table_row(tbuf, idx_ref, kg + s) for s in range(_PAIR_GROUP))


def _peer_u_kernel(idx_ref, ng_ref, x_ref, u_hbm, o_ref, tbuf, sem, z_ref):
    half = pl.program_id(0)

    @pl.when(pl.program_id(1) == 0)
    def _():
        _load_table_half(u_hbm, tbuf, sem, half)

    @pl.when((half == 0) & (pl.program_id(1) == 0))
    def _():
        z_ref[...] = jnp.zeros(z_ref.shape, F32)

    sub = lax.broadcasted_iota(jnp.int32, (SUBLANES, LANES), 0)

    def tok_body(t, _):
        xt = x_ref[t]

        def grp_body(g, tiles):
            nxt = _load_pair_group(tbuf, idx_ref, t, g + 1)
            kg = pl.multiple_of(t * _NPAIR + g * _PAIR_GROUP, _PAIR_GROUP)
            prods = [tl * xt for tl in tiles]
            for q in range(_PAIR_GROUP // SUBLANES):
                z_ref[pl.ds(kg + q * SUBLANES, SUBLANES), :] = _fold8(prods[q * SUBLANES:(q + 1) * SUBLANES], sub)
            return nxt

        g0, g1 = _group_range(half, ng_ref[0, 0, t])
        lax.fori_loop(g0, g1, grp_body, _load_pair_group(tbuf, idx_ref, t, g0))
        return 0

    lax.fori_loop(0, _TBP, tok_body, 0)

    ones = jnp.ones((LANES, LANES), BF16)
    eye = (lax.broadcasted_iota(jnp.int32, (_NPAIR, LANES), 0)
           == lax.broadcasted_iota(jnp.int32, (_NPAIR, LANES), 1))

    def chunk_body(c, _):
        r0 = pl.multiple_of(c * SUBLANES * _NPAIR, SUBLANES * _NPAIR)
        z = z_ref[pl.ds(r0, SUBLANES * _NPAIR), :]
        zh = z.astype(BF16)
        zl = (z - zh.astype(F32)).astype(BF16)
        hv = jnp.dot(zh, ones, preferred_element_type=F32) + jnp.dot(zl, ones, preferred_element_type=F32)
        rows = [jnp.sum(jnp.where(eye, hv[tt * _NPAIR:(tt + 1) * _NPAIR], 0.0), axis=0, keepdims=True)
                for tt in range(SUBLANES)]
        o_ref[0, pl.ds(pl.multiple_of(c * SUBLANES, SUBLANES), SUBLANES), :] = jnp.concatenate(rows, axis=0)
        return 0

    lax.fori_loop(0, _TBP // SUBLANES, chunk_body, 0)


def _peer_u(idx, ngrp, xn_tiles, u_tbl):
    N = xn_tiles.shape[0] // _ROW_TILE
    nt = N // _TBP
    return pl.pallas_call(
        _peer_u_kernel,
        grid=(2, nt),
        in_specs=[_flat_smem_spec(nt, _NPAIR), _flat_smem_spec(nt, 1),
                  pl.BlockSpec((_TBP, _ROW_TILE, LANES), lambda hf, i: (i, 0, 0)),
                  pl.BlockSpec(memory_space=pl.ANY)],
        out_specs=pl.BlockSpec((1, _TBP, _NPAIR), lambda hf, i: (hf, i, 0)),
        out_shape=jax.ShapeDtypeStruct((2, N, _NPAIR), F32),
        scratch_shapes=[pltpu.VMEM(((PEER_HALF_EXPERTS + 1) * _ROW_TILE, LANES), F32),
                        pltpu.SemaphoreType.DMA(()),
                        pltpu.VMEM((_TBP * _NPAIR, LANES), F32)],
        name="peer_u",
        compiler_params=_cparams(("arbitrary", "arbitrary")),
    )(_flat_blocks(idx, nt), _flat_blocks(ngrp, nt), xn_tiles.reshape(N, _ROW_TILE, LANES),
      _rows_to_tiles(u_tbl))


def _peer_coef_kernel(hv_ref, gate_ref, o_ref):
    hv = hv_ref[...]
    o_ref[...] = gate_ref[...] * (0.5 * hv * (1.0 + lax.erf(hv * (0.5 ** 0.5))))


def _peer_coef(hval, gate):
    N = gate.shape[1]
    tm = 1024 if N % 1024 == 0 else _TBP
    spec = pl.BlockSpec((2, tm, _NPAIR), lambda i: (0, i, 0))
    return pl.pallas_call(
        _peer_coef_kernel,
        grid=(N // tm,),
        in_specs=[spec, spec],
        out_specs=spec,
        out_shape=jax.ShapeDtypeStruct((2, N, _NPAIR), F32),
        name="peer_coef",
        compiler_params=_cparams(("arbitrary",)),
    )(hval, gate)


def _peer_v_kernel(idx_ref, ng_ref, coef_ref, v_hbm, o_ref, tbuf, sem):
    half = pl.program_id(0)

    @pl.when(pl.program_id(1) == 0)
    def _():
        _load_table_half(v_hbm, tbuf, sem, half)

    n_acc = 4

    def weighted_group(t, g):
        g = jnp.minimum(g, _NPAIR // _PAIR_GROUP - 1)
        kg = pl.multiple_of(t * _NPAIR + g * _PAIR_GROUP, _PAIR_GROUP)
        return tuple(coef_ref[0, 0, kg + s] * _table_row(tbuf, idx_ref, kg + s) for s in range(_PAIR_GROUP))

    def tok_body(t, _):
        def grp_body(g, carry):
            prods, accs = carry
            nxt = weighted_group(t, g + 1)
            accs = list(accs)
            for s in range(_PAIR_GROUP):
                accs[s % n_acc] = accs[s % n_acc] + prods[s]
            return nxt, tuple(accs)

        zero = jnp.zeros((_ROW_TILE, LANES), F32)
        g0, g1 = _group_range(half, ng_ref[0, 0, t])
        _, accs = lax.fori_loop(g0, g1, grp_body, (weighted_group(t, g0), (zero,) * n_acc))
        o_ref[0, t] = (accs[0] + accs[1]) + (accs[2] + accs[3])
        return 0

    lax.fori_loop(0, _TBP, tok_body, 0)


def _peer_v(idx, ngrp, coef, v_tbl):
    N = coef.shape[1]
    nt = N // _TBP
    out = pl.pallas_call(
        _peer_v_kernel,
        grid=(2, nt),
        in_specs=[_flat_smem_spec(nt, _NPAIR), _flat_smem_spec(nt, 1), _flat_smem_spec(nt, _NPAIR),
                  pl.BlockSpec(memory_space=pl.ANY)],
        out_specs=pl.BlockSpec((1, _TBP, _ROW_TILE, LANES), lambda hf, i: (hf, i, 0, 0)),
        out_shape=jax.ShapeDtypeStruct((2, N, _ROW_TILE, LANES), F32),
        scratch_shapes=[pltpu.VMEM(((PEER_HALF_EXPERTS + 1) * _ROW_TILE, LANES), F32),
                        pltpu.SemaphoreType.DMA(())],
        name="peer_v",
        compiler_params=_cparams(("arbitrary", "arbitrary")),
    )(_flat_blocks(idx, nt), _flat_blocks(ngrp, nt), _flat_blocks(coef, nt), _rows_to_tiles(v_tbl))
    return out.reshape(2, N * _ROW_TILE, LANES)


def _final_kernel(h_ref, y_ref, w_ref, o_ref, *, normalize):
    T = h_ref.shape[0]
    y = jnp.concatenate(
        [y_ref[0, pl.ds(c, T, stride=SUBLANES), :] + y_ref[1, pl.ds(c, T, stride=SUBLANES), :]
         for c in range(D_MODEL // LANES)], axis=1)
    h = h_ref[...] + y
    if normalize:
        ms = jnp.mean(h * h, axis=-1, keepdims=True)
        h = h * lax.rsqrt(ms + NORM_EPS) * w_ref[...]
    o_ref[...] = h


def _final(h, ypart, w):
    N = h.shape[0]
    tm = 512 if N % 512 == 0 else _TBP
    normalize = w is not None
    if not normalize:
        w = jnp.ones((D_MODEL,), F32)
    return pl.pallas_call(
        functools.partial(_final_kernel, normalize=normalize),
        grid=(N // tm,),
        in_specs=[pl.BlockSpec((tm, D_MODEL), lambda i: (i, 0)),
                  pl.BlockSpec((2, tm * SUBLANES, LANES), lambda i: (0, i, 0)),
                  pl.BlockSpec((1, D_MODEL), lambda i: (0, 0))],
        out_specs=pl.BlockSpec((tm, D_MODEL), lambda i: (i, 0)),
        out_shape=jax.ShapeDtypeStruct((N, D_MODEL), F32),
        name="final_norm",
        compiler_params=_cparams(("arbitrary",)),
    )(h, ypart, w.reshape(1, D_MODEL))


_TM_IN = 256


def _pages_feature_major(cache_l):
    n_phys, page = cache_l.shape[:2]
    return jnp.swapaxes(cache_l.reshape(n_phys, page, -1), 1, 2)


def kernel(x_prompt, x_sample, cache_k, cache_v, cache_idx_k, state_ret, page_table, norm_attn_w, w_in, ret_gn_w,
           w_out, norm_ffn_w, peer_w_q, peer_sub_keys_1, peer_sub_keys_2, peer_u, peer_v, final_norm_w):
    B, S, D = x_prompt.shape
    Bd, T, _ = x_sample.shape
    depth = w_in.shape[0]
    n_pages = page_table.shape[1]
    past_len = n_pages * PAGE_SIZE
    n_phys = cache_k.shape[1]
    Np, Ns = B * S, Bd * SROWS
    assert D == D_MODEL and T <= SROWS and S % _TM_IN == 0 and Ns % _TM_IN == 0 and S % _CK == 0

    hs_pad = jnp.pad(x_sample, ((0, 0), (0, SROWS - T), (0, 0)))
    h_all = jnp.concatenate([x_prompt.reshape(Np, D), hs_pad.reshape(Ns, D)], axis=0)

    pos_s = past_len + jnp.arange(SROWS)
    pos = jnp.concatenate([jnp.arange(S), jnp.tile(pos_s, _TM_IN // SROWS)])
    tab_att, tab_ret = _rope_tables(pos)
    tiles_per_seq = S // _TM_IN
    tab_blocks = [i % tiles_per_seq for i in range(Np // _TM_IN)] + [tiles_per_seq] * (Ns // _TM_IN)
    ret_consts_p = _ret_constants(RET_CHUNK)
    ret_consts_s = _ret_constants(T)

    outs = {n: [] for n in ("kp", "vp", "ikp", "sp", "ks", "vs", "iks", "ss")}
    for l in range(depth):
        m = _in_proj(h_all, norm_attn_w[l], _pad_w_in(w_in[l]), tab_att, tab_ret, tab_blocks, _TM_IN)
        att_p = _prompt_dsa(m["qi"], m["kiwi"], m["qa"], m["ka"], m["va"], B, S)
        att_s = _sample_dsa(page_table, m["qi"], m["kiwi"], m["qa"], m["ka"], m["va"],
                            _pages_feature_major(cache_idx_k[l]), _pages_feature_major(cache_k[l]),
                            _pages_feature_major(cache_v[l]), T, Np)
        ret_p, s_p = _retention(m["qr"], m["kr"], m["vr"], m["gr"], ret_gn_w[l],
                                jnp.zeros((B, RET_HEADS, RET_HEAD_DIM, RET_HEAD_DIM), F32),
                                ret_consts_p, B, S // RET_CHUNK, RET_CHUNK)
        ret_s, s_s = _retention(m["qr"], m["kr"], m["vr"], m["gr"], ret_gn_w[l], state_ret[l].astype(F32),
                                ret_consts_s, Bd, 1, SROWS, row0=Np)
        h_mid, xn, idx, gate, ngrp = _post_mix(
            h_all, att_p, att_s, ret_p, ret_s, w_out[l].astype(BF16), norm_ffn_w[l],
            peer_w_q[l].T.astype(BF16), peer_sub_keys_1[l].astype(BF16), peer_sub_keys_2[l].astype(BF16))
        coef = _peer_coef(_peer_u(idx, ngrp, xn, peer_u[l]), gate)
        ypart = _peer_v(idx, ngrp, coef, peer_v[l])
        last = l == depth - 1
        h_all = _final(h_mid, ypart, final_norm_w if last else None)

        sample = lambda a, w: a[Np:].reshape(Bd, SROWS, w)[:, :T]
        outs["kp"].append(m["ka"][:Np].reshape(B, S, KV_HEADS, ATT_HEAD_DIM))
        outs["vp"].append(m["va"][:Np].reshape(B, S, KV_HEADS, ATT_HEAD_DIM))
        outs["ikp"].append(m["kiwi"][:Np, :IDX_DIM].reshape(B, S, IDX_DIM))
        outs["sp"].append(s_p.astype(x_prompt.dtype))
        outs["ks"].append(sample(m["ka"], KV_WIDTH).reshape(Bd, T, KV_HEADS, ATT_HEAD_DIM))
        outs["vs"].append(sample(m["va"], KV_WIDTH).reshape(Bd, T, KV_HEADS, ATT_HEAD_DIM))
        outs["iks"].append(sample(m["kiwi"], _KIWI_WIDTH)[:, :, :IDX_DIM])
        outs["ss"].append(s_s.astype(state_ret.dtype))

    y_prompt = h_all[:Np].reshape(B, S, D)
    y_sample = h_all[Np:].reshape(Bd, SROWS, D)[:, :T]
    st = lambda n: jnp.stack(outs[n])
    return (y_prompt, y_sample, st("kp"), st("vp"), st("ikp"), st("sp"),
            st("ks"), st("vs"), st("iks"), st("ss"))
```

```python
import functools

import jax
import jax.numpy as jnp
import numpy as np
from jax import lax
from jax.experimental import pallas as pl
from jax.experimental.pallas import tpu as pltpu

F32 = jnp.float32
BF16 = jnp.bfloat16

D_MODEL = 1024
PAGE_SIZE = 128
ATT_HEADS = 8
ATT_HEAD_DIM = 64
KV_HEADS = 2
ATT_GROUP = ATT_HEADS // KV_HEADS
ATT_WIDTH = ATT_HEADS * ATT_HEAD_DIM
KV_WIDTH = KV_HEADS * ATT_HEAD_DIM
ROPE_THETA = 500000.0
ATT_ROPE_DIMS = ATT_HEAD_DIM // 4
IDX_HEADS = 8
IDX_DIM = 64
IDX_WIDTH = IDX_HEADS * IDX_DIM
TOPK_MAX = 256
RET_HEADS = 4
RET_HEAD_DIM = 128
RET_WIDTH = RET_HEADS * RET_HEAD_DIM
RET_ROPE_THETA = 10000.0
RET_CHUNK = 128
PEER_HEADS = 8
PEER_NKEYS = 128
PEER_EXPERTS = PEER_NKEYS * PEER_NKEYS
PEER_KEY_DIM = 128
PEER_HALF = PEER_KEY_DIM // 2
PEER_TOPK = 16
NORM_EPS = 1e-6
GN_EPS = 1e-6

LANES = 128
SUBLANES = 8
VMEM_LIMIT_BYTES = 56 * 1024 * 1024

NEG_BIG = -1e30

_IN_SPLITS = (ATT_WIDTH, KV_WIDTH, KV_WIDTH, IDX_WIDTH, IDX_DIM, IDX_HEADS,
              RET_WIDTH, RET_WIDTH, RET_WIDTH, RET_WIDTH)
_KIWI_WIDTH = LANES
_PAD_OFFS = {}
_off = 0
for _name, _w in (("qa", ATT_WIDTH), ("ka", KV_WIDTH), ("va", KV_WIDTH), ("qi", IDX_WIDTH),
                  ("kiwi", _KIWI_WIDTH), ("qr", RET_WIDTH), ("kr", RET_WIDTH),
                  ("vr", RET_WIDTH), ("gr", RET_WIDTH)):
    _PAD_OFFS[_name] = (_off, _w)
    _off += _w
IN_WIDTH_PADDED = _off


def _cparams(sem):
    return pltpu.CompilerParams(dimension_semantics=sem, vmem_limit_bytes=VMEM_LIMIT_BYTES)


def _dot(a, b):
    return jnp.dot(a.astype(BF16), b.astype(BF16), preferred_element_type=F32)


def _dot_nt(a, b):
    return lax.dot_general(a.astype(BF16), b.astype(BF16), (((1,), (1,)), ((), ())),
                           preferred_element_type=F32)


def _pad_w_in(w_in_l):
    cols = []
    off = 0
    parts = []
    for n in _IN_SPLITS:
        parts.append(w_in_l[:, off:off + n])
        off += n
    qa, ka, va, qi, ki, wi, qr, kr, vr, gr = parts
    kiwi = jnp.concatenate(
        [ki, wi, jnp.zeros((w_in_l.shape[0], _KIWI_WIDTH - IDX_DIM - IDX_HEADS), w_in_l.dtype)], axis=1)
    cols = [qa, ka, va, qi, kiwi, qr, kr, vr, gr]
    return jnp.concatenate(cols, axis=1).astype(BF16)


def _rope_tables(pos):
    posf = pos.astype(F32)
    half = ATT_ROPE_DIMS // 2
    inv = 1.0 / (ROPE_THETA ** (jnp.arange(half, dtype=F32) / half))
    ang = posf[:, None] * inv[None, :]
    cos, sin = jnp.cos(ang), jnp.sin(ang)
    P = pos.shape[0]
    one = jnp.ones((P, ATT_HEAD_DIM - ATT_ROPE_DIMS), F32)
    zero = jnp.zeros((P, ATT_HEAD_DIM - ATT_ROPE_DIMS), F32)
    zh = jnp.zeros((P, half), F32)
    c_head = jnp.concatenate([cos, cos, one], axis=1)
    s1_head = jnp.concatenate([-sin, zh, zero], axis=1)
    s2_head = jnp.concatenate([zh, sin, zero], axis=1)
    rep = LANES // ATT_HEAD_DIM
    att = jnp.stack([jnp.tile(c_head, (1, rep)), jnp.tile(s1_head, (1, rep)), jnp.tile(s2_head, (1, rep))])
    halfr = RET_HEAD_DIM // 2
    invr = 1.0 / (RET_ROPE_THETA ** (jnp.arange(halfr, dtype=F32) / halfr))
    angr = posf[:, None] * invr[None, :]
    cr, sr = jnp.cos(angr), jnp.sin(angr)
    ret = jnp.stack([jnp.concatenate([cr, cr], axis=1), jnp.concatenate([-sr, sr], axis=1)])
    return att, ret


def _in_proj_kernel(x_ref, nw_ref, w_ref, ta_ref, tr_ref,
                    qa_ref, ka_ref, va_ref, qi_ref, kiwi_ref, qr_ref, kr_ref, vr_ref, gr_ref):
    x = x_ref[...]
    ms = jnp.mean(x * x, axis=-1, keepdims=True)
    xn = (x * lax.rsqrt(ms + NORM_EPS) * nw_ref[...]).astype(BF16)
    ca, s1a, s2a = ta_ref[0], ta_ref[1], ta_ref[2]
    cr, sr = tr_ref[0], tr_ref[1]

    def proj(name):
        off, w = _PAD_OFFS[name]
        return jnp.dot(xn, w_ref[:, off:off + w], preferred_element_type=F32)

    def rope_att_chunk(vc):
        return (vc * ca + pltpu.roll(vc, LANES - ATT_ROPE_DIMS // 2, 1) * s1a
                + pltpu.roll(vc, ATT_ROPE_DIMS // 2, 1) * s2a)

    def rope_ret_chunk(vc):
        return vc * cr + pltpu.roll(vc, RET_HEAD_DIM // 2, 1) * sr

    def per_chunk(v, fn):
        n = v.shape[1] // LANES
        return jnp.concatenate([fn(v[:, c * LANES:(c + 1) * LANES]) for c in range(n)], axis=1)

    qa_ref[...] = per_chunk(proj("qa"), rope_att_chunk)
    ka_ref[...] = per_chunk(proj("ka"), rope_att_chunk)
    va_ref[...] = proj("va")
    qi_ref[...] = per_chunk(proj("qi"), rope_att_chunk)
    kiwi = proj("kiwi")
    lane = lax.broadcasted_iota(jnp.int32, kiwi.shape, 1)
    kiwi_ref[...] = jnp.where(lane < IDX_DIM, rope_att_chunk(kiwi), kiwi * (IDX_HEADS ** -0.5))
    qr_ref[...] = per_chunk(proj("qr"), rope_ret_chunk)
    kr_ref[...] = per_chunk(proj("kr"), rope_ret_chunk) * (RET_HEAD_DIM ** -0.5)
    vr_ref[...] = proj("vr")
    gr_ref[...] = proj("gr")


def _in_proj(x, norm_w, w_pad, tab_att, tab_ret, tab_block_of_tile, tm):
    N = x.shape[0]
    nt = N // tm
    tab_idx = jnp.asarray(tab_block_of_tile, jnp.int32)
    names = ("qa", "ka", "va", "qi", "kiwi", "qr", "kr", "vr", "gr")
    out_shape = [jax.ShapeDtypeStruct((N, _PAD_OFFS[n][1]), F32) for n in names]
    out_specs = [pl.BlockSpec((tm, _PAD_OFFS[n][1]), lambda i, t: (i, 0)) for n in names]
    grid_spec = pltpu.PrefetchScalarGridSpec(
        num_scalar_prefetch=1,
        grid=(nt,),
        in_specs=[
            pl.BlockSpec((tm, D_MODEL), lambda i, t: (i, 0)),
            pl.BlockSpec((1, D_MODEL), lambda i, t: (0, 0)),
            pl.BlockSpec((D_MODEL, IN_WIDTH_PADDED), lambda i, t: (0, 0)),
            pl.BlockSpec((3, tm, LANES), lambda i, t: (0, t[i], 0)),
            pl.BlockSpec((2, tm, LANES), lambda i, t: (0, t[i], 0)),
        ],
        out_specs=out_specs,
    )

    def body(t_ref, *refs):
        _in_proj_kernel(*refs)

    outs = pl.pallas_call(
        body, grid_spec=grid_spec, out_shape=out_shape, name="in_proj",
        compiler_params=_cparams(("arbitrary",)),
    )(tab_idx, x, norm_w.reshape(1, D_MODEL), w_pad, tab_att, tab_ret)
    return dict(zip(names, outs))


def _ret_constants(c_eff):
    C = RET_CHUNK
    lg = jnp.log1p(-(2.0 ** (-5.0 - jnp.arange(RET_HEADS, dtype=F32))))
    i = jnp.arange(C, dtype=F32)
    diff = i[:, None] - i[None, :]
    dmask = jnp.where(diff >= 0, jnp.exp(lg[:, None, None] * jnp.maximum(diff, 0.0)), 0.0)
    real = (i < c_eff)
    dmask = jnp.where(real[None, :, None] & real[None, None, :], dmask, 0.0)
    q_dec = jnp.exp(lg[:, None] * (i[None, :] + 1.0))
    k_dec = jnp.where(real[None, :], jnp.exp(lg[:, None] * (c_eff - 1.0 - i[None, :])), 0.0)
    chunk_dec = jnp.exp(lg * c_eff)
    bc = lambda a: jnp.broadcast_to(a[:, :, None], (RET_HEADS, C, C))
    cdec = jnp.broadcast_to(chunk_dec[:, None, None], (RET_HEADS, C, C))
    return dmask, bc(q_dec), bc(k_dec), cdec


def _retention_kernel(q_ref, k_ref, v_ref, g_ref, gnw_ref, s0_ref, dm_ref, qd_ref, kd_ref, cd_ref,
                      o_ref, s_out_ref, state_ref, *, rows):
    c = pl.program_id(1)
    nc = pl.num_programs(1)

    @pl.when(c == 0)
    def _():
        state_ref[...] = s0_ref[0]

    def padded(ref):
        v = ref[...]
        if rows < RET_CHUNK:
            v = jnp.concatenate([v, jnp.zeros((RET_CHUNK - rows, v.shape[1]), v.dtype)], axis=0)
        return v

    q, k, v, g = padded(q_ref), padded(k_ref), padded(v_ref), g_ref[...]
    gnw = gnw_ref[...]
    outs = []
    for h in range(RET_HEADS):
        sl = slice(h * RET_HEAD_DIM, (h + 1) * RET_HEAD_DIM)
        qh, kh, vh = q[:, sl], k[:, sl], v[:, sl]
        st = state_ref[h]
        att = _dot_nt(qh, kh) * dm_ref[h]
        o = _dot(att, vh) + _dot(qh, st) * qd_ref[h]
        state_ref[h] = st * cd_ref[h] + _dot((kh * kd_ref[h]).T, vh)
        o = o[:rows]
        mu = jnp.mean(o, axis=-1, keepdims=True)
        var = jnp.mean(jnp.square(o - mu), axis=-1, keepdims=True)
        outs.append((o - mu) * lax.rsqrt(var + GN_EPS) * gnw[:, sl])
    on = jnp.concatenate(outs, axis=1)
    o_ref[...] = g * (1.0 / (1.0 + jnp.exp(-g))) * on

    @pl.when(c == nc - 1)
    def _():
        s_out_ref[0] = state_ref[...]


def _retention(q, k, v, g, gn_w, s0, consts, nb, nc, rows, row0=0):
    blk0 = row0 // rows
    in_row = pl.BlockSpec((rows, RET_WIDTH), lambda b, c: (blk0 + b * nc + c, 0))
    out_row = pl.BlockSpec((rows, RET_WIDTH), lambda b, c: (b * nc + c, 0))
    st_spec = pl.BlockSpec((1, RET_HEADS, RET_HEAD_DIM, RET_HEAD_DIM), lambda b, c: (b, 0, 0, 0))
    const_spec = pl.BlockSpec((RET_HEADS, RET_CHUNK, RET_CHUNK), lambda b, c: (0, 0, 0))
    return pl.pallas_call(
        functools.partial(_retention_kernel, rows=rows),
        grid=(nb, nc),
        in_specs=[in_row, in_row, in_row, in_row,
                  pl.BlockSpec((1, RET_WIDTH), lambda b, c: (0, 0)),
                  st_spec, const_spec, const_spec, const_spec, const_spec],
        out_specs=[out_row, st_spec],
        out_shape=[jax.ShapeDtypeStruct((nb * nc * rows, RET_WIDTH), F32),
                   jax.ShapeDtypeStruct(s0.shape, F32)],
        scratch_shapes=[pltpu.VMEM((RET_HEADS, RET_HEAD_DIM, RET_HEAD_DIM), F32)],
        name="retention",
        compiler_params=_cparams(("arbitrary", "arbitrary")),
    )(q, k, v, g, gn_w.reshape(1, RET_WIDTH), s0, *consts)


_BISECT_PLAIN_ITERS = 26
_BISECT_MAX_ITERS = 400


def _lane_tiles(x):
    return [x[:, j * LANES:(j + 1) * LANES] for j in range(x.shape[1] // LANES)]


def _rowsum_b(x):
    return jnp.broadcast_to(jnp.sum(x, axis=1, keepdims=True), x.shape)


def _rowmax_b(x):
    return jnp.broadcast_to(jnp.max(x, axis=1, keepdims=True), x.shape)


def _rowmin_b(x):
    return jnp.broadcast_to(jnp.min(x, axis=1, keepdims=True), x.shape)


def _threshold_search(sc_ref, nck, kk, lo0, hi0, n_causal, row_active):
    R = lo0.shape[0]
    zeros = jnp.zeros((R, LANES), F32)

    def count_pass(mid, snap):
        def body(kc, carry):
            x = sc_ref[kc]
            cnt, amin, bmax = carry
            for xt in _lane_tiles(x):
                ge = xt >= mid
                cnt = cnt + jnp.where(ge, 1.0, 0.0)
                if snap:
                    amin = jnp.minimum(amin, jnp.where(ge, xt, jnp.inf))
                    bmax = jnp.maximum(bmax, jnp.where(ge, -jnp.inf, xt))
            return cnt, amin, bmax
        cnt, amin, bmax = lax.fori_loop(
            0, nck, body, (zeros, jnp.full((R, LANES), jnp.inf, F32), jnp.full((R, LANES), -jnp.inf, F32)))
        if snap:
            return _rowsum_b(cnt), _rowmin_b(amin), _rowmax_b(bmax)
        return _rowsum_b(cnt), None, None

    def not_done(c_lo, lo, hib):
        pending = row_active & (c_lo != kk) & (lo != hib)
        return jnp.max(jnp.where(pending, 1.0, 0.0)) > 0.0

    def make_step(snap):
        def step(carry):
            it, lo, hi, hib, c_lo, c_hi = carry
            mid = 0.5 * (lo + hi)
            if snap:
                mid = 0.5 * (lo + jnp.minimum(hi, hib))
                mid = jnp.where(mid > lo, mid, jnp.minimum(hi, hib))
            c, amin, bmax = count_pass(mid, snap)
            ge = c >= kk
            if snap:
                lo = jnp.where(ge, amin, lo)
                hib = jnp.where(ge, hib, bmax)
            else:
                lo = jnp.where(ge, mid, lo)
            hi = jnp.where(ge, hi, mid)
            c_lo = jnp.where(ge, c, c_lo)
            c_hi = jnp.where(ge, c_hi, c)
            return it + 1, lo, hi, hib, c_lo, c_hi
        return step

    init = (jnp.int32(0), lo0, hi0, jnp.full((R, LANES), jnp.inf, F32), n_causal, zeros)
    carry = lax.while_loop(
        lambda c: (c[0] < _BISECT_PLAIN_ITERS) & not_done(c[4], c[1], c[3]), make_step(False), init)
    carry = lax.while_loop(
        lambda c: (c[0] < _BISECT_MAX_ITERS) & not_done(c[4], c[1], c[3]), make_step(True), carry)
    _, lo, hi, hib, c_lo, c_hi = carry
    tied = row_active & (c_lo != kk)
    return lo, hi, c_hi, tied


def _select_chunk(x, lo, hi, need, tied_any, tied, prefix0):
    R, CK = x.shape
    rep = CK // LANES
    wide = lambda a: jnp.concatenate([a] * rep, axis=1)

    def plain(_):
        return jnp.where(x >= wide(lo), 1.0, 0.0), prefix0

    def with_ties(_):
        low, hiw = wide(lo), wide(hi)
        band = (x >= low) & (x < hiw)
        bandf = jnp.where(band, 1.0, 0.0)
        r_i = lax.broadcasted_iota(jnp.int32, (CK, CK), 0)
        c_i = lax.broadcasted_iota(jnp.int32, (CK, CK), 1)
        tri = jnp.where(r_i <= c_i, 1.0, 0.0).astype(BF16)
        rank = jnp.dot(bandf.astype(BF16), tri, preferred_element_type=F32) + wide(prefix0)
        take_tie = jnp.where(rank <= wide(need), bandf, 0.0)
        sel_tied = jnp.where(x >= hiw, 1.0, take_tie)
        sel = jnp.where(wide(tied) > 0.0, sel_tied, jnp.where(x >= low, 1.0, 0.0))
        return sel, prefix0 + _rowsum_b(sum(_lane_tiles(bandf)))

    return lax.cond(tied_any, with_ties, plain, 0)


_QB = 128
_CK = 512


def _col_partial(x, op):
    n_chain = 4
    g = x.shape[0] // (SUBLANES * n_chain)
    x4 = x.reshape(n_chain, g, SUBLANES, x.shape[1])
    c = [op(x4[k], axis=0) for k in range(n_chain)]
    pair = jnp.stack([c[0], c[1]]), jnp.stack([c[2], c[3]])
    return op(jnp.stack([op(pair[0], axis=0), op(pair[1], axis=0)]), axis=0)


def _col_fold(x, op):
    return op(_col_partial(x, op), axis=0, keepdims=True)


def _threshold_search_t(sc_ref, nck, kk, lo0, hi0, n_causal):
    Q = lo0.shape[1]

    def count_pass(mid, snap):
        def body(kc, carry):
            x = sc_ref[kc]
            cnt, amin, bmax = carry
            ge = x >= mid
            cnt = cnt + _col_partial(jnp.where(ge, 1.0, 0.0), jnp.sum)
            if snap:
                amin = jnp.minimum(amin, _col_partial(jnp.where(ge, x, jnp.inf), jnp.min))
                bmax = jnp.maximum(bmax, _col_partial(jnp.where(ge, -jnp.inf, x), jnp.max))
            return cnt, amin, bmax
        init = (jnp.zeros((SUBLANES, Q), F32), jnp.full((SUBLANES, Q), jnp.inf, F32),
                jnp.full((SUBLANES, Q), -jnp.inf, F32))
        cnt, amin, bmax = lax.fori_loop(0, nck, body, init)
        return (jnp.sum(cnt, axis=0, keepdims=True), jnp.min(amin, axis=0, keepdims=True),
                jnp.max(bmax, axis=0, keepdims=True))

    def not_done(c_lo, lo, hib):
        pending = (c_lo != kk) & (lo != hib)
        return jnp.max(jnp.where(pending, 1.0, 0.0)) > 0.0

    def make_iter(snap, reps):
        def one(state):
            lo, hi, hib, c_lo, c_hi = state
            if snap:
                mid = 0.5 * (lo + jnp.minimum(hi, hib))
                mid = jnp.where(mid > lo, mid, jnp.minimum(hi, hib))
            else:
                mid = 0.5 * (lo + hi)
            c, amin, bmax = count_pass(mid, snap)
            ge = c >= kk
            if snap:
                lo = jnp.where(ge, amin, lo)
                hib = jnp.where(ge, hib, bmax)
            else:
                lo = jnp.where(ge, mid, lo)
            hi = jnp.where(ge, hi, mid)
            return lo, hi, hib, jnp.where(ge, c, c_lo), jnp.where(ge, c_hi, c)

        def step(carry):
            it, state = carry[0], carry[1:]
            for _ in range(reps):
                state = one(state)
            return (it + reps, *state)
        return step

    init = (jnp.int32(0), lo0, hi0, jnp.full((1, Q), jnp.inf, F32), n_causal, jnp.zeros((1, Q), F32))
    carry = lax.while_loop(
        lambda c: (c[0] < _BISECT_PLAIN_ITERS) & not_done(c[4], c[1], c[3]), make_iter(False, 2), init)
    carry = lax.while_loop(
        lambda c: (c[0] < _BISECT_MAX_ITERS) & not_done(c[4], c[1], c[3]), make_iter(True, 1), carry)
    _, lo, hi, hib, c_lo, c_hi = carry
    return lo, hi, c_hi, c_lo != kk


def _select_chunk_t(x, lo, hi, need, tied_any, tiedf, prefix0):
    ck = x.shape[0]

    def plain(_):
        return jnp.where(x >= lo, 1.0, 0.0), prefix0

    def with_ties(_):
        bandf = jnp.where((x >= lo) & (x < hi), 1.0, 0.0)
        r_i = lax.broadcasted_iota(jnp.int32, (ck, ck), 0)
        c_i = lax.broadcasted_iota(jnp.int32, (ck, ck), 1)
        tri = jnp.where(c_i <= r_i, 1.0, 0.0).astype(BF16)
        rank = jnp.dot(tri, bandf.astype(BF16), preferred_element_type=F32) + prefix0
        take_tie = jnp.where(rank <= need, bandf, 0.0)
        sel_tied = jnp.where(x >= hi, 1.0, take_tie)
        sel = jnp.where(tiedf > 0.0, sel_tied, jnp.where(x >= lo, 1.0, 0.0))
        return sel, prefix0 + _col_fold(bandf, jnp.sum)

    return lax.cond(tied_any, with_ties, plain, 0)


def _prompt_dsa_kernel(qi_ref, kiwiq_ref, kiwi_ref, qa_ref, ka_ref, va_ref, o_ref,
                         sc_ref, vt_ref, acc_ref, *, topk, seq):
    i = pl.program_id(1)
    t0 = i * _QB
    nck = (t0 + _QB + _CK - 1) // _CK
    Q = _QB
    HQ = ATT_HEADS * Q

    @pl.when(i == 0)
    def _():
        for j in range(seq // _CK):
            vt_ref[:, j * _CK:(j + 1) * _CK] = va_ref[j * _CK:(j + 1) * _CK, :].T

    qpos = t0 + lax.broadcasted_iota(jnp.int32, (1, Q), 1)
    key_row = lax.broadcasted_iota(jnp.int32, (_CK, Q), 0)

    qi = qi_ref[...]
    w_t = kiwiq_ref[...].T
    q_heads, w_rows = [], []
    for h in range(IDX_HEADS):
        qh = qi[:, h * IDX_DIM:(h + 1) * IDX_DIM] * (IDX_DIM ** -0.5)
        q_heads.append(jnp.concatenate([qh, jnp.zeros((Q, LANES - IDX_DIM), F32)], axis=1).astype(BF16))
        w_rows.append(w_t[IDX_DIM + h:IDX_DIM + h + 1, :])

    def score_body(kc, carry):
        mn, mx = carry
        k0 = pl.multiple_of(kc * _CK, _CK)
        kch = kiwi_ref[pl.ds(k0, _CK), :].astype(BF16)
        acc = jnp.zeros((_CK, Q), F32)
        for h in range(IDX_HEADS):
            acc = acc + w_rows[h] * jnp.maximum(_dot_nt(kch, q_heads[h]), 0.0)
        causal = (k0 + key_row) <= qpos
        sc_ref[kc] = jnp.where(causal, acc, -jnp.inf)
        mn = jnp.minimum(mn, _col_fold(jnp.where(causal, acc, jnp.inf), jnp.min))
        mx = jnp.maximum(mx, _col_fold(jnp.where(causal, acc, -jnp.inf), jnp.max))
        return mn, mx

    mn, mx = lax.fori_loop(0, nck, score_body,
                           (jnp.full((1, Q), jnp.inf, F32), jnp.full((1, Q), -jnp.inf, F32)))
    n_causal = (qpos + 1).astype(F32)
    kk = jnp.minimum(n_causal, float(topk))
    lo, hi, c_hi, tied = _threshold_search_t(sc_ref, nck, kk, mn, mx + (mx - mn) + 1.0, n_causal)
    need = kk - c_hi
    tiedf = jnp.where(tied, 1.0, 0.0)
    tied_any = jnp.max(tiedf) > 0.0

    qa = qa_ref[...]
    q_rows = []
    for h in range(ATT_HEADS):
        c = h // ATT_GROUP
        parts = [jnp.zeros((Q, ATT_HEAD_DIM), F32)] * KV_HEADS
        parts[c] = qa[:, h * ATT_HEAD_DIM:(h + 1) * ATT_HEAD_DIM] * (ATT_HEAD_DIM ** -0.5)
        q_rows.append(jnp.concatenate(parts, axis=1))
    q_all = jnp.concatenate(q_rows, axis=0).astype(BF16)
    acc_ref[...] = jnp.zeros(acc_ref.shape, F32)

    def att_body(kc, carry):
        m_old, l_old, prefix = carry
        k0 = pl.multiple_of(kc * _CK, _CK)
        kch = ka_ref[pl.ds(k0, _CK), :].astype(BF16)
        vch_t = vt_ref[:, pl.ds(k0, _CK)].astype(BF16)
        sel, prefix = _select_chunk_t(sc_ref[kc], lo, hi, need, tied_any, tiedf, prefix)
        selb = jnp.concatenate([sel] * ATT_HEADS, axis=1) > 0.0
        s = jnp.where(selb, _dot_nt(kch, q_all), NEG_BIG)
        m_new = jnp.maximum(m_old, _col_fold(s, jnp.max))
        alpha = jnp.exp(m_old - m_new)
        p = jnp.exp(s - m_new)
        l_new = l_old * alpha + _col_fold(p, jnp.sum)
        acc_ref[...] = acc_ref[...] * alpha + jnp.dot(vch_t, p.astype(BF16), preferred_element_type=F32)
        return m_new, l_new, prefix

    init = (jnp.full((1, HQ), NEG_BIG, F32), jnp.zeros((1, HQ), F32), jnp.zeros((1, Q), F32))
    _, l_fin, _ = lax.fori_loop(0, nck, att_body, init)
    o_t = acc_ref[...] / l_fin
    outs = []
    for h in range(ATT_HEADS):
        c = h // ATT_GROUP
        outs.append(o_t[:, h * Q:(h + 1) * Q].T[:, c * ATT_HEAD_DIM:(c + 1) * ATT_HEAD_DIM])
    o_ref[...] = jnp.concatenate(outs, axis=1)


def _prompt_dsa(qi, kiwi, qa, ka, va, B, S):
    nq = S // _QB
    topk = min(TOPK_MAX, S // 4)
    qblk = lambda w: pl.BlockSpec((_QB, w), lambda b, i: (b * nq + i, 0))
    allk = lambda w: pl.BlockSpec((S, w), lambda b, i: (b, 0))
    return pl.pallas_call(
        functools.partial(_prompt_dsa_kernel, topk=topk, seq=S),
        grid=(B, nq),
        in_specs=[qblk(IDX_WIDTH), qblk(_KIWI_WIDTH), allk(_KIWI_WIDTH), qblk(ATT_WIDTH),
                  allk(KV_WIDTH), allk(KV_WIDTH)],
        out_specs=qblk(ATT_WIDTH),
        out_shape=jax.ShapeDtypeStruct((B * S, ATT_WIDTH), F32),
        scratch_shapes=[pltpu.VMEM((S // _CK, _CK, _QB), F32),
                        pltpu.VMEM((KV_WIDTH, S), F32),
                        pltpu.VMEM((KV_WIDTH, ATT_HEADS * _QB), F32)],
        name="prompt_dsa",
        compiler_params=_cparams(("arbitrary", "arbitrary")),
    )(qi, kiwi, kiwi, qa, ka, va)


SROWS = SUBLANES
_CKS = 640


def _sample_dsa_kernel(pt_ref, qi_ref, kiwi_ref, qa_ref, ka_ref, va_ref, cik_hbm, ck_hbm, cv_hbm,
                       o_ref, ikbuf, kbuf, vbuf, sems, sc_ref, *, n_pages, t_real, topk):
    b = pl.program_id(0)
    nb = pl.num_programs(0)
    slot = b % 2
    past_len = n_pages * PAGE_SIZE
    L = past_len + PAGE_SIZE
    nck = L // _CKS
    R = SROWS

    def page_copies(bb, s, p):
        phys = pt_ref[bb, p]
        cols = pl.ds(pl.multiple_of(p * PAGE_SIZE, PAGE_SIZE), PAGE_SIZE)
        return (pltpu.make_async_copy(cik_hbm.at[phys], ikbuf.at[s, :, cols], sems.at[s, 0]),
                pltpu.make_async_copy(ck_hbm.at[phys], kbuf.at[s, :, cols], sems.at[s, 1]),
                pltpu.make_async_copy(cv_hbm.at[phys], vbuf.at[s, :, cols], sems.at[s, 2]))

    def start_fetch(bb, s):
        def body(p, _):
            for cp in page_copies(bb, s, p):
                cp.start()
            return 0
        lax.fori_loop(0, n_pages, body, 0)

    def wait_fetch(bb, s):
        def body(p, _):
            for cp in page_copies(bb, s, p):
                cp.wait()
            return 0
        lax.fori_loop(0, n_pages, body, 0)

    @pl.when(b == 0)
    def _():
        tail = pl.ds(past_len, PAGE_SIZE)
        for s in range(2):
            ikbuf[s, :, tail] = jnp.zeros((IDX_DIM, PAGE_SIZE), F32)
            kbuf[s, :, tail] = jnp.zeros((KV_WIDTH, PAGE_SIZE), F32)
            vbuf[s, :, tail] = jnp.zeros((KV_WIDTH, PAGE_SIZE), F32)
        start_fetch(0, 0)

    @pl.when(b + 1 < nb)
    def _():
        start_fetch(b + 1, 1 - slot)

    kiwi = kiwi_ref[...]

    def as_columns(rows):
        padded = jnp.concatenate([rows, jnp.zeros((LANES - R, LANES), F32)], axis=0)
        return padded.T[:, :R]

    new_cols = pl.ds(past_len, R)
    ikbuf[slot, :, new_cols] = as_columns(kiwi)[:IDX_DIM]
    kbuf[slot, :, new_cols] = as_columns(ka_ref[...])
    vbuf[slot, :, new_cols] = as_columns(va_ref[...])
    wait_fetch(b, slot)

    row = lax.broadcasted_iota(jnp.int32, (R, LANES), 0)
    row_c = lax.broadcasted_iota(jnp.int32, (R, _CKS), 0)
    lane_c = lax.broadcasted_iota(jnp.int32, (R, _CKS), 1)
    rep = _CKS // LANES

    qi = qi_ref[...]
    q_all = jnp.concatenate(
        [qi[:, h * IDX_DIM:(h + 1) * IDX_DIM] * (IDX_DIM ** -0.5) for h in range(IDX_HEADS)], axis=0).astype(BF16)
    w_heads = [jnp.broadcast_to(kiwi[:, IDX_DIM + h:IDX_DIM + h + 1], (R, _CKS)) for h in range(IDX_HEADS)]

    def score_body(kc, carry):
        mn, mx = carry
        k0 = pl.multiple_of(kc * _CKS, LANES)
        keys_t = ikbuf[slot, :, pl.ds(k0, _CKS)].astype(BF16)
        d = jnp.dot(q_all, keys_t, preferred_element_type=F32)
        acc = jnp.zeros((R, _CKS), F32)
        for h in range(IDX_HEADS):
            acc = acc + w_heads[h] * jnp.maximum(d[h * R:(h + 1) * R], 0.0)
        causal = (k0 + lane_c) <= (past_len + row_c)
        sc_ref[kc] = jnp.where(causal, acc, -jnp.inf)
        for j in range(rep):
            a = acc[:, j * LANES:(j + 1) * LANES]
            cz = causal[:, j * LANES:(j + 1) * LANES]
            mn = jnp.minimum(mn, jnp.where(cz, a, jnp.inf))
            mx = jnp.maximum(mx, jnp.where(cz, a, -jnp.inf))
        return mn, mx

    mn, mx = lax.fori_loop(0, nck, score_body,
                           (jnp.full((R, LANES), jnp.inf, F32), jnp.full((R, LANES), -jnp.inf, F32)))
    mn, mx = _rowmin_b(mn), _rowmax_b(mx)
    n_causal = (past_len + row + 1).astype(F32)
    kk = jnp.minimum(n_causal, float(topk))
    hi0 = mx + (mx - mn) + 1.0
    row_active = row < t_real
    lo, hi, c_hi, tied = _threshold_search(sc_ref, nck, kk, mn, hi0, n_causal, row_active)
    need = kk - c_hi
    tiedf = jnp.where(tied, 1.0, 0.0)
    tied_any = jnp.max(tiedf) > 0.0

    qa = qa_ref[...]
    q_rows = []
    for h in range(ATT_HEADS):
        c = h // ATT_GROUP
        z = jnp.zeros((R, ATT_HEAD_DIM), F32)
        parts = [z] * KV_HEADS
        parts[c] = qa[:, h * ATT_HEAD_DIM:(h + 1) * ATT_HEAD_DIM] * (ATT_HEAD_DIM ** -0.5)
        q_rows.append(jnp.concatenate(parts, axis=1))
    q_big = jnp.concatenate(q_rows, axis=0).astype(BF16)
    HR = ATT_HEADS * R

    def att_body(kc, carry):
        m_old, l_old, acc, prefix = carry
        k0 = pl.multiple_of(kc * _CKS, LANES)
        kch_t = kbuf[slot, :, pl.ds(k0, _CKS)].astype(BF16)
        vch_t = vbuf[slot, :, pl.ds(k0, _CKS)].astype(BF16)
        self_, prefix = _select_chunk(sc_ref[kc], lo, hi, need, tied_any, tiedf, prefix)
        selb = jnp.concatenate([self_] * ATT_HEADS, axis=0) > 0.0
        s = jnp.dot(q_big, kch_t, preferred_element_type=F32)
        s = jnp.where(selb, s, NEG_BIG)
        m_new = jnp.maximum(m_old, _rowmax_b(functools.reduce(jnp.maximum, _lane_tiles(s))))
        alpha = jnp.exp(m_old - m_new)
        p = jnp.exp(s - jnp.concatenate([m_new] * rep, axis=1))
        l_new = l_old * alpha + _rowsum_b(sum(_lane_tiles(p)))
        acc = acc * alpha + _dot_nt(p, vch_t)
        return m_new, l_new, acc, prefix

    init = (jnp.full((HR, LANES), NEG_BIG, F32), jnp.zeros((HR, LANES), F32), jnp.zeros((HR, LANES), F32),
            jnp.zeros((R, LANES), F32))
    _, l_fin, acc, _ = lax.fori_loop(0, nck, att_body, init)
    o = acc / l_fin
    outs = []
    for h in range(ATT_HEADS):
        c = h // ATT_GROUP
        outs.append(o[h * R:(h + 1) * R, c * ATT_HEAD_DIM:(c + 1) * ATT_HEAD_DIM])
    o_ref[...] = jnp.concatenate(outs, axis=1)


def _sample_dsa(page_table, qi, kiwi, qa, ka, va, cache_idx_k_l, cache_k_l, cache_v_l, t_real, row0):
    Bd, n_pages = page_table.shape
    past_len = n_pages * PAGE_SIZE
    L = past_len + PAGE_SIZE
    assert L % _CKS == 0 and row0 % SROWS == 0
    topk = min(TOPK_MAX, (past_len + t_real) // 4)
    blk0 = row0 // SROWS
    blk = lambda w: pl.BlockSpec((SROWS, w), lambda b, pt: (blk0 + b, 0))
    any_spec = pl.BlockSpec(memory_space=pl.ANY)
    grid_spec = pltpu.PrefetchScalarGridSpec(
        num_scalar_prefetch=1,
        grid=(Bd,),
        in_specs=[blk(IDX_WIDTH), blk(_KIWI_WIDTH), blk(ATT_WIDTH), blk(KV_WIDTH), blk(KV_WIDTH),
                  any_spec, any_spec, any_spec],
        out_specs=pl.BlockSpec((SROWS, ATT_WIDTH), lambda b, pt: (b, 0)),
        scratch_shapes=[pltpu.VMEM((2, IDX_DIM, L), F32),
                        pltpu.VMEM((2, KV_WIDTH, L), F32),
                        pltpu.VMEM((2, KV_WIDTH, L), F32),
                        pltpu.SemaphoreType.DMA((2, 3)),
                        pltpu.VMEM((L // _CKS, SROWS, _CKS), F32)],
    )
    return pl.pallas_call(
        functools.partial(_sample_dsa_kernel, n_pages=n_pages, t_real=t_real, topk=topk),
        grid_spec=grid_spec,
        out_shape=jax.ShapeDtypeStruct((Bd * SROWS, ATT_WIDTH), F32),
        name="sample_dsa",
        compiler_params=_cparams(("arbitrary",)),
    )(page_table, qi, kiwi, qa, ka, va, cache_idx_k_l, cache_k_l, cache_v_l)


_TMP = 256
_PAIR_GROUP = 16
PEER_HALF_EXPERTS = PEER_EXPERTS // 2


def _extract_topk(s, pos, n, k, payload=None):
    vals, idxs = [], []
    for _ in range(k):
        m = jnp.max(s, axis=0, keepdims=True)
        p = jnp.min(jnp.where(s == m, pos, n), axis=0, keepdims=True)
        hit = pos == p
        vals.append(m)
        if payload is None:
            idxs.append(p)
        else:
            idxs.append(jnp.max(jnp.where(hit, payload, -1), axis=0, keepdims=True))
        s = jnp.where(hit, -jnp.inf, s)
    return vals, idxs


def _post_mix_kernel(x_ref, attp_ref, atts_ref, retp_ref, rets_ref, wo_ref, nw_ref, wqt_ref, sk1_ref, sk2_ref,
                     h_ref, xn_ref, idx_ref, gate_ref, ng_ref, qt_ref, et_ref, gt_ref, la_ref, lg_ref,
                     *, n_prompt_tiles):
    T = x_ref.shape[0]
    is_prompt = pl.program_id(0) < n_prompt_tiles
    att = jnp.where(is_prompt, attp_ref[...], atts_ref[...])
    ret = jnp.where(is_prompt, retp_ref[...], rets_ref[...])
    h = x_ref[...] + _dot(att, wo_ref[:ATT_WIDTH, :]) + _dot(ret, wo_ref[ATT_WIDTH:, :])
    h_ref[...] = h
    ms = jnp.mean(h * h, axis=-1, keepdims=True)
    xn = h * lax.rsqrt(ms + NORM_EPS) * nw_ref[...]
    for c in range(D_MODEL // LANES):
        xn_ref[pl.ds(c, T, stride=SUBLANES), :] = xn[:, c * LANES:(c + 1) * LANES]
    qt_ref[...] = _dot_nt(wqt_ref[...], xn)

    pos_k = lax.broadcasted_iota(jnp.int32, (PEER_NKEYS, T), 0)
    n_b = [PEER_TOPK // (a + 1) for a in range(PEER_TOPK)]
    n_cand = -(-sum(n_b) // SUBLANES) * SUBLANES
    pos_c = lax.broadcasted_iota(jnp.int32, (n_cand, T), 0)

    def head_body(hd, _):
        q0 = pl.multiple_of(hd * PEER_KEY_DIM, PEER_KEY_DIM)
        q1 = qt_ref[pl.ds(q0, PEER_HALF), :]
        q2 = qt_ref[pl.ds(q0 + PEER_HALF, PEER_HALF), :]
        s1 = _dot(sk1_ref[...], q1)
        s2 = _dot(sk2_ref[...], q2)
        v1, i1 = _extract_topk(s1, pos_k, PEER_NKEYS, PEER_TOPK)
        v2, i2 = _extract_topk(s2, pos_k, PEER_NKEYS, PEER_TOPK)
        v2m = jnp.concatenate(v2, axis=0)
        i2m = jnp.concatenate(i2, axis=0)
        n_fill = n_cand - sum(n_b)
        cand = jnp.concatenate([v1[a] + v2m[:n_b[a]] for a in range(PEER_TOPK)]
                               + [jnp.full((n_fill, T), -jnp.inf, F32)], axis=0)
        cid = jnp.concatenate([i1[a] * PEER_NKEYS + i2m[:n_b[a]] for a in range(PEER_TOPK)]
                              + [jnp.full((n_fill, T), -1, jnp.int32)], axis=0)
        sv, eid = _extract_topk(cand, pos_c, n_cand, PEER_TOPK, payload=cid)
        svm = jnp.concatenate(sv, axis=0)
        g = jnp.exp(svm - sv[0])
        r0 = pl.multiple_of(hd * PEER_TOPK, PEER_TOPK)
        gt_ref[pl.ds(r0, PEER_TOPK), :] = g / jnp.sum(g, axis=0, keepdims=True)
        et_ref[pl.ds(r0, PEER_TOPK), :] = jnp.concatenate(eid, axis=0).astype(F32)
        return 0

    lax.fori_loop(0, PEER_HEADS, head_body, 0)

    e = et_ref[...]
    g = gt_ref[...]
    npair = e.shape[0]
    is0 = e < float(PEER_HALF_EXPERTS)
    r_i = lax.broadcasted_iota(jnp.int32, (npair, npair), 0)
    c_i = lax.broadcasted_iota(jnp.int32, (npair, npair), 1)
    tri = jnp.where(c_i <= r_i, 1.0, 0.0).astype(BF16)
    rank0 = jnp.dot(tri, jnp.where(is0, 1.0, 0.0).astype(BF16), preferred_element_type=F32)
    rowf = lax.broadcasted_iota(jnp.int32, (npair, T), 0).astype(F32)
    place = jnp.where(is0, rank0 - 1.0, (npair - 1.0) - (rowf - rank0))
    off = jnp.where(is0, e, e - float(PEER_HALF_EXPERTS)) * float(SUBLANES)
    for p in range(npair):
        m = place == float(p)
        la_ref[p:p + 1, :] = jnp.sum(jnp.where(m, off, 0.0), axis=0, keepdims=True)
        lg_ref[p:p + 1, :] = jnp.sum(jnp.where(m, g, 0.0), axis=0, keepdims=True)
    n0 = rank0[npair - 1:npair, :]
    in0 = rowf < n0
    offp, gp = la_ref[...], lg_ref[...]
    dummy = float(PEER_HALF_EXPERTS * SUBLANES)
    idx_ref[0] = jnp.where(in0, offp, dummy).T.astype(jnp.int32)
    idx_ref[1] = jnp.where(in0, dummy, offp).T.astype(jnp.int32)
    gate_ref[0] = jnp.where(in0, gp, 0.0).T
    gate_ref[1] = jnp.where(in0, 0.0, gp).T
    inv_grp = 1.0 / _PAIR_GROUP
    ng_ref[0:1, :] = jnp.floor((n0 + (_PAIR_GROUP - 1.0)) * inv_grp).astype(jnp.int32)
    ng_ref[1:2, :] = jnp.floor(((npair - n0) + (_PAIR_GROUP - 1.0)) * inv_grp).astype(jnp.int32)


def _post_mix(x, att_p, att_s, ret_p, ret_s, w_out_bf, norm_w, wq_t_bf, sk1_bf, sk2_bf):
    N = x.shape[0]
    nt = N // _TMP
    ntp = att_p.shape[0] // _TMP
    nts = att_s.shape[0] // _TMP
    assert ntp + nts == nt and ntp > 0 and nts > 0
    npair = PEER_HEADS * PEER_TOPK
    row = lambda w: pl.BlockSpec((_TMP, w), lambda i: (i, 0))
    prow = lambda w: pl.BlockSpec((_TMP, w), lambda i: (jnp.minimum(i, ntp - 1), 0))
    srow = lambda w: pl.BlockSpec((_TMP, w), lambda i: (jnp.maximum(i - ntp, 0), 0))
    full = lambda a: pl.BlockSpec(a.shape, lambda i: (0,) * a.ndim)
    nw = norm_w.reshape(1, D_MODEL)
    return pl.pallas_call(
        functools.partial(_post_mix_kernel, n_prompt_tiles=ntp),
        grid=(nt,),
        in_specs=[row(D_MODEL), prow(ATT_WIDTH), srow(ATT_WIDTH), prow(RET_WIDTH), srow(RET_WIDTH),
                  full(w_out_bf), full(nw), full(wq_t_bf), full(sk1_bf), full(sk2_bf)],
        out_specs=[row(D_MODEL), pl.BlockSpec((_TMP * SUBLANES, LANES), lambda i: (i, 0)),
                   pl.BlockSpec((2, _TMP, npair), lambda i: (0, i, 0)),
                   pl.BlockSpec((2, _TMP, npair), lambda i: (0, i, 0)),
                   pl.BlockSpec((2, _TMP), lambda i: (0, i))],
        out_shape=[jax.ShapeDtypeStruct((N, D_MODEL), F32), jax.ShapeDtypeStruct((N * SUBLANES, LANES), F32),
                   jax.ShapeDtypeStruct((2, N, npair), jnp.int32), jax.ShapeDtypeStruct((2, N, npair), F32),
                   jax.ShapeDtypeStruct((2, N), jnp.int32)],
        scratch_shapes=[pltpu.VMEM((PEER_HEADS * PEER_KEY_DIM, _TMP), F32),
                        pltpu.VMEM((npair, _TMP), F32), pltpu.VMEM((npair, _TMP), F32),
                        pltpu.VMEM((npair, _TMP), F32), pltpu.VMEM((npair, _TMP), F32)],
        name="post_mix",
        compiler_params=_cparams(("arbitrary",)),
    )(x, att_p, att_s, ret_p, ret_s, w_out_bf, nw, wq_t_bf, sk1_bf, sk2_bf)


_TBP = 128
_NPAIR = PEER_HEADS * PEER_TOPK
_ROW_TILE = D_MODEL // LANES
assert _ROW_TILE == SUBLANES


def _rows_to_tiles_kernel(t_ref, o_ref):
    n = t_ref.shape[0]
    for c in range(_ROW_TILE):
        o_ref[pl.ds(c, n, stride=_ROW_TILE), :] = t_ref[:, c * LANES:(c + 1) * LANES]


def _rows_to_tiles(tbl):
    n_exp = tbl.shape[0]
    blk = 512
    return pl.pallas_call(
        _rows_to_tiles_kernel,
        grid=(n_exp // blk,),
        in_specs=[pl.BlockSpec((blk, D_MODEL), lambda i: (i, 0))],
        out_specs=pl.BlockSpec((blk * _ROW_TILE, LANES), lambda i: (i, 0)),
        out_shape=jax.ShapeDtypeStruct((n_exp * _ROW_TILE, LANES), tbl.dtype),
        name="rows_to_tiles",
        compiler_params=_cparams(("arbitrary",)),
    )(tbl)


def _load_table_half(tbl_hbm, tbuf, sem, half):
    rows = PEER_HALF_EXPERTS * _ROW_TILE
    tbuf[pl.ds(rows, _ROW_TILE), :] = jnp.zeros((_ROW_TILE, LANES), F32)
    cp = pltpu.make_async_copy(tbl_hbm.at[pl.ds(half * rows, rows), :], tbuf.at[pl.ds(0, rows), :], sem)
    cp.start()
    cp.wait()


def _fold_pair(a, b, k, sub):
    m = (sub & k) == 0
    return jnp.where(m, a, pltpu.roll(b, k, 0)) + jnp.where(m, pltpu.roll(a, SUBLANES - k, 0), b)


def _fold8(p, sub):
    a, b, c, d, e, f, g, h = p[0], p[4], p[2], p[6], p[1], p[5], p[3], p[7]
    t1, t2, t3, t4 = (_fold_pair(a, b, 4, sub), _fold_pair(c, d, 4, sub),
                      _fold_pair(e, f, 4, sub), _fold_pair(g, h, 4, sub))
    u1, u2 = _fold_pair(t1, t2, 2, sub), _fold_pair(t3, t4, 2, sub)
    return _fold_pair(u1, u2, 1, sub)


def _flat_smem_spec(nt, per_token):
    return pl.BlockSpec((1, 1, _TBP * per_token), lambda hf, i: (hf * nt + i, 0, 0), memory_space=pltpu.SMEM)


def _flat_blocks(a, nt):
    return a.reshape(2 * nt, 1, -1)


def _table_row(tbuf, idx_ref, k):
    off = pl.multiple_of(idx_ref[0, 0, k], _ROW_TILE)
    return tbuf[pl.ds(off, _ROW_TILE), :]


def _group_range(half, ng):
    n_groups = _NPAIR // _PAIR_GROUP
    g0 = jnp.where(half == 0, 0, n_groups - ng)
    return g0, g0 + ng


def _load_pair_group(tbuf, idx_ref, t, g):
    g = jnp.minimum(g, _NPAIR // _PAIR_GROUP - 1)
    kg = pl.multiple_of(t * _NPAIR + g * _PAIR_GROUP, _PAIR_GROUP)
    return tuple(_table_row(tbuf, idx_ref, kg + s) for s in range(_PAIR_GROUP))


def _peer_u_kernel(idx_ref, ng_ref, x_ref, u_hbm, o_ref, tbuf, sem, z_ref):
    half = pl.program_id(0)

    @pl.when(pl.program_id(1) == 0)
    def _():
        _load_table_half(u_hbm, tbuf, sem, half)

    @pl.when((half == 0) & (pl.program_id(1) == 0))
    def _():
        z_ref[...] = jnp.zeros(z_ref.shape, F32)

    sub = lax.broadcasted_iota(jnp.int32, (SUBLANES, LANES), 0)

    def tok_body(t, _):
        xt = x_ref[t]

        def grp_body(g, tiles):
            nxt = _load_pair_group(tbuf, idx_ref, t, g + 1)
            kg = pl.multiple_of(t * _NPAIR + g * _PAIR_GROUP, _PAIR_GROUP)
            prods = [tl * xt for tl in tiles]
            for q in range(_PAIR_GROUP // SUBLANES):
                z_ref[pl.ds(kg + q * SUBLANES, SUBLANES), :] = _fold8(prods[q * SUBLANES:(q + 1) * SUBLANES], sub)
            return nxt

        g0, g1 = _group_range(half, ng_ref[0, 0, t])
        lax.fori_loop(g0, g1, grp_body, _load_pair_group(tbuf, idx_ref, t, g0))
        return 0

    lax.fori_loop(0, _TBP, tok_body, 0)

    ones = jnp.ones((LANES, LANES), BF16)
    eye = (lax.broadcasted_iota(jnp.int32, (_NPAIR, LANES), 0)
           == lax.broadcasted_iota(jnp.int32, (_NPAIR, LANES), 1))

    def chunk_body(c, _):
        r0 = pl.multiple_of(c * SUBLANES * _NPAIR, SUBLANES * _NPAIR)
        z = z_ref[pl.ds(r0, SUBLANES * _NPAIR), :]
        zh = z.astype(BF16)
        zl = (z - zh.astype(F32)).astype(BF16)
        hv = jnp.dot(zh, ones, preferred_element_type=F32) + jnp.dot(zl, ones, preferred_element_type=F32)
        rows = [jnp.sum(jnp.where(eye, hv[tt * _NPAIR:(tt + 1) * _NPAIR], 0.0), axis=0, keepdims=True)
                for tt in range(SUBLANES)]
        o_ref[0, pl.ds(pl.multiple_of(c * SUBLANES, SUBLANES), SUBLANES), :] = jnp.concatenate(rows, axis=0)
        return 0

    lax.fori_loop(0, _TBP // SUBLANES, chunk_body, 0)


def _peer_u(idx, ngrp, xn_tiles, u_tbl):
    N = xn_tiles.shape[0] // _ROW_TILE
    nt = N // _TBP
    return pl.pallas_call(
        _peer_u_kernel,
        grid=(2, nt),
        in_specs=[_flat_smem_spec(nt, _NPAIR), _flat_smem_spec(nt, 1),
                  pl.BlockSpec((_TBP, _ROW_TILE, LANES), lambda hf, i: (i, 0, 0)),
                  pl.BlockSpec(memory_space=pl.ANY)],
        out_specs=pl.BlockSpec((1, _TBP, _NPAIR), lambda hf, i: (hf, i, 0)),
        out_shape=jax.ShapeDtypeStruct((2, N, _NPAIR), F32),
        scratch_shapes=[pltpu.VMEM(((PEER_HALF_EXPERTS + 1) * _ROW_TILE, LANES), F32),
                        pltpu.SemaphoreType.DMA(()),
                        pltpu.VMEM((_TBP * _NPAIR, LANES), F32)],
        name="peer_u",
        compiler_params=_cparams(("arbitrary", "arbitrary")),
    )(_flat_blocks(idx, nt), _flat_blocks(ngrp, nt), xn_tiles.reshape(N, _ROW_TILE, LANES),
      _rows_to_tiles(u_tbl))


def _peer_coef_kernel(hv_ref, gate_ref, o_ref):
    hv = hv_ref[...]
    o_ref[...] = gate_ref[...] * (0.5 * hv * (1.0 + lax.erf(hv * (0.5 ** 0.5))))


def _peer_coef(hval, gate):
    N = gate.shape[1]
    tm = 1024 if N % 1024 == 0 else _TBP
    spec = pl.BlockSpec((2, tm, _NPAIR), lambda i: (0, i, 0))
    return pl.pallas_call(
        _peer_coef_kernel,
        grid=(N // tm,),
        in_specs=[spec, spec],
        out_specs=spec,
        out_shape=jax.ShapeDtypeStruct((2, N, _NPAIR), F32),
        name="peer_coef",
        compiler_params=_cparams(("arbitrary",)),
    )(hval, gate)


def _peer_v_kernel(idx_ref, ng_ref, coef_ref, v_hbm, o_ref, tbuf, sem):
    half = pl.program_id(0)

    @pl.when(pl.program_id(1) == 0)
    def _():
        _load_table_half(v_hbm, tbuf, sem, half)

    n_acc = 4

    def weighted_group(t, g):
        g = jnp.minimum(g, _NPAIR // _PAIR_GROUP - 1)
        kg = pl.multiple_of(t * _NPAIR + g * _PAIR_GROUP, _PAIR_GROUP)
        return tuple(coef_ref[0, 0, kg + s] * _table_row(tbuf, idx_ref, kg + s) for s in range(_PAIR_GROUP))

    def tok_body(t, _):
        def grp_body(g, carry):
            prods, accs = carry
            nxt = weighted_group(t, g + 1)
            accs = list(accs)
            for s in range(_PAIR_GROUP):
                accs[s % n_acc] = accs[s % n_acc] + prods[s]
            return nxt, tuple(accs)

        zero = jnp.zeros((_ROW_TILE, LANES), F32)
        g0, g1 = _group_range(half, ng_ref[0, 0, t])
        _, accs = lax.fori_loop(g0, g1, grp_body, (weighted_group(t, g0), (zero,) * n_acc))
        o_ref[0, t] = (accs[0] + accs[1]) + (accs[2] + accs[3])
        return 0

    lax.fori_loop(0, _TBP, tok_body, 0)


def _peer_v(idx, ngrp, coef, v_tbl):
    N = coef.shape[1]
    nt = N // _TBP
    out = pl.pallas_call(
        _peer_v_kernel,
        grid=(2, nt),
        in_specs=[_flat_smem_spec(nt, _NPAIR), _flat_smem_spec(nt, 1), _flat_smem_spec(nt, _NPAIR),
                  pl.BlockSpec(memory_space=pl.ANY)],
        out_specs=pl.BlockSpec((1, _TBP, _ROW_TILE, LANES), lambda hf, i: (hf, i, 0, 0)),
        out_shape=jax.ShapeDtypeStruct((2, N, _ROW_TILE, LANES), F32),
        scratch_shapes=[pltpu.VMEM(((PEER_HALF_EXPERTS + 1) * _ROW_TILE, LANES), F32),
                        pltpu.SemaphoreType.DMA(())],
        name="peer_v",
        compiler_params=_cparams(("arbitrary", "arbitrary")),
    )(_flat_blocks(idx, nt), _flat_blocks(ngrp, nt), _flat_blocks(coef, nt), _rows_to_tiles(v_tbl))
    return out.reshape(2, N * _ROW_TILE, LANES)


def _final_kernel(h_ref, y_ref, w_ref, o_ref, *, normalize):
    T = h_ref.shape[0]
    y = jnp.concatenate(
        [y_ref[0, pl.ds(c, T, stride=SUBLANES), :] + y_ref[1, pl.ds(c, T, stride=SUBLANES), :]
         for c in range(D_MODEL // LANES)], axis=1)
    h = h_ref[...] + y
    if normalize:
        ms = jnp.mean(h * h, axis=-1, keepdims=True)
        h = h * lax.rsqrt(ms + NORM_EPS) * w_ref[...]
    o_ref[...] = h


def _final(h, ypart, w):
    N = h.shape[0]
    tm = 512 if N % 512 == 0 else _TBP
    normalize = w is not None
    if not normalize:
        w = jnp.ones((D_MODEL,), F32)
    return pl.pallas_call(
        functools.partial(_final_kernel, normalize=normalize),
        grid=(N // tm,),
        in_specs=[pl.BlockSpec((tm, D_MODEL), lambda i: (i, 0)),
                  pl.BlockSpec((2, tm * SUBLANES, LANES), lambda i: (0, i, 0)),
                  pl.BlockSpec((1, D_MODEL), lambda i: (0, 0))],
        out_specs=pl.BlockSpec((tm, D_MODEL), lambda i: (i, 0)),
        out_shape=jax.ShapeDtypeStruct((N, D_MODEL), F32),
        name="final_norm",
        compiler_params=_cparams(("arbitrary",)),
    )(h, ypart, w.reshape(1, D_MODEL))


_TM_IN = 256


def _pages_feature_major(cache_l):
    n_phys, page = cache_l.shape[:2]
    return jnp.swapaxes(cache_l.reshape(n_phys, page, -1), 1, 2)


def kernel(x_prompt, x_sample, cache_k, cache_v, cache_idx_k, state_ret, page_table, norm_attn_w, w_in, ret_gn_w,
           w_out, norm_ffn_w, peer_w_q, peer_sub_keys_1, peer_sub_keys_2, peer_u, peer_v, final_norm_w):
    B, S, D = x_prompt.shape
    Bd, T, _ = x_sample.shape
    depth = w_in.shape[0]
    n_pages = page_table.shape[1]
    past_len = n_pages * PAGE_SIZE
    n_phys = cache_k.shape[1]
    Np, Ns = B * S, Bd * SROWS
    assert D == D_MODEL and T <= SROWS and S % _TM_IN == 0 and Ns % _TM_IN == 0 and S % _CK == 0

    hs_pad = jnp.pad(x_sample, ((0, 0), (0, SROWS - T), (0, 0)))
    h_all = jnp.concatenate([x_prompt.reshape(Np, D), hs_pad.reshape(Ns, D)], axis=0)

    pos_s = past_len + jnp.arange(SROWS)
    pos = jnp.concatenate([jnp.arange(S), jnp.tile(pos_s, _TM_IN // SROWS)])
    tab_att, tab_ret = _rope_tables(pos)
    tiles_per_seq = S // _TM_IN
    tab_blocks = [i % tiles_per_seq for i in range(Np // _TM_IN)] + [tiles_per_seq] * (Ns // _TM_IN)
    ret_consts_p = _ret_constants(RET_CHUNK)
    ret_consts_s = _ret_constants(T)

    outs = {n: [] for n in ("kp", "vp", "ikp", "sp", "ks", "vs", "iks", "ss")}
    for l in range(depth):
        m = _in_proj(h_all, norm_attn_w[l], _pad_w_in(w_in[l]), tab_att, tab_ret, tab_blocks, _TM_IN)
        att_p = _prompt_dsa(m["qi"], m["kiwi"], m["qa"], m["ka"], m["va"], B, S)
        att_s = _sample_dsa(page_table, m["qi"], m["kiwi"], m["qa"], m["ka"], m["va"],
                            _pages_feature_major(cache_idx_k[l]), _pages_feature_major(cache_k[l]),
                            _pages_feature_major(cache_v[l]), T, Np)
        ret_p, s_p = _retention(m["qr"], m["kr"], m["vr"], m["gr"], ret_gn_w[l],
                                jnp.zeros((B, RET_HEADS, RET_HEAD_DIM, RET_HEAD_DIM), F32),
                                ret_consts_p, B, S // RET_CHUNK, RET_CHUNK)
        ret_s, s_s = _retention(m["qr"], m["kr"], m["vr"], m["gr"], ret_gn_w[l], state_ret[l].astype(F32),
                                ret_consts_s, Bd, 1, SROWS, row0=Np)
        h_mid, xn, idx, gate, ngrp = _post_mix(
            h_all, att_p, att_s, ret_p, ret_s, w_out[l].astype(BF16), norm_ffn_w[l],
            peer_w_q[l].T.astype(BF16), peer_sub_keys_1[l].astype(BF16), peer_sub_keys_2[l].astype(BF16))
        coef = _peer_coef(_peer_u(idx, ngrp, xn, peer_u[l]), gate)
        ypart = _peer_v(idx, ngrp, coef, peer_v[l])
        last = l == depth - 1
        h_all = _final(h_mid, ypart, final_norm_w if last else None)

        sample = lambda a, w: a[Np:].reshape(Bd, SROWS, w)[:, :T]
        outs["kp"].append(m["ka"][:Np].reshape(B, S, KV_HEADS, ATT_HEAD_DIM))
        outs["vp"].append(m["va"][:Np].reshape(B, S, KV_HEADS, ATT_HEAD_DIM))
        outs["ikp"].append(m["kiwi"][:Np, :IDX_DIM].reshape(B, S, IDX_DIM))
        outs["sp"].append(s_p.astype(x_prompt.dtype))
        outs["ks"].append(sample(m["ka"], KV_WIDTH).reshape(Bd, T, KV_HEADS, ATT_HEAD_DIM))
        outs["vs"].append(sample(m["va"], KV_WIDTH).reshape(Bd, T, KV_HEADS, ATT_HEAD_DIM))
        outs["iks"].append(sample(m["kiwi"], _KIWI_WIDTH)[:, :, :IDX_DIM])
        outs["ss"].append(s_s.astype(state_ret.dtype))

    y_prompt = h_all[:Np].reshape(B, S, D)
    y_sample = h_all[Np:].reshape(Bd, SROWS, D)[:, :T]
    st = lambda n: jnp.stack(outs[n])
    return (y_prompt, y_sample, st("kp"), st("vp"), st("ikp"), st("sp"),
            st("ks"), st("vs"), st("iks"), st("ss"))
```

```python
import functools

import jax
import jax.numpy as jnp
import numpy as np
from jax import lax
from jax.experimental import pallas as pl
from jax.experimental.pallas import tpu as pltpu

F32 = jnp.float32
BF16 = jnp.bfloat16

D_MODEL = 1024
PAGE_SIZE = 128
ATT_HEADS = 8
ATT_HEAD_DIM = 64
KV_HEADS = 2
ATT_GROUP = ATT_HEADS // KV_HEADS
ATT_WIDTH = ATT_HEADS * ATT_HEAD_DIM
KV_WIDTH = KV_HEADS * ATT_HEAD_DIM
ROPE_THETA = 500000.0
ATT_ROPE_DIMS = ATT_HEAD_DIM // 4
IDX_HEADS = 8
IDX_DIM = 64
IDX_WIDTH = IDX_HEADS * IDX_DIM
TOPK_MAX = 256
RET_HEADS = 4
RET_HEAD_DIM = 128
RET_WIDTH = RET_HEADS * RET_HEAD_DIM
RET_ROPE_THETA = 10000.0
RET_CHUNK = 128
PEER_HEADS = 8
PEER_NKEYS = 128
PEER_EXPERTS = PEER_NKEYS * PEER_NKEYS
PEER_KEY_DIM = 128
PEER_HALF = PEER_KEY_DIM // 2
PEER_TOPK = 16
NORM_EPS = 1e-6
GN_EPS = 1e-6

LANES = 128
SUBLANES = 8
VMEM_LIMIT_BYTES = 56 * 1024 * 1024

NEG_BIG = -1e30

_IN_SPLITS = (ATT_WIDTH, KV_WIDTH, KV_WIDTH, IDX_WIDTH, IDX_DIM, IDX_HEADS,
              RET_WIDTH, RET_WIDTH, RET_WIDTH, RET_WIDTH)
_KIWI_WIDTH = LANES
_PAD_OFFS = {}
_off = 0
for _name, _w in (("qa", ATT_WIDTH), ("ka", KV_WIDTH), ("va", KV_WIDTH), ("qi", IDX_WIDTH),
                  ("kiwi", _KIWI_WIDTH), ("qr", RET_WIDTH), ("kr", RET_WIDTH),
                  ("vr", RET_WIDTH), ("gr", RET_WIDTH)):
    _PAD_OFFS[_name] = (_off, _w)
    _off += _w
IN_WIDTH_PADDED = _off


def _cparams(sem):
    return pltpu.CompilerParams(dimension_semantics=sem, vmem_limit_bytes=VMEM_LIMIT_BYTES)


def _dot(a, b):
    return jnp.dot(a.astype(BF16), b.astype(BF16), preferred_element_type=F32)


def _dot_nt(a, b):
    return lax.dot_general(a.astype(BF16), b.astype(BF16), (((1,), (1,)), ((), ())),
                           preferred_element_type=F32)


def _pad_w_in(w_in_l):
    cols = []
    off = 0
    parts = []
    for n in _IN_SPLITS:
        parts.append(w_in_l[:, off:off + n])
        off += n
    qa, ka, va, qi, ki, wi, qr, kr, vr, gr = parts
    kiwi = jnp.concatenate(
        [ki, wi, jnp.zeros((w_in_l.shape[0], _KIWI_WIDTH - IDX_DIM - IDX_HEADS), w_in_l.dtype)], axis=1)
    cols = [qa, ka, va, qi, kiwi, qr, kr, vr, gr]
    return jnp.concatenate(cols, axis=1).astype(BF16)


def _rope_tables(pos):
    posf = pos.astype(F32)
    half = ATT_ROPE_DIMS // 2
    inv = 1.0 / (ROPE_THETA ** (jnp.arange(half, dtype=F32) / half))
    ang = posf[:, None] * inv[None, :]
    cos, sin = jnp.cos(ang), jnp.sin(ang)
    P = pos.shape[0]
    one = jnp.ones((P, ATT_HEAD_DIM - ATT_ROPE_DIMS), F32)
    zero = jnp.zeros((P, ATT_HEAD_DIM - ATT_ROPE_DIMS), F32)
    zh = jnp.zeros((P, half), F32)
    c_head = jnp.concatenate([cos, cos, one], axis=1)
    s1_head = jnp.concatenate([-sin, zh, zero], axis=1)
    s2_head = jnp.concatenate([zh, sin, zero], axis=1)
    rep = LANES // ATT_HEAD_DIM
    att = jnp.stack([jnp.tile(c_head, (1, rep)), jnp.tile(s1_head, (1, rep)), jnp.tile(s2_head, (1, rep))])
    halfr = RET_HEAD_DIM // 2
    invr = 1.0 / (RET_ROPE_THETA ** (jnp.arange(halfr, dtype=F32) / halfr))
    angr = posf[:, None] * invr[None, :]
    cr, sr = jnp.cos(angr), jnp.sin(angr)
    ret = jnp.stack([jnp.concatenate([cr, cr], axis=1), jnp.concatenate([-sr, sr], axis=1)])
    return att, ret


def _in_proj_kernel(x_ref, nw_ref, w_ref, ta_ref, tr_ref,
                    qa_ref, ka_ref, va_ref, qi_ref, kiwi_ref, qr_ref, kr_ref, vr_ref, gr_ref):
    x = x_ref[...]
    ms = jnp.mean(x * x, axis=-1, keepdims=True)
    xn = (x * lax.rsqrt(ms + NORM_EPS) * nw_ref[...]).astype(BF16)
    ca, s1a, s2a = ta_ref[0], ta_ref[1], ta_ref[2]
    cr, sr = tr_ref[0], tr_ref[1]

    def proj(name):
        off, w = _PAD_OFFS[name]
        return jnp.dot(xn, w_ref[:, off:off + w], preferred_element_type=F32)

    def rope_att_chunk(vc):
        return (vc * ca + pltpu.roll(vc, LANES - ATT_ROPE_DIMS // 2, 1) * s1a
                + pltpu.roll(vc, ATT_ROPE_DIMS // 2, 1) * s2a)

    def rope_ret_chunk(vc):
        return vc * cr + pltpu.roll(vc, RET_HEAD_DIM // 2, 1) * sr

    def per_chunk(v, fn):
        n = v.shape[1] // LANES
        return jnp.concatenate([fn(v[:, c * LANES:(c + 1) * LANES]) for c in range(n)], axis=1)

    qa_ref[...] = per_chunk(proj("qa"), rope_att_chunk)
    ka_ref[...] = per_chunk(proj("ka"), rope_att_chunk)
    va_ref[...] = proj("va")
    qi_ref[...] = per_chunk(proj("qi"), rope_att_chunk)
    kiwi = proj("kiwi")
    lane = lax.broadcasted_iota(jnp.int32, kiwi.shape, 1)
    kiwi_ref[...] = jnp.where(lane < IDX_DIM, rope_att_chunk(kiwi), kiwi * (IDX_HEADS ** -0.5))
    qr_ref[...] = per_chunk(proj("qr"), rope_ret_chunk)
    kr_ref[...] = per_chunk(proj("kr"), rope_ret_chunk) * (RET_HEAD_DIM ** -0.5)
    vr_ref[...] = proj("vr")
    gr_ref[...] = proj("gr")


def _in_proj(x, norm_w, w_pad, tab_att, tab_ret, tab_block_of_tile, tm):
    N = x.shape[0]
    nt = N // tm
    tab_idx = jnp.asarray(tab_block_of_tile, jnp.int32)
    names = ("qa", "ka", "va", "qi", "kiwi", "qr", "kr", "vr", "gr")
    out_shape = [jax.ShapeDtypeStruct((N, _PAD_OFFS[n][1]), F32) for n in names]
    out_specs = [pl.BlockSpec((tm, _PAD_OFFS[n][1]), lambda i, t: (i, 0)) for n in names]
    grid_spec = pltpu.PrefetchScalarGridSpec(
        num_scalar_prefetch=1,
        grid=(nt,),
        in_specs=[
            pl.BlockSpec((tm, D_MODEL), lambda i, t: (i, 0)),
            pl.BlockSpec((1, D_MODEL), lambda i, t: (0, 0)),
            pl.BlockSpec((D_MODEL, IN_WIDTH_PADDED), lambda i, t: (0, 0)),
            pl.BlockSpec((3, tm, LANES), lambda i, t: (0, t[i], 0)),
            pl.BlockSpec((2, tm, LANES), lambda i, t: (0, t[i], 0)),
        ],
        out_specs=out_specs,
    )

    def body(t_ref, *refs):
        _in_proj_kernel(*refs)

    outs = pl.pallas_call(
        body, grid_spec=grid_spec, out_shape=out_shape, name="in_proj",
        compiler_params=_cparams(("arbitrary",)),
    )(tab_idx, x, norm_w.reshape(1, D_MODEL), w_pad, tab_att, tab_ret)
    return dict(zip(names, outs))


def _ret_constants(c_eff):
    C = RET_CHUNK
    lg = jnp.log1p(-(2.0 ** (-5.0 - jnp.arange(RET_HEADS, dtype=F32))))
    i = jnp.arange(C, dtype=F32)
    diff = i[:, None] - i[None, :]
    dmask = jnp.where(diff >= 0, jnp.exp(lg[:, None, None] * jnp.maximum(diff, 0.0)), 0.0)
    real = (i < c_eff)
    dmask = jnp.where(real[None, :, None] & real[None, None, :], dmask, 0.0)
    q_dec = jnp.exp(lg[:, None] * (i[None, :] + 1.0))
    k_dec = jnp.where(real[None, :], jnp.exp(lg[:, None] * (c_eff - 1.0 - i[None, :])), 0.0)
    chunk_dec = jnp.exp(lg * c_eff)
    bc = lambda a: jnp.broadcast_to(a[:, :, None], (RET_HEADS, C, C))
    cdec = jnp.broadcast_to(chunk_dec[:, None, None], (RET_HEADS, C, C))
    return dmask, bc(q_dec), bc(k_dec), cdec


def _retention_kernel(q_ref, k_ref, v_ref, g_ref, gnw_ref, s0_ref, dm_ref, qd_ref, kd_ref, cd_ref,
                      o_ref, s_out_ref, state_ref, *, rows):
    c = pl.program_id(1)
    nc = pl.num_programs(1)

    @pl.when(c == 0)
    def _():
        state_ref[...] = s0_ref[0]

    def padded(ref):
        v = ref[...]
        if rows < RET_CHUNK:
            v = jnp.concatenate([v, jnp.zeros((RET_CHUNK - rows, v.shape[1]), v.dtype)], axis=0)
        return v

    q, k, v, g = padded(q_ref), padded(k_ref), padded(v_ref), g_ref[...]
    gnw = gnw_ref[...]
    outs = []
    for h in range(RET_HEADS):
        sl = slice(h * RET_HEAD_DIM, (h + 1) * RET_HEAD_DIM)
        qh, kh, vh = q[:, sl], k[:, sl], v[:, sl]
        st = state_ref[h]
        att = _dot_nt(qh, kh) * dm_ref[h]
        o = _dot(att, vh) + _dot(qh, st) * qd_ref[h]
        state_ref[h] = st * cd_ref[h] + _dot((kh * kd_ref[h]).T, vh)
        o = o[:rows]
        mu = jnp.mean(o, axis=-1, keepdims=True)
        var = jnp.mean(jnp.square(o - mu), axis=-1, keepdims=True)
        outs.append((o - mu) * lax.rsqrt(var + GN_EPS) * gnw[:, sl])
    on = jnp.concatenate(outs, axis=1)
    o_ref[...] = g * (1.0 / (1.0 + jnp.exp(-g))) * on

    @pl.when(c == nc - 1)
    def _():
        s_out_ref[0] = state_ref[...]


def _retention(q, k, v, g, gn_w, s0, consts, nb, nc, rows, row0=0):
    blk0 = row0 // rows
    in_row = pl.BlockSpec((rows, RET_WIDTH), lambda b, c: (blk0 + b * nc + c, 0))
    out_row = pl.BlockSpec((rows, RET_WIDTH), lambda b, c: (b * nc + c, 0))
    st_spec = pl.BlockSpec((1, RET_HEADS, RET_HEAD_DIM, RET_HEAD_DIM), lambda b, c: (b, 0, 0, 0))
    const_spec = pl.BlockSpec((RET_HEADS, RET_CHUNK, RET_CHUNK), lambda b, c: (0, 0, 0))
    return pl.pallas_call(
        functools.partial(_retention_kernel, rows=rows),
        grid=(nb, nc),
        in_specs=[in_row, in_row, in_row, in_row,
                  pl.BlockSpec((1, RET_WIDTH), lambda b, c: (0, 0)),
                  st_spec, const_spec, const_spec, const_spec, const_spec],
        out_specs=[out_row, st_spec],
        out_shape=[jax.ShapeDtypeStruct((nb * nc * rows, RET_WIDTH), F32),
                   jax.ShapeDtypeStruct(s0.shape, F32)],
        scratch_shapes=[pltpu.VMEM((RET_HEADS, RET_HEAD_DIM, RET_HEAD_DIM), F32)],
        name="retention",
        compiler_params=_cparams(("arbitrary", "arbitrary")),
    )(q, k, v, g, gn_w.reshape(1, RET_WIDTH), s0, *consts)


_BISECT_PLAIN_ITERS = 26
_BISECT_MAX_ITERS = 400


def _lane_tiles(x):
    return [x[:, j * LANES:(j + 1) * LANES] for j in range(x.shape[1] // LANES)]


def _rowsum_b(x):
    return jnp.broadcast_to(jnp.sum(x, axis=1, keepdims=True), x.shape)


def _rowmax_b(x):
    return jnp.broadcast_to(jnp.max(x, axis=1, keepdims=True), x.shape)


def _rowmin_b(x):
    return jnp.broadcast_to(jnp.min(x, axis=1, keepdims=True), x.shape)


def _threshold_search(sc_ref, nck, kk, lo0, hi0, n_causal, row_active):
    R = lo0.shape[0]
    zeros = jnp.zeros((R, LANES), F32)

    def count_pass(mid, snap):
        def body(kc, carry):
            x = sc_ref[kc]
            cnt, amin, bmax = carry
            for xt in _lane_tiles(x):
                ge = xt >= mid
                cnt = cnt + jnp.where(ge, 1.0, 0.0)
                if snap:
                    amin = jnp.minimum(amin, jnp.where(ge, xt, jnp.inf))
                    bmax = jnp.maximum(bmax, jnp.where(ge, -jnp.inf, xt))
            return cnt, amin, bmax
        cnt, amin, bmax = lax.fori_loop(
            0, nck, body, (zeros, jnp.full((R, LANES), jnp.inf, F32), jnp.full((R, LANES), -jnp.inf, F32)))
        if snap:
            return _rowsum_b(cnt), _rowmin_b(amin), _rowmax_b(bmax)
        return _rowsum_b(cnt), None, None

    def not_done(c_lo, lo, hib):
        pending = row_active & (c_lo != kk) & (lo != hib)
        return jnp.max(jnp.where(pending, 1.0, 0.0)) > 0.0

    def make_step(snap):
        def step(carry):
            it, lo, hi, hib, c_lo, c_hi = carry
            mid = 0.5 * (lo + hi)
            if snap:
                mid = 0.5 * (lo + jnp.minimum(hi, hib))
                mid = jnp.where(mid > lo, mid, jnp.minimum(hi, hib))
            c, amin, bmax = count_pass(mid, snap)
            ge = c >= kk
            if snap:
                lo = jnp.where(ge, amin, lo)
                hib = jnp.where(ge, hib, bmax)
            else:
                lo = jnp.where(ge, mid, lo)
            hi = jnp.where(ge, hi, mid)
            c_lo = jnp.where(ge, c, c_lo)
            c_hi = jnp.where(ge, c_hi, c)
            return it + 1, lo, hi, hib, c_lo, c_hi
        return step

    init = (jnp.int32(0), lo0, hi0, jnp.full((R, LANES), jnp.inf, F32), n_causal, zeros)
    carry = lax.while_loop(
        lambda c: (c[0] < _BISECT_PLAIN_ITERS) & not_done(c[4], c[1], c[3]), make_step(False), init)
    carry = lax.while_loop(
        lambda c: (c[0] < _BISECT_MAX_ITERS) & not_done(c[4], c[1], c[3]), make_step(True), carry)
    _, lo, hi, hib, c_lo, c_hi = carry
    tied = row_active & (c_lo != kk)
    return lo, hi, c_hi, tied


def _select_chunk(x, lo, hi, need, tied_any, tied, prefix0):
    R, CK = x.shape
    rep = CK // LANES
    wide = lambda a: jnp.concatenate([a] * rep, axis=1)

    def plain(_):
        return jnp.where(x >= wide(lo), 1.0, 0.0), prefix0

    def with_ties(_):
        low, hiw = wide(lo), wide(hi)
        band = (x >= low) & (x < hiw)
        bandf = jnp.where(band, 1.0, 0.0)
        r_i = lax.broadcasted_iota(jnp.int32, (CK, CK), 0)
        c_i = lax.broadcasted_iota(jnp.int32, (CK, CK), 1)
        tri = jnp.where(r_i <= c_i, 1.0, 0.0).astype(BF16)
        rank = jnp.dot(bandf.astype(BF16), tri, preferred_element_type=F32) + wide(prefix0)
        take_tie = jnp.where(rank <= wide(need), bandf, 0.0)
        sel_tied = jnp.where(x >= hiw, 1.0, take_tie)
        sel = jnp.where(wide(tied) > 0.0, sel_tied, jnp.where(x >= low, 1.0, 0.0))
        return sel, prefix0 + _rowsum_b(sum(_lane_tiles(bandf)))

    return lax.cond(tied_any, with_ties, plain, 0)


_QB = 128
_CK = 512


def _col_partial(x, op):
    n_chain = 4
    g = x.shape[0] // (SUBLANES * n_chain)
    x4 = x.reshape(n_chain, g, SUBLANES, x.shape[1])
    c = [op(x4[k], axis=0) for k in range(n_chain)]
    pair = jnp.stack([c[0], c[1]]), jnp.stack([c[2], c[3]])
    return op(jnp.stack([op(pair[0], axis=0), op(pair[1], axis=0)]), axis=0)


def _col_fold(x, op):
    return op(_col_partial(x, op), axis=0, keepdims=True)


def _threshold_search_t(sc_ref, nck, kk, lo0, hi0, n_causal):
    Q = lo0.shape[1]

    def count_pass(mid, snap):
        def body(kc, carry):
            x = sc_ref[kc]
            cnt, amin, bmax = carry
            ge = x >= mid
            cnt = cnt + _col_partial(jnp.where(ge, 1.0, 0.0), jnp.sum)
            if snap:
                amin = jnp.minimum(amin, _col_partial(jnp.where(ge, x, jnp.inf), jnp.min))
                bmax = jnp.maximum(bmax, _col_partial(jnp.where(ge, -jnp.inf, x), jnp.max))
            return cnt, amin, bmax
        init = (jnp.zeros((SUBLANES, Q), F32), jnp.full((SUBLANES, Q), jnp.inf, F32),
                jnp.full((SUBLANES, Q), -jnp.inf, F32))
        cnt, amin, bmax = lax.fori_loop(0, nck, body, init)
        return (jnp.sum(cnt, axis=0, keepdims=True), jnp.min(amin, axis=0, keepdims=True),
                jnp.max(bmax, axis=0, keepdims=True))

    def not_done(c_lo, lo, hib):
        pending = (c_lo != kk) & (lo != hib)
        return jnp.max(jnp.where(pending, 1.0, 0.0)) > 0.0

    def make_iter(snap, reps):
        def one(state):
            lo, hi, hib, c_lo, c_hi = state
            if snap:
                mid = 0.5 * (lo + jnp.minimum(hi, hib))
                mid = jnp.where(mid > lo, mid, jnp.minimum(hi, hib))
            else:
                mid = 0.5 * (lo + hi)
            c, amin, bmax = count_pass(mid, snap)
            ge = c >= kk
            if snap:
                lo = jnp.where(ge, amin, lo)
                hib = jnp.where(ge, hib, bmax)
            else:
                lo = jnp.where(ge, mid, lo)
            hi = jnp.where(ge, hi, mid)
            return lo, hi, hib, jnp.where(ge, c, c_lo), jnp.where(ge, c_hi, c)

        def step(carry):
            it, state = carry[0], carry[1:]
            for _ in range(reps):
                state = one(state)
            return (it + reps, *state)
        return step

    init = (jnp.int32(0), lo0, hi0, jnp.full((1, Q), jnp.inf, F32), n_causal, jnp.zeros((1, Q), F32))
    carry = lax.while_loop(
        lambda c: (c[0] < _BISECT_PLAIN_ITERS) & not_done(c[4], c[1], c[3]), make_iter(False, 2), init)
    carry = lax.while_loop(
        lambda c: (c[0] < _BISECT_MAX_ITERS) & not_done(c[4], c[1], c[3]), make_iter(True, 1), carry)
    _, lo, hi, hib, c_lo, c_hi = carry
    return lo, hi, c_hi, c_lo != kk


def _select_chunk_t(x, lo, hi, need, tied_any, tiedf, prefix0):
    ck = x.shape[0]

    def plain(_):
        return jnp.where(x >= lo, 1.0, 0.0), prefix0

    def with_ties(_):
        bandf = jnp.where((x >= lo) & (x < hi), 1.0, 0.0)
        r_i = lax.broadcasted_iota(jnp.int32, (ck, ck), 0)
        c_i = lax.broadcasted_iota(jnp.int32, (ck, ck), 1)
        tri = jnp.where(c_i <= r_i, 1.0, 0.0).astype(BF16)
        rank = jnp.dot(tri, bandf.astype(BF16), preferred_element_type=F32) + prefix0
        take_tie = jnp.where(rank <= need, bandf, 0.0)
        sel_tied = jnp.where(x >= hi, 1.0, take_tie)
        sel = jnp.where(tiedf > 0.0, sel_tied, jnp.where(x >= lo, 1.0, 0.0))
        return sel, prefix0 + _col_fold(bandf, jnp.sum)

    return lax.cond(tied_any, with_ties, plain, 0)


def _prompt_dsa_kernel(qi_ref, kiwiq_ref, kiwi_ref, qa_ref, ka_ref, va_ref, o_ref,
                         sc_ref, vt_ref, acc_ref, *, topk, seq):
    i = pl.program_id(1)
    t0 = i * _QB
    nck = (t0 + _QB + _CK - 1) // _CK
    Q = _QB
    HQ = ATT_HEADS * Q

    @pl.when(i == 0)
    def _():
        for j in range(seq // _CK):
            vt_ref[:, j * _CK:(j + 1) * _CK] = va_ref[j * _CK:(j + 1) * _CK, :].T

    qpos = t0 + lax.broadcasted_iota(jnp.int32, (1, Q), 1)
    key_row = lax.broadcasted_iota(jnp.int32, (_CK, Q), 0)

    qi = qi_ref[...]
    w_t = kiwiq_ref[...].T
    q_heads, w_rows = [], []
    for h in range(IDX_HEADS):
        qh = qi[:, h * IDX_DIM:(h + 1) * IDX_DIM] * (IDX_DIM ** -0.5)
        q_heads.append(jnp.concatenate([qh, jnp.zeros((Q, LANES - IDX_DIM), F32)], axis=1).astype(BF16))
        w_rows.append(w_t[IDX_DIM + h:IDX_DIM + h + 1, :])

    def score_body(kc, carry):
        mn, mx = carry
        k0 = pl.multiple_of(kc * _CK, _CK)
        kch = kiwi_ref[pl.ds(k0, _CK), :].astype(BF16)
        acc = jnp.zeros((_CK, Q), F32)
        for h in range(IDX_HEADS):
            acc = acc + w_rows[h] * jnp.maximum(_dot_nt(kch, q_heads[h]), 0.0)
        causal = (k0 + key_row) <= qpos
        sc_ref[kc] = jnp.where(causal, acc, -jnp.inf)
        mn = jnp.minimum(mn, _col_fold(jnp.where(causal, acc, jnp.inf), jnp.min))
        mx = jnp.maximum(mx, _col_fold(jnp.where(causal, acc, -jnp.inf), jnp.max))
        return mn, mx

    mn, mx = lax.fori_loop(0, nck, score_body,
                           (jnp.full((1, Q), jnp.inf, F32), jnp.full((1, Q), -jnp.inf, F32)))
    n_causal = (qpos + 1).astype(F32)
    kk = jnp.minimum(n_causal, float(topk))
    lo, hi, c_hi, tied = _threshold_search_t(sc_ref, nck, kk, mn, mx + (mx - mn) + 1.0, n_causal)
    need = kk - c_hi
    tiedf = jnp.where(tied, 1.0, 0.0)
    tied_any = jnp.max(tiedf) > 0.0

    qa = qa_ref[...]
    q_rows = []
    for h in range(ATT_HEADS):
        c = h // ATT_GROUP
        parts = [jnp.zeros((Q, ATT_HEAD_DIM), F32)] * KV_HEADS
        parts[c] = qa[:, h * ATT_HEAD_DIM:(h + 1) * ATT_HEAD_DIM] * (ATT_HEAD_DIM ** -0.5)
        q_rows.append(jnp.concatenate(parts, axis=1))
    q_all = jnp.concatenate(q_rows, axis=0).astype(BF16)
    acc_ref[...] = jnp.zeros(acc_ref.shape, F32)

    def att_body(kc, carry):
        m_old, l_old, prefix = carry
        k0 = pl.multiple_of(kc * _CK, _CK)
        kch = ka_ref[pl.ds(k0, _CK), :].astype(BF16)
        vch_t = vt_ref[:, pl.ds(k0, _CK)].astype(BF16)
        sel, prefix = _select_chunk_t(sc_ref[kc], lo, hi, need, tied_any, tiedf, prefix)
        selb = jnp.concatenate([sel] * ATT_HEADS, axis=1) > 0.0
        s = jnp.where(selb, _dot_nt(kch, q_all), NEG_BIG)
        m_new = jnp.maximum(m_old, _col_fold(s, jnp.max))
        alpha = jnp.exp(m_old - m_new)
        p = jnp.exp(s - m_new)
        l_new = l_old * alpha + _col_fold(p, jnp.sum)
        acc_ref[...] = acc_ref[...] * alpha + jnp.dot(vch_t, p.astype(BF16), preferred_element_type=F32)
        return m_new, l_new, prefix

    init = (jnp.full((1, HQ), NEG_BIG, F32), jnp.zeros((1, HQ), F32), jnp.zeros((1, Q), F32))
    _, l_fin, _ = lax.fori_loop(0, nck, att_body, init)
    o_t = acc_ref[...] / l_fin
    outs = []
    for h in range(ATT_HEADS):
        c = h // ATT_GROUP
        outs.append(o_t[:, h * Q:(h + 1) * Q].T[:, c * ATT_HEAD_DIM:(c + 1) * ATT_HEAD_DIM])
    o_ref[...] = jnp.concatenate(outs, axis=1)


def _prompt_dsa(qi, kiwi, qa, ka, va, B, S):
    nq = S // _QB
    topk = min(TOPK_MAX, S // 4)
    qblk = lambda w: pl.BlockSpec((_QB, w), lambda b, i: (b * nq + i, 0))
    allk = lambda w: pl.BlockSpec((S, w), lambda b, i: (b, 0))
    return pl.pallas_call(
        functools.partial(_prompt_dsa_kernel, topk=topk, seq=S),
        grid=(B, nq),
        in_specs=[qblk(IDX_WIDTH), qblk(_KIWI_WIDTH), allk(_KIWI_WIDTH), qblk(ATT_WIDTH),
                  allk(KV_WIDTH), allk(KV_WIDTH)],
        out_specs=qblk(ATT_WIDTH),
        out_shape=jax.ShapeDtypeStruct((B * S, ATT_WIDTH), F32),
        scratch_shapes=[pltpu.VMEM((S // _CK, _CK, _QB), F32),
                        pltpu.VMEM((KV_WIDTH, S), F32),
                        pltpu.VMEM((KV_WIDTH, ATT_HEADS * _QB), F32)],
        name="prompt_dsa",
        compiler_params=_cparams(("arbitrary", "arbitrary")),
    )(qi, kiwi, kiwi, qa, ka, va)


SROWS = SUBLANES
_CKS = 640


def _sample_dsa_kernel(pt_ref, qi_ref, kiwi_ref, qa_ref, ka_ref, va_ref, cik_hbm, ck_hbm, cv_hbm,
                       o_ref, ikbuf, kbuf, vbuf, sems, sc_ref, *, n_pages, t_real, topk):
    b = pl.program_id(0)
    nb = pl.num_programs(0)
    slot = b % 2
    past_len = n_pages * PAGE_SIZE
    L = past_len + PAGE_SIZE
    nck = L // _CKS
    R = SROWS

    def page_copies(bb, s, p):
        phys = pt_ref[bb, p]
        cols = pl.ds(pl.multiple_of(p * PAGE_SIZE, PAGE_SIZE), PAGE_SIZE)
        return (pltpu.make_async_copy(cik_hbm.at[phys], ikbuf.at[s, :, cols], sems.at[s, 0]),
                pltpu.make_async_copy(ck_hbm.at[phys], kbuf.at[s, :, cols], sems.at[s, 1]),
                pltpu.make_async_copy(cv_hbm.at[phys], vbuf.at[s, :, cols], sems.at[s, 2]))

    def start_fetch(bb, s):
        def body(p, _):
            for cp in page_copies(bb, s, p):
                cp.start()
            return 0
        lax.fori_loop(0, n_pages, body, 0)

    def wait_fetch(bb, s):
        def body(p, _):
            for cp in page_copies(bb, s, p):
                cp.wait()
            return 0
        lax.fori_loop(0, n_pages, body, 0)

    @pl.when(b == 0)
    def _():
        tail = pl.ds(past_len, PAGE_SIZE)
        for s in range(2):
            ikbuf[s, :, tail] = jnp.zeros((IDX_DIM, PAGE_SIZE), F32)
            kbuf[s, :, tail] = jnp.zeros((KV_WIDTH, PAGE_SIZE), F32)
            vbuf[s, :, tail] = jnp.zeros((KV_WIDTH, PAGE_SIZE), F32)
        start_fetch(0, 0)

    @pl.when(b + 1 < nb)
    def _():
        start_fetch(b + 1, 1 - slot)

    kiwi = kiwi_ref[...]

    def as_columns(rows):
        padded = jnp.concatenate([rows, jnp.zeros((LANES - R, LANES), F32)], axis=0)
        return padded.T[:, :R]

    new_cols = pl.ds(past_len, R)
    ikbuf[slot, :, new_cols] = as_columns(kiwi)[:IDX_DIM]
    kbuf[slot, :, new_cols] = as_columns(ka_ref[...])
    vbuf[slot, :, new_cols] = as_columns(va_ref[...])
    wait_fetch(b, slot)

    row = lax.broadcasted_iota(jnp.int32, (R, LANES), 0)
    row_c = lax.broadcasted_iota(jnp.int32, (R, _CKS), 0)
    lane_c = lax.broadcasted_iota(jnp.int32, (R, _CKS), 1)
    rep = _CKS // LANES

    qi = qi_ref[...]
    q_all = jnp.concatenate(
        [qi[:, h * IDX_DIM:(h + 1) * IDX_DIM] * (IDX_DIM ** -0.5) for h in range(IDX_HEADS)], axis=0).astype(BF16)
    w_heads = [jnp.broadcast_to(kiwi[:, IDX_DIM + h:IDX_DIM + h + 1], (R, _CKS)) for h in range(IDX_HEADS)]

    def score_body(kc, carry):
        mn, mx = carry
        k0 = pl.multiple_of(kc * _CKS, LANES)
        keys_t = ikbuf[slot, :, pl.ds(k0, _CKS)].astype(BF16)
        d = jnp.dot(q_all, keys_t, preferred_element_type=F32)
        acc = jnp.zeros((R, _CKS), F32)
        for h in range(IDX_HEADS):
            acc = acc + w_heads[h] * jnp.maximum(d[h * R:(h + 1) * R], 0.0)
        causal = (k0 + lane_c) <= (past_len + row_c)
        sc_ref[kc] = jnp.where(causal, acc, -jnp.inf)
        for j in range(rep):
            a = acc[:, j * LANES:(j + 1) * LANES]
            cz = causal[:, j * LANES:(j + 1) * LANES]
            mn = jnp.minimum(mn, jnp.where(cz, a, jnp.inf))
            mx = jnp.maximum(mx, jnp.where(cz, a, -jnp.inf))
        return mn, mx

    mn, mx = lax.fori_loop(0, nck, score_body,
                           (jnp.full((R, LANES), jnp.inf, F32), jnp.full((R, LANES), -jnp.inf, F32)))
    mn, mx = _rowmin_b(mn), _rowmax_b(mx)
    n_causal = (past_len + row + 1).astype(F32)
    kk = jnp.minimum(n_causal, float(topk))
    hi0 = mx + (mx - mn) + 1.0
    row_active = row < t_real
    lo, hi, c_hi, tied = _threshold_search(sc_ref, nck, kk, mn, hi0, n_causal, row_active)
    need = kk - c_hi
    tiedf = jnp.where(tied, 1.0, 0.0)
    tied_any = jnp.max(tiedf) > 0.0

    qa = qa_ref[...]
    q_rows = []
    for h in range(ATT_HEADS):
        c = h // ATT_GROUP
        z = jnp.zeros((R, ATT_HEAD_DIM), F32)
        parts = [z] * KV_HEADS
        parts[c] = qa[:, h * ATT_HEAD_DIM:(h + 1) * ATT_HEAD_DIM] * (ATT_HEAD_DIM ** -0.5)
        q_rows.append(jnp.concatenate(parts, axis=1))
    q_big = jnp.concatenate(q_rows, axis=0).astype(BF16)
    HR = ATT_HEADS * R

    def att_body(kc, carry):
        m_old, l_old, acc, prefix = carry
        k0 = pl.multiple_of(kc * _CKS, LANES)
        kch_t = kbuf[slot, :, pl.ds(k0, _CKS)].astype(BF16)
        vch_t = vbuf[slot, :, pl.ds(k0, _CKS)].astype(BF16)
        self_, prefix = _select_chunk(sc_ref[kc], lo, hi, need, tied_any, tiedf, prefix)
        selb = jnp.concatenate([self_] * ATT_HEADS, axis=0) > 0.0
        s = jnp.dot(q_big, kch_t, preferred_element_type=F32)
        s = jnp.where(selb, s, NEG_BIG)
        m_new = jnp.maximum(m_old, _rowmax_b(functools.reduce(jnp.maximum, _lane_tiles(s))))
        alpha = jnp.exp(m_old - m_new)
        p = jnp.exp(s - jnp.concatenate([m_new] * rep, axis=1))
        l_new = l_old * alpha + _rowsum_b(sum(_lane_tiles(p)))
        acc = acc * alpha + _dot_nt(p, vch_t)
        return m_new, l_new, acc, prefix

    init = (jnp.full((HR, LANES), NEG_BIG, F32), jnp.zeros((HR, LANES), F32), jnp.zeros((HR, LANES), F32),
            jnp.zeros((R, LANES), F32))
    _, l_fin, acc, _ = lax.fori_loop(0, nck, att_body, init)
    o = acc / l_fin
    outs = []
    for h in range(ATT_HEADS):
        c = h // ATT_GROUP
        outs.append(o[h * R:(h + 1) * R, c * ATT_HEAD_DIM:(c + 1) * ATT_HEAD_DIM])
    o_ref[...] = jnp.concatenate(outs, axis=1)


def _sample_dsa(page_table, qi, kiwi, qa, ka, va, cache_idx_k_l, cache_k_l, cache_v_l, t_real, row0):
    Bd, n_pages = page_table.shape
    past_len = n_pages * PAGE_SIZE
    L = past_len + PAGE_SIZE
    assert L % _CKS == 0 and row0 % SROWS == 0
    topk = min(TOPK_MAX, (past_len + t_real) // 4)
    blk0 = row0 // SROWS
    blk = lambda w: pl.BlockSpec((SROWS, w), lambda b, pt: (blk0 + b, 0))
    any_spec = pl.BlockSpec(memory_space=pl.ANY)
    grid_spec = pltpu.PrefetchScalarGridSpec(
        num_scalar_prefetch=1,
        grid=(Bd,),
        in_specs=[blk(IDX_WIDTH), blk(_KIWI_WIDTH), blk(ATT_WIDTH), blk(KV_WIDTH), blk(KV_WIDTH),
                  any_spec, any_spec, any_spec],
        out_specs=pl.BlockSpec((SROWS, ATT_WIDTH), lambda b, pt: (b, 0)),
        scratch_shapes=[pltpu.VMEM((2, IDX_DIM, L), F32),
                        pltpu.VMEM((2, KV_WIDTH, L), F32),
                        pltpu.VMEM((2, KV_WIDTH, L), F32),
                        pltpu.SemaphoreType.DMA((2, 3)),
                        pltpu.VMEM((L // _CKS, SROWS, _CKS), F32)],
    )
    return pl.pallas_call(
        functools.partial(_sample_dsa_kernel, n_pages=n_pages, t_real=t_real, topk=topk),
        grid_spec=grid_spec,
        out_shape=jax.ShapeDtypeStruct((Bd * SROWS, ATT_WIDTH), F32),
        name="sample_dsa",
        compiler_params=_cparams(("arbitrary",)),
    )(page_table, qi, kiwi, qa, ka, va, cache_idx_k_l, cache_k_l, cache_v_l)


_TMP = 256
_PAIR_GROUP = 16
PEER_HALF_EXPERTS = PEER_EXPERTS // 2


def _extract_topk(s, pos, n, k, payload=None):
    vals, idxs = [], []
    for _ in range(k):
        m = jnp.max(s, axis=0, keepdims=True)
        p = jnp.min(jnp.where(s == m, pos, n), axis=0, keepdims=True)
        hit = pos == p
        vals.append(m)
        if payload is None:
            idxs.append(p)
        else:
            idxs.append(jnp.max(jnp.where(hit, payload, -1), axis=0, keepdims=True))
        s = jnp.where(hit, -jnp.inf, s)
    return vals, idxs


def _post_mix_kernel(x_ref, attp_ref, atts_ref, retp_ref, rets_ref, wo_ref, nw_ref, wqt_ref, sk1_ref, sk2_ref,
                     h_ref, xn_ref, idx_ref, gate_ref, n0_ref, qt_ref, et_ref, gt_ref, la_ref, lg_ref,
                     *, n_prompt_tiles):
    T = x_ref.shape[0]
    is_prompt = pl.program_id(0) < n_prompt_tiles
    att = jnp.where(is_prompt, attp_ref[...], atts_ref[...])
    ret = jnp.where(is_prompt, retp_ref[...], rets_ref[...])
    h = x_ref[...] + _dot(att, wo_ref[:ATT_WIDTH, :]) + _dot(ret, wo_ref[ATT_WIDTH:, :])
    h_ref[...] = h
    ms = jnp.mean(h * h, axis=-1, keepdims=True)
    xn = h * lax.rsqrt(ms + NORM_EPS) * nw_ref[...]
    for c in range(D_MODEL // LANES):
        xn_ref[pl.ds(c, T, stride=SUBLANES), :] = xn[:, c * LANES:(c + 1) * LANES]
    qt_ref[...] = _dot_nt(wqt_ref[...], xn)

    pos_k = lax.broadcasted_iota(jnp.int32, (PEER_NKEYS, T), 0)
    n_b = [PEER_TOPK // (a + 1) for a in range(PEER_TOPK)]
    n_cand = -(-sum(n_b) // SUBLANES) * SUBLANES
    pos_c = lax.broadcasted_iota(jnp.int32, (n_cand, T), 0)

    def head_body(hd, _):
        q0 = pl.multiple_of(hd * PEER_KEY_DIM, PEER_KEY_DIM)
        q1 = qt_ref[pl.ds(q0, PEER_HALF), :]
        q2 = qt_ref[pl.ds(q0 + PEER_HALF, PEER_HALF), :]
        s1 = _dot(sk1_ref[...], q1)
        s2 = _dot(sk2_ref[...], q2)
        v1, i1 = _extract_topk(s1, pos_k, PEER_NKEYS, PEER_TOPK)
        v2, i2 = _extract_topk(s2, pos_k, PEER_NKEYS, PEER_TOPK)
        v2m = jnp.concatenate(v2, axis=0)
        i2m = jnp.concatenate(i2, axis=0)
        n_fill = n_cand - sum(n_b)
        cand = jnp.concatenate([v1[a] + v2m[:n_b[a]] for a in range(PEER_TOPK)]
                               + [jnp.full((n_fill, T), -jnp.inf, F32)], axis=0)
        cid = jnp.concatenate([i1[a] * PEER_NKEYS + i2m[:n_b[a]] for a in range(PEER_TOPK)]
                              + [jnp.full((n_fill, T), -1, jnp.int32)], axis=0)
        sv, eid = _extract_topk(cand, pos_c, n_cand, PEER_TOPK, payload=cid)
        svm = jnp.concatenate(sv, axis=0)
        g = jnp.exp(svm - sv[0])
        r0 = pl.multiple_of(hd * PEER_TOPK, PEER_TOPK)
        gt_ref[pl.ds(r0, PEER_TOPK), :] = g / jnp.sum(g, axis=0, keepdims=True)
        et_ref[pl.ds(r0, PEER_TOPK), :] = jnp.concatenate(eid, axis=0).astype(F32)
        return 0

    lax.fori_loop(0, PEER_HEADS, head_body, 0)

    e = et_ref[...]
    g = gt_ref[...]
    npair = e.shape[0]
    is0 = e < float(PEER_HALF_EXPERTS)
    r_i = lax.broadcasted_iota(jnp.int32, (npair, npair), 0)
    c_i = lax.broadcasted_iota(jnp.int32, (npair, npair), 1)
    tri = jnp.where(c_i <= r_i, 1.0, 0.0).astype(BF16)
    rank0 = jnp.dot(tri, jnp.where(is0, 1.0, 0.0).astype(BF16), preferred_element_type=F32)
    rowf = lax.broadcasted_iota(jnp.int32, (npair, T), 0).astype(F32)
    n0 = rank0[npair - 1:npair, :]
    place = jnp.where(is0, rank0 - 1.0, n0 + (rowf - rank0))
    off = jnp.where(is0, e, e - float(PEER_HALF_EXPERTS)) * float(SUBLANES)
    for p in range(npair):
        m = place == float(p)
        la_ref[p:p + 1, :] = jnp.sum(jnp.where(m, off, 0.0), axis=0, keepdims=True)
        lg_ref[p:p + 1, :] = jnp.sum(jnp.where(m, g, 0.0), axis=0, keepdims=True)
    idx_ref[...] = la_ref[...].T.astype(jnp.int32)
    gate_ref[...] = lg_ref[...].T
    n0_ref[...] = n0.astype(jnp.int32)


def _post_mix(x, att_p, att_s, ret_p, ret_s, w_out_bf, norm_w, wq_t_bf, sk1_bf, sk2_bf):
    N = x.shape[0]
    nt = N // _TMP
    ntp = att_p.shape[0] // _TMP
    nts = att_s.shape[0] // _TMP
    assert ntp + nts == nt and ntp > 0 and nts > 0
    npair = PEER_HEADS * PEER_TOPK
    row = lambda w: pl.BlockSpec((_TMP, w), lambda i: (i, 0))
    prow = lambda w: pl.BlockSpec((_TMP, w), lambda i: (jnp.minimum(i, ntp - 1), 0))
    srow = lambda w: pl.BlockSpec((_TMP, w), lambda i: (jnp.maximum(i - ntp, 0), 0))
    full = lambda a: pl.BlockSpec(a.shape, lambda i: (0,) * a.ndim)
    nw = norm_w.reshape(1, D_MODEL)
    return pl.pallas_call(
        functools.partial(_post_mix_kernel, n_prompt_tiles=ntp),
        grid=(nt,),
        in_specs=[row(D_MODEL), prow(ATT_WIDTH), srow(ATT_WIDTH), prow(RET_WIDTH), srow(RET_WIDTH),
                  full(w_out_bf), full(nw), full(wq_t_bf), full(sk1_bf), full(sk2_bf)],
        out_specs=[row(D_MODEL), pl.BlockSpec((_TMP * SUBLANES, LANES), lambda i: (i, 0)),
                   row(npair), row(npair), pl.BlockSpec((1, _TMP), lambda i: (0, i))],
        out_shape=[jax.ShapeDtypeStruct((N, D_MODEL), F32), jax.ShapeDtypeStruct((N * SUBLANES, LANES), F32),
                   jax.ShapeDtypeStruct((N, npair), jnp.int32), jax.ShapeDtypeStruct((N, npair), F32),
                   jax.ShapeDtypeStruct((1, N), jnp.int32)],
        scratch_shapes=[pltpu.VMEM((PEER_HEADS * PEER_KEY_DIM, _TMP), F32),
                        pltpu.VMEM((npair, _TMP), F32), pltpu.VMEM((npair, _TMP), F32),
                        pltpu.VMEM((npair, _TMP), F32), pltpu.VMEM((npair, _TMP), F32)],
        name="post_mix",
        compiler_params=_cparams(("arbitrary",)),
    )(x, att_p, att_s, ret_p, ret_s, w_out_bf, nw, wq_t_bf, sk1_bf, sk2_bf)


_TBP = 128
_NPAIR = PEER_HEADS * PEER_TOPK
_ROW_TILE = D_MODEL // LANES
assert _ROW_TILE == SUBLANES
_HIGH_HALF = -65536


def _bf16_bits(x):
    return lax.bitcast_convert_type(x.astype(BF16).astype(F32), jnp.int32)


def _pack_table_kernel(lo_ref, hi_ref, o_ref):
    n = lo_ref.shape[0]
    words = (lax.shift_right_logical(_bf16_bits(lo_ref[...]), 16)
             | (_bf16_bits(hi_ref[...]) & _HIGH_HALF))
    for c in range(_ROW_TILE):
        o_ref[pl.ds(c, n, stride=_ROW_TILE), :] = words[:, c * LANES:(c + 1) * LANES]


def _pack_table(tbl):
    blk = 512
    nb = PEER_HALF_EXPERTS // blk
    return pl.pallas_call(
        _pack_table_kernel,
        grid=(nb,),
        in_specs=[pl.BlockSpec((blk, D_MODEL), lambda i: (i, 0)),
                  pl.BlockSpec((blk, D_MODEL), lambda i: (i + nb, 0))],
        out_specs=pl.BlockSpec((blk * _ROW_TILE, LANES), lambda i: (i, 0)),
        out_shape=jax.ShapeDtypeStruct((PEER_HALF_EXPERTS * _ROW_TILE, LANES), jnp.int32),
        name="pack_table",
        compiler_params=_cparams(("arbitrary",)),
    )(tbl, tbl)


def _unpack(words, half):
    bits = lax.shift_left(words, 16) if half == 0 else words & _HIGH_HALF
    return lax.bitcast_convert_type(bits, F32)


def _load_table(tbl_hbm, tbuf, sem):
    cp = pltpu.make_async_copy(tbl_hbm, tbuf, sem)
    cp.start()
    cp.wait()


def _fold_pair(a, b, k, sub):
    m = (sub & k) == 0
    return jnp.where(m, a, pltpu.roll(b, k, 0)) + jnp.where(m, pltpu.roll(a, SUBLANES - k, 0), b)


def _fold8(p, sub):
    a, b, c, d, e, f, g, h = p[0], p[4], p[2], p[6], p[1], p[5], p[3], p[7]
    t1, t2, t3, t4 = (_fold_pair(a, b, 4, sub), _fold_pair(c, d, 4, sub),
                      _fold_pair(e, f, 4, sub), _fold_pair(g, h, 4, sub))
    u1, u2 = _fold_pair(t1, t2, 2, sub), _fold_pair(t3, t4, 2, sub)
    return _fold_pair(u1, u2, 1, sub)


def _flat_smem_spec(nt, per_token):
    return pl.BlockSpec((1, 1, _TBP * per_token), lambda i: (i, 0, 0), memory_space=pltpu.SMEM)


def _flat_blocks(a, nt):
    return a.reshape(nt, 1, -1)


_N_GROUPS = _NPAIR // _PAIR_GROUP


def _group_start(t, g):
    g = jnp.minimum(g, _N_GROUPS - 1)
    return pl.multiple_of(t * _NPAIR + g * _PAIR_GROUP, _PAIR_GROUP)


def _table_row(tbuf, idx_ref, k):
    off = pl.multiple_of(idx_ref[0, 0, k], _ROW_TILE)
    return tbuf[pl.ds(off, _ROW_TILE), :]


def _token_segments(n0):
    g_mid = lax.shift_right_logical(n0, _PAIR_GROUP.bit_length() - 1)
    rem = n0 & (_PAIR_GROUP - 1)
    g_hi = g_mid + jnp.where(rem != 0, 1, 0)
    return g_mid, rem, g_hi


def _mixed_unpack(words, s, rem):
    return jnp.where(s < rem, _unpack(words, 0), _unpack(words, 1))


def _peer_u_kernel(idx_ref, n0_ref, x_ref, u_hbm, o_ref, tbuf, sem, z_ref):
    @pl.when(pl.program_id(0) == 0)
    def _():
        _load_table(u_hbm, tbuf, sem)

    sub = lax.broadcasted_iota(jnp.int32, (SUBLANES, LANES), 0)

    def tok_body(t, _):
        xt = x_ref[t]
        g_mid, rem, g_hi = _token_segments(n0_ref[0, 0, t])

        def load_words(g):
            kg = _group_start(t, g)
            return tuple(_table_row(tbuf, idx_ref, kg + s) for s in range(_PAIR_GROUP))

        def fold_group(g, vals):
            kg = _group_start(t, g)
            prods = [v * xt for v in vals]
            for q in range(_PAIR_GROUP // SUBLANES):
                z_ref[pl.ds(kg + q * SUBLANES, SUBLANES), :] = _fold8(prods[q * SUBLANES:(q + 1) * SUBLANES], sub)

        def uniform_groups(ga, gb, half):
            def grp_body(g, words):
                nxt = load_words(g + 1)
                fold_group(g, [_unpack(w, half) for w in words])
                return nxt
            lax.fori_loop(ga, gb, grp_body, load_words(ga))

        uniform_groups(0, g_mid, 0)

        @pl.when(rem != 0)
        def _():
            fold_group(g_mid, [_mixed_unpack(w, s, rem) for s, w in enumerate(load_words(g_mid))])

        uniform_groups(g_hi, _N_GROUPS, 1)
        return 0

    lax.fori_loop(0, _TBP, tok_body, 0)

    ones = jnp.ones((LANES, LANES), BF16)
    eye = (lax.broadcasted_iota(jnp.int32, (_NPAIR, LANES), 0)
           == lax.broadcasted_iota(jnp.int32, (_NPAIR, LANES), 1))

    def chunk_body(c, _):
        r0 = pl.multiple_of(c * SUBLANES * _NPAIR, SUBLANES * _NPAIR)
        z = z_ref[pl.ds(r0, SUBLANES * _NPAIR), :]
        zh = z.astype(BF16)
        zl = (z - zh.astype(F32)).astype(BF16)
        hv = jnp.dot(zh, ones, preferred_element_type=F32) + jnp.dot(zl, ones, preferred_element_type=F32)
        rows = [jnp.sum(jnp.where(eye, hv[tt * _NPAIR:(tt + 1) * _NPAIR], 0.0), axis=0, keepdims=True)
                for tt in range(SUBLANES)]
        o_ref[pl.ds(pl.multiple_of(c * SUBLANES, SUBLANES), SUBLANES), :] = jnp.concatenate(rows, axis=0)
        return 0

    lax.fori_loop(0, _TBP // SUBLANES, chunk_body, 0)


def _peer_u(idx, n0, xn_tiles, u_packed):
    N = xn_tiles.shape[0] // _ROW_TILE
    nt = N // _TBP
    return pl.pallas_call(
        _peer_u_kernel,
        grid=(nt,),
        in_specs=[_flat_smem_spec(nt, _NPAIR), _flat_smem_spec(nt, 1),
                  pl.BlockSpec((_TBP, _ROW_TILE, LANES), lambda i: (i, 0, 0)),
                  pl.BlockSpec(memory_space=pl.ANY)],
        out_specs=pl.BlockSpec((_TBP, _NPAIR), lambda i: (i, 0)),
        out_shape=jax.ShapeDtypeStruct((N, _NPAIR), F32),
        scratch_shapes=[pltpu.VMEM((PEER_HALF_EXPERTS * _ROW_TILE, LANES), jnp.int32),
                        pltpu.SemaphoreType.DMA(()),
                        pltpu.VMEM((_TBP * _NPAIR, LANES), F32)],
        name="peer_u",
        compiler_params=_cparams(("arbitrary",)),
    )(_flat_blocks(idx, nt), _flat_blocks(n0, nt), xn_tiles.reshape(N, _ROW_TILE, LANES), u_packed)


def _peer_coef_kernel(hv_ref, gate_ref, o_ref):
    hv = hv_ref[...]
    o_ref[...] = gate_ref[...] * (0.5 * hv * (1.0 + lax.erf(hv * (0.5 ** 0.5))))


def _peer_coef(hval, gate):
    N = gate.shape[0]
    tm = 1024 if N % 1024 == 0 else _TBP
    spec = pl.BlockSpec((tm, _NPAIR), lambda i: (i, 0))
    return pl.pallas_call(
        _peer_coef_kernel,
        grid=(N // tm,),
        in_specs=[spec, spec],
        out_specs=spec,
        out_shape=jax.ShapeDtypeStruct((N, _NPAIR), F32),
        name="peer_coef",
        compiler_params=_cparams(("arbitrary",)),
    )(hval, gate)


def _peer_v_kernel(idx_ref, n0_ref, coef_ref, v_hbm, o_ref, tbuf, sem):
    @pl.when(pl.program_id(0) == 0)
    def _():
        _load_table(v_hbm, tbuf, sem)

    n_acc = 4

    def accumulate(accs, prods):
        accs = list(accs)
        for s in range(_PAIR_GROUP):
            accs[s % n_acc] = accs[s % n_acc] + prods[s]
        return tuple(accs)

    def tok_body(t, _):
        g_mid, rem, g_hi = _token_segments(n0_ref[0, 0, t])

        def weighted_group(g, unpack):
            kg = _group_start(t, g)
            return tuple(coef_ref[0, 0, kg + s] * unpack(_table_row(tbuf, idx_ref, kg + s), s)
                         for s in range(_PAIR_GROUP))

        def uniform_groups(ga, gb, half, accs):
            unpack = lambda w, s: _unpack(w, half)

            def grp_body(g, carry):
                prods, accs = carry
                nxt = weighted_group(g + 1, unpack)
                return nxt, accumulate(accs, prods)

            return lax.fori_loop(ga, gb, grp_body, (weighted_group(ga, unpack), accs))[1]

        zero = jnp.zeros((_ROW_TILE, LANES), F32)
        accs = uniform_groups(0, g_mid, 0, (zero,) * n_acc)
        accs = lax.cond(
            rem != 0,
            lambda a: accumulate(a, weighted_group(g_mid, lambda w, s: _mixed_unpack(w, s, rem))),
            lambda a: a, accs)
        accs = uniform_groups(g_hi, _N_GROUPS, 1, accs)
        o_ref[t] = (accs[0] + accs[1]) + (accs[2] + accs[3])
        return 0

    lax.fori_loop(0, _TBP, tok_body, 0)


def _peer_v(idx, n0, coef, v_packed):
    N = coef.shape[0]
    nt = N // _TBP
    out = pl.pallas_call(
        _peer_v_kernel,
        grid=(nt,),
        in_specs=[_flat_smem_spec(nt, _NPAIR), _flat_smem_spec(nt, 1), _flat_smem_spec(nt, _NPAIR),
                  pl.BlockSpec(memory_space=pl.ANY)],
        out_specs=pl.BlockSpec((_TBP, _ROW_TILE, LANES), lambda i: (i, 0, 0)),
        out_shape=jax.ShapeDtypeStruct((N, _ROW_TILE, LANES), F32),
        scratch_shapes=[pltpu.VMEM((PEER_HALF_EXPERTS * _ROW_TILE, LANES), jnp.int32),
                        pltpu.SemaphoreType.DMA(())],
        name="peer_v",
        compiler_params=_cparams(("arbitrary",)),
    )(_flat_blocks(idx, nt), _flat_blocks(n0, nt), _flat_blocks(coef, nt), v_packed)
    return out.reshape(N * _ROW_TILE, LANES)


def _final_kernel(h_ref, y_ref, w_ref, o_ref, *, normalize):
    T = h_ref.shape[0]
    y = jnp.concatenate(
        [y_ref[pl.ds(c, T, stride=SUBLANES), :] for c in range(D_MODEL // LANES)], axis=1)
    h = h_ref[...] + y
    if normalize:
        ms = jnp.mean(h * h, axis=-1, keepdims=True)
        h = h * lax.rsqrt(ms + NORM_EPS) * w_ref[...]
    o_ref[...] = h


def _final(h, ypart, w):
    N = h.shape[0]
    tm = 512 if N % 512 == 0 else _TBP
    normalize = w is not None
    if not normalize:
        w = jnp.ones((D_MODEL,), F32)
    return pl.pallas_call(
        functools.partial(_final_kernel, normalize=normalize),
        grid=(N // tm,),
        in_specs=[pl.BlockSpec((tm, D_MODEL), lambda i: (i, 0)),
                  pl.BlockSpec((tm * SUBLANES, LANES), lambda i: (i, 0)),
                  pl.BlockSpec((1, D_MODEL), lambda i: (0, 0))],
        out_specs=pl.BlockSpec((tm, D_MODEL), lambda i: (i, 0)),
        out_shape=jax.ShapeDtypeStruct((N, D_MODEL), F32),
        name="final_norm",
        compiler_params=_cparams(("arbitrary",)),
    )(h, ypart, w.reshape(1, D_MODEL))


_TM_IN = 256


def _pages_feature_major(cache_l):
    n_phys, page = cache_l.shape[:2]
    return jnp.swapaxes(cache_l.reshape(n_phys, page, -1), 1, 2)


def kernel(x_prompt, x_sample, cache_k, cache_v, cache_idx_k, state_ret, page_table, norm_attn_w, w_in, ret_gn_w,
           w_out, norm_ffn_w, peer_w_q, peer_sub_keys_1, peer_sub_keys_2, peer_u, peer_v, final_norm_w):
    B, S, D = x_prompt.shape
    Bd, T, _ = x_sample.shape
    depth = w_in.shape[0]
    n_pages = page_table.shape[1]
    past_len = n_pages * PAGE_SIZE
    n_phys = cache_k.shape[1]
    Np, Ns = B * S, Bd * SROWS
    assert D == D_MODEL and T <= SROWS and S % _TM_IN == 0 and Ns % _TM_IN == 0 and S % _CK == 0

    hs_pad = jnp.pad(x_sample, ((0, 0), (0, SROWS - T), (0, 0)))
    h_all = jnp.concatenate([x_prompt.reshape(Np, D), hs_pad.reshape(Ns, D)], axis=0)

    pos_s = past_len + jnp.arange(SROWS)
    pos = jnp.concatenate([jnp.arange(S), jnp.tile(pos_s, _TM_IN // SROWS)])
    tab_att, tab_ret = _rope_tables(pos)
    tiles_per_seq = S // _TM_IN
    tab_blocks = [i % tiles_per_seq for i in range(Np // _TM_IN)] + [tiles_per_seq] * (Ns // _TM_IN)
    ret_consts_p = _ret_constants(RET_CHUNK)
    ret_consts_s = _ret_constants(T)

    outs = {n: [] for n in ("kp", "vp", "ikp", "sp", "ks", "vs", "iks", "ss")}
    for l in range(depth):
        m = _in_proj(h_all, norm_attn_w[l], _pad_w_in(w_in[l]), tab_att, tab_ret, tab_blocks, _TM_IN)
        att_p = _prompt_dsa(m["qi"], m["kiwi"], m["qa"], m["ka"], m["va"], B, S)
        att_s = _sample_dsa(page_table, m["qi"], m["kiwi"], m["qa"], m["ka"], m["va"],
                            _pages_feature_major(cache_idx_k[l]), _pages_feature_major(cache_k[l]),
                            _pages_feature_major(cache_v[l]), T, Np)
        ret_p, s_p = _retention(m["qr"], m["kr"], m["vr"], m["gr"], ret_gn_w[l],
                                jnp.zeros((B, RET_HEADS, RET_HEAD_DIM, RET_HEAD_DIM), F32),
                                ret_consts_p, B, S // RET_CHUNK, RET_CHUNK)
        ret_s, s_s = _retention(m["qr"], m["kr"], m["vr"], m["gr"], ret_gn_w[l], state_ret[l].astype(F32),
                                ret_consts_s, Bd, 1, SROWS, row0=Np)
        h_mid, xn, idx, gate, n0 = _post_mix(
            h_all, att_p, att_s, ret_p, ret_s, w_out[l].astype(BF16), norm_ffn_w[l],
            peer_w_q[l].T.astype(BF16), peer_sub_keys_1[l].astype(BF16), peer_sub_keys_2[l].astype(BF16))
        coef = _peer_coef(_peer_u(idx, n0, xn, _pack_table(peer_u[l])), gate)
        ypart = _peer_v(idx, n0, coef, _pack_table(peer_v[l]))
        last = l == depth - 1
        h_all = _final(h_mid, ypart, final_norm_w if last else None)

        sample = lambda a, w: a[Np:].reshape(Bd, SROWS, w)[:, :T]
        outs["kp"].append(m["ka"][:Np].reshape(B, S, KV_HEADS, ATT_HEAD_DIM))
        outs["vp"].append(m["va"][:Np].reshape(B, S, KV_HEADS, ATT_HEAD_DIM))
        outs["ikp"].append(m["kiwi"][:Np, :IDX_DIM].reshape(B, S, IDX_DIM))
        outs["sp"].append(s_p.astype(x_prompt.dtype))
        outs["ks"].append(sample(m["ka"], KV_WIDTH).reshape(Bd, T, KV_HEADS, ATT_HEAD_DIM))
        outs["vs"].append(sample(m["va"], KV_WIDTH).reshape(Bd, T, KV_HEADS, ATT_HEAD_DIM))
        outs["iks"].append(sample(m["kiwi"], _KIWI_WIDTH)[:, :, :IDX_DIM])
        outs["ss"].append(s_s.astype(state_ret.dtype))

    y_prompt = h_all[:Np].reshape(B, S, D)
    y_sample = h_all[Np:].reshape(Bd, SROWS, D)[:, :T]
    st = lambda n: jnp.stack(outs[n])
    return (y_prompt, y_sample, st("kp"), st("vp"), st("ikp"), st("sp"),
            st("ks"), st("vs"), st("iks"), st("ss"))
```

```python
import functools

import jax
import jax.numpy as jnp
import numpy as np
from jax import lax
from jax.experimental import pallas as pl
from jax.experimental.pallas import tpu as pltpu

F32 = jnp.float32
BF16 = jnp.bfloat16

D_MODEL = 1024
PAGE_SIZE = 128
ATT_HEADS = 8
ATT_HEAD_DIM = 64
KV_HEADS = 2
ATT_GROUP = ATT_HEADS // KV_HEADS
ATT_WIDTH = ATT_HEADS * ATT_HEAD_DIM
KV_WIDTH = KV_HEADS * ATT_HEAD_DIM
ROPE_THETA = 500000.0
ATT_ROPE_DIMS = ATT_HEAD_DIM // 4
IDX_HEADS = 8
IDX_DIM = 64
IDX_WIDTH = IDX_HEADS * IDX_DIM
TOPK_MAX = 256
RET_HEADS = 4
RET_HEAD_DIM = 128
RET_WIDTH = RET_HEADS * RET_HEAD_DIM
RET_ROPE_THETA = 10000.0
RET_CHUNK = 128
PEER_HEADS = 8
PEER_NKEYS = 128
PEER_EXPERTS = PEER_NKEYS * PEER_NKEYS
PEER_KEY_DIM = 128
PEER_HALF = PEER_KEY_DIM // 2
PEER_TOPK = 16
NORM_EPS = 1e-6
GN_EPS = 1e-6

LANES = 128
SUBLANES = 8
VMEM_LIMIT_BYTES = 56 * 1024 * 1024

NEG_BIG = -1e30

_IN_SPLITS = (ATT_WIDTH, KV_WIDTH, KV_WIDTH, IDX_WIDTH, IDX_DIM, IDX_HEADS,
              RET_WIDTH, RET_WIDTH, RET_WIDTH, RET_WIDTH)
_KIWI_WIDTH = LANES
_PAD_OFFS = {}
_off = 0
for _name, _w in (("qa", ATT_WIDTH), ("ka", KV_WIDTH), ("va", KV_WIDTH), ("qi", IDX_WIDTH),
                  ("kiwi", _KIWI_WIDTH), ("qr", RET_WIDTH), ("kr", RET_WIDTH),
                  ("vr", RET_WIDTH), ("gr", RET_WIDTH)):
    _PAD_OFFS[_name] = (_off, _w)
    _off += _w
IN_WIDTH_PADDED = _off


def _cparams(sem):
    return pltpu.CompilerParams(dimension_semantics=sem, vmem_limit_bytes=VMEM_LIMIT_BYTES)


def _dot(a, b):
    return jnp.dot(a.astype(BF16), b.astype(BF16), preferred_element_type=F32)


def _dot_nt(a, b):
    return lax.dot_general(a.astype(BF16), b.astype(BF16), (((1,), (1,)), ((), ())),
                           preferred_element_type=F32)


def _pad_w_in(w_in_l):
    cols = []
    off = 0
    parts = []
    for n in _IN_SPLITS:
        parts.append(w_in_l[:, off:off + n])
        off += n
    qa, ka, va, qi, ki, wi, qr, kr, vr, gr = parts
    kiwi = jnp.concatenate(
        [ki, wi, jnp.zeros((w_in_l.shape[0], _KIWI_WIDTH - IDX_DIM - IDX_HEADS), w_in_l.dtype)], axis=1)
    cols = [qa, ka, va, qi, kiwi, qr, kr, vr, gr]
    return jnp.concatenate(cols, axis=1).astype(BF16)


def _rope_tables(pos):
    posf = pos.astype(F32)
    half = ATT_ROPE_DIMS // 2
    inv = 1.0 / (ROPE_THETA ** (jnp.arange(half, dtype=F32) / half))
    ang = posf[:, None] * inv[None, :]
    cos, sin = jnp.cos(ang), jnp.sin(ang)
    P = pos.shape[0]
    one = jnp.ones((P, ATT_HEAD_DIM - ATT_ROPE_DIMS), F32)
    zero = jnp.zeros((P, ATT_HEAD_DIM - ATT_ROPE_DIMS), F32)
    zh = jnp.zeros((P, half), F32)
    c_head = jnp.concatenate([cos, cos, one], axis=1)
    s1_head = jnp.concatenate([-sin, zh, zero], axis=1)
    s2_head = jnp.concatenate([zh, sin, zero], axis=1)
    rep = LANES // ATT_HEAD_DIM
    att = jnp.stack([jnp.tile(c_head, (1, rep)), jnp.tile(s1_head, (1, rep)), jnp.tile(s2_head, (1, rep))])
    halfr = RET_HEAD_DIM // 2
    invr = 1.0 / (RET_ROPE_THETA ** (jnp.arange(halfr, dtype=F32) / halfr))
    angr = posf[:, None] * invr[None, :]
    cr, sr = jnp.cos(angr), jnp.sin(angr)
    ret = jnp.stack([jnp.concatenate([cr, cr], axis=1), jnp.concatenate([-sr, sr], axis=1)])
    return att, ret


def _in_proj_kernel(x_ref, nw_ref, w_ref, ta_ref, tr_ref,
                    qa_ref, ka_ref, va_ref, qi_ref, kiwi_ref, qr_ref, kr_ref, vr_ref, gr_ref):
    x = x_ref[...]
    ms = jnp.mean(x * x, axis=-1, keepdims=True)
    xn = (x * lax.rsqrt(ms + NORM_EPS) * nw_ref[...]).astype(BF16)
    ca, s1a, s2a = ta_ref[0], ta_ref[1], ta_ref[2]
    cr, sr = tr_ref[0], tr_ref[1]

    def proj(name):
        off, w = _PAD_OFFS[name]
        return jnp.dot(xn, w_ref[:, off:off + w], preferred_element_type=F32)

    def rope_att_chunk(vc):
        return (vc * ca + pltpu.roll(vc, LANES - ATT_ROPE_DIMS // 2, 1) * s1a
                + pltpu.roll(vc, ATT_ROPE_DIMS // 2, 1) * s2a)

    def rope_ret_chunk(vc):
        return vc * cr + pltpu.roll(vc, RET_HEAD_DIM // 2, 1) * sr

    def per_chunk(v, fn):
        n = v.shape[1] // LANES
        return jnp.concatenate([fn(v[:, c * LANES:(c + 1) * LANES]) for c in range(n)], axis=1)

    qa_ref[...] = per_chunk(proj("qa"), rope_att_chunk)
    ka_ref[...] = per_chunk(proj("ka"), rope_att_chunk)
    va_ref[...] = proj("va")
    qi_ref[...] = per_chunk(proj("qi"), rope_att_chunk)
    kiwi = proj("kiwi")
    lane = lax.broadcasted_iota(jnp.int32, kiwi.shape, 1)
    kiwi_ref[...] = jnp.where(lane < IDX_DIM, rope_att_chunk(kiwi), kiwi * (IDX_HEADS ** -0.5))
    qr_ref[...] = per_chunk(proj("qr"), rope_ret_chunk)
    kr_ref[...] = per_chunk(proj("kr"), rope_ret_chunk) * (RET_HEAD_DIM ** -0.5)
    vr_ref[...] = proj("vr")
    gr_ref[...] = proj("gr")


def _in_proj(x, norm_w, w_pad, tab_att, tab_ret, tab_block_of_tile, tm):
    N = x.shape[0]
    nt = N // tm
    tab_idx = jnp.asarray(tab_block_of_tile, jnp.int32)
    names = ("qa", "ka", "va", "qi", "kiwi", "qr", "kr", "vr", "gr")
    out_shape = [jax.ShapeDtypeStruct((N, _PAD_OFFS[n][1]), F32) for n in names]
    out_specs = [pl.BlockSpec((tm, _PAD_OFFS[n][1]), lambda i, t: (i, 0)) for n in names]
    grid_spec = pltpu.PrefetchScalarGridSpec(
        num_scalar_prefetch=1,
        grid=(nt,),
        in_specs=[
            pl.BlockSpec((tm, D_MODEL), lambda i, t: (i, 0)),
            pl.BlockSpec((1, D_MODEL), lambda i, t: (0, 0)),
            pl.BlockSpec((D_MODEL, IN_WIDTH_PADDED), lambda i, t: (0, 0)),
            pl.BlockSpec((3, tm, LANES), lambda i, t: (0, t[i], 0)),
            pl.BlockSpec((2, tm, LANES), lambda i, t: (0, t[i], 0)),
        ],
        out_specs=out_specs,
    )

    def body(t_ref, *refs):
        _in_proj_kernel(*refs)

    outs = pl.pallas_call(
        body, grid_spec=grid_spec, out_shape=out_shape, name="in_proj",
        compiler_params=_cparams(("arbitrary",)),
    )(tab_idx, x, norm_w.reshape(1, D_MODEL), w_pad, tab_att, tab_ret)
    return dict(zip(names, outs))


def _ret_constants(c_eff):
    C = RET_CHUNK
    lg = jnp.log1p(-(2.0 ** (-5.0 - jnp.arange(RET_HEADS, dtype=F32))))
    i = jnp.arange(C, dtype=F32)
    diff = i[:, None] - i[None, :]
    dmask = jnp.where(diff >= 0, jnp.exp(lg[:, None, None] * jnp.maximum(diff, 0.0)), 0.0)
    real = (i < c_eff)
    dmask = jnp.where(real[None, :, None] & real[None, None, :], dmask, 0.0)
    q_dec = jnp.exp(lg[:, None] * (i[None, :] + 1.0))
    k_dec = jnp.where(real[None, :], jnp.exp(lg[:, None] * (c_eff - 1.0 - i[None, :])), 0.0)
    chunk_dec = jnp.exp(lg * c_eff)
    bc = lambda a: jnp.broadcast_to(a[:, :, None], (RET_HEADS, C, C))
    cdec = jnp.broadcast_to(chunk_dec[:, None, None], (RET_HEADS, C, C))
    return dmask, bc(q_dec), bc(k_dec), cdec


def _retention_kernel(q_ref, k_ref, v_ref, g_ref, gnw_ref, s0_ref, dm_ref, qd_ref, kd_ref, cd_ref,
                      o_ref, s_out_ref, state_ref, *, rows):
    c = pl.program_id(1)
    nc = pl.num_programs(1)

    @pl.when(c == 0)
    def _():
        state_ref[...] = s0_ref[0]

    def padded(ref):
        v = ref[...]
        if rows < RET_CHUNK:
            v = jnp.concatenate([v, jnp.zeros((RET_CHUNK - rows, v.shape[1]), v.dtype)], axis=0)
        return v

    q, k, v, g = padded(q_ref), padded(k_ref), padded(v_ref), g_ref[...]
    gnw = gnw_ref[...]
    outs = []
    for h in range(RET_HEADS):
        sl = slice(h * RET_HEAD_DIM, (h + 1) * RET_HEAD_DIM)
        qh, kh, vh = q[:, sl], k[:, sl], v[:, sl]
        st = state_ref[h]
        att = _dot_nt(qh, kh) * dm_ref[h]
        o = _dot(att, vh) + _dot(qh, st) * qd_ref[h]
        state_ref[h] = st * cd_ref[h] + _dot((kh * kd_ref[h]).T, vh)
        o = o[:rows]
        mu = jnp.mean(o, axis=-1, keepdims=True)
        var = jnp.mean(jnp.square(o - mu), axis=-1, keepdims=True)
        outs.append((o - mu) * lax.rsqrt(var + GN_EPS) * gnw[:, sl])
    on = jnp.concatenate(outs, axis=1)
    o_ref[...] = g * (1.0 / (1.0 + jnp.exp(-g))) * on

    @pl.when(c == nc - 1)
    def _():
        s_out_ref[0] = state_ref[...]


def _retention(q, k, v, g, gn_w, s0, consts, nb, nc, rows, row0=0):
    blk0 = row0 // rows
    in_row = pl.BlockSpec((rows, RET_WIDTH), lambda b, c: (blk0 + b * nc + c, 0))
    out_row = pl.BlockSpec((rows, RET_WIDTH), lambda b, c: (b * nc + c, 0))
    st_spec = pl.BlockSpec((1, RET_HEADS, RET_HEAD_DIM, RET_HEAD_DIM), lambda b, c: (b, 0, 0, 0))
    const_spec = pl.BlockSpec((RET_HEADS, RET_CHUNK, RET_CHUNK), lambda b, c: (0, 0, 0))
    return pl.pallas_call(
        functools.partial(_retention_kernel, rows=rows),
        grid=(nb, nc),
        in_specs=[in_row, in_row, in_row, in_row,
                  pl.BlockSpec((1, RET_WIDTH), lambda b, c: (0, 0)),
                  st_spec, const_spec, const_spec, const_spec, const_spec],
        out_specs=[out_row, st_spec],
        out_shape=[jax.ShapeDtypeStruct((nb * nc * rows, RET_WIDTH), F32),
                   jax.ShapeDtypeStruct(s0.shape, F32)],
        scratch_shapes=[pltpu.VMEM((RET_HEADS, RET_HEAD_DIM, RET_HEAD_DIM), F32)],
        name="retention",
        compiler_params=_cparams(("arbitrary", "arbitrary")),
    )(q, k, v, g, gn_w.reshape(1, RET_WIDTH), s0, *consts)


_BISECT_PLAIN_ITERS = 26
_BISECT_MAX_ITERS = 400


def _lane_tiles(x):
    return [x[:, j * LANES:(j + 1) * LANES] for j in range(x.shape[1] // LANES)]


def _rowsum_b(x):
    return jnp.broadcast_to(jnp.sum(x, axis=1, keepdims=True), x.shape)


def _select_chunk(x, lo, hi, need, tied_any, tied, prefix0):
    R, CK = x.shape
    rep = CK // LANES
    wide = lambda a: jnp.concatenate([a] * rep, axis=1)

    def plain(_):
        return jnp.where(x >= wide(lo), 1.0, 0.0), prefix0

    def with_ties(_):
        low, hiw = wide(lo), wide(hi)
        band = (x >= low) & (x < hiw)
        bandf = jnp.where(band, 1.0, 0.0)
        r_i = lax.broadcasted_iota(jnp.int32, (CK, CK), 0)
        c_i = lax.broadcasted_iota(jnp.int32, (CK, CK), 1)
        tri = jnp.where(r_i <= c_i, 1.0, 0.0).astype(BF16)
        rank = jnp.dot(bandf.astype(BF16), tri, preferred_element_type=F32) + wide(prefix0)
        take_tie = jnp.where(rank <= wide(need), bandf, 0.0)
        sel_tied = jnp.where(x >= hiw, 1.0, take_tie)
        sel = jnp.where(wide(tied) > 0.0, sel_tied, jnp.where(x >= low, 1.0, 0.0))
        return sel, prefix0 + _rowsum_b(sum(_lane_tiles(bandf)))

    return lax.cond(tied_any, with_ties, plain, 0)


_QB = 128
_CK = 512


def _col_partial(x, op):
    n_chain = 4
    g = x.shape[0] // (SUBLANES * n_chain)
    x4 = x.reshape(n_chain, g, SUBLANES, x.shape[1])
    c = [op(x4[k], axis=0) for k in range(n_chain)]
    pair = jnp.stack([c[0], c[1]]), jnp.stack([c[2], c[3]])
    return op(jnp.stack([op(pair[0], axis=0), op(pair[1], axis=0)]), axis=0)


def _col_fold(x, op):
    return op(_col_partial(x, op), axis=0, keepdims=True)


def _bisect(count_pass, kk, lo0, hi0, n_causal, active):
    def not_done(c_lo, lo, hib):
        pending = active & (c_lo != kk) & (lo != hib)
        return jnp.max(jnp.where(pending, 1.0, 0.0)) > 0.0

    def make_iter(snap, reps):
        def one(state):
            lo, hi, hib, c_lo, c_hi = state
            if snap:
                mid = 0.5 * (lo + jnp.minimum(hi, hib))
                mid = jnp.where(mid > lo, mid, jnp.minimum(hi, hib))
            else:
                mid = 0.5 * (lo + hi)
            c, amin, bmax = count_pass(mid, snap)
            ge = c >= kk
            if snap:
                lo = jnp.where(ge, amin, lo)
                hib = jnp.where(ge, hib, bmax)
            else:
                lo = jnp.where(ge, mid, lo)
            hi = jnp.where(ge, hi, mid)
            return lo, hi, hib, jnp.where(ge, c, c_lo), jnp.where(ge, c_hi, c)

        def step(carry):
            it, state = carry[0], carry[1:]
            for _ in range(reps):
                state = one(state)
            return (it + reps, *state)
        return step

    init = (jnp.int32(0), lo0, hi0, jnp.full(lo0.shape, jnp.inf, F32), n_causal, jnp.zeros(lo0.shape, F32))
    carry = lax.while_loop(
        lambda c: (c[0] < _BISECT_PLAIN_ITERS) & not_done(c[4], c[1], c[3]), make_iter(False, 2), init)
    carry = lax.while_loop(
        lambda c: (c[0] < _BISECT_MAX_ITERS) & not_done(c[4], c[1], c[3]), make_iter(True, 1), carry)
    _, lo, hi, hib, c_lo, c_hi = carry
    return lo, hi, c_hi, active & (c_lo != kk)


def _threshold_search_t(sc_ref, nck, kk, lo0, hi0, n_causal):
    Q = lo0.shape[1]

    def count_pass(mid, snap):
        def body(kc, carry):
            x = sc_ref[kc]
            cnt, amin, bmax = carry
            ge = x >= mid
            cnt = cnt + _col_partial(jnp.where(ge, 1.0, 0.0), jnp.sum)
            if snap:
                amin = jnp.minimum(amin, _col_partial(jnp.where(ge, x, jnp.inf), jnp.min))
                bmax = jnp.maximum(bmax, _col_partial(jnp.where(ge, -jnp.inf, x), jnp.max))
            return cnt, amin, bmax
        init = (jnp.zeros((SUBLANES, Q), F32), jnp.full((SUBLANES, Q), jnp.inf, F32),
                jnp.full((SUBLANES, Q), -jnp.inf, F32))
        cnt, amin, bmax = lax.fori_loop(0, nck, body, init)
        return (jnp.sum(cnt, axis=0, keepdims=True), jnp.min(amin, axis=0, keepdims=True),
                jnp.max(bmax, axis=0, keepdims=True))

    return _bisect(count_pass, kk, lo0, hi0, n_causal, lo0 == lo0)


def _threshold_search_rows(sc_ref, kk, lo0, hi0, n_causal, active):
    def count_pass(mid, snap):
        x = sc_ref[...]
        ge = x >= mid
        cnt = jnp.sum(jnp.where(ge, 1.0, 0.0), axis=1, keepdims=True)
        if not snap:
            return cnt, None, None
        return (cnt, jnp.min(jnp.where(ge, x, jnp.inf), axis=1, keepdims=True),
                jnp.max(jnp.where(ge, -jnp.inf, x), axis=1, keepdims=True))

    return _bisect(count_pass, kk, lo0, hi0, n_causal, active)


def _select_chunk_t(x, lo, hi, need, tied_any, tiedf, prefix0):
    ck = x.shape[0]

    def plain(_):
        return jnp.where(x >= lo, 1.0, 0.0), prefix0

    def with_ties(_):
        bandf = jnp.where((x >= lo) & (x < hi), 1.0, 0.0)
        r_i = lax.broadcasted_iota(jnp.int32, (ck, ck), 0)
        c_i = lax.broadcasted_iota(jnp.int32, (ck, ck), 1)
        tri = jnp.where(c_i <= r_i, 1.0, 0.0).astype(BF16)
        rank = jnp.dot(tri, bandf.astype(BF16), preferred_element_type=F32) + prefix0
        take_tie = jnp.where(rank <= need, bandf, 0.0)
        sel_tied = jnp.where(x >= hi, 1.0, take_tie)
        sel = jnp.where(tiedf > 0.0, sel_tied, jnp.where(x >= lo, 1.0, 0.0))
        return sel, prefix0 + _col_fold(bandf, jnp.sum)

    return lax.cond(tied_any, with_ties, plain, 0)


def _prompt_dsa_kernel(qi_ref, kiwiq_ref, kiwi_ref, qa_ref, ka_ref, va_ref, o_ref,
                         sc_ref, vt_ref, acc_ref, *, topk, seq):
    i = pl.program_id(1)
    t0 = i * _QB
    nck = (t0 + _QB + _CK - 1) // _CK
    Q = _QB
    HQ = ATT_HEADS * Q

    @pl.when(i == 0)
    def _():
        for j in range(seq // _CK):
            vt_ref[:, j * _CK:(j + 1) * _CK] = va_ref[j * _CK:(j + 1) * _CK, :].T

    qpos = t0 + lax.broadcasted_iota(jnp.int32, (1, Q), 1)
    key_row = lax.broadcasted_iota(jnp.int32, (_CK, Q), 0)

    qi = qi_ref[...]
    w_t = kiwiq_ref[...].T
    q_heads, w_rows = [], []
    for h in range(IDX_HEADS):
        qh = qi[:, h * IDX_DIM:(h + 1) * IDX_DIM] * (IDX_DIM ** -0.5)
        q_heads.append(jnp.concatenate([qh, jnp.zeros((Q, LANES - IDX_DIM), F32)], axis=1).astype(BF16))
        w_rows.append(w_t[IDX_DIM + h:IDX_DIM + h + 1, :])

    def score_body(kc, carry):
        mn, mx = carry
        k0 = pl.multiple_of(kc * _CK, _CK)
        kch = kiwi_ref[pl.ds(k0, _CK), :].astype(BF16)
        acc = jnp.zeros((_CK, Q), F32)
        for h in range(IDX_HEADS):
            acc = acc + w_rows[h] * jnp.maximum(_dot_nt(kch, q_heads[h]), 0.0)
        causal = (k0 + key_row) <= qpos
        sc_ref[kc] = jnp.where(causal, acc, -jnp.inf)
        mn = jnp.minimum(mn, _col_fold(jnp.where(causal, acc, jnp.inf), jnp.min))
        mx = jnp.maximum(mx, _col_fold(jnp.where(causal, acc, -jnp.inf), jnp.max))
        return mn, mx

    mn, mx = lax.fori_loop(0, nck, score_body,
                           (jnp.full((1, Q), jnp.inf, F32), jnp.full((1, Q), -jnp.inf, F32)))
    n_causal = (qpos + 1).astype(F32)
    kk = jnp.minimum(n_causal, float(topk))
    lo, hi, c_hi, tied = _threshold_search_t(sc_ref, nck, kk, mn, mx + (mx - mn) + 1.0, n_causal)
    need = kk - c_hi
    tiedf = jnp.where(tied, 1.0, 0.0)
    tied_any = jnp.max(tiedf) > 0.0

    qa = qa_ref[...]
    q_rows = []
    for h in range(ATT_HEADS):
        c = h // ATT_GROUP
        parts = [jnp.zeros((Q, ATT_HEAD_DIM), F32)] * KV_HEADS
        parts[c] = qa[:, h * ATT_HEAD_DIM:(h + 1) * ATT_HEAD_DIM] * (ATT_HEAD_DIM ** -0.5)
        q_rows.append(jnp.concatenate(parts, axis=1))
    q_all = jnp.concatenate(q_rows, axis=0).astype(BF16)
    acc_ref[...] = jnp.zeros(acc_ref.shape, F32)

    def att_body(kc, carry):
        m_old, l_old, prefix = carry
        k0 = pl.multiple_of(kc * _CK, _CK)
        kch = ka_ref[pl.ds(k0, _CK), :].astype(BF16)
        vch_t = vt_ref[:, pl.ds(k0, _CK)].astype(BF16)
        sel, prefix = _select_chunk_t(sc_ref[kc], lo, hi, need, tied_any, tiedf, prefix)
        selb = jnp.concatenate([sel] * ATT_HEADS, axis=1) > 0.0
        s = jnp.where(selb, _dot_nt(kch, q_all), NEG_BIG)
        m_new = jnp.maximum(m_old, _col_fold(s, jnp.max))
        alpha = jnp.exp(m_old - m_new)
        p = jnp.exp(s - m_new)
        l_new = l_old * alpha + _col_fold(p, jnp.sum)
        acc_ref[...] = acc_ref[...] * alpha + jnp.dot(vch_t, p.astype(BF16), preferred_element_type=F32)
        return m_new, l_new, prefix

    init = (jnp.full((1, HQ), NEG_BIG, F32), jnp.zeros((1, HQ), F32), jnp.zeros((1, Q), F32))
    _, l_fin, _ = lax.fori_loop(0, nck, att_body, init)
    o_t = acc_ref[...] / l_fin
    outs = []
    for h in range(ATT_HEADS):
        c = h // ATT_GROUP
        outs.append(o_t[:, h * Q:(h + 1) * Q].T[:, c * ATT_HEAD_DIM:(c + 1) * ATT_HEAD_DIM])
    o_ref[...] = jnp.concatenate(outs, axis=1)


def _prompt_dsa(qi, kiwi, qa, ka, va, B, S):
    nq = S // _QB
    topk = min(TOPK_MAX, S // 4)
    qblk = lambda w: pl.BlockSpec((_QB, w), lambda b, i: (b * nq + i, 0))
    allk = lambda w: pl.BlockSpec((S, w), lambda b, i: (b, 0))
    return pl.pallas_call(
        functools.partial(_prompt_dsa_kernel, topk=topk, seq=S),
        grid=(B, nq),
        in_specs=[qblk(IDX_WIDTH), qblk(_KIWI_WIDTH), allk(_KIWI_WIDTH), qblk(ATT_WIDTH),
                  allk(KV_WIDTH), allk(KV_WIDTH)],
        out_specs=qblk(ATT_WIDTH),
        out_shape=jax.ShapeDtypeStruct((B * S, ATT_WIDTH), F32),
        scratch_shapes=[pltpu.VMEM((S // _CK, _CK, _QB), F32),
                        pltpu.VMEM((KV_WIDTH, S), F32),
                        pltpu.VMEM((KV_WIDTH, ATT_HEADS * _QB), F32)],
        name="prompt_dsa",
        compiler_params=_cparams(("arbitrary", "arbitrary")),
    )(qi, kiwi, kiwi, qa, ka, va)


SROWS = SUBLANES
_CKS = 640


def _sample_dsa_kernel(pt_ref, qi_ref, kiwi_ref, qa_ref, ka_ref, va_ref, cik_hbm, ck_hbm, cv_hbm,
                       o_ref, ikbuf, kbuf, vbuf, sems, sc_ref, sel_ref, *, n_pages, t_real, topk):
    b = pl.program_id(0)
    nb = pl.num_programs(0)
    slot = b % 2
    past_len = n_pages * PAGE_SIZE
    L = past_len + PAGE_SIZE
    nck = L // _CKS
    R = SROWS

    def page_copies(bb, s, p):
        phys = pt_ref[bb, p]
        cols = pl.ds(pl.multiple_of(p * PAGE_SIZE, PAGE_SIZE), PAGE_SIZE)
        return (pltpu.make_async_copy(cik_hbm.at[phys], ikbuf.at[s, :, cols], sems.at[s, 0]),
                pltpu.make_async_copy(ck_hbm.at[phys], kbuf.at[s, :, cols], sems.at[s, 1]),
                pltpu.make_async_copy(cv_hbm.at[phys], vbuf.at[s, :, cols], sems.at[s, 2]))

    def start_fetch(bb, s):
        def body(p, _):
            for cp in page_copies(bb, s, p):
                cp.start()
            return 0
        lax.fori_loop(0, n_pages, body, 0)

    def wait_fetch(bb, s):
        def body(p, _):
            for cp in page_copies(bb, s, p):
                cp.wait()
            return 0
        lax.fori_loop(0, n_pages, body, 0)

    @pl.when(b == 0)
    def _():
        tail = pl.ds(past_len, PAGE_SIZE)
        for s in range(2):
            ikbuf[s, :, tail] = jnp.zeros((IDX_DIM, PAGE_SIZE), F32)
            kbuf[s, :, tail] = jnp.zeros((KV_WIDTH, PAGE_SIZE), F32)
            vbuf[s, :, tail] = jnp.zeros((KV_WIDTH, PAGE_SIZE), F32)
        start_fetch(0, 0)

    @pl.when(b + 1 < nb)
    def _():
        start_fetch(b + 1, 1 - slot)

    kiwi = kiwi_ref[...]

    def as_columns(rows):
        padded = jnp.concatenate([rows, jnp.zeros((LANES - R, LANES), F32)], axis=0)
        return padded.T[:, :R]

    new_cols = pl.ds(past_len, R)
    ikbuf[slot, :, new_cols] = as_columns(kiwi)[:IDX_DIM]
    kbuf[slot, :, new_cols] = as_columns(ka_ref[...])
    vbuf[slot, :, new_cols] = as_columns(va_ref[...])
    wait_fetch(b, slot)

    row = lax.broadcasted_iota(jnp.int32, (R, 1), 0)
    key_pos = lax.broadcasted_iota(jnp.int32, (R, L), 1)

    qi = qi_ref[...]
    q_all = jnp.concatenate(
        [qi[:, h * IDX_DIM:(h + 1) * IDX_DIM] * (IDX_DIM ** -0.5) for h in range(IDX_HEADS)], axis=0).astype(BF16)
    d = jnp.dot(q_all, ikbuf[slot].astype(BF16), preferred_element_type=F32)
    acc = jnp.zeros((R, L), F32)
    for h in range(IDX_HEADS):
        acc = acc + kiwi[:, IDX_DIM + h:IDX_DIM + h + 1] * jnp.maximum(d[h * R:(h + 1) * R], 0.0)
    causal = key_pos <= past_len + row
    sc_ref[...] = jnp.where(causal, acc, -jnp.inf)
    mn = jnp.min(jnp.where(causal, acc, jnp.inf), axis=1, keepdims=True)
    mx = jnp.max(jnp.where(causal, acc, -jnp.inf), axis=1, keepdims=True)
    n_causal = (past_len + row + 1).astype(F32)
    kk = jnp.minimum(n_causal, float(topk))
    lo, hi, c_hi, tied = _threshold_search_rows(sc_ref, kk, mn, mx + (mx - mn) + 1.0, n_causal, row < t_real)
    tiedf = jnp.where(tied, 1.0, 0.0)
    tied_any = jnp.max(tiedf) > 0.0
    sel_ref[...] = jnp.where(sc_ref[...] >= lo, 1.0, 0.0)

    @pl.when(tied_any)
    def _():
        wide = lambda a: jnp.broadcast_to(a, (R, LANES))

        def tie_body(kc, prefix):
            cols = pl.ds(pl.multiple_of(kc * _CKS, LANES), _CKS)
            sel, prefix = _select_chunk(sc_ref[:, cols], wide(lo), wide(hi), wide(kk - c_hi), tied_any,
                                        wide(tiedf), prefix)
            sel_ref[:, cols] = sel
            return prefix

        lax.fori_loop(0, nck, tie_body, jnp.zeros((R, LANES), F32))

    qa = qa_ref[...]
    q_rows = []
    for h in range(ATT_HEADS):
        c = h // ATT_GROUP
        z = jnp.zeros((R, ATT_HEAD_DIM), F32)
        parts = [z] * KV_HEADS
        parts[c] = qa[:, h * ATT_HEAD_DIM:(h + 1) * ATT_HEAD_DIM] * (ATT_HEAD_DIM ** -0.5)
        q_rows.append(jnp.concatenate(parts, axis=1))
    q_big = jnp.concatenate(q_rows, axis=0).astype(BF16)
    selb = jnp.concatenate([sel_ref[...]] * ATT_HEADS, axis=0) > 0.0
    s = jnp.where(selb, jnp.dot(q_big, kbuf[slot].astype(BF16), preferred_element_type=F32), NEG_BIG)
    p = jnp.exp(s - jnp.max(s, axis=1, keepdims=True))
    o = _dot_nt(p, vbuf[slot]) / jnp.sum(p, axis=1, keepdims=True)
    outs = []
    for h in range(ATT_HEADS):
        c = h // ATT_GROUP
        outs.append(o[h * R:(h + 1) * R, c * ATT_HEAD_DIM:(c + 1) * ATT_HEAD_DIM])
    o_ref[...] = jnp.concatenate(outs, axis=1)


def _sample_dsa(page_table, qi, kiwi, qa, ka, va, cache_idx_k_l, cache_k_l, cache_v_l, t_real, row0):
    Bd, n_pages = page_table.shape
    past_len = n_pages * PAGE_SIZE
    L = past_len + PAGE_SIZE
    assert L % _CKS == 0 and row0 % SROWS == 0
    topk = min(TOPK_MAX, (past_len + t_real) // 4)
    blk0 = row0 // SROWS
    blk = lambda w: pl.BlockSpec((SROWS, w), lambda b, pt: (blk0 + b, 0))
    any_spec = pl.BlockSpec(memory_space=pl.ANY)
    grid_spec = pltpu.PrefetchScalarGridSpec(
        num_scalar_prefetch=1,
        grid=(Bd,),
        in_specs=[blk(IDX_WIDTH), blk(_KIWI_WIDTH), blk(ATT_WIDTH), blk(KV_WIDTH), blk(KV_WIDTH),
                  any_spec, any_spec, any_spec],
        out_specs=pl.BlockSpec((SROWS, ATT_WIDTH), lambda b, pt: (b, 0)),
        scratch_shapes=[pltpu.VMEM((2, IDX_DIM, L), F32),
                        pltpu.VMEM((2, KV_WIDTH, L), F32),
                        pltpu.VMEM((2, KV_WIDTH, L), F32),
                        pltpu.SemaphoreType.DMA((2, 3)),
                        pltpu.VMEM((SROWS, L), F32),
                        pltpu.VMEM((SROWS, L), F32)],
    )
    return pl.pallas_call(
        functools.partial(_sample_dsa_kernel, n_pages=n_pages, t_real=t_real, topk=topk),
        grid_spec=grid_spec,
        out_shape=jax.ShapeDtypeStruct((Bd * SROWS, ATT_WIDTH), F32),
        name="sample_dsa",
        compiler_params=_cparams(("arbitrary",)),
    )(page_table, qi, kiwi, qa, ka, va, cache_idx_k_l, cache_k_l, cache_v_l)


_TMP = 256
_PAIR_GROUP = 16
PEER_HALF_EXPERTS = PEER_EXPERTS // 2


def _extract_topk(s, pos, n, k, payload=None):
    vals, idxs = [], []
    for _ in range(k):
        m = jnp.max(s, axis=0, keepdims=True)
        p = jnp.min(jnp.where(s == m, pos, n), axis=0, keepdims=True)
        hit = pos == p
        vals.append(m)
        if payload is None:
            idxs.append(p)
        else:
            idxs.append(jnp.max(jnp.where(hit, payload, -1), axis=0, keepdims=True))
        s = jnp.where(hit, -jnp.inf, s)
    return vals, idxs


def _post_mix_kernel(x_ref, attp_ref, atts_ref, retp_ref, rets_ref, wo_ref, nw_ref, wqt_ref, sk1_ref, sk2_ref,
                     h_ref, xn_ref, idx_ref, gate_ref, n0_ref, qt_ref, et_ref, gt_ref, la_ref, lg_ref,
                     *, n_prompt_tiles):
    T = x_ref.shape[0]
    is_prompt = pl.program_id(0) < n_prompt_tiles
    att = jnp.where(is_prompt, attp_ref[...], atts_ref[...])
    ret = jnp.where(is_prompt, retp_ref[...], rets_ref[...])
    h = x_ref[...] + _dot(att, wo_ref[:ATT_WIDTH, :]) + _dot(ret, wo_ref[ATT_WIDTH:, :])
    h_ref[...] = h
    ms = jnp.mean(h * h, axis=-1, keepdims=True)
    xn = h * lax.rsqrt(ms + NORM_EPS) * nw_ref[...]
    for c in range(D_MODEL // LANES):
        xn_ref[pl.ds(c, T, stride=SUBLANES), :] = xn[:, c * LANES:(c + 1) * LANES]
    qt_ref[...] = _dot_nt(wqt_ref[...], xn)

    pos_k = lax.broadcasted_iota(jnp.int32, (PEER_NKEYS, T), 0)
    n_b = [PEER_TOPK // (a + 1) for a in range(PEER_TOPK)]
    n_cand = -(-sum(n_b) // SUBLANES) * SUBLANES
    pos_c = lax.broadcasted_iota(jnp.int32, (n_cand, T), 0)

    def head_body(hd, _):
        q0 = pl.multiple_of(hd * PEER_KEY_DIM, PEER_KEY_DIM)
        q1 = qt_ref[pl.ds(q0, PEER_HALF), :]
        q2 = qt_ref[pl.ds(q0 + PEER_HALF, PEER_HALF), :]
        s1 = _dot(sk1_ref[...], q1)
        s2 = _dot(sk2_ref[...], q2)
        v1, i1 = _extract_topk(s1, pos_k, PEER_NKEYS, PEER_TOPK)
        v2, i2 = _extract_topk(s2, pos_k, PEER_NKEYS, PEER_TOPK)
        v2m = jnp.concatenate(v2, axis=0)
        i2m = jnp.concatenate(i2, axis=0)
        n_fill = n_cand - sum(n_b)
        cand = jnp.concatenate([v1[a] + v2m[:n_b[a]] for a in range(PEER_TOPK)]
                               + [jnp.full((n_fill, T), -jnp.inf, F32)], axis=0)
        cid = jnp.concatenate([i1[a] * PEER_NKEYS + i2m[:n_b[a]] for a in range(PEER_TOPK)]
                              + [jnp.full((n_fill, T), -1, jnp.int32)], axis=0)
        sv, eid = _extract_topk(cand, pos_c, n_cand, PEER_TOPK, payload=cid)
        svm = jnp.concatenate(sv, axis=0)
        g = jnp.exp(svm - sv[0])
        r0 = pl.multiple_of(hd * PEER_TOPK, PEER_TOPK)
        gt_ref[pl.ds(r0, PEER_TOPK), :] = g / jnp.sum(g, axis=0, keepdims=True)
        et_ref[pl.ds(r0, PEER_TOPK), :] = jnp.concatenate(eid, axis=0).astype(F32)
        return 0

    lax.fori_loop(0, PEER_HEADS, head_body, 0)

    e = et_ref[...]
    g = gt_ref[...]
    npair = e.shape[0]
    is0 = e < float(PEER_HALF_EXPERTS)
    r_i = lax.broadcasted_iota(jnp.int32, (npair, npair), 0)
    c_i = lax.broadcasted_iota(jnp.int32, (npair, npair), 1)
    tri = jnp.where(c_i <= r_i, 1.0, 0.0).astype(BF16)
    rank0 = jnp.dot(tri, jnp.where(is0, 1.0, 0.0).astype(BF16), preferred_element_type=F32)
    rowf = lax.broadcasted_iota(jnp.int32, (npair, T), 0).astype(F32)
    n0 = rank0[npair - 1:npair, :]
    place = jnp.where(is0, rank0 - 1.0, n0 + (rowf - rank0))
    off = jnp.where(is0, e, e - float(PEER_HALF_EXPERTS)) * float(SUBLANES)
    for p in range(npair):
        m = place == float(p)
        la_ref[p:p + 1, :] = jnp.sum(jnp.where(m, off, 0.0), axis=0, keepdims=True)
        lg_ref[p:p + 1, :] = jnp.sum(jnp.where(m, g, 0.0), axis=0, keepdims=True)
    idx_ref[...] = la_ref[...].T.astype(jnp.int32)
    gate_ref[...] = lg_ref[...].T
    n0_ref[...] = n0.astype(jnp.int32)


def _post_mix(x, att_p, att_s, ret_p, ret_s, w_out_bf, norm_w, wq_t_bf, sk1_bf, sk2_bf):
    N = x.shape[0]
    nt = N // _TMP
    ntp = att_p.shape[0] // _TMP
    nts = att_s.shape[0] // _TMP
    assert ntp + nts == nt and ntp > 0 and nts > 0
    npair = PEER_HEADS * PEER_TOPK
    row = lambda w: pl.BlockSpec((_TMP, w), lambda i: (i, 0))
    prow = lambda w: pl.BlockSpec((_TMP, w), lambda i: (jnp.minimum(i, ntp - 1), 0))
    srow = lambda w: pl.BlockSpec((_TMP, w), lambda i: (jnp.maximum(i - ntp, 0), 0))
    full = lambda a: pl.BlockSpec(a.shape, lambda i: (0,) * a.ndim)
    nw = norm_w.reshape(1, D_MODEL)
    return pl.pallas_call(
        functools.partial(_post_mix_kernel, n_prompt_tiles=ntp),
        grid=(nt,),
        in_specs=[row(D_MODEL), prow(ATT_WIDTH), srow(ATT_WIDTH), prow(RET_WIDTH), srow(RET_WIDTH),
                  full(w_out_bf), full(nw), full(wq_t_bf), full(sk1_bf), full(sk2_bf)],
        out_specs=[row(D_MODEL), pl.BlockSpec((_TMP * SUBLANES, LANES), lambda i: (i, 0)),
                   row(npair), row(npair), pl.BlockSpec((1, _TMP), lambda i: (0, i))],
        out_shape=[jax.ShapeDtypeStruct((N, D_MODEL), F32), jax.ShapeDtypeStruct((N * SUBLANES, LANES), F32),
                   jax.ShapeDtypeStruct((N, npair), jnp.int32), jax.ShapeDtypeStruct((N, npair), F32),
                   jax.ShapeDtypeStruct((1, N), jnp.int32)],
        scratch_shapes=[pltpu.VMEM((PEER_HEADS * PEER_KEY_DIM, _TMP), F32),
                        pltpu.VMEM((npair, _TMP), F32), pltpu.VMEM((npair, _TMP), F32),
                        pltpu.VMEM((npair, _TMP), F32), pltpu.VMEM((npair, _TMP), F32)],
        name="post_mix",
        compiler_params=_cparams(("arbitrary",)),
    )(x, att_p, att_s, ret_p, ret_s, w_out_bf, nw, wq_t_bf, sk1_bf, sk2_bf)


_TBP = 128
_NPAIR = PEER_HEADS * PEER_TOPK
_ROW_TILE = D_MODEL // LANES
assert _ROW_TILE == SUBLANES
_HIGH_HALF = -65536


def _bf16_bits(x):
    return lax.bitcast_convert_type(x.astype(BF16).astype(F32), jnp.int32)


def _pack_table_kernel(lo_ref, hi_ref, o_ref):
    n = lo_ref.shape[0]
    words = (lax.shift_right_logical(_bf16_bits(lo_ref[...]), 16)
             | (_bf16_bits(hi_ref[...]) & _HIGH_HALF))
    for c in range(_ROW_TILE):
        o_ref[pl.ds(c, n, stride=_ROW_TILE), :] = words[:, c * LANES:(c + 1) * LANES]


def _pack_table(tbl):
    blk = 512
    nb = PEER_HALF_EXPERTS // blk
    return pl.pallas_call(
        _pack_table_kernel,
        grid=(nb,),
        in_specs=[pl.BlockSpec((blk, D_MODEL), lambda i: (i, 0)),
                  pl.BlockSpec((blk, D_MODEL), lambda i: (i + nb, 0))],
        out_specs=pl.BlockSpec((blk * _ROW_TILE, LANES), lambda i: (i, 0)),
        out_shape=jax.ShapeDtypeStruct((PEER_HALF_EXPERTS * _ROW_TILE, LANES), jnp.int32),
        name="pack_table",
        compiler_params=_cparams(("arbitrary",)),
    )(tbl, tbl)


def _unpack(words, half):
    bits = lax.shift_left(words, 16) if half == 0 else words & _HIGH_HALF
    return lax.bitcast_convert_type(bits, F32)


def _load_table(tbl_hbm, tbuf, sem):
    cp = pltpu.make_async_copy(tbl_hbm, tbuf, sem)
    cp.start()
    cp.wait()


def _fold_pair(a, b, k, sub):
    m = (sub & k) == 0
    return jnp.where(m, a, pltpu.roll(b, k, 0)) + jnp.where(m, pltpu.roll(a, SUBLANES - k, 0), b)


def _fold8(p, sub):
    a, b, c, d, e, f, g, h = p[0], p[4], p[2], p[6], p[1], p[5], p[3], p[7]
    t1, t2, t3, t4 = (_fold_pair(a, b, 4, sub), _fold_pair(c, d, 4, sub),
                      _fold_pair(e, f, 4, sub), _fold_pair(g, h, 4, sub))
    u1, u2 = _fold_pair(t1, t2, 2, sub), _fold_pair(t3, t4, 2, sub)
    return _fold_pair(u1, u2, 1, sub)


def _flat_smem_spec(nt, per_token):
    return pl.BlockSpec((1, 1, _TBP * per_token), lambda i: (i, 0, 0), memory_space=pltpu.SMEM)


def _flat_blocks(a, nt):
    return a.reshape(nt, 1, -1)


_N_GROUPS = _NPAIR // _PAIR_GROUP


def _group_start(t, g):
    g = jnp.minimum(g, _N_GROUPS - 1)
    return pl.multiple_of(t * _NPAIR + g * _PAIR_GROUP, _PAIR_GROUP)


def _table_row(tbuf, idx_ref, k):
    off = pl.multiple_of(idx_ref[0, 0, k], _ROW_TILE)
    return tbuf[pl.ds(off, _ROW_TILE), :]


def _token_segments(n0):
    g_mid = lax.shift_right_logical(n0, _PAIR_GROUP.bit_length() - 1)
    rem = n0 & (_PAIR_GROUP - 1)
    g_hi = g_mid + jnp.where(rem != 0, 1, 0)
    return g_mid, rem, g_hi


def _mixed_unpack(words, s, rem):
    return jnp.where(s < rem, _unpack(words, 0), _unpack(words, 1))


def _peer_u_kernel(idx_ref, n0_ref, x_ref, u_hbm, o_ref, tbuf, sem, z_ref):
    @pl.when(pl.program_id(0) == 0)
    def _():
        _load_table(u_hbm, tbuf, sem)

    sub = lax.broadcasted_iota(jnp.int32, (SUBLANES, LANES), 0)

    def tok_body(t, _):
        xt = x_ref[t]
        g_mid, rem, g_hi = _token_segments(n0_ref[0, 0, t])

        def load_words(g):
            kg = _group_start(t, g)
            return tuple(_table_row(tbuf, idx_ref, kg + s) for s in range(_PAIR_GROUP))

        def fold_group(g, vals):
            kg = _group_start(t, g)
            prods = [v * xt for v in vals]
            for q in range(_PAIR_GROUP // SUBLANES):
                z_ref[pl.ds(kg + q * SUBLANES, SUBLANES), :] = _fold8(prods[q * SUBLANES:(q + 1) * SUBLANES], sub)

        def uniform_groups(ga, gb, half):
            def grp_body(g, words):
                nxt = load_words(g + 1)
                fold_group(g, [_unpack(w, half) for w in words])
                return nxt
            lax.fori_loop(ga, gb, grp_body, load_words(ga))

        uniform_groups(0, g_mid, 0)

        @pl.when(rem != 0)
        def _():
            fold_group(g_mid, [_mixed_unpack(w, s, rem) for s, w in enumerate(load_words(g_mid))])

        uniform_groups(g_hi, _N_GROUPS, 1)
        return 0

    lax.fori_loop(0, _TBP, tok_body, 0)

    ones = jnp.ones((LANES, LANES), BF16)
    eye = (lax.broadcasted_iota(jnp.int32, (_NPAIR, LANES), 0)
           == lax.broadcasted_iota(jnp.int32, (_NPAIR, LANES), 1))

    def chunk_body(c, _):
        r0 = pl.multiple_of(c * SUBLANES * _NPAIR, SUBLANES * _NPAIR)
        z = z_ref[pl.ds(r0, SUBLANES * _NPAIR), :]
        zh = z.astype(BF16)
        zl = (z - zh.astype(F32)).astype(BF16)
        hv = jnp.dot(zh, ones, preferred_element_type=F32) + jnp.dot(zl, ones, preferred_element_type=F32)
        rows = [jnp.sum(jnp.where(eye, hv[tt * _NPAIR:(tt + 1) * _NPAIR], 0.0), axis=0, keepdims=True)
                for tt in range(SUBLANES)]
        o_ref[pl.ds(pl.multiple_of(c * SUBLANES, SUBLANES), SUBLANES), :] = jnp.concatenate(rows, axis=0)
        return 0

    lax.fori_loop(0, _TBP // SUBLANES, chunk_body, 0)


def _peer_u(idx, n0, xn_tiles, u_packed):
    N = xn_tiles.shape[0] // _ROW_TILE
    nt = N // _TBP
    return pl.pallas_call(
        _peer_u_kernel,
        grid=(nt,),
        in_specs=[_flat_smem_spec(nt, _NPAIR), _flat_smem_spec(nt, 1),
                  pl.BlockSpec((_TBP, _ROW_TILE, LANES), lambda i: (i, 0, 0)),
                  pl.BlockSpec(memory_space=pl.ANY)],
        out_specs=pl.BlockSpec((_TBP, _NPAIR), lambda i: (i, 0)),
        out_shape=jax.ShapeDtypeStruct((N, _NPAIR), F32),
        scratch_shapes=[pltpu.VMEM((PEER_HALF_EXPERTS * _ROW_TILE, LANES), jnp.int32),
                        pltpu.SemaphoreType.DMA(()),
                        pltpu.VMEM((_TBP * _NPAIR, LANES), F32)],
        name="peer_u",
        compiler_params=_cparams(("arbitrary",)),
    )(_flat_blocks(idx, nt), _flat_blocks(n0, nt), xn_tiles.reshape(N, _ROW_TILE, LANES), u_packed)


def _peer_coef_kernel(hv_ref, gate_ref, o_ref):
    hv = hv_ref[...]
    o_ref[...] = gate_ref[...] * (0.5 * hv * (1.0 + lax.erf(hv * (0.5 ** 0.5))))


def _peer_coef(hval, gate):
    N = gate.shape[0]
    tm = 1024 if N % 1024 == 0 else _TBP
    spec = pl.BlockSpec((tm, _NPAIR), lambda i: (i, 0))
    return pl.pallas_call(
        _peer_coef_kernel,
        grid=(N // tm,),
        in_specs=[spec, spec],
        out_specs=spec,
        out_shape=jax.ShapeDtypeStruct((N, _NPAIR), F32),
        name="peer_coef",
        compiler_params=_cparams(("arbitrary",)),
    )(hval, gate)


def _peer_v_kernel(idx_ref, n0_ref, coef_ref, v_hbm, o_ref, tbuf, sem):
    @pl.when(pl.program_id(0) == 0)
    def _():
        _load_table(v_hbm, tbuf, sem)

    n_acc = 4

    def accumulate(accs, prods):
        accs = list(accs)
        for s in range(_PAIR_GROUP):
            accs[s % n_acc] = accs[s % n_acc] + prods[s]
        return tuple(accs)

    def tok_body(t, _):
        g_mid, rem, g_hi = _token_segments(n0_ref[0, 0, t])

        def weighted_group(g, unpack):
            kg = _group_start(t, g)
            return tuple(coef_ref[0, 0, kg + s] * unpack(_table_row(tbuf, idx_ref, kg + s), s)
                         for s in range(_PAIR_GROUP))

        def uniform_groups(ga, gb, half, accs):
            unpack = lambda w, s: _unpack(w, half)

            def grp_body(g, carry):
                prods, accs = carry
                nxt = weighted_group(g + 1, unpack)
                return nxt, accumulate(accs, prods)

            return lax.fori_loop(ga, gb, grp_body, (weighted_group(ga, unpack), accs))[1]

        zero = jnp.zeros((_ROW_TILE, LANES), F32)
        accs = uniform_groups(0, g_mid, 0, (zero,) * n_acc)
        accs = lax.cond(
            rem != 0,
            lambda a: accumulate(a, weighted_group(g_mid, lambda w, s: _mixed_unpack(w, s, rem))),
            lambda a: a, accs)
        accs = uniform_groups(g_hi, _N_GROUPS, 1, accs)
        o_ref[t] = (accs[0] + accs[1]) + (accs[2] + accs[3])
        return 0

    lax.fori_loop(0, _TBP, tok_body, 0)


def _peer_v(idx, n0, coef, v_packed):
    N = coef.shape[0]
    nt = N // _TBP
    out = pl.pallas_call(
        _peer_v_kernel,
        grid=(nt,),
        in_specs=[_flat_smem_spec(nt, _NPAIR), _flat_smem_spec(nt, 1), _flat_smem_spec(nt, _NPAIR),
                  pl.BlockSpec(memory_space=pl.ANY)],
        out_specs=pl.BlockSpec((_TBP, _ROW_TILE, LANES), lambda i: (i, 0, 0)),
        out_shape=jax.ShapeDtypeStruct((N, _ROW_TILE, LANES), F32),
        scratch_shapes=[pltpu.VMEM((PEER_HALF_EXPERTS * _ROW_TILE, LANES), jnp.int32),
                        pltpu.SemaphoreType.DMA(())],
        name="peer_v",
        compiler_params=_cparams(("arbitrary",)),
    )(_flat_blocks(idx, nt), _flat_blocks(n0, nt), _flat_blocks(coef, nt), v_packed)
    return out.reshape(N * _ROW_TILE, LANES)


def _final_kernel(h_ref, y_ref, w_ref, o_ref, *, normalize):
    T = h_ref.shape[0]
    y = jnp.concatenate(
        [y_ref[pl.ds(c, T, stride=SUBLANES), :] for c in range(D_MODEL // LANES)], axis=1)
    h = h_ref[...] + y
    if normalize:
        ms = jnp.mean(h * h, axis=-1, keepdims=True)
        h = h * lax.rsqrt(ms + NORM_EPS) * w_ref[...]
    o_ref[...] = h


def _final(h, ypart, w):
    N = h.shape[0]
    tm = 512 if N % 512 == 0 else _TBP
    normalize = w is not None
    if not normalize:
        w = jnp.ones((D_MODEL,), F32)
    return pl.pallas_call(
        functools.partial(_final_kernel, normalize=normalize),
        grid=(N // tm,),
        in_specs=[pl.BlockSpec((tm, D_MODEL), lambda i: (i, 0)),
                  pl.BlockSpec((tm * SUBLANES, LANES), lambda i: (i, 0)),
                  pl.BlockSpec((1, D_MODEL), lambda i: (0, 0))],
        out_specs=pl.BlockSpec((tm, D_MODEL), lambda i: (i, 0)),
        out_shape=jax.ShapeDtypeStruct((N, D_MODEL), F32),
        name="final_norm",
        compiler_params=_cparams(("arbitrary",)),
    )(h, ypart, w.reshape(1, D_MODEL))


_TM_IN = 256


def _pages_feature_major(cache_l):
    n_phys, page = cache_l.shape[:2]
    return jnp.swapaxes(cache_l.reshape(n_phys, page, -1), 1, 2)


def kernel(x_prompt, x_sample, cache_k, cache_v, cache_idx_k, state_ret, page_table, norm_attn_w, w_in, ret_gn_w,
           w_out, norm_ffn_w, peer_w_q, peer_sub_keys_1, peer_sub_keys_2, peer_u, peer_v, final_norm_w):
    B, S, D = x_prompt.shape
    Bd, T, _ = x_sample.shape
    depth = w_in.shape[0]
    n_pages = page_table.shape[1]
    past_len = n_pages * PAGE_SIZE
    n_phys = cache_k.shape[1]
    Np, Ns = B * S, Bd * SROWS
    assert D == D_MODEL and T <= SROWS and S % _TM_IN == 0 and Ns % _TM_IN == 0 and S % _CK == 0

    hs_pad = jnp.pad(x_sample, ((0, 0), (0, SROWS - T), (0, 0)))
    h_all = jnp.concatenate([x_prompt.reshape(Np, D), hs_pad.reshape(Ns, D)], axis=0)

    pos_s = past_len + jnp.arange(SROWS)
    pos = jnp.concatenate([jnp.arange(S), jnp.tile(pos_s, _TM_IN // SROWS)])
    tab_att, tab_ret = _rope_tables(pos)
    tiles_per_seq = S // _TM_IN
    tab_blocks = [i % tiles_per_seq for i in range(Np // _TM_IN)] + [tiles_per_seq] * (Ns // _TM_IN)
    ret_consts_p = _ret_constants(RET_CHUNK)
    ret_consts_s = _ret_constants(T)

    outs = {n: [] for n in ("kp", "vp", "ikp", "sp", "ks", "vs", "iks", "ss")}
    for l in range(depth):
        m = _in_proj(h_all, norm_attn_w[l], _pad_w_in(w_in[l]), tab_att, tab_ret, tab_blocks, _TM_IN)
        att_p = _prompt_dsa(m["qi"], m["kiwi"], m["qa"], m["ka"], m["va"], B, S)
        att_s = _sample_dsa(page_table, m["qi"], m["kiwi"], m["qa"], m["ka"], m["va"],
                            _pages_feature_major(cache_idx_k[l]), _pages_feature_major(cache_k[l]),
                            _pages_feature_major(cache_v[l]), T, Np)
        ret_p, s_p = _retention(m["qr"], m["kr"], m["vr"], m["gr"], ret_gn_w[l],
                                jnp.zeros((B, RET_HEADS, RET_HEAD_DIM, RET_HEAD_DIM), F32),
                                ret_consts_p, B, S // RET_CHUNK, RET_CHUNK)
        ret_s, s_s = _retention(m["qr"], m["kr"], m["vr"], m["gr"], ret_gn_w[l], state_ret[l].astype(F32),
                                ret_consts_s, Bd, 1, SROWS, row0=Np)
        h_mid, xn, idx, gate, n0 = _post_mix(
            h_all, att_p, att_s, ret_p, ret_s, w_out[l].astype(BF16), norm_ffn_w[l],
            peer_w_q[l].T.astype(BF16), peer_sub_keys_1[l].astype(BF16), peer_sub_keys_2[l].astype(BF16))
        coef = _peer_coef(_peer_u(idx, n0, xn, _pack_table(peer_u[l])), gate)
        ypart = _peer_v(idx, n0, coef, _pack_table(peer_v[l]))
        last = l == depth - 1
        h_all = _final(h_mid, ypart, final_norm_w if last else None)

        sample = lambda a, w: a[Np:].reshape(Bd, SROWS, w)[:, :T]
        outs["kp"].append(m["ka"][:Np].reshape(B, S, KV_HEADS, ATT_HEAD_DIM))
        outs["vp"].append(m["va"][:Np].reshape(B, S, KV_HEADS, ATT_HEAD_DIM))
        outs["ikp"].append(m["kiwi"][:Np, :IDX_DIM].reshape(B, S, IDX_DIM))
        outs["sp"].append(s_p.astype(x_prompt.dtype))
        outs["ks"].append(sample(m["ka"], KV_WIDTH).reshape(Bd, T, KV_HEADS, ATT_HEAD_DIM))
        outs["vs"].append(sample(m["va"], KV_WIDTH).reshape(Bd, T, KV_HEADS, ATT_HEAD_DIM))
        outs["iks"].append(sample(m["kiwi"], _KIWI_WIDTH)[:, :, :IDX_DIM])
        outs["ss"].append(s_s.astype(state_ret.dtype))

    y_prompt = h_all[:Np].reshape(B, S, D)
    y_sample = h_all[Np:].reshape(Bd, SROWS, D)[:, :T]
    st = lambda n: jnp.stack(outs[n])
    return (y_prompt, y_sample, st("kp"), st("vp"), st("ikp"), st("sp"),
            st("ks"), st("vs"), st("iks"), st("ss"))
```

```python
import functools

import jax
import jax.numpy as jnp
import numpy as np
from jax import lax
from jax.experimental import pallas as pl
from jax.experimental.pallas import tpu as pltpu

F32 = jnp.float32
BF16 = jnp.bfloat16

D_MODEL = 1024
PAGE_SIZE = 128
ATT_HEADS = 8
ATT_HEAD_DIM = 64
KV_HEADS = 2
ATT_GROUP = ATT_HEADS // KV_HEADS
ATT_WIDTH = ATT_HEADS * ATT_HEAD_DIM
KV_WIDTH = KV_HEADS * ATT_HEAD_DIM
ROPE_THETA = 500000.0
ATT_ROPE_DIMS = ATT_HEAD_DIM // 4
IDX_HEADS = 8
IDX_DIM = 64
IDX_WIDTH = IDX_HEADS * IDX_DIM
TOPK_MAX = 256
RET_HEADS = 4
RET_HEAD_DIM = 128
RET_WIDTH = RET_HEADS * RET_HEAD_DIM
RET_ROPE_THETA = 10000.0
RET_CHUNK = 128
PEER_HEADS = 8
PEER_NKEYS = 128
PEER_EXPERTS = PEER_NKEYS * PEER_NKEYS
PEER_KEY_DIM = 128
PEER_HALF = PEER_KEY_DIM // 2
PEER_TOPK = 16
NORM_EPS = 1e-6
GN_EPS = 1e-6

LANES = 128
SUBLANES = 8
VMEM_LIMIT_BYTES = 56 * 1024 * 1024

NEG_BIG = -1e30

_IN_SPLITS = (ATT_WIDTH, KV_WIDTH, KV_WIDTH, IDX_WIDTH, IDX_DIM, IDX_HEADS,
              RET_WIDTH, RET_WIDTH, RET_WIDTH, RET_WIDTH)
_KIWI_WIDTH = LANES
_PAD_OFFS = {}
_off = 0
for _name, _w in (("qa", ATT_WIDTH), ("ka", KV_WIDTH), ("va", KV_WIDTH), ("qi", IDX_WIDTH),
                  ("kiwi", _KIWI_WIDTH), ("qr", RET_WIDTH), ("kr", RET_WIDTH),
                  ("vr", RET_WIDTH), ("gr", RET_WIDTH)):
    _PAD_OFFS[_name] = (_off, _w)
    _off += _w
IN_WIDTH_PADDED = _off


def _cparams(sem):
    return pltpu.CompilerParams(dimension_semantics=sem, vmem_limit_bytes=VMEM_LIMIT_BYTES)


def _dot(a, b):
    return jnp.dot(a.astype(BF16), b.astype(BF16), preferred_element_type=F32)


def _dot_nt(a, b):
    return lax.dot_general(a.astype(BF16), b.astype(BF16), (((1,), (1,)), ((), ())),
                           preferred_element_type=F32)


def _pad_w_in(w_in_l):
    cols = []
    off = 0
    parts = []
    for n in _IN_SPLITS:
        parts.append(w_in_l[:, off:off + n])
        off += n
    qa, ka, va, qi, ki, wi, qr, kr, vr, gr = parts
    kiwi = jnp.concatenate(
        [ki, wi, jnp.zeros((w_in_l.shape[0], _KIWI_WIDTH - IDX_DIM - IDX_HEADS), w_in_l.dtype)], axis=1)
    cols = [qa, ka, va, qi, kiwi, qr, kr, vr, gr]
    return jnp.concatenate(cols, axis=1).astype(BF16)


def _rope_tables(pos):
    posf = pos.astype(F32)
    half = ATT_ROPE_DIMS // 2
    inv = 1.0 / (ROPE_THETA ** (jnp.arange(half, dtype=F32) / half))
    ang = posf[:, None] * inv[None, :]
    cos, sin = jnp.cos(ang), jnp.sin(ang)
    P = pos.shape[0]
    one = jnp.ones((P, ATT_HEAD_DIM - ATT_ROPE_DIMS), F32)
    zero = jnp.zeros((P, ATT_HEAD_DIM - ATT_ROPE_DIMS), F32)
    zh = jnp.zeros((P, half), F32)
    c_head = jnp.concatenate([cos, cos, one], axis=1)
    s1_head = jnp.concatenate([-sin, zh, zero], axis=1)
    s2_head = jnp.concatenate([zh, sin, zero], axis=1)
    rep = LANES // ATT_HEAD_DIM
    att = jnp.stack([jnp.tile(c_head, (1, rep)), jnp.tile(s1_head, (1, rep)), jnp.tile(s2_head, (1, rep))])
    halfr = RET_HEAD_DIM // 2
    invr = 1.0 / (RET_ROPE_THETA ** (jnp.arange(halfr, dtype=F32) / halfr))
    angr = posf[:, None] * invr[None, :]
    cr, sr = jnp.cos(angr), jnp.sin(angr)
    ret = jnp.stack([jnp.concatenate([cr, cr], axis=1), jnp.concatenate([-sr, sr], axis=1)])
    return att, ret


def _in_proj_kernel(x_ref, nw_ref, w_ref, ta_ref, tr_ref,
                    qa_ref, ka_ref, va_ref, qi_ref, kiwi_ref, qr_ref, kr_ref, vr_ref, gr_ref):
    x = x_ref[...]
    ms = jnp.mean(x * x, axis=-1, keepdims=True)
    xn = (x * lax.rsqrt(ms + NORM_EPS) * nw_ref[...]).astype(BF16)
    ca, s1a, s2a = ta_ref[0], ta_ref[1], ta_ref[2]
    cr, sr = tr_ref[0], tr_ref[1]

    def proj(name):
        off, w = _PAD_OFFS[name]
        return jnp.dot(xn, w_ref[:, off:off + w], preferred_element_type=F32)

    def rope_att_chunk(vc):
        return (vc * ca + pltpu.roll(vc, LANES - ATT_ROPE_DIMS // 2, 1) * s1a
                + pltpu.roll(vc, ATT_ROPE_DIMS // 2, 1) * s2a)

    def rope_ret_chunk(vc):
        return vc * cr + pltpu.roll(vc, RET_HEAD_DIM // 2, 1) * sr

    def per_chunk(v, fn):
        n = v.shape[1] // LANES
        return jnp.concatenate([fn(v[:, c * LANES:(c + 1) * LANES]) for c in range(n)], axis=1)

    qa_ref[...] = per_chunk(proj("qa"), rope_att_chunk)
    ka_ref[...] = per_chunk(proj("ka"), rope_att_chunk)
    va_ref[...] = proj("va")
    qi_ref[...] = per_chunk(proj("qi"), rope_att_chunk)
    kiwi = proj("kiwi")
    lane = lax.broadcasted_iota(jnp.int32, kiwi.shape, 1)
    kiwi_ref[...] = jnp.where(lane < IDX_DIM, rope_att_chunk(kiwi), kiwi * (IDX_HEADS ** -0.5))
    qr_ref[...] = per_chunk(proj("qr"), rope_ret_chunk)
    kr_ref[...] = per_chunk(proj("kr"), rope_ret_chunk) * (RET_HEAD_DIM ** -0.5)
    vr_ref[...] = proj("vr")
    gr_ref[...] = proj("gr")


def _in_proj(x, norm_w, w_pad, tab_att, tab_ret, tab_block_of_tile, tm):
    N = x.shape[0]
    nt = N // tm
    tab_idx = jnp.asarray(tab_block_of_tile, jnp.int32)
    names = ("qa", "ka", "va", "qi", "kiwi", "qr", "kr", "vr", "gr")
    out_shape = [jax.ShapeDtypeStruct((N, _PAD_OFFS[n][1]), F32) for n in names]
    out_specs = [pl.BlockSpec((tm, _PAD_OFFS[n][1]), lambda i, t: (i, 0)) for n in names]
    grid_spec = pltpu.PrefetchScalarGridSpec(
        num_scalar_prefetch=1,
        grid=(nt,),
        in_specs=[
            pl.BlockSpec((tm, D_MODEL), lambda i, t: (i, 0)),
            pl.BlockSpec((1, D_MODEL), lambda i, t: (0, 0)),
            pl.BlockSpec((D_MODEL, IN_WIDTH_PADDED), lambda i, t: (0, 0)),
            pl.BlockSpec((3, tm, LANES), lambda i, t: (0, t[i], 0)),
            pl.BlockSpec((2, tm, LANES), lambda i, t: (0, t[i], 0)),
        ],
        out_specs=out_specs,
    )

    def body(t_ref, *refs):
        _in_proj_kernel(*refs)

    outs = pl.pallas_call(
        body, grid_spec=grid_spec, out_shape=out_shape, name="in_proj",
        compiler_params=_cparams(("arbitrary",)),
    )(tab_idx, x, norm_w.reshape(1, D_MODEL), w_pad, tab_att, tab_ret)
    return dict(zip(names, outs))


def _ret_constants(c_eff):
    C = RET_CHUNK
    lg = jnp.log1p(-(2.0 ** (-5.0 - jnp.arange(RET_HEADS, dtype=F32))))
    i = jnp.arange(C, dtype=F32)
    diff = i[:, None] - i[None, :]
    dmask = jnp.where(diff >= 0, jnp.exp(lg[:, None, None] * jnp.maximum(diff, 0.0)), 0.0)
    real = (i < c_eff)
    dmask = jnp.where(real[None, :, None] & real[None, None, :], dmask, 0.0)
    q_dec = jnp.exp(lg[:, None] * (i[None, :] + 1.0))
    k_dec = jnp.where(real[None, :], jnp.exp(lg[:, None] * (c_eff - 1.0 - i[None, :])), 0.0)
    chunk_dec = jnp.exp(lg * c_eff)
    bc = lambda a: jnp.broadcast_to(a[:, :, None], (RET_HEADS, C, C))
    cdec = jnp.broadcast_to(chunk_dec[:, None, None], (RET_HEADS, C, C))
    return dmask, bc(q_dec), bc(k_dec), cdec


def _retention_kernel(q_ref, k_ref, v_ref, g_ref, gnw_ref, s0_ref, dm_ref, qd_ref, kd_ref, cd_ref,
                      o_ref, s_out_ref, state_ref, *, rows):
    c = pl.program_id(1)
    nc = pl.num_programs(1)

    @pl.when(c == 0)
    def _():
        state_ref[...] = s0_ref[0]

    def padded(ref):
        v = ref[...]
        if rows < RET_CHUNK:
            v = jnp.concatenate([v, jnp.zeros((RET_CHUNK - rows, v.shape[1]), v.dtype)], axis=0)
        return v

    q, k, v, g = padded(q_ref), padded(k_ref), padded(v_ref), g_ref[...]
    gnw = gnw_ref[...]
    outs = []
    for h in range(RET_HEADS):
        sl = slice(h * RET_HEAD_DIM, (h + 1) * RET_HEAD_DIM)
        qh, kh, vh = q[:, sl], k[:, sl], v[:, sl]
        st = state_ref[h]
        att = _dot_nt(qh, kh) * dm_ref[h]
        o = _dot(att, vh) + _dot(qh, st) * qd_ref[h]
        state_ref[h] = st * cd_ref[h] + _dot((kh * kd_ref[h]).T, vh)
        o = o[:rows]
        mu = jnp.mean(o, axis=-1, keepdims=True)
        var = jnp.mean(jnp.square(o - mu), axis=-1, keepdims=True)
        outs.append((o - mu) * lax.rsqrt(var + GN_EPS) * gnw[:, sl])
    on = jnp.concatenate(outs, axis=1)
    o_ref[...] = g * (1.0 / (1.0 + jnp.exp(-g))) * on

    @pl.when(c == nc - 1)
    def _():
        s_out_ref[0] = state_ref[...]


def _retention(q, k, v, g, gn_w, s0, consts, nb, nc, rows, row0=0):
    blk0 = row0 // rows
    in_row = pl.BlockSpec((rows, RET_WIDTH), lambda b, c: (blk0 + b * nc + c, 0))
    out_row = pl.BlockSpec((rows, RET_WIDTH), lambda b, c: (b * nc + c, 0))
    st_spec = pl.BlockSpec((1, RET_HEADS, RET_HEAD_DIM, RET_HEAD_DIM), lambda b, c: (b, 0, 0, 0))
    const_spec = pl.BlockSpec((RET_HEADS, RET_CHUNK, RET_CHUNK), lambda b, c: (0, 0, 0))
    return pl.pallas_call(
        functools.partial(_retention_kernel, rows=rows),
        grid=(nb, nc),
        in_specs=[in_row, in_row, in_row, in_row,
                  pl.BlockSpec((1, RET_WIDTH), lambda b, c: (0, 0)),
                  st_spec, const_spec, const_spec, const_spec, const_spec],
        out_specs=[out_row, st_spec],
        out_shape=[jax.ShapeDtypeStruct((nb * nc * rows, RET_WIDTH), F32),
                   jax.ShapeDtypeStruct(s0.shape, F32)],
        scratch_shapes=[pltpu.VMEM((RET_HEADS, RET_HEAD_DIM, RET_HEAD_DIM), F32)],
        name="retention",
        compiler_params=_cparams(("arbitrary", "arbitrary")),
    )(q, k, v, g, gn_w.reshape(1, RET_WIDTH), s0, *consts)


_BISECT_PLAIN_ITERS = 26
_BISECT_MAX_ITERS = 400


def _lane_tiles(x):
    return [x[:, j * LANES:(j + 1) * LANES] for j in range(x.shape[1] // LANES)]


def _rowsum_b(x):
    return jnp.broadcast_to(jnp.sum(x, axis=1, keepdims=True), x.shape)


def _select_chunk(x, lo, hi, need, tied_any, tied, prefix0):
    R, CK = x.shape
    rep = CK // LANES
    wide = lambda a: jnp.concatenate([a] * rep, axis=1)

    def plain(_):
        return jnp.where(x >= wide(lo), 1.0, 0.0), prefix0

    def with_ties(_):
        low, hiw = wide(lo), wide(hi)
        band = (x >= low) & (x < hiw)
        bandf = jnp.where(band, 1.0, 0.0)
        r_i = lax.broadcasted_iota(jnp.int32, (CK, CK), 0)
        c_i = lax.broadcasted_iota(jnp.int32, (CK, CK), 1)
        tri = jnp.where(r_i <= c_i, 1.0, 0.0).astype(BF16)
        rank = jnp.dot(bandf.astype(BF16), tri, preferred_element_type=F32) + wide(prefix0)
        take_tie = jnp.where(rank <= wide(need), bandf, 0.0)
        sel_tied = jnp.where(x >= hiw, 1.0, take_tie)
        sel = jnp.where(wide(tied) > 0.0, sel_tied, jnp.where(x >= low, 1.0, 0.0))
        return sel, prefix0 + _rowsum_b(sum(_lane_tiles(bandf)))

    return lax.cond(tied_any, with_ties, plain, 0)


_QB = 128
_CK = 512


def _col_partial(x, op):
    n_chain = 4
    g = x.shape[0] // (SUBLANES * n_chain)
    x4 = x.reshape(n_chain, g, SUBLANES, x.shape[1])
    c = [op(x4[k], axis=0) for k in range(n_chain)]
    pair = jnp.stack([c[0], c[1]]), jnp.stack([c[2], c[3]])
    return op(jnp.stack([op(pair[0], axis=0), op(pair[1], axis=0)]), axis=0)


def _col_fold(x, op):
    return op(_col_partial(x, op), axis=0, keepdims=True)


def _bisect(count_pass, kk, lo0, hi0, n_causal, active):
    def not_done(c_lo, lo, hib):
        pending = active & (c_lo != kk) & (lo != hib)
        return jnp.max(jnp.where(pending, 1.0, 0.0)) > 0.0

    def make_iter(snap, reps):
        def one(state):
            lo, hi, hib, c_lo, c_hi = state
            if snap:
                mid = 0.5 * (lo + jnp.minimum(hi, hib))
                mid = jnp.where(mid > lo, mid, jnp.minimum(hi, hib))
            else:
                mid = 0.5 * (lo + hi)
            c, amin, bmax = count_pass(mid, snap)
            ge = c >= kk
            if snap:
                lo = jnp.where(ge, amin, lo)
                hib = jnp.where(ge, hib, bmax)
            else:
                lo = jnp.where(ge, mid, lo)
            hi = jnp.where(ge, hi, mid)
            return lo, hi, hib, jnp.where(ge, c, c_lo), jnp.where(ge, c_hi, c)

        def step(carry):
            it, state = carry[0], carry[1:]
            for _ in range(reps):
                state = one(state)
            return (it + reps, *state)
        return step

    init = (jnp.int32(0), lo0, hi0, jnp.full(lo0.shape, jnp.inf, F32), n_causal, jnp.zeros(lo0.shape, F32))
    carry = lax.while_loop(
        lambda c: (c[0] < _BISECT_PLAIN_ITERS) & not_done(c[4], c[1], c[3]), make_iter(False, 2), init)
    carry = lax.while_loop(
        lambda c: (c[0] < _BISECT_MAX_ITERS) & not_done(c[4], c[1], c[3]), make_iter(True, 1), carry)
    _, lo, hi, hib, c_lo, c_hi = carry
    return lo, hi, c_hi, active & (c_lo != kk)


def _threshold_search_t(sc_ref, nck, kk, lo0, hi0, n_causal):
    Q = lo0.shape[1]

    def count_pass(mid, snap):
        def body(kc, carry):
            x = sc_ref[kc]
            cnt, amin, bmax = carry
            ge = x >= mid
            cnt = cnt + _col_partial(jnp.where(ge, 1.0, 0.0), jnp.sum)
            if snap:
                amin = jnp.minimum(amin, _col_partial(jnp.where(ge, x, jnp.inf), jnp.min))
                bmax = jnp.maximum(bmax, _col_partial(jnp.where(ge, -jnp.inf, x), jnp.max))
            return cnt, amin, bmax
        init = (jnp.zeros((SUBLANES, Q), F32), jnp.full((SUBLANES, Q), jnp.inf, F32),
                jnp.full((SUBLANES, Q), -jnp.inf, F32))
        cnt, amin, bmax = lax.fori_loop(0, nck, body, init)
        return (jnp.sum(cnt, axis=0, keepdims=True), jnp.min(amin, axis=0, keepdims=True),
                jnp.max(bmax, axis=0, keepdims=True))

    return _bisect(count_pass, kk, lo0, hi0, n_causal, lo0 == lo0)


def _threshold_search_rows(sc_ref, kk, lo0, hi0, n_causal, active):
    def count_pass(mid, snap):
        x = sc_ref[...]
        ge = x >= mid
        cnt = jnp.sum(jnp.where(ge, 1.0, 0.0), axis=1, keepdims=True)
        if not snap:
            return cnt, None, None
        return (cnt, jnp.min(jnp.where(ge, x, jnp.inf), axis=1, keepdims=True),
                jnp.max(jnp.where(ge, -jnp.inf, x), axis=1, keepdims=True))

    return _bisect(count_pass, kk, lo0, hi0, n_causal, active)


def _select_chunk_t(x, lo, hi, need, tied_any, tiedf, prefix0):
    ck = x.shape[0]

    def plain(_):
        return jnp.where(x >= lo, 1.0, 0.0), prefix0

    def with_ties(_):
        bandf = jnp.where((x >= lo) & (x < hi), 1.0, 0.0)
        r_i = lax.broadcasted_iota(jnp.int32, (ck, ck), 0)
        c_i = lax.broadcasted_iota(jnp.int32, (ck, ck), 1)
        tri = jnp.where(c_i <= r_i, 1.0, 0.0).astype(BF16)
        rank = jnp.dot(tri, bandf.astype(BF16), preferred_element_type=F32) + prefix0
        take_tie = jnp.where(rank <= need, bandf, 0.0)
        sel_tied = jnp.where(x >= hi, 1.0, take_tie)
        sel = jnp.where(tiedf > 0.0, sel_tied, jnp.where(x >= lo, 1.0, 0.0))
        return sel, prefix0 + _col_fold(bandf, jnp.sum)

    return lax.cond(tied_any, with_ties, plain, 0)


def _prompt_dsa_kernel(qi_ref, kiwiq_ref, kiwi_ref, qa_ref, ka_ref, va_ref, o_ref,
                         sc_ref, vt_ref, acc_ref, *, topk, seq):
    i = pl.program_id(1)
    t0 = i * _QB
    nck = (t0 + _QB + _CK - 1) // _CK
    Q = _QB
    HQ = ATT_HEADS * Q

    @pl.when(i == 0)
    def _():
        for j in range(seq // _CK):
            vt_ref[:, j * _CK:(j + 1) * _CK] = va_ref[j * _CK:(j + 1) * _CK, :].T

    qpos = t0 + lax.broadcasted_iota(jnp.int32, (1, Q), 1)
    key_row = lax.broadcasted_iota(jnp.int32, (_CK, Q), 0)

    qi = qi_ref[...]
    w_t = kiwiq_ref[...].T
    q_heads, w_rows = [], []
    for h in range(IDX_HEADS):
        qh = qi[:, h * IDX_DIM:(h + 1) * IDX_DIM] * (IDX_DIM ** -0.5)
        q_heads.append(jnp.concatenate([qh, jnp.zeros((Q, LANES - IDX_DIM), F32)], axis=1).astype(BF16))
        w_rows.append(w_t[IDX_DIM + h:IDX_DIM + h + 1, :])

    def score_body(kc, carry):
        mn, mx = carry
        k0 = pl.multiple_of(kc * _CK, _CK)
        kch = kiwi_ref[pl.ds(k0, _CK), :].astype(BF16)
        acc = jnp.zeros((_CK, Q), F32)
        for h in range(IDX_HEADS):
            acc = acc + w_rows[h] * jnp.maximum(_dot_nt(kch, q_heads[h]), 0.0)
        causal = (k0 + key_row) <= qpos
        sc_ref[kc] = jnp.where(causal, acc, -jnp.inf)
        mn = jnp.minimum(mn, _col_fold(jnp.where(causal, acc, jnp.inf), jnp.min))
        mx = jnp.maximum(mx, _col_fold(jnp.where(causal, acc, -jnp.inf), jnp.max))
        return mn, mx

    mn, mx = lax.fori_loop(0, nck, score_body,
                           (jnp.full((1, Q), jnp.inf, F32), jnp.full((1, Q), -jnp.inf, F32)))
    n_causal = (qpos + 1).astype(F32)
    kk = jnp.minimum(n_causal, float(topk))
    lo, hi, c_hi, tied = _threshold_search_t(sc_ref, nck, kk, mn, mx + (mx - mn) + 1.0, n_causal)
    need = kk - c_hi
    tiedf = jnp.where(tied, 1.0, 0.0)
    tied_any = jnp.max(tiedf) > 0.0

    qa = qa_ref[...]
    q_rows = []
    for h in range(ATT_HEADS):
        c = h // ATT_GROUP
        parts = [jnp.zeros((Q, ATT_HEAD_DIM), F32)] * KV_HEADS
        parts[c] = qa[:, h * ATT_HEAD_DIM:(h + 1) * ATT_HEAD_DIM] * (ATT_HEAD_DIM ** -0.5)
        q_rows.append(jnp.concatenate(parts, axis=1))
    q_all = jnp.concatenate(q_rows, axis=0).astype(BF16)
    acc_ref[...] = jnp.zeros(acc_ref.shape, F32)

    def att_body(kc, carry):
        m_old, l_old, prefix = carry
        k0 = pl.multiple_of(kc * _CK, _CK)
        kch = ka_ref[pl.ds(k0, _CK), :].astype(BF16)
        vch_t = vt_ref[:, pl.ds(k0, _CK)].astype(BF16)
        sel, prefix = _select_chunk_t(sc_ref[kc], lo, hi, need, tied_any, tiedf, prefix)
        selb = jnp.concatenate([sel] * ATT_HEADS, axis=1) > 0.0
        s = jnp.where(selb, _dot_nt(kch, q_all), NEG_BIG)
        m_new = jnp.maximum(m_old, _col_fold(s, jnp.max))
        alpha = jnp.exp(m_old - m_new)
        p = jnp.exp(s - m_new)
        l_new = l_old * alpha + _col_fold(p, jnp.sum)
        acc_ref[...] = acc_ref[...] * alpha + jnp.dot(vch_t, p.astype(BF16), preferred_element_type=F32)
        return m_new, l_new, prefix

    init = (jnp.full((1, HQ), NEG_BIG, F32), jnp.zeros((1, HQ), F32), jnp.zeros((1, Q), F32))
    _, l_fin, _ = lax.fori_loop(0, nck, att_body, init)
    o_t = acc_ref[...] / l_fin
    outs = []
    for h in range(ATT_HEADS):
        c = h // ATT_GROUP
        outs.append(o_t[:, h * Q:(h + 1) * Q].T[:, c * ATT_HEAD_DIM:(c + 1) * ATT_HEAD_DIM])
    o_ref[...] = jnp.concatenate(outs, axis=1)


def _prompt_dsa(qi, kiwi, qa, ka, va, B, S):
    nq = S // _QB
    topk = min(TOPK_MAX, S // 4)
    qblk = lambda w: pl.BlockSpec((_QB, w), lambda b, i: (b * nq + i, 0))
    allk = lambda w: pl.BlockSpec((S, w), lambda b, i: (b, 0))
    return pl.pallas_call(
        functools.partial(_prompt_dsa_kernel, topk=topk, seq=S),
        grid=(B, nq),
        in_specs=[qblk(IDX_WIDTH), qblk(_KIWI_WIDTH), allk(_KIWI_WIDTH), qblk(ATT_WIDTH),
                  allk(KV_WIDTH), allk(KV_WIDTH)],
        out_specs=qblk(ATT_WIDTH),
        out_shape=jax.ShapeDtypeStruct((B * S, ATT_WIDTH), F32),
        scratch_shapes=[pltpu.VMEM((S // _CK, _CK, _QB), F32),
                        pltpu.VMEM((KV_WIDTH, S), F32),
                        pltpu.VMEM((KV_WIDTH, ATT_HEADS * _QB), F32)],
        name="prompt_dsa",
        compiler_params=_cparams(("arbitrary", "arbitrary")),
    )(qi, kiwi, kiwi, qa, ka, va)


SROWS = SUBLANES
_CKS = 640


def _sample_dsa_kernel(pt_ref, qi_ref, kiwi_ref, qa_ref, ka_ref, va_ref, cik_hbm, ck_hbm, cv_hbm,
                       o_ref, ikbuf, kbuf, vbuf, sems, sc_ref, sel_ref, *, n_pages, t_real, topk):
    b = pl.program_id(0)
    nb = pl.num_programs(0)
    slot = b % 2
    past_len = n_pages * PAGE_SIZE
    L = past_len + PAGE_SIZE
    nck = L // _CKS
    R = SROWS

    def page_copies(bb, s, p):
        phys = pt_ref[bb, p]
        cols = pl.ds(pl.multiple_of(p * PAGE_SIZE, PAGE_SIZE), PAGE_SIZE)
        return (pltpu.make_async_copy(cik_hbm.at[phys], ikbuf.at[s, :, cols], sems.at[s, 0]),
                pltpu.make_async_copy(ck_hbm.at[phys], kbuf.at[s, :, cols], sems.at[s, 1]),
                pltpu.make_async_copy(cv_hbm.at[phys], vbuf.at[s, :, cols], sems.at[s, 2]))

    def start_fetch(bb, s):
        def body(p, _):
            for cp in page_copies(bb, s, p):
                cp.start()
            return 0
        lax.fori_loop(0, n_pages, body, 0)

    def wait_fetch(bb, s):
        def body(p, _):
            for cp in page_copies(bb, s, p):
                cp.wait()
            return 0
        lax.fori_loop(0, n_pages, body, 0)

    @pl.when(b == 0)
    def _():
        tail = pl.ds(past_len, PAGE_SIZE)
        for s in range(2):
            ikbuf[s, :, tail] = jnp.zeros((IDX_DIM, PAGE_SIZE), F32)
            kbuf[s, :, tail] = jnp.zeros((KV_WIDTH, PAGE_SIZE), F32)
            vbuf[s, :, tail] = jnp.zeros((KV_WIDTH, PAGE_SIZE), F32)
        start_fetch(0, 0)

    @pl.when(b + 1 < nb)
    def _():
        start_fetch(b + 1, 1 - slot)

    kiwi = kiwi_ref[...]

    def as_columns(rows):
        padded = jnp.concatenate([rows, jnp.zeros((LANES - R, LANES), F32)], axis=0)
        return padded.T[:, :R]

    new_cols = pl.ds(past_len, R)
    ikbuf[slot, :, new_cols] = as_columns(kiwi)[:IDX_DIM]
    kbuf[slot, :, new_cols] = as_columns(ka_ref[...])
    vbuf[slot, :, new_cols] = as_columns(va_ref[...])
    wait_fetch(b, slot)

    row = lax.broadcasted_iota(jnp.int32, (R, 1), 0)
    key_pos = lax.broadcasted_iota(jnp.int32, (R, L), 1)

    qi = qi_ref[...]
    q_all = jnp.concatenate(
        [qi[:, h * IDX_DIM:(h + 1) * IDX_DIM] * (IDX_DIM ** -0.5) for h in range(IDX_HEADS)], axis=0).astype(BF16)
    d = jnp.dot(q_all, ikbuf[slot].astype(BF16), preferred_element_type=F32)
    acc = jnp.zeros((R, L), F32)
    for h in range(IDX_HEADS):
        acc = acc + kiwi[:, IDX_DIM + h:IDX_DIM + h + 1] * jnp.maximum(d[h * R:(h + 1) * R], 0.0)
    causal = key_pos <= past_len + row
    sc_ref[...] = jnp.where(causal, acc, -jnp.inf)
    mn = jnp.min(jnp.where(causal, acc, jnp.inf), axis=1, keepdims=True)
    mx = jnp.max(jnp.where(causal, acc, -jnp.inf), axis=1, keepdims=True)
    n_causal = (past_len + row + 1).astype(F32)
    kk = jnp.minimum(n_causal, float(topk))
    lo, hi, c_hi, tied = _threshold_search_rows(sc_ref, kk, mn, mx + (mx - mn) + 1.0, n_causal, row < t_real)
    tiedf = jnp.where(tied, 1.0, 0.0)
    tied_any = jnp.max(tiedf) > 0.0
    sel_ref[...] = jnp.where(sc_ref[...] >= lo, 1.0, 0.0)

    @pl.when(tied_any)
    def _():
        wide = lambda a: jnp.broadcast_to(a, (R, LANES))

        def tie_body(kc, prefix):
            cols = pl.ds(pl.multiple_of(kc * _CKS, LANES), _CKS)
            sel, prefix = _select_chunk(sc_ref[:, cols], wide(lo), wide(hi), wide(kk - c_hi), tied_any,
                                        wide(tiedf), prefix)
            sel_ref[:, cols] = sel
            return prefix

        lax.fori_loop(0, nck, tie_body, jnp.zeros((R, LANES), F32))

    qa = qa_ref[...]
    q_rows = []
    for h in range(ATT_HEADS):
        c = h // ATT_GROUP
        z = jnp.zeros((R, ATT_HEAD_DIM), F32)
        parts = [z] * KV_HEADS
        parts[c] = qa[:, h * ATT_HEAD_DIM:(h + 1) * ATT_HEAD_DIM] * (ATT_HEAD_DIM ** -0.5)
        q_rows.append(jnp.concatenate(parts, axis=1))
    q_big = jnp.concatenate(q_rows, axis=0).astype(BF16)
    selb = jnp.concatenate([sel_ref[...]] * ATT_HEADS, axis=0) > 0.0
    s = jnp.where(selb, jnp.dot(q_big, kbuf[slot].astype(BF16), preferred_element_type=F32), NEG_BIG)
    p = jnp.exp(s - jnp.max(s, axis=1, keepdims=True))
    o = _dot_nt(p, vbuf[slot]) / jnp.sum(p, axis=1, keepdims=True)
    outs = []
    for h in range(ATT_HEADS):
        c = h // ATT_GROUP
        outs.append(o[h * R:(h + 1) * R, c * ATT_HEAD_DIM:(c + 1) * ATT_HEAD_DIM])
    o_ref[...] = jnp.concatenate(outs, axis=1)


def _sample_dsa(page_table, qi, kiwi, qa, ka, va, cache_idx_k_l, cache_k_l, cache_v_l, t_real, row0):
    Bd, n_pages = page_table.shape
    past_len = n_pages * PAGE_SIZE
    L = past_len + PAGE_SIZE
    assert L % _CKS == 0 and row0 % SROWS == 0
    topk = min(TOPK_MAX, (past_len + t_real) // 4)
    blk0 = row0 // SROWS
    blk = lambda w: pl.BlockSpec((SROWS, w), lambda b, pt: (blk0 + b, 0))
    any_spec = pl.BlockSpec(memory_space=pl.ANY)
    grid_spec = pltpu.PrefetchScalarGridSpec(
        num_scalar_prefetch=1,
        grid=(Bd,),
        in_specs=[blk(IDX_WIDTH), blk(_KIWI_WIDTH), blk(ATT_WIDTH), blk(KV_WIDTH), blk(KV_WIDTH),
                  any_spec, any_spec, any_spec],
        out_specs=pl.BlockSpec((SROWS, ATT_WIDTH), lambda b, pt: (b, 0)),
        scratch_shapes=[pltpu.VMEM((2, IDX_DIM, L), F32),
                        pltpu.VMEM((2, KV_WIDTH, L), F32),
                        pltpu.VMEM((2, KV_WIDTH, L), F32),
                        pltpu.SemaphoreType.DMA((2, 3)),
                        pltpu.VMEM((SROWS, L), F32),
                        pltpu.VMEM((SROWS, L), F32)],
    )
    return pl.pallas_call(
        functools.partial(_sample_dsa_kernel, n_pages=n_pages, t_real=t_real, topk=topk),
        grid_spec=grid_spec,
        out_shape=jax.ShapeDtypeStruct((Bd * SROWS, ATT_WIDTH), F32),
        name="sample_dsa",
        compiler_params=_cparams(("arbitrary",)),
    )(page_table, qi, kiwi, qa, ka, va, cache_idx_k_l, cache_k_l, cache_v_l)


_TMP = 256
_PAIR_GROUP = 16
PEER_HALF_EXPERTS = PEER_EXPERTS // 2


def _extract_topk(s, pos, n, k, payload=None):
    vals, idxs = [], []
    for _ in range(k):
        m = jnp.max(s, axis=0, keepdims=True)
        p = jnp.min(jnp.where(s == m, pos, n), axis=0, keepdims=True)
        hit = pos == p
        vals.append(m)
        if payload is None:
            idxs.append(p)
        else:
            idxs.append(jnp.max(jnp.where(hit, payload, -1), axis=0, keepdims=True))
        s = jnp.where(hit, -jnp.inf, s)
    return vals, idxs


def _post_mix_kernel(x_ref, attp_ref, atts_ref, retp_ref, rets_ref, wo_ref, nw_ref, wqt_ref, sk1_ref, sk2_ref,
                     h_ref, xn_ref, idx_ref, gate_ref, n0_ref, qt_ref, et_ref, gt_ref, la_ref, lg_ref,
                     *, n_prompt_tiles):
    T = x_ref.shape[0]
    is_prompt = pl.program_id(0) < n_prompt_tiles
    att = jnp.where(is_prompt, attp_ref[...], atts_ref[...])
    ret = jnp.where(is_prompt, retp_ref[...], rets_ref[...])
    h = x_ref[...] + _dot(att, wo_ref[:ATT_WIDTH, :]) + _dot(ret, wo_ref[ATT_WIDTH:, :])
    h_ref[...] = h
    ms = jnp.mean(h * h, axis=-1, keepdims=True)
    xn = h * lax.rsqrt(ms + NORM_EPS) * nw_ref[...]
    for c in range(D_MODEL // LANES):
        xn_ref[pl.ds(c, T, stride=SUBLANES), :] = xn[:, c * LANES:(c + 1) * LANES]
    qt_ref[...] = _dot_nt(wqt_ref[...], xn)

    pos_k = lax.broadcasted_iota(jnp.int32, (PEER_NKEYS, T), 0)
    n_b = [PEER_TOPK // (a + 1) for a in range(PEER_TOPK)]
    n_cand = -(-sum(n_b) // SUBLANES) * SUBLANES
    pos_c = lax.broadcasted_iota(jnp.int32, (n_cand, T), 0)

    def head_body(hd, _):
        q0 = pl.multiple_of(hd * PEER_KEY_DIM, PEER_KEY_DIM)
        q1 = qt_ref[pl.ds(q0, PEER_HALF), :]
        q2 = qt_ref[pl.ds(q0 + PEER_HALF, PEER_HALF), :]
        s1 = _dot(sk1_ref[...], q1)
        s2 = _dot(sk2_ref[...], q2)
        v1, i1 = _extract_topk(s1, pos_k, PEER_NKEYS, PEER_TOPK)
        v2, i2 = _extract_topk(s2, pos_k, PEER_NKEYS, PEER_TOPK)
        v2m = jnp.concatenate(v2, axis=0)
        i2m = jnp.concatenate(i2, axis=0)
        n_fill = n_cand - sum(n_b)
        cand = jnp.concatenate([v1[a] + v2m[:n_b[a]] for a in range(PEER_TOPK)]
                               + [jnp.full((n_fill, T), -jnp.inf, F32)], axis=0)
        cid = jnp.concatenate([i1[a] * PEER_NKEYS + i2m[:n_b[a]] for a in range(PEER_TOPK)]
                              + [jnp.full((n_fill, T), -1, jnp.int32)], axis=0)
        sv, eid = _extract_topk(cand, pos_c, n_cand, PEER_TOPK, payload=cid)
        svm = jnp.concatenate(sv, axis=0)
        g = jnp.exp(svm - sv[0])
        r0 = pl.multiple_of(hd * PEER_TOPK, PEER_TOPK)
        gt_ref[pl.ds(r0, PEER_TOPK), :] = g / jnp.sum(g, axis=0, keepdims=True)
        et_ref[pl.ds(r0, PEER_TOPK), :] = jnp.concatenate(eid, axis=0).astype(F32)
        return 0

    lax.fori_loop(0, PEER_HEADS, head_body, 0)

    e = et_ref[...]
    g = gt_ref[...]
    npair = e.shape[0]
    is0 = e < float(PEER_HALF_EXPERTS)
    r_i = lax.broadcasted_iota(jnp.int32, (npair, npair), 0)
    c_i = lax.broadcasted_iota(jnp.int32, (npair, npair), 1)
    tri = jnp.where(c_i <= r_i, 1.0, 0.0).astype(BF16)
    rank0 = jnp.dot(tri, jnp.where(is0, 1.0, 0.0).astype(BF16), preferred_element_type=F32)
    rowf = lax.broadcasted_iota(jnp.int32, (npair, T), 0).astype(F32)
    n0 = rank0[npair - 1:npair, :]
    place = jnp.where(is0, rank0 - 1.0, n0 + (rowf - rank0))
    off = jnp.where(is0, e, e - float(PEER_HALF_EXPERTS)) * float(SUBLANES)
    for p in range(npair):
        m = place == float(p)
        la_ref[p:p + 1, :] = jnp.sum(jnp.where(m, off, 0.0), axis=0, keepdims=True)
        lg_ref[p:p + 1, :] = jnp.sum(jnp.where(m, g, 0.0), axis=0, keepdims=True)
    idx_ref[...] = la_ref[...].T.astype(jnp.int32)
    gate_ref[...] = lg_ref[...].T
    n0_ref[...] = n0.astype(jnp.int32)


def _post_mix(x, att_p, att_s, ret_p, ret_s, w_out_bf, norm_w, wq_t_bf, sk1_bf, sk2_bf):
    N = x.shape[0]
    nt = N // _TMP
    ntp = att_p.shape[0] // _TMP
    nts = att_s.shape[0] // _TMP
    assert ntp + nts == nt and ntp > 0 and nts > 0
    npair = PEER_HEADS * PEER_TOPK
    row = lambda w: pl.BlockSpec((_TMP, w), lambda i: (i, 0))
    prow = lambda w: pl.BlockSpec((_TMP, w), lambda i: (jnp.minimum(i, ntp - 1), 0))
    srow = lambda w: pl.BlockSpec((_TMP, w), lambda i: (jnp.maximum(i - ntp, 0), 0))
    full = lambda a: pl.BlockSpec(a.shape, lambda i: (0,) * a.ndim)
    nw = norm_w.reshape(1, D_MODEL)
    return pl.pallas_call(
        functools.partial(_post_mix_kernel, n_prompt_tiles=ntp),
        grid=(nt,),
        in_specs=[row(D_MODEL), prow(ATT_WIDTH), srow(ATT_WIDTH), prow(RET_WIDTH), srow(RET_WIDTH),
                  full(w_out_bf), full(nw), full(wq_t_bf), full(sk1_bf), full(sk2_bf)],
        out_specs=[row(D_MODEL), pl.BlockSpec((_TMP * SUBLANES, LANES), lambda i: (i, 0)),
                   row(npair), row(npair), pl.BlockSpec((1, _TMP), lambda i: (0, i))],
        out_shape=[jax.ShapeDtypeStruct((N, D_MODEL), F32), jax.ShapeDtypeStruct((N * SUBLANES, LANES), F32),
                   jax.ShapeDtypeStruct((N, npair), jnp.int32), jax.ShapeDtypeStruct((N, npair), F32),
                   jax.ShapeDtypeStruct((1, N), jnp.int32)],
        scratch_shapes=[pltpu.VMEM((PEER_HEADS * PEER_KEY_DIM, _TMP), F32),
                        pltpu.VMEM((npair, _TMP), F32), pltpu.VMEM((npair, _TMP), F32),
                        pltpu.VMEM((npair, _TMP), F32), pltpu.VMEM((npair, _TMP), F32)],
        name="post_mix",
        compiler_params=_cparams(("arbitrary",)),
    )(x, att_p, att_s, ret_p, ret_s, w_out_bf, nw, wq_t_bf, sk1_bf, sk2_bf)


_TBP = 128
_NPAIR = PEER_HEADS * PEER_TOPK
_ROW_TILE = D_MODEL // LANES
assert _ROW_TILE == SUBLANES
_HIGH_HALF = -65536


def _bf16_bits(x):
    return lax.bitcast_convert_type(x.astype(BF16).astype(F32), jnp.int32)


def _pack_table_kernel(lo_ref, hi_ref, o_ref):
    n = lo_ref.shape[0]
    words = (lax.shift_right_logical(_bf16_bits(lo_ref[...]), 16)
             | (_bf16_bits(hi_ref[...]) & _HIGH_HALF))
    for c in range(_ROW_TILE):
        o_ref[pl.ds(c, n, stride=_ROW_TILE), :] = words[:, c * LANES:(c + 1) * LANES]


def _pack_table(tbl):
    blk = 512
    nb = PEER_HALF_EXPERTS // blk
    return pl.pallas_call(
        _pack_table_kernel,
        grid=(nb,),
        in_specs=[pl.BlockSpec((blk, D_MODEL), lambda i: (i, 0)),
                  pl.BlockSpec((blk, D_MODEL), lambda i: (i + nb, 0))],
        out_specs=pl.BlockSpec((blk * _ROW_TILE, LANES), lambda i: (i, 0)),
        out_shape=jax.ShapeDtypeStruct((PEER_HALF_EXPERTS * _ROW_TILE, LANES), jnp.int32),
        name="pack_table",
        compiler_params=_cparams(("arbitrary",)),
    )(tbl, tbl)


def _unpack(words, half):
    bits = lax.shift_left(words, 16) if half == 0 else words & _HIGH_HALF
    return lax.bitcast_convert_type(bits, F32)


def _load_table(tbl_hbm, tbuf, sem):
    cp = pltpu.make_async_copy(tbl_hbm, tbuf, sem)
    cp.start()
    cp.wait()


def _fold_pair(a, b, k, sub):
    m = (sub & k) == 0
    return jnp.where(m, a, pltpu.roll(b, k, 0)) + jnp.where(m, pltpu.roll(a, SUBLANES - k, 0), b)


def _fold8(p, sub):
    a, b, c, d, e, f, g, h = p[0], p[4], p[2], p[6], p[1], p[5], p[3], p[7]
    t1, t2, t3, t4 = (_fold_pair(a, b, 4, sub), _fold_pair(c, d, 4, sub),
                      _fold_pair(e, f, 4, sub), _fold_pair(g, h, 4, sub))
    u1, u2 = _fold_pair(t1, t2, 2, sub), _fold_pair(t3, t4, 2, sub)
    return _fold_pair(u1, u2, 1, sub)


def _flat_smem_spec(nt, per_token):
    return pl.BlockSpec((1, 1, _TBP * per_token), lambda i: (i, 0, 0), memory_space=pltpu.SMEM)


def _flat_blocks(a, nt):
    return a.reshape(nt, 1, -1)


_N_GROUPS = _NPAIR // _PAIR_GROUP


def _group_start(t, g):
    g = jnp.minimum(g, _N_GROUPS - 1)
    return pl.multiple_of(t * _NPAIR + g * _PAIR_GROUP, _PAIR_GROUP)


def _table_row(tbuf, idx_ref, k):
    off = pl.multiple_of(idx_ref[0, 0, k], _ROW_TILE)
    return tbuf[pl.ds(off, _ROW_TILE), :]


def _group_values(words, g, n0):
    g_mid = lax.shift_right_logical(n0, _PAIR_GROUP.bit_length() - 1)
    rem = n0 & (_PAIR_GROUP - 1)

    def uniform(_):
        shift = jnp.where(g < g_mid, 16, 0)
        return tuple(lax.bitcast_convert_type(lax.shift_left(w, shift) & _HIGH_HALF, F32) for w in words)

    def mixed(_):
        return tuple(jnp.where(s < rem, _unpack(w, 0), _unpack(w, 1)) for s, w in enumerate(words))

    return lax.cond((g == g_mid) & (rem != 0), mixed, uniform, 0)


def _peer_u_kernel(idx_ref, n0_ref, x_ref, u_hbm, o_ref, tbuf, sem, z_ref):
    @pl.when(pl.program_id(0) == 0)
    def _():
        _load_table(u_hbm, tbuf, sem)

    sub = lax.broadcasted_iota(jnp.int32, (SUBLANES, LANES), 0)

    def tok_body(t, _):
        xt = x_ref[t]
        n0 = n0_ref[0, 0, t]

        def load_words(g):
            kg = _group_start(t, g)
            return tuple(_table_row(tbuf, idx_ref, kg + s) for s in range(_PAIR_GROUP))

        def grp_body(g, words):
            nxt = load_words(g + 1)
            kg = _group_start(t, g)
            prods = [v * xt for v in _group_values(words, g, n0)]
            for q in range(_PAIR_GROUP // SUBLANES):
                z_ref[pl.ds(kg + q * SUBLANES, SUBLANES), :] = _fold8(prods[q * SUBLANES:(q + 1) * SUBLANES], sub)
            return nxt

        lax.fori_loop(0, _N_GROUPS, grp_body, load_words(0))
        return 0

    lax.fori_loop(0, _TBP, tok_body, 0)

    ones = jnp.ones((SUBLANES, LANES), BF16)

    def chunk_body(c, _):
        r0 = pl.multiple_of(c * SUBLANES * _NPAIR, SUBLANES * _NPAIR)
        z = z_ref[pl.ds(r0, SUBLANES * _NPAIR), :]
        zh = z.astype(BF16)
        zl = (z - zh.astype(F32)).astype(BF16)
        hv = _dot_nt(ones, zh) + _dot_nt(ones, zl)
        rows = [hv[0:1, tt * _NPAIR:(tt + 1) * _NPAIR] for tt in range(SUBLANES)]
        o_ref[pl.ds(pl.multiple_of(c * SUBLANES, SUBLANES), SUBLANES), :] = jnp.concatenate(rows, axis=0)
        return 0

    lax.fori_loop(0, _TBP // SUBLANES, chunk_body, 0)


def _peer_u(idx, n0, xn_tiles, u_packed):
    N = xn_tiles.shape[0] // _ROW_TILE
    nt = N // _TBP
    return pl.pallas_call(
        _peer_u_kernel,
        grid=(nt,),
        in_specs=[_flat_smem_spec(nt, _NPAIR), _flat_smem_spec(nt, 1),
                  pl.BlockSpec((_TBP, _ROW_TILE, LANES), lambda i: (i, 0, 0)),
                  pl.BlockSpec(memory_space=pl.ANY)],
        out_specs=pl.BlockSpec((_TBP, _NPAIR), lambda i: (i, 0)),
        out_shape=jax.ShapeDtypeStruct((N, _NPAIR), F32),
        scratch_shapes=[pltpu.VMEM((PEER_HALF_EXPERTS * _ROW_TILE, LANES), jnp.int32),
                        pltpu.SemaphoreType.DMA(()),
                        pltpu.VMEM((_TBP * _NPAIR, LANES), F32)],
        name="peer_u",
        compiler_params=_cparams(("arbitrary",)),
    )(_flat_blocks(idx, nt), _flat_blocks(n0, nt), xn_tiles.reshape(N, _ROW_TILE, LANES), u_packed)


def _peer_coef_kernel(hv_ref, gate_ref, o_ref):
    hv = hv_ref[...]
    o_ref[...] = gate_ref[...] * (0.5 * hv * (1.0 + lax.erf(hv * (0.5 ** 0.5))))


def _peer_coef(hval, gate):
    N = gate.shape[0]
    tm = 1024 if N % 1024 == 0 else _TBP
    spec = pl.BlockSpec((tm, _NPAIR), lambda i: (i, 0))
    return pl.pallas_call(
        _peer_coef_kernel,
        grid=(N // tm,),
        in_specs=[spec, spec],
        out_specs=spec,
        out_shape=jax.ShapeDtypeStruct((N, _NPAIR), F32),
        name="peer_coef",
        compiler_params=_cparams(("arbitrary",)),
    )(hval, gate)


def _peer_v_kernel(idx_ref, n0_ref, coef_ref, v_hbm, o_ref, tbuf, sem):
    @pl.when(pl.program_id(0) == 0)
    def _():
        _load_table(v_hbm, tbuf, sem)

    n_acc = 4

    def tok_body(t, _):
        n0 = n0_ref[0, 0, t]

        def load_words(g):
            kg = _group_start(t, g)
            return tuple(_table_row(tbuf, idx_ref, kg + s) for s in range(_PAIR_GROUP))

        def grp_body(g, carry):
            words, accs = carry
            nxt = load_words(g + 1)
            kg = _group_start(t, g)
            accs = list(accs)
            for s, v in enumerate(_group_values(words, g, n0)):
                accs[s % n_acc] = accs[s % n_acc] + coef_ref[0, 0, kg + s] * v
            return nxt, tuple(accs)

        zero = jnp.zeros((_ROW_TILE, LANES), F32)
        _, accs = lax.fori_loop(0, _N_GROUPS, grp_body, (load_words(0), (zero,) * n_acc))
        o_ref[t] = (accs[0] + accs[1]) + (accs[2] + accs[3])
        return 0

    lax.fori_loop(0, _TBP, tok_body, 0)


def _peer_v(idx, n0, coef, v_packed):
    N = coef.shape[0]
    nt = N // _TBP
    out = pl.pallas_call(
        _peer_v_kernel,
        grid=(nt,),
        in_specs=[_flat_smem_spec(nt, _NPAIR), _flat_smem_spec(nt, 1), _flat_smem_spec(nt, _NPAIR),
                  pl.BlockSpec(memory_space=pl.ANY)],
        out_specs=pl.BlockSpec((_TBP, _ROW_TILE, LANES), lambda i: (i, 0, 0)),
        out_shape=jax.ShapeDtypeStruct((N, _ROW_TILE, LANES), F32),
        scratch_shapes=[pltpu.VMEM((PEER_HALF_EXPERTS * _ROW_TILE, LANES), jnp.int32),
                        pltpu.SemaphoreType.DMA(())],
        name="peer_v",
        compiler_params=_cparams(("arbitrary",)),
    )(_flat_blocks(idx, nt), _flat_blocks(n0, nt), _flat_blocks(coef, nt), v_packed)
    return out.reshape(N * _ROW_TILE, LANES)


def _final_kernel(h_ref, y_ref, w_ref, o_ref, *, normalize):
    T = h_ref.shape[0]
    y = jnp.concatenate(
        [y_ref[pl.ds(c, T, stride=SUBLANES), :] for c in range(D_MODEL // LANES)], axis=1)
    h = h_ref[...] + y
    if normalize:
        ms = jnp.mean(h * h, axis=-1, keepdims=True)
        h = h * lax.rsqrt(ms + NORM_EPS) * w_ref[...]
    o_ref[...] = h


def _final(h, ypart, w):
    N = h.shape[0]
    tm = 512 if N % 512 == 0 else _TBP
    normalize = w is not None
    if not normalize:
        w = jnp.ones((D_MODEL,), F32)
    return pl.pallas_call(
        functools.partial(_final_kernel, normalize=normalize),
        grid=(N // tm,),
        in_specs=[pl.BlockSpec((tm, D_MODEL), lambda i: (i, 0)),
                  pl.BlockSpec((tm * SUBLANES, LANES), lambda i: (i, 0)),
                  pl.BlockSpec((1, D_MODEL), lambda i: (0, 0))],
        out_specs=pl.BlockSpec((tm, D_MODEL), lambda i: (i, 0)),
        out_shape=jax.ShapeDtypeStruct((N, D_MODEL), F32),
        name="final_norm",
        compiler_params=_cparams(("arbitrary",)),
    )(h, ypart, w.reshape(1, D_MODEL))


_TM_IN = 256


def _pages_feature_major(cache_l):
    n_phys, page = cache_l.shape[:2]
    return jnp.swapaxes(cache_l.reshape(n_phys, page, -1), 1, 2)


def kernel(x_prompt, x_sample, cache_k, cache_v, cache_idx_k, state_ret, page_table, norm_attn_w, w_in, ret_gn_w,
           w_out, norm_ffn_w, peer_w_q, peer_sub_keys_1, peer_sub_keys_2, peer_u, peer_v, final_norm_w):
    B, S, D = x_prompt.shape
    Bd, T, _ = x_sample.shape
    depth = w_in.shape[0]
    n_pages = page_table.shape[1]
    past_len = n_pages * PAGE_SIZE
    n_phys = cache_k.shape[1]
    Np, Ns = B * S, Bd * SROWS
    assert D == D_MODEL and T <= SROWS and S % _TM_IN == 0 and Ns % _TM_IN == 0 and S % _CK == 0

    hs_pad = jnp.pad(x_sample, ((0, 0), (0, SROWS - T), (0, 0)))
    h_all = jnp.concatenate([x_prompt.reshape(Np, D), hs_pad.reshape(Ns, D)], axis=0)

    pos_s = past_len + jnp.arange(SROWS)
    pos = jnp.concatenate([jnp.arange(S), jnp.tile(pos_s, _TM_IN // SROWS)])
    tab_att, tab_ret = _rope_tables(pos)
    tiles_per_seq = S // _TM_IN
    tab_blocks = [i % tiles_per_seq for i in range(Np // _TM_IN)] + [tiles_per_seq] * (Ns // _TM_IN)
    ret_consts_p = _ret_constants(RET_CHUNK)
    ret_consts_s = _ret_constants(T)

    outs = {n: [] for n in ("kp", "vp", "ikp", "sp", "ks", "vs", "iks", "ss")}
    for l in range(depth):
        m = _in_proj(h_all, norm_attn_w[l], _pad_w_in(w_in[l]), tab_att, tab_ret, tab_blocks, _TM_IN)
        att_p = _prompt_dsa(m["qi"], m["kiwi"], m["qa"], m["ka"], m["va"], B, S)
        att_s = _sample_dsa(page_table, m["qi"], m["kiwi"], m["qa"], m["ka"], m["va"],
                            _pages_feature_major(cache_idx_k[l]), _pages_feature_major(cache_k[l]),
                            _pages_feature_major(cache_v[l]), T, Np)
        ret_p, s_p = _retention(m["qr"], m["kr"], m["vr"], m["gr"], ret_gn_w[l],
                                jnp.zeros((B, RET_HEADS, RET_HEAD_DIM, RET_HEAD_DIM), F32),
                                ret_consts_p, B, S // RET_CHUNK, RET_CHUNK)
        ret_s, s_s = _retention(m["qr"], m["kr"], m["vr"], m["gr"], ret_gn_w[l], state_ret[l].astype(F32),
                                ret_consts_s, Bd, 1, SROWS, row0=Np)
        h_mid, xn, idx, gate, n0 = _post_mix(
            h_all, att_p, att_s, ret_p, ret_s, w_out[l].astype(BF16), norm_ffn_w[l],
            peer_w_q[l].T.astype(BF16), peer_sub_keys_1[l].astype(BF16), peer_sub_keys_2[l].astype(BF16))
        coef = _peer_coef(_peer_u(idx, n0, xn, _pack_table(peer_u[l])), gate)
        ypart = _peer_v(idx, n0, coef, _pack_table(peer_v[l]))
        last = l == depth - 1
        h_all = _final(h_mid, ypart, final_norm_w if last else None)

        sample = lambda a, w: a[Np:].reshape(Bd, SROWS, w)[:, :T]
        outs["kp"].append(m["ka"][:Np].reshape(B, S, KV_HEADS, ATT_HEAD_DIM))
        outs["vp"].append(m["va"][:Np].reshape(B, S, KV_HEADS, ATT_HEAD_DIM))
        outs["ikp"].append(m["kiwi"][:Np, :IDX_DIM].reshape(B, S, IDX_DIM))
        outs["sp"].append(s_p.astype(x_prompt.dtype))
        outs["ks"].append(sample(m["ka"], KV_WIDTH).reshape(Bd, T, KV_HEADS, ATT_HEAD_DIM))
        outs["vs"].append(sample(m["va"], KV_WIDTH).reshape(Bd, T, KV_HEADS, ATT_HEAD_DIM))
        outs["iks"].append(sample(m["kiwi"], _KIWI_WIDTH)[:, :, :IDX_DIM])
        outs["ss"].append(s_s.astype(state_ret.dtype))

    y_prompt = h_all[:Np].reshape(B, S, D)
    y_sample = h_all[Np:].reshape(Bd, SROWS, D)[:, :T]
    st = lambda n: jnp.stack(outs[n])
    return (y_prompt, y_sample, st("kp"), st("vp"), st("ikp"), st("sp"),
            st("ks"), st("vs"), st("iks"), st("ss"))
```

```python
import functools

import jax
import jax.numpy as jnp
import numpy as np
from jax import lax
from jax.experimental import pallas as pl
from jax.experimental.pallas import tpu as pltpu

F32 = jnp.float32
BF16 = jnp.bfloat16

D_MODEL = 1024
PAGE_SIZE = 128
ATT_HEADS = 8
ATT_HEAD_DIM = 64
KV_HEADS = 2
ATT_GROUP = ATT_HEADS // KV_HEADS
ATT_WIDTH = ATT_HEADS * ATT_HEAD_DIM
KV_WIDTH = KV_HEADS * ATT_HEAD_DIM
ROPE_THETA = 500000.0
ATT_ROPE_DIMS = ATT_HEAD_DIM // 4
IDX_HEADS = 8
IDX_DIM = 64
IDX_WIDTH = IDX_HEADS * IDX_DIM
TOPK_MAX = 256
RET_HEADS = 4
RET_HEAD_DIM = 128
RET_WIDTH = RET_HEADS * RET_HEAD_DIM
RET_ROPE_THETA = 10000.0
RET_CHUNK = 128
PEER_HEADS = 8
PEER_NKEYS = 128
PEER_EXPERTS = PEER_NKEYS * PEER_NKEYS
PEER_KEY_DIM = 128
PEER_HALF = PEER_KEY_DIM // 2
PEER_TOPK = 16
NORM_EPS = 1e-6
GN_EPS = 1e-6

LANES = 128
SUBLANES = 8
VMEM_LIMIT_BYTES = 56 * 1024 * 1024

NEG_BIG = -1e30

_IN_SPLITS = (ATT_WIDTH, KV_WIDTH, KV_WIDTH, IDX_WIDTH, IDX_DIM, IDX_HEADS,
              RET_WIDTH, RET_WIDTH, RET_WIDTH, RET_WIDTH)
_KIWI_WIDTH = LANES
_PAD_OFFS = {}
_off = 0
for _name, _w in (("qa", ATT_WIDTH), ("ka", KV_WIDTH), ("va", KV_WIDTH), ("qi", IDX_WIDTH),
                  ("kiwi", _KIWI_WIDTH), ("qr", RET_WIDTH), ("kr", RET_WIDTH),
                  ("vr", RET_WIDTH), ("gr", RET_WIDTH)):
    _PAD_OFFS[_name] = (_off, _w)
    _off += _w
IN_WIDTH_PADDED = _off


def _cparams(sem):
    return pltpu.CompilerParams(dimension_semantics=sem, vmem_limit_bytes=VMEM_LIMIT_BYTES)


def _dot(a, b):
    return jnp.dot(a.astype(BF16), b.astype(BF16), preferred_element_type=F32)


def _dot_nt(a, b):
    return lax.dot_general(a.astype(BF16), b.astype(BF16), (((1,), (1,)), ((), ())),
                           preferred_element_type=F32)


def _pad_w_in(w_in_l):
    cols = []
    off = 0
    parts = []
    for n in _IN_SPLITS:
        parts.append(w_in_l[:, off:off + n])
        off += n
    qa, ka, va, qi, ki, wi, qr, kr, vr, gr = parts
    kiwi = jnp.concatenate(
        [ki, wi, jnp.zeros((w_in_l.shape[0], _KIWI_WIDTH - IDX_DIM - IDX_HEADS), w_in_l.dtype)], axis=1)
    cols = [qa, ka, va, qi, kiwi, qr, kr, vr, gr]
    return jnp.concatenate(cols, axis=1).astype(BF16)


def _rope_tables(pos):
    posf = pos.astype(F32)
    half = ATT_ROPE_DIMS // 2
    inv = 1.0 / (ROPE_THETA ** (jnp.arange(half, dtype=F32) / half))
    ang = posf[:, None] * inv[None, :]
    cos, sin = jnp.cos(ang), jnp.sin(ang)
    P = pos.shape[0]
    one = jnp.ones((P, ATT_HEAD_DIM - ATT_ROPE_DIMS), F32)
    zero = jnp.zeros((P, ATT_HEAD_DIM - ATT_ROPE_DIMS), F32)
    zh = jnp.zeros((P, half), F32)
    c_head = jnp.concatenate([cos, cos, one], axis=1)
    s1_head = jnp.concatenate([-sin, zh, zero], axis=1)
    s2_head = jnp.concatenate([zh, sin, zero], axis=1)
    rep = LANES // ATT_HEAD_DIM
    att = jnp.stack([jnp.tile(c_head, (1, rep)), jnp.tile(s1_head, (1, rep)), jnp.tile(s2_head, (1, rep))])
    halfr = RET_HEAD_DIM // 2
    invr = 1.0 / (RET_ROPE_THETA ** (jnp.arange(halfr, dtype=F32) / halfr))
    angr = posf[:, None] * invr[None, :]
    cr, sr = jnp.cos(angr), jnp.sin(angr)
    ret = jnp.stack([jnp.concatenate([cr, cr], axis=1), jnp.concatenate([-sr, sr], axis=1)])
    return att, ret


def _in_proj_kernel(x_ref, nw_ref, w_ref, ta_ref, tr_ref,
                    qa_ref, ka_ref, va_ref, qi_ref, kiwi_ref, qr_ref, kr_ref, vr_ref, gr_ref):
    x = x_ref[...]
    ms = jnp.mean(x * x, axis=-1, keepdims=True)
    xn = (x * lax.rsqrt(ms + NORM_EPS) * nw_ref[...]).astype(BF16)
    ca, s1a, s2a = ta_ref[0], ta_ref[1], ta_ref[2]
    cr, sr = tr_ref[0], tr_ref[1]

    def proj(name):
        off, w = _PAD_OFFS[name]
        return jnp.dot(xn, w_ref[:, off:off + w], preferred_element_type=F32)

    def rope_att_chunk(vc):
        return (vc * ca + pltpu.roll(vc, LANES - ATT_ROPE_DIMS // 2, 1) * s1a
                + pltpu.roll(vc, ATT_ROPE_DIMS // 2, 1) * s2a)

    def rope_ret_chunk(vc):
        return vc * cr + pltpu.roll(vc, RET_HEAD_DIM // 2, 1) * sr

    def per_chunk(v, fn):
        n = v.shape[1] // LANES
        return jnp.concatenate([fn(v[:, c * LANES:(c + 1) * LANES]) for c in range(n)], axis=1)

    qa_ref[...] = per_chunk(proj("qa"), rope_att_chunk)
    ka_ref[...] = per_chunk(proj("ka"), rope_att_chunk)
    va_ref[...] = proj("va")
    qi_ref[...] = per_chunk(proj("qi"), rope_att_chunk)
    kiwi = proj("kiwi")
    lane = lax.broadcasted_iota(jnp.int32, kiwi.shape, 1)
    kiwi_ref[...] = jnp.where(lane < IDX_DIM, rope_att_chunk(kiwi), kiwi * (IDX_HEADS ** -0.5))
    qr_ref[...] = per_chunk(proj("qr"), rope_ret_chunk)
    kr_ref[...] = per_chunk(proj("kr"), rope_ret_chunk) * (RET_HEAD_DIM ** -0.5)
    vr_ref[...] = proj("vr")
    gr_ref[...] = proj("gr")


def _in_proj(x, norm_w, w_pad, tab_att, tab_ret, tab_block_of_tile, tm):
    N = x.shape[0]
    nt = N // tm
    tab_idx = jnp.asarray(tab_block_of_tile, jnp.int32)
    names = ("qa", "ka", "va", "qi", "kiwi", "qr", "kr", "vr", "gr")
    out_shape = [jax.ShapeDtypeStruct((N, _PAD_OFFS[n][1]), F32) for n in names]
    out_specs = [pl.BlockSpec((tm, _PAD_OFFS[n][1]), lambda i, t: (i, 0)) for n in names]
    grid_spec = pltpu.PrefetchScalarGridSpec(
        num_scalar_prefetch=1,
        grid=(nt,),
        in_specs=[
            pl.BlockSpec((tm, D_MODEL), lambda i, t: (i, 0)),
            pl.BlockSpec((1, D_MODEL), lambda i, t: (0, 0)),
            pl.BlockSpec((D_MODEL, IN_WIDTH_PADDED), lambda i, t: (0, 0)),
            pl.BlockSpec((3, tm, LANES), lambda i, t: (0, t[i], 0)),
            pl.BlockSpec((2, tm, LANES), lambda i, t: (0, t[i], 0)),
        ],
        out_specs=out_specs,
    )

    def body(t_ref, *refs):
        _in_proj_kernel(*refs)

    outs = pl.pallas_call(
        body, grid_spec=grid_spec, out_shape=out_shape, name="in_proj",
        compiler_params=_cparams(("arbitrary",)),
    )(tab_idx, x, norm_w.reshape(1, D_MODEL), w_pad, tab_att, tab_ret)
    return dict(zip(names, outs))


def _ret_constants(c_eff):
    C = RET_CHUNK
    lg = jnp.log1p(-(2.0 ** (-5.0 - jnp.arange(RET_HEADS, dtype=F32))))
    i = jnp.arange(C, dtype=F32)
    diff = i[:, None] - i[None, :]
    dmask = jnp.where(diff >= 0, jnp.exp(lg[:, None, None] * jnp.maximum(diff, 0.0)), 0.0)
    real = (i < c_eff)
    dmask = jnp.where(real[None, :, None] & real[None, None, :], dmask, 0.0)
    q_dec = jnp.exp(lg[:, None] * (i[None, :] + 1.0))
    k_dec = jnp.where(real[None, :], jnp.exp(lg[:, None] * (c_eff - 1.0 - i[None, :])), 0.0)
    chunk_dec = jnp.exp(lg * c_eff)
    bc = lambda a: jnp.broadcast_to(a[:, :, None], (RET_HEADS, C, C))
    cdec = jnp.broadcast_to(chunk_dec[:, None, None], (RET_HEADS, C, C))
    return dmask, bc(q_dec), bc(k_dec), cdec


def _retention_kernel(q_ref, k_ref, v_ref, g_ref, gnw_ref, s0_ref, dm_ref, qd_ref, kd_ref, cd_ref,
                      o_ref, s_out_ref, state_ref, *, rows):
    c = pl.program_id(1)
    nc = pl.num_programs(1)

    @pl.when(c == 0)
    def _():
        state_ref[...] = s0_ref[0]

    def padded(ref):
        v = ref[...]
        if rows < RET_CHUNK:
            v = jnp.concatenate([v, jnp.zeros((RET_CHUNK - rows, v.shape[1]), v.dtype)], axis=0)
        return v

    q, k, v, g = padded(q_ref), padded(k_ref), padded(v_ref), g_ref[...]
    gnw = gnw_ref[...]
    outs = []
    for h in range(RET_HEADS):
        sl = slice(h * RET_HEAD_DIM, (h + 1) * RET_HEAD_DIM)
        qh, kh, vh = q[:, sl], k[:, sl], v[:, sl]
        st = state_ref[h]
        att = _dot_nt(qh, kh) * dm_ref[h]
        o = _dot(att, vh) + _dot(qh, st) * qd_ref[h]
        state_ref[h] = st * cd_ref[h] + _dot((kh * kd_ref[h]).T, vh)
        o = o[:rows]
        mu = jnp.mean(o, axis=-1, keepdims=True)
        var = jnp.mean(jnp.square(o - mu), axis=-1, keepdims=True)
        outs.append((o - mu) * lax.rsqrt(var + GN_EPS) * gnw[:, sl])
    on = jnp.concatenate(outs, axis=1)
    o_ref[...] = g * (1.0 / (1.0 + jnp.exp(-g))) * on

    @pl.when(c == nc - 1)
    def _():
        s_out_ref[0] = state_ref[...]


def _retention(q, k, v, g, gn_w, s0, consts, nb, nc, rows, row0=0):
    blk0 = row0 // rows
    in_row = pl.BlockSpec((rows, RET_WIDTH), lambda b, c: (blk0 + b * nc + c, 0))
    out_row = pl.BlockSpec((rows, RET_WIDTH), lambda b, c: (b * nc + c, 0))
    st_spec = pl.BlockSpec((1, RET_HEADS, RET_HEAD_DIM, RET_HEAD_DIM), lambda b, c: (b, 0, 0, 0))
    const_spec = pl.BlockSpec((RET_HEADS, RET_CHUNK, RET_CHUNK), lambda b, c: (0, 0, 0))
    return pl.pallas_call(
        functools.partial(_retention_kernel, rows=rows),
        grid=(nb, nc),
        in_specs=[in_row, in_row, in_row, in_row,
                  pl.BlockSpec((1, RET_WIDTH), lambda b, c: (0, 0)),
                  st_spec, const_spec, const_spec, const_spec, const_spec],
        out_specs=[out_row, st_spec],
        out_shape=[jax.ShapeDtypeStruct((nb * nc * rows, RET_WIDTH), F32),
                   jax.ShapeDtypeStruct(s0.shape, F32)],
        scratch_shapes=[pltpu.VMEM((RET_HEADS, RET_HEAD_DIM, RET_HEAD_DIM), F32)],
        name="retention",
        compiler_params=_cparams(("arbitrary", "arbitrary")),
    )(q, k, v, g, gn_w.reshape(1, RET_WIDTH), s0, *consts)


_BISECT_PLAIN_ITERS = 26
_BISECT_MAX_ITERS = 400


def _lane_tiles(x):
    return [x[:, j * LANES:(j + 1) * LANES] for j in range(x.shape[1] // LANES)]


def _rowsum_b(x):
    return jnp.broadcast_to(jnp.sum(x, axis=1, keepdims=True), x.shape)


def _select_chunk(x, lo, hi, need, tied_any, tied, prefix0):
    R, CK = x.shape
    rep = CK // LANES
    wide = lambda a: jnp.concatenate([a] * rep, axis=1)

    def plain(_):
        return jnp.where(x >= wide(lo), 1.0, 0.0), prefix0

    def with_ties(_):
        low, hiw = wide(lo), wide(hi)
        band = (x >= low) & (x < hiw)
        bandf = jnp.where(band, 1.0, 0.0)
        r_i = lax.broadcasted_iota(jnp.int32, (CK, CK), 0)
        c_i = lax.broadcasted_iota(jnp.int32, (CK, CK), 1)
        tri = jnp.where(r_i <= c_i, 1.0, 0.0).astype(BF16)
        rank = jnp.dot(bandf.astype(BF16), tri, preferred_element_type=F32) + wide(prefix0)
        take_tie = jnp.where(rank <= wide(need), bandf, 0.0)
        sel_tied = jnp.where(x >= hiw, 1.0, take_tie)
        sel = jnp.where(wide(tied) > 0.0, sel_tied, jnp.where(x >= low, 1.0, 0.0))
        return sel, prefix0 + _rowsum_b(sum(_lane_tiles(bandf)))

    return lax.cond(tied_any, with_ties, plain, 0)


_QB = 128
_CK = 512


def _col_partial(x, op):
    n_chain = 4
    g = x.shape[0] // (SUBLANES * n_chain)
    x4 = x.reshape(n_chain, g, SUBLANES, x.shape[1])
    c = [op(x4[k], axis=0) for k in range(n_chain)]
    pair = jnp.stack([c[0], c[1]]), jnp.stack([c[2], c[3]])
    return op(jnp.stack([op(pair[0], axis=0), op(pair[1], axis=0)]), axis=0)


def _col_fold(x, op):
    return op(_col_partial(x, op), axis=0, keepdims=True)


def _bisect(count_pass, kk, lo0, hi0, n_causal, active):
    def not_done(c_lo, lo, hib):
        pending = active & (c_lo != kk) & (lo != hib)
        return jnp.max(jnp.where(pending, 1.0, 0.0)) > 0.0

    def make_iter(snap, reps):
        def one(state):
            lo, hi, hib, c_lo, c_hi = state
            if snap:
                mid = 0.5 * (lo + jnp.minimum(hi, hib))
                mid = jnp.where(mid > lo, mid, jnp.minimum(hi, hib))
            else:
                mid = 0.5 * (lo + hi)
            c, amin, bmax = count_pass(mid, snap)
            ge = c >= kk
            if snap:
                lo = jnp.where(ge, amin, lo)
                hib = jnp.where(ge, hib, bmax)
            else:
                lo = jnp.where(ge, mid, lo)
            hi = jnp.where(ge, hi, mid)
            return lo, hi, hib, jnp.where(ge, c, c_lo), jnp.where(ge, c_hi, c)

        def step(carry):
            it, state = carry[0], carry[1:]
            for _ in range(reps):
                state = one(state)
            return (it + reps, *state)
        return step

    init = (jnp.int32(0), lo0, hi0, jnp.full(lo0.shape, jnp.inf, F32), n_causal, jnp.zeros(lo0.shape, F32))
    carry = lax.while_loop(
        lambda c: (c[0] < _BISECT_PLAIN_ITERS) & not_done(c[4], c[1], c[3]), make_iter(False, 2), init)
    carry = lax.while_loop(
        lambda c: (c[0] < _BISECT_MAX_ITERS) & not_done(c[4], c[1], c[3]), make_iter(True, 1), carry)
    _, lo, hi, hib, c_lo, c_hi = carry
    return lo, hi, c_hi, active & (c_lo != kk)


def _threshold_search_t(sc_ref, nck, kk, lo0, hi0, n_causal):
    Q = lo0.shape[1]

    def count_pass(mid, snap):
        def body(kc, carry):
            x = sc_ref[kc]
            cnt, amin, bmax = carry
            ge = x >= mid
            cnt = cnt + _col_partial(jnp.where(ge, 1.0, 0.0), jnp.sum)
            if snap:
                amin = jnp.minimum(amin, _col_partial(jnp.where(ge, x, jnp.inf), jnp.min))
                bmax = jnp.maximum(bmax, _col_partial(jnp.where(ge, -jnp.inf, x), jnp.max))
            return cnt, amin, bmax
        init = (jnp.zeros((SUBLANES, Q), F32), jnp.full((SUBLANES, Q), jnp.inf, F32),
                jnp.full((SUBLANES, Q), -jnp.inf, F32))
        cnt, amin, bmax = lax.fori_loop(0, nck, body, init)
        return (jnp.sum(cnt, axis=0, keepdims=True), jnp.min(amin, axis=0, keepdims=True),
                jnp.max(bmax, axis=0, keepdims=True))

    return _bisect(count_pass, kk, lo0, hi0, n_causal, lo0 == lo0)


def _threshold_search_rows(sc_ref, kk, lo0, hi0, n_causal, active):
    def count_pass(mid, snap):
        x = sc_ref[...]
        ge = x >= mid
        cnt = jnp.sum(jnp.where(ge, 1.0, 0.0), axis=1, keepdims=True)
        if not snap:
            return cnt, None, None
        return (cnt, jnp.min(jnp.where(ge, x, jnp.inf), axis=1, keepdims=True),
                jnp.max(jnp.where(ge, -jnp.inf, x), axis=1, keepdims=True))

    return _bisect(count_pass, kk, lo0, hi0, n_causal, active)


def _select_chunk_t(x, lo, hi, need, tied_any, tiedf, prefix0):
    ck = x.shape[0]

    def plain(_):
        return jnp.where(x >= lo, 1.0, 0.0), prefix0

    def with_ties(_):
        bandf = jnp.where((x >= lo) & (x < hi), 1.0, 0.0)
        r_i = lax.broadcasted_iota(jnp.int32, (ck, ck), 0)
        c_i = lax.broadcasted_iota(jnp.int32, (ck, ck), 1)
        tri = jnp.where(c_i <= r_i, 1.0, 0.0).astype(BF16)
        rank = jnp.dot(tri, bandf.astype(BF16), preferred_element_type=F32) + prefix0
        take_tie = jnp.where(rank <= need, bandf, 0.0)
        sel_tied = jnp.where(x >= hi, 1.0, take_tie)
        sel = jnp.where(tiedf > 0.0, sel_tied, jnp.where(x >= lo, 1.0, 0.0))
        return sel, prefix0 + _col_fold(bandf, jnp.sum)

    return lax.cond(tied_any, with_ties, plain, 0)


def _prompt_dsa_kernel(qi_ref, kiwiq_ref, kiwi_ref, qa_ref, ka_ref, va_ref, o_ref,
                         sc_ref, vt_ref, acc_ref, *, topk, seq):
    i = pl.program_id(1)
    t0 = i * _QB
    nck = (t0 + _QB + _CK - 1) // _CK
    Q = _QB
    HQ = ATT_HEADS * Q

    @pl.when(i == 0)
    def _():
        for j in range(seq // _CK):
            vt_ref[:, j * _CK:(j + 1) * _CK] = va_ref[j * _CK:(j + 1) * _CK, :].T

    qpos = t0 + lax.broadcasted_iota(jnp.int32, (1, Q), 1)
    key_row = lax.broadcasted_iota(jnp.int32, (_CK, Q), 0)

    qi = qi_ref[...]
    w_t = kiwiq_ref[...].T
    q_heads, w_rows = [], []
    for h in range(IDX_HEADS):
        qh = qi[:, h * IDX_DIM:(h + 1) * IDX_DIM] * (IDX_DIM ** -0.5)
        q_heads.append(jnp.concatenate([qh, jnp.zeros((Q, LANES - IDX_DIM), F32)], axis=1).astype(BF16))
        w_rows.append(w_t[IDX_DIM + h:IDX_DIM + h + 1, :])

    def score_body(kc, carry):
        mn, mx = carry
        k0 = pl.multiple_of(kc * _CK, _CK)
        kch = kiwi_ref[pl.ds(k0, _CK), :].astype(BF16)
        acc = jnp.zeros((_CK, Q), F32)
        for h in range(IDX_HEADS):
            acc = acc + w_rows[h] * jnp.maximum(_dot_nt(kch, q_heads[h]), 0.0)
        causal = (k0 + key_row) <= qpos
        sc_ref[kc] = jnp.where(causal, acc, -jnp.inf)
        mn = jnp.minimum(mn, _col_fold(jnp.where(causal, acc, jnp.inf), jnp.min))
        mx = jnp.maximum(mx, _col_fold(jnp.where(causal, acc, -jnp.inf), jnp.max))
        return mn, mx

    mn, mx = lax.fori_loop(0, nck, score_body,
                           (jnp.full((1, Q), jnp.inf, F32), jnp.full((1, Q), -jnp.inf, F32)))
    n_causal = (qpos + 1).astype(F32)
    kk = jnp.minimum(n_causal, float(topk))
    lo, hi, c_hi, tied = _threshold_search_t(sc_ref, nck, kk, mn, mx + (mx - mn) + 1.0, n_causal)
    need = kk - c_hi
    tiedf = jnp.where(tied, 1.0, 0.0)
    tied_any = jnp.max(tiedf) > 0.0

    qa = qa_ref[...]
    q_rows = []
    for h in range(ATT_HEADS):
        c = h // ATT_GROUP
        parts = [jnp.zeros((Q, ATT_HEAD_DIM), F32)] * KV_HEADS
        parts[c] = qa[:, h * ATT_HEAD_DIM:(h + 1) * ATT_HEAD_DIM] * (ATT_HEAD_DIM ** -0.5)
        q_rows.append(jnp.concatenate(parts, axis=1))
    q_all = jnp.concatenate(q_rows, axis=0).astype(BF16)
    acc_ref[...] = jnp.zeros(acc_ref.shape, F32)

    def att_body(kc, carry):
        m_old, l_old, prefix = carry
        k0 = pl.multiple_of(kc * _CK, _CK)
        kch = ka_ref[pl.ds(k0, _CK), :].astype(BF16)
        vch_t = vt_ref[:, pl.ds(k0, _CK)].astype(BF16)
        sel, prefix = _select_chunk_t(sc_ref[kc], lo, hi, need, tied_any, tiedf, prefix)
        selb = jnp.concatenate([sel] * ATT_HEADS, axis=1) > 0.0
        s = jnp.where(selb, _dot_nt(kch, q_all), NEG_BIG)
        m_new = jnp.maximum(m_old, _col_fold(s, jnp.max))
        alpha = jnp.exp(m_old - m_new)
        p = jnp.exp(s - m_new)
        l_new = l_old * alpha + _col_fold(p, jnp.sum)
        acc_ref[...] = acc_ref[...] * alpha + jnp.dot(vch_t, p.astype(BF16), preferred_element_type=F32)
        return m_new, l_new, prefix

    init = (jnp.full((1, HQ), NEG_BIG, F32), jnp.zeros((1, HQ), F32), jnp.zeros((1, Q), F32))
    _, l_fin, _ = lax.fori_loop(0, nck, att_body, init)
    o_t = acc_ref[...] / l_fin
    outs = []
    for h in range(ATT_HEADS):
        c = h // ATT_GROUP
        outs.append(o_t[:, h * Q:(h + 1) * Q].T[:, c * ATT_HEAD_DIM:(c + 1) * ATT_HEAD_DIM])
    o_ref[...] = jnp.concatenate(outs, axis=1)


def _prompt_dsa(qi, kiwi, qa, ka, va, B, S):
    nq = S // _QB
    topk = min(TOPK_MAX, S // 4)
    qblk = lambda w: pl.BlockSpec((_QB, w), lambda b, i: (b * nq + i, 0))
    allk = lambda w: pl.BlockSpec((S, w), lambda b, i: (b, 0))
    return pl.pallas_call(
        functools.partial(_prompt_dsa_kernel, topk=topk, seq=S),
        grid=(B, nq),
        in_specs=[qblk(IDX_WIDTH), qblk(_KIWI_WIDTH), allk(_KIWI_WIDTH), qblk(ATT_WIDTH),
                  allk(KV_WIDTH), allk(KV_WIDTH)],
        out_specs=qblk(ATT_WIDTH),
        out_shape=jax.ShapeDtypeStruct((B * S, ATT_WIDTH), F32),
        scratch_shapes=[pltpu.VMEM((S // _CK, _CK, _QB), F32),
                        pltpu.VMEM((KV_WIDTH, S), F32),
                        pltpu.VMEM((KV_WIDTH, ATT_HEADS * _QB), F32)],
        name="prompt_dsa",
        compiler_params=_cparams(("arbitrary", "arbitrary")),
    )(qi, kiwi, kiwi, qa, ka, va)


SROWS = SUBLANES
_CKS = 640


def _sample_dsa_kernel(pt_ref, qi_ref, kiwi_ref, qa_ref, ka_ref, va_ref, cik_hbm, ck_hbm, cv_hbm,
                       o_ref, ikbuf, kbuf, vbuf, sems, sc_ref, sel_ref, *, n_pages, t_real, topk):
    b = pl.program_id(0)
    nb = pl.num_programs(0)
    slot = b % 2
    past_len = n_pages * PAGE_SIZE
    L = past_len + PAGE_SIZE
    nck = L // _CKS
    R = SROWS

    def page_copies(bb, s, p):
        phys = pt_ref[bb, p]
        cols = pl.ds(pl.multiple_of(p * PAGE_SIZE, PAGE_SIZE), PAGE_SIZE)
        return (pltpu.make_async_copy(cik_hbm.at[phys], ikbuf.at[s, :, cols], sems.at[s, 0]),
                pltpu.make_async_copy(ck_hbm.at[phys], kbuf.at[s, :, cols], sems.at[s, 1]),
                pltpu.make_async_copy(cv_hbm.at[phys], vbuf.at[s, :, cols], sems.at[s, 2]))

    def start_fetch(bb, s):
        def body(p, _):
            for cp in page_copies(bb, s, p):
                cp.start()
            return 0
        lax.fori_loop(0, n_pages, body, 0)

    def wait_fetch(bb, s):
        def body(p, _):
            for cp in page_copies(bb, s, p):
                cp.wait()
            return 0
        lax.fori_loop(0, n_pages, body, 0)

    @pl.when(b == 0)
    def _():
        tail = pl.ds(past_len, PAGE_SIZE)
        for s in range(2):
            ikbuf[s, :, tail] = jnp.zeros((IDX_DIM, PAGE_SIZE), F32)
            kbuf[s, :, tail] = jnp.zeros((KV_WIDTH, PAGE_SIZE), F32)
            vbuf[s, :, tail] = jnp.zeros((KV_WIDTH, PAGE_SIZE), F32)
        start_fetch(0, 0)

    @pl.when(b + 1 < nb)
    def _():
        start_fetch(b + 1, 1 - slot)

    kiwi = kiwi_ref[...]

    def as_columns(rows):
        padded = jnp.concatenate([rows, jnp.zeros((LANES - R, LANES), F32)], axis=0)
        return padded.T[:, :R]

    new_cols = pl.ds(past_len, R)
    ikbuf[slot, :, new_cols] = as_columns(kiwi)[:IDX_DIM]
    kbuf[slot, :, new_cols] = as_columns(ka_ref[...])
    vbuf[slot, :, new_cols] = as_columns(va_ref[...])
    wait_fetch(b, slot)

    row = lax.broadcasted_iota(jnp.int32, (R, 1), 0)
    key_pos = lax.broadcasted_iota(jnp.int32, (R, L), 1)

    qi = qi_ref[...]
    q_all = jnp.concatenate(
        [qi[:, h * IDX_DIM:(h + 1) * IDX_DIM] * (IDX_DIM ** -0.5) for h in range(IDX_HEADS)], axis=0).astype(BF16)
    d = jnp.dot(q_all, ikbuf[slot].astype(BF16), preferred_element_type=F32)
    acc = jnp.zeros((R, L), F32)
    for h in range(IDX_HEADS):
        acc = acc + kiwi[:, IDX_DIM + h:IDX_DIM + h + 1] * jnp.maximum(d[h * R:(h + 1) * R], 0.0)
    causal = key_pos <= past_len + row
    sc_ref[...] = jnp.where(causal, acc, -jnp.inf)
    mn = jnp.min(jnp.where(causal, acc, jnp.inf), axis=1, keepdims=True)
    mx = jnp.max(jnp.where(causal, acc, -jnp.inf), axis=1, keepdims=True)
    n_causal = (past_len + row + 1).astype(F32)
    kk = jnp.minimum(n_causal, float(topk))
    lo, hi, c_hi, tied = _threshold_search_rows(sc_ref, kk, mn, mx + (mx - mn) + 1.0, n_causal, row < t_real)
    tiedf = jnp.where(tied, 1.0, 0.0)
    tied_any = jnp.max(tiedf) > 0.0
    sel_ref[...] = jnp.where(sc_ref[...] >= lo, 1.0, 0.0)

    @pl.when(tied_any)
    def _():
        wide = lambda a: jnp.broadcast_to(a, (R, LANES))

        def tie_body(kc, prefix):
            cols = pl.ds(pl.multiple_of(kc * _CKS, LANES), _CKS)
            sel, prefix = _select_chunk(sc_ref[:, cols], wide(lo), wide(hi), wide(kk - c_hi), tied_any,
                                        wide(tiedf), prefix)
            sel_ref[:, cols] = sel
            return prefix

        lax.fori_loop(0, nck, tie_body, jnp.zeros((R, LANES), F32))

    qa = qa_ref[...]
    q_rows = []
    for h in range(ATT_HEADS):
        c = h // ATT_GROUP
        z = jnp.zeros((R, ATT_HEAD_DIM), F32)
        parts = [z] * KV_HEADS
        parts[c] = qa[:, h * ATT_HEAD_DIM:(h + 1) * ATT_HEAD_DIM] * (ATT_HEAD_DIM ** -0.5)
        q_rows.append(jnp.concatenate(parts, axis=1))
    q_big = jnp.concatenate(q_rows, axis=0).astype(BF16)
    selb = jnp.concatenate([sel_ref[...]] * ATT_HEADS, axis=0) > 0.0
    s = jnp.where(selb, jnp.dot(q_big, kbuf[slot].astype(BF16), preferred_element_type=F32), NEG_BIG)
    p = jnp.exp(s - jnp.max(s, axis=1, keepdims=True))
    o = _dot_nt(p, vbuf[slot]) / jnp.sum(p, axis=1, keepdims=True)
    outs = []
    for h in range(ATT_HEADS):
        c = h // ATT_GROUP
        outs.append(o[h * R:(h + 1) * R, c * ATT_HEAD_DIM:(c + 1) * ATT_HEAD_DIM])
    o_ref[...] = jnp.concatenate(outs, axis=1)


def _sample_dsa(page_table, qi, kiwi, qa, ka, va, cache_idx_k_l, cache_k_l, cache_v_l, t_real, row0):
    Bd, n_pages = page_table.shape
    past_len = n_pages * PAGE_SIZE
    L = past_len + PAGE_SIZE
    assert L % _CKS == 0 and row0 % SROWS == 0
    topk = min(TOPK_MAX, (past_len + t_real) // 4)
    blk0 = row0 // SROWS
    blk = lambda w: pl.BlockSpec((SROWS, w), lambda b, pt: (blk0 + b, 0))
    any_spec = pl.BlockSpec(memory_space=pl.ANY)
    grid_spec = pltpu.PrefetchScalarGridSpec(
        num_scalar_prefetch=1,
        grid=(Bd,),
        in_specs=[blk(IDX_WIDTH), blk(_KIWI_WIDTH), blk(ATT_WIDTH), blk(KV_WIDTH), blk(KV_WIDTH),
                  any_spec, any_spec, any_spec],
        out_specs=pl.BlockSpec((SROWS, ATT_WIDTH), lambda b, pt: (b, 0)),
        scratch_shapes=[pltpu.VMEM((2, IDX_DIM, L), F32),
                        pltpu.VMEM((2, KV_WIDTH, L), F32),
                        pltpu.VMEM((2, KV_WIDTH, L), F32),
                        pltpu.SemaphoreType.DMA((2, 3)),
                        pltpu.VMEM((SROWS, L), F32),
                        pltpu.VMEM((SROWS, L), F32)],
    )
    return pl.pallas_call(
        functools.partial(_sample_dsa_kernel, n_pages=n_pages, t_real=t_real, topk=topk),
        grid_spec=grid_spec,
        out_shape=jax.ShapeDtypeStruct((Bd * SROWS, ATT_WIDTH), F32),
        name="sample_dsa",
        compiler_params=_cparams(("arbitrary",)),
    )(page_table, qi, kiwi, qa, ka, va, cache_idx_k_l, cache_k_l, cache_v_l)


_TMP = 256
_PAIR_GROUP = 16
PEER_HALF_EXPERTS = PEER_EXPERTS // 2


def _extract_topk(s, pos, n, k, payload=None):
    vals, idxs = [], []
    for _ in range(k):
        m = jnp.max(s, axis=0, keepdims=True)
        p = jnp.min(jnp.where(s == m, pos, n), axis=0, keepdims=True)
        hit = pos == p
        vals.append(m)
        if payload is None:
            idxs.append(p)
        else:
            idxs.append(jnp.max(jnp.where(hit, payload, -1), axis=0, keepdims=True))
        s = jnp.where(hit, -jnp.inf, s)
    return vals, idxs


def _post_mix_kernel(x_ref, attp_ref, atts_ref, retp_ref, rets_ref, wo_ref, nw_ref, wqt_ref, sk1_ref, sk2_ref,
                     h_ref, xn_ref, idx_ref, gate_ref, n0_ref, qt_ref, et_ref, gt_ref, la_ref, lg_ref,
                     *, n_prompt_tiles):
    T = x_ref.shape[0]
    is_prompt = pl.program_id(0) < n_prompt_tiles
    att = jnp.where(is_prompt, attp_ref[...], atts_ref[...])
    ret = jnp.where(is_prompt, retp_ref[...], rets_ref[...])
    h = x_ref[...] + _dot(att, wo_ref[:ATT_WIDTH, :]) + _dot(ret, wo_ref[ATT_WIDTH:, :])
    h_ref[...] = h
    ms = jnp.mean(h * h, axis=-1, keepdims=True)
    xn = h * lax.rsqrt(ms + NORM_EPS) * nw_ref[...]
    for c in range(D_MODEL // LANES):
        xn_ref[pl.ds(c, T, stride=SUBLANES), :] = xn[:, c * LANES:(c + 1) * LANES]
    qt_ref[...] = _dot_nt(wqt_ref[...], xn)

    pos_k = lax.broadcasted_iota(jnp.int32, (PEER_NKEYS, T), 0)
    n_b = [PEER_TOPK // (a + 1) for a in range(PEER_TOPK)]
    n_cand = -(-sum(n_b) // SUBLANES) * SUBLANES
    pos_c = lax.broadcasted_iota(jnp.int32, (n_cand, T), 0)

    def head_body(hd, _):
        q0 = pl.multiple_of(hd * PEER_KEY_DIM, PEER_KEY_DIM)
        q1 = qt_ref[pl.ds(q0, PEER_HALF), :]
        q2 = qt_ref[pl.ds(q0 + PEER_HALF, PEER_HALF), :]
        s1 = _dot(sk1_ref[...], q1)
        s2 = _dot(sk2_ref[...], q2)
        v1, i1 = _extract_topk(s1, pos_k, PEER_NKEYS, PEER_TOPK)
        v2, i2 = _extract_topk(s2, pos_k, PEER_NKEYS, PEER_TOPK)
        v2m = jnp.concatenate(v2, axis=0)
        i2m = jnp.concatenate(i2, axis=0)
        n_fill = n_cand - sum(n_b)
        cand = jnp.concatenate([v1[a] + v2m[:n_b[a]] for a in range(PEER_TOPK)]
                               + [jnp.full((n_fill, T), -jnp.inf, F32)], axis=0)
        cid = jnp.concatenate([i1[a] * PEER_NKEYS + i2m[:n_b[a]] for a in range(PEER_TOPK)]
                              + [jnp.full((n_fill, T), -1, jnp.int32)], axis=0)
        sv, eid = _extract_topk(cand, pos_c, n_cand, PEER_TOPK, payload=cid)
        svm = jnp.concatenate(sv, axis=0)
        g = jnp.exp(svm - sv[0])
        r0 = pl.multiple_of(hd * PEER_TOPK, PEER_TOPK)
        gt_ref[pl.ds(r0, PEER_TOPK), :] = g / jnp.sum(g, axis=0, keepdims=True)
        et_ref[pl.ds(r0, PEER_TOPK), :] = jnp.concatenate(eid, axis=0).astype(F32)
        return 0

    lax.fori_loop(0, PEER_HEADS, head_body, 0)

    e = et_ref[...]
    g = gt_ref[...]
    npair = e.shape[0]
    is0 = e < float(PEER_HALF_EXPERTS)
    r_i = lax.broadcasted_iota(jnp.int32, (npair, npair), 0)
    c_i = lax.broadcasted_iota(jnp.int32, (npair, npair), 1)
    tri = jnp.where(c_i <= r_i, 1.0, 0.0).astype(BF16)
    rank0 = jnp.dot(tri, jnp.where(is0, 1.0, 0.0).astype(BF16), preferred_element_type=F32)
    rowf = lax.broadcasted_iota(jnp.int32, (npair, T), 0).astype(F32)
    n0 = rank0[npair - 1:npair, :]
    place = jnp.where(is0, rank0 - 1.0, n0 + (rowf - rank0))
    off = jnp.where(is0, e, e - float(PEER_HALF_EXPERTS)) * float(SUBLANES)
    for p in range(npair):
        m = place == float(p)
        la_ref[p:p + 1, :] = jnp.sum(jnp.where(m, off, 0.0), axis=0, keepdims=True)
        lg_ref[p:p + 1, :] = jnp.sum(jnp.where(m, g, 0.0), axis=0, keepdims=True)
    idx_ref[...] = la_ref[...].T.astype(jnp.int32)
    gate_ref[...] = lg_ref[...].T
    n0_ref[...] = n0.astype(jnp.int32)


def _post_mix(x, att_p, att_s, ret_p, ret_s, w_out_bf, norm_w, wq_t_bf, sk1_bf, sk2_bf):
    N = x.shape[0]
    nt = N // _TMP
    ntp = att_p.shape[0] // _TMP
    nts = att_s.shape[0] // _TMP
    assert ntp + nts == nt and ntp > 0 and nts > 0
    npair = PEER_HEADS * PEER_TOPK
    row = lambda w: pl.BlockSpec((_TMP, w), lambda i: (i, 0))
    prow = lambda w: pl.BlockSpec((_TMP, w), lambda i: (jnp.minimum(i, ntp - 1), 0))
    srow = lambda w: pl.BlockSpec((_TMP, w), lambda i: (jnp.maximum(i - ntp, 0), 0))
    full = lambda a: pl.BlockSpec(a.shape, lambda i: (0,) * a.ndim)
    nw = norm_w.reshape(1, D_MODEL)
    return pl.pallas_call(
        functools.partial(_post_mix_kernel, n_prompt_tiles=ntp),
        grid=(nt,),
        in_specs=[row(D_MODEL), prow(ATT_WIDTH), srow(ATT_WIDTH), prow(RET_WIDTH), srow(RET_WIDTH),
                  full(w_out_bf), full(nw), full(wq_t_bf), full(sk1_bf), full(sk2_bf)],
        out_specs=[row(D_MODEL), pl.BlockSpec((_TMP * SUBLANES, LANES), lambda i: (i, 0)),
                   row(npair), row(npair), pl.BlockSpec((1, _TMP), lambda i: (0, i))],
        out_shape=[jax.ShapeDtypeStruct((N, D_MODEL), F32), jax.ShapeDtypeStruct((N * SUBLANES, LANES), F32),
                   jax.ShapeDtypeStruct((N, npair), jnp.int32), jax.ShapeDtypeStruct((N, npair), F32),
                   jax.ShapeDtypeStruct((1, N), jnp.int32)],
        scratch_shapes=[pltpu.VMEM((PEER_HEADS * PEER_KEY_DIM, _TMP), F32),
                        pltpu.VMEM((npair, _TMP), F32), pltpu.VMEM((npair, _TMP), F32),
                        pltpu.VMEM((npair, _TMP), F32), pltpu.VMEM((npair, _TMP), F32)],
        name="post_mix",
        compiler_params=_cparams(("arbitrary",)),
    )(x, att_p, att_s, ret_p, ret_s, w_out_bf, nw, wq_t_bf, sk1_bf, sk2_bf)


_TBP = 128
_NPAIR = PEER_HEADS * PEER_TOPK
_ROW_TILE = D_MODEL // LANES
assert _ROW_TILE == SUBLANES
_HIGH_HALF = -65536


def _bf16_bits(x):
    return lax.bitcast_convert_type(x.astype(BF16).astype(F32), jnp.int32)


def _pack_table_kernel(lo_ref, hi_ref, o_ref):
    n = lo_ref.shape[0]
    words = (lax.shift_right_logical(_bf16_bits(lo_ref[...]), 16)
             | (_bf16_bits(hi_ref[...]) & _HIGH_HALF))
    for c in range(_ROW_TILE):
        o_ref[pl.ds(c, n, stride=_ROW_TILE), :] = words[:, c * LANES:(c + 1) * LANES]


def _pack_table(tbl):
    blk = 512
    nb = PEER_HALF_EXPERTS // blk
    return pl.pallas_call(
        _pack_table_kernel,
        grid=(nb,),
        in_specs=[pl.BlockSpec((blk, D_MODEL), lambda i: (i, 0)),
                  pl.BlockSpec((blk, D_MODEL), lambda i: (i + nb, 0))],
        out_specs=pl.BlockSpec((blk * _ROW_TILE, LANES), lambda i: (i, 0)),
        out_shape=jax.ShapeDtypeStruct((PEER_HALF_EXPERTS * _ROW_TILE, LANES), jnp.int32),
        name="pack_table",
        compiler_params=_cparams(("arbitrary",)),
    )(tbl, tbl)


def _unpack(words, half):
    bits = lax.shift_left(words, 16) if half == 0 else words & _HIGH_HALF
    return lax.bitcast_convert_type(bits, F32)


def _load_table(tbl_hbm, tbuf, sem):
    cp = pltpu.make_async_copy(tbl_hbm, tbuf, sem)
    cp.start()
    cp.wait()


def _fold_pair(a, b, k, sub):
    m = (sub & k) == 0
    return jnp.where(m, a, pltpu.roll(b, k, 0)) + jnp.where(m, pltpu.roll(a, SUBLANES - k, 0), b)


def _fold8(p, sub):
    a, b, c, d, e, f, g, h = p[0], p[4], p[2], p[6], p[1], p[5], p[3], p[7]
    t1, t2, t3, t4 = (_fold_pair(a, b, 4, sub), _fold_pair(c, d, 4, sub),
                      _fold_pair(e, f, 4, sub), _fold_pair(g, h, 4, sub))
    u1, u2 = _fold_pair(t1, t2, 2, sub), _fold_pair(t3, t4, 2, sub)
    return _fold_pair(u1, u2, 1, sub)


def _flat_smem_spec(nt, per_token):
    return pl.BlockSpec((1, 1, _TBP * per_token), lambda i: (i, 0, 0), memory_space=pltpu.SMEM)


def _flat_blocks(a, nt):
    return a.reshape(nt, 1, -1)


_N_GROUPS = _NPAIR // _PAIR_GROUP


def _group_start(t, g):
    g = jnp.minimum(g, _N_GROUPS - 1)
    return pl.multiple_of(t * _NPAIR + g * _PAIR_GROUP, _PAIR_GROUP)


def _table_row(tbuf, idx_ref, k):
    off = pl.multiple_of(idx_ref[0, 0, k], _ROW_TILE)
    return tbuf[pl.ds(off, _ROW_TILE), :]


def _token_segments(n0):
    g_mid = lax.shift_right_logical(n0, _PAIR_GROUP.bit_length() - 1)
    rem = n0 & (_PAIR_GROUP - 1)
    g_hi = g_mid + jnp.where(rem != 0, 1, 0)
    return g_mid, rem, g_hi


def _mixed_unpack(words, s, rem):
    return jnp.where(s < rem, _unpack(words, 0), _unpack(words, 1))


def _peer_u_kernel(idx_ref, n0_ref, x_ref, u_hbm, o_ref, tbuf, sem, z_ref):
    @pl.when(pl.program_id(0) == 0)
    def _():
        _load_table(u_hbm, tbuf, sem)

    sub = lax.broadcasted_iota(jnp.int32, (SUBLANES, LANES), 0)

    def tok_body(t, _):
        xt = x_ref[t]
        g_mid, rem, g_hi = _token_segments(n0_ref[0, 0, t])

        def load_words(g):
            kg = _group_start(t, g)
            return tuple(_table_row(tbuf, idx_ref, kg + s) for s in range(_PAIR_GROUP))

        def fold_group(g, vals):
            kg = _group_start(t, g)
            prods = [v * xt for v in vals]
            for q in range(_PAIR_GROUP // SUBLANES):
                z_ref[pl.ds(kg + q * SUBLANES, SUBLANES), :] = _fold8(prods[q * SUBLANES:(q + 1) * SUBLANES], sub)

        def uniform_groups(ga, gb, half):
            def grp_body(g, words):
                nxt = load_words(g + 1)
                fold_group(g, [_unpack(w, half) for w in words])
                return nxt
            lax.fori_loop(ga, gb, grp_body, load_words(ga))

        uniform_groups(0, g_mid, 0)

        @pl.when(rem != 0)
        def _():
            fold_group(g_mid, [_mixed_unpack(w, s, rem) for s, w in enumerate(load_words(g_mid))])

        uniform_groups(g_hi, _N_GROUPS, 1)
        return 0

    lax.fori_loop(0, _TBP, tok_body, 0)

    ones = jnp.ones((SUBLANES, LANES), BF16)

    def chunk_body(c, _):
        r0 = pl.multiple_of(c * SUBLANES * _NPAIR, SUBLANES * _NPAIR)
        z = z_ref[pl.ds(r0, SUBLANES * _NPAIR), :]
        zh = z.astype(BF16)
        zl = (z - zh.astype(F32)).astype(BF16)
        hv = _dot_nt(ones, zh) + _dot_nt(ones, zl)
        rows = [hv[0:1, tt * _NPAIR:(tt + 1) * _NPAIR] for tt in range(SUBLANES)]
        o_ref[pl.ds(pl.multiple_of(c * SUBLANES, SUBLANES), SUBLANES), :] = jnp.concatenate(rows, axis=0)
        return 0

    lax.fori_loop(0, _TBP // SUBLANES, chunk_body, 0)


def _peer_u(idx, n0, xn_tiles, u_packed):
    N = xn_tiles.shape[0] // _ROW_TILE
    nt = N // _TBP
    return pl.pallas_call(
        _peer_u_kernel,
        grid=(nt,),
        in_specs=[_flat_smem_spec(nt, _NPAIR), _flat_smem_spec(nt, 1),
                  pl.BlockSpec((_TBP, _ROW_TILE, LANES), lambda i: (i, 0, 0)),
                  pl.BlockSpec(memory_space=pl.ANY)],
        out_specs=pl.BlockSpec((_TBP, _NPAIR), lambda i: (i, 0)),
        out_shape=jax.ShapeDtypeStruct((N, _NPAIR), F32),
        scratch_shapes=[pltpu.VMEM((PEER_HALF_EXPERTS * _ROW_TILE, LANES), jnp.int32),
                        pltpu.SemaphoreType.DMA(()),
                        pltpu.VMEM((_TBP * _NPAIR, LANES), F32)],
        name="peer_u",
        compiler_params=_cparams(("arbitrary",)),
    )(_flat_blocks(idx, nt), _flat_blocks(n0, nt), xn_tiles.reshape(N, _ROW_TILE, LANES), u_packed)


def _peer_coef_kernel(hv_ref, gate_ref, o_ref):
    hv = hv_ref[...]
    o_ref[...] = gate_ref[...] * (0.5 * hv * (1.0 + lax.erf(hv * (0.5 ** 0.5))))


def _peer_coef(hval, gate):
    N = gate.shape[0]
    tm = 1024 if N % 1024 == 0 else _TBP
    spec = pl.BlockSpec((tm, _NPAIR), lambda i: (i, 0))
    return pl.pallas_call(
        _peer_coef_kernel,
        grid=(N // tm,),
        in_specs=[spec, spec],
        out_specs=spec,
        out_shape=jax.ShapeDtypeStruct((N, _NPAIR), F32),
        name="peer_coef",
        compiler_params=_cparams(("arbitrary",)),
    )(hval, gate)


def _peer_v_kernel(idx_ref, n0_ref, coef_ref, v_hbm, o_ref, tbuf, sem):
    @pl.when(pl.program_id(0) == 0)
    def _():
        _load_table(v_hbm, tbuf, sem)

    n_acc = 4

    def accumulate(accs, prods):
        accs = list(accs)
        for s in range(_PAIR_GROUP):
            accs[s % n_acc] = accs[s % n_acc] + prods[s]
        return tuple(accs)

    def tok_body(t, _):
        g_mid, rem, g_hi = _token_segments(n0_ref[0, 0, t])

        def weighted_group(g, unpack):
            kg = _group_start(t, g)
            return tuple(coef_ref[0, 0, kg + s] * unpack(_table_row(tbuf, idx_ref, kg + s), s)
                         for s in range(_PAIR_GROUP))

        def uniform_groups(ga, gb, half, accs):
            unpack = lambda w, s: _unpack(w, half)

            def grp_body(g, carry):
                prods, accs = carry
                nxt = weighted_group(g + 1, unpack)
                return nxt, accumulate(accs, prods)

            return lax.fori_loop(ga, gb, grp_body, (weighted_group(ga, unpack), accs))[1]

        zero = jnp.zeros((_ROW_TILE, LANES), F32)
        accs = uniform_groups(0, g_mid, 0, (zero,) * n_acc)
        accs = lax.cond(
            rem != 0,
            lambda a: accumulate(a, weighted_group(g_mid, lambda w, s: _mixed_unpack(w, s, rem))),
            lambda a: a, accs)
        accs = uniform_groups(g_hi, _N_GROUPS, 1, accs)
        o_ref[t] = (accs[0] + accs[1]) + (accs[2] + accs[3])
        return 0

    lax.fori_loop(0, _TBP, tok_body, 0)


def _peer_v(idx, n0, coef, v_packed):
    N = coef.shape[0]
    nt = N // _TBP
    out = pl.pallas_call(
        _peer_v_kernel,
        grid=(nt,),
        in_specs=[_flat_smem_spec(nt, _NPAIR), _flat_smem_spec(nt, 1), _flat_smem_spec(nt, _NPAIR),
                  pl.BlockSpec(memory_space=pl.ANY)],
        out_specs=pl.BlockSpec((_TBP, _ROW_TILE, LANES), lambda i: (i, 0, 0)),
        out_shape=jax.ShapeDtypeStruct((N, _ROW_TILE, LANES), F32),
        scratch_shapes=[pltpu.VMEM((PEER_HALF_EXPERTS * _ROW_TILE, LANES), jnp.int32),
                        pltpu.SemaphoreType.DMA(())],
        name="peer_v",
        compiler_params=_cparams(("arbitrary",)),
    )(_flat_blocks(idx, nt), _flat_blocks(n0, nt), _flat_blocks(coef, nt), v_packed)
    return out.reshape(N * _ROW_TILE, LANES)


def _final_kernel(h_ref, y_ref, w_ref, o_ref, *, normalize):
    T = h_ref.shape[0]
    y = jnp.concatenate(
        [y_ref[pl.ds(c, T, stride=SUBLANES), :] for c in range(D_MODEL // LANES)], axis=1)
    h = h_ref[...] + y
    if normalize:
        ms = jnp.mean(h * h, axis=-1, keepdims=True)
        h = h * lax.rsqrt(ms + NORM_EPS) * w_ref[...]
    o_ref[...] = h


def _final(h, ypart, w):
    N = h.shape[0]
    tm = 512 if N % 512 == 0 else _TBP
    normalize = w is not None
    if not normalize:
        w = jnp.ones((D_MODEL,), F32)
    return pl.pallas_call(
        functools.partial(_final_kernel, normalize=normalize),
        grid=(N // tm,),
        in_specs=[pl.BlockSpec((tm, D_MODEL), lambda i: (i, 0)),
                  pl.BlockSpec((tm * SUBLANES, LANES), lambda i: (i, 0)),
                  pl.BlockSpec((1, D_MODEL), lambda i: (0, 0))],
        out_specs=pl.BlockSpec((tm, D_MODEL), lambda i: (i, 0)),
        out_shape=jax.ShapeDtypeStruct((N, D_MODEL), F32),
        name="final_norm",
        compiler_params=_cparams(("arbitrary",)),
    )(h, ypart, w.reshape(1, D_MODEL))


_TM_IN = 256


def _pages_feature_major(cache_l):
    n_phys, page = cache_l.shape[:2]
    return jnp.swapaxes(cache_l.reshape(n_phys, page, -1), 1, 2)


def kernel(x_prompt, x_sample, cache_k, cache_v, cache_idx_k, state_ret, page_table, norm_attn_w, w_in, ret_gn_w,
           w_out, norm_ffn_w, peer_w_q, peer_sub_keys_1, peer_sub_keys_2, peer_u, peer_v, final_norm_w):
    B, S, D = x_prompt.shape
    Bd, T, _ = x_sample.shape
    depth = w_in.shape[0]
    n_pages = page_table.shape[1]
    past_len = n_pages * PAGE_SIZE
    n_phys = cache_k.shape[1]
    Np, Ns = B * S, Bd * SROWS
    assert D == D_MODEL and T <= SROWS and S % _TM_IN == 0 and Ns % _TM_IN == 0 and S % _CK == 0

    hs_pad = jnp.pad(x_sample, ((0, 0), (0, SROWS - T), (0, 0)))
    h_all = jnp.concatenate([x_prompt.reshape(Np, D), hs_pad.reshape(Ns, D)], axis=0)

    pos_s = past_len + jnp.arange(SROWS)
    pos = jnp.concatenate([jnp.arange(S), jnp.tile(pos_s, _TM_IN // SROWS)])
    tab_att, tab_ret = _rope_tables(pos)
    tiles_per_seq = S // _TM_IN
    tab_blocks = [i % tiles_per_seq for i in range(Np // _TM_IN)] + [tiles_per_seq] * (Ns // _TM_IN)
    ret_consts_p = _ret_constants(RET_CHUNK)
    ret_consts_s = _ret_constants(T)

    outs = {n: [] for n in ("kp", "vp", "ikp", "sp", "ks", "vs", "iks", "ss")}
    for l in range(depth):
        m = _in_proj(h_all, norm_attn_w[l], _pad_w_in(w_in[l]), tab_att, tab_ret, tab_blocks, _TM_IN)
        att_p = _prompt_dsa(m["qi"], m["kiwi"], m["qa"], m["ka"], m["va"], B, S)
        att_s = _sample_dsa(page_table, m["qi"], m["kiwi"], m["qa"], m["ka"], m["va"],
                            _pages_feature_major(cache_idx_k[l]), _pages_feature_major(cache_k[l]),
                            _pages_feature_major(cache_v[l]), T, Np)
        ret_p, s_p = _retention(m["qr"], m["kr"], m["vr"], m["gr"], ret_gn_w[l],
                                jnp.zeros((B, RET_HEADS, RET_HEAD_DIM, RET_HEAD_DIM), F32),
                                ret_consts_p, B, S // RET_CHUNK, RET_CHUNK)
        ret_s, s_s = _retention(m["qr"], m["kr"], m["vr"], m["gr"], ret_gn_w[l], state_ret[l].astype(F32),
                                ret_consts_s, Bd, 1, SROWS, row0=Np)
        h_mid, xn, idx, gate, n0 = _post_mix(
            h_all, att_p, att_s, ret_p, ret_s, w_out[l].astype(BF16), norm_ffn_w[l],
            peer_w_q[l].T.astype(BF16), peer_sub_keys_1[l].astype(BF16), peer_sub_keys_2[l].astype(BF16))
        coef = _peer_coef(_peer_u(idx, n0, xn, _pack_table(peer_u[l])), gate)
        ypart = _peer_v(idx, n0, coef, _pack_table(peer_v[l]))
        last = l == depth - 1
        h_all = _final(h_mid, ypart, final_norm_w if last else None)

        sample = lambda a, w: a[Np:].reshape(Bd, SROWS, w)[:, :T]
        outs["kp"].append(m["ka"][:Np].reshape(B, S, KV_HEADS, ATT_HEAD_DIM))
        outs["vp"].append(m["va"][:Np].reshape(B, S, KV_HEADS, ATT_HEAD_DIM))
        outs["ikp"].append(m["kiwi"][:Np, :IDX_DIM].reshape(B, S, IDX_DIM))
        outs["sp"].append(s_p.astype(x_prompt.dtype))
        outs["ks"].append(sample(m["ka"], KV_WIDTH).reshape(Bd, T, KV_HEADS, ATT_HEAD_DIM))
        outs["vs"].append(sample(m["va"], KV_WIDTH).reshape(Bd, T, KV_HEADS, ATT_HEAD_DIM))
        outs["iks"].append(sample(m["kiwi"], _KIWI_WIDTH)[:, :, :IDX_DIM])
        outs["ss"].append(s_s.astype(state_ret.dtype))

    y_prompt = h_all[:Np].reshape(B, S, D)
    y_sample = h_all[Np:].reshape(Bd, SROWS, D)[:, :T]
    st = lambda n: jnp.stack(outs[n])
    return (y_prompt, y_sample, st("kp"), st("vp"), st("ikp"), st("sp"),
            st("ks"), st("vs"), st("iks"), st("ss"))
```

```python
import functools

import jax
import jax.numpy as jnp
from jax import lax
from jax.experimental import pallas as pl
from jax.experimental.pallas import tpu as pltpu

F32 = jnp.float32
BF16 = jnp.bfloat16

D_MODEL = 1024
PAGE_SIZE = 128
ATT_HEADS = 8
ATT_HEAD_DIM = 64
KV_HEADS = 2
ATT_GROUP = ATT_HEADS // KV_HEADS
ATT_WIDTH = ATT_HEADS * ATT_HEAD_DIM
KV_WIDTH = KV_HEADS * ATT_HEAD_DIM
ROPE_THETA = 500000.0
ATT_ROPE_DIMS = ATT_HEAD_DIM // 4
IDX_HEADS = 8
IDX_DIM = 64
IDX_WIDTH = IDX_HEADS * IDX_DIM
TOPK_MAX = 256
RET_HEADS = 4
RET_HEAD_DIM = 128
RET_WIDTH = RET_HEADS * RET_HEAD_DIM
RET_ROPE_THETA = 10000.0
RET_CHUNK = 128
PEER_HEADS = 8
PEER_NKEYS = 128
PEER_EXPERTS = PEER_NKEYS * PEER_NKEYS
PEER_KEY_DIM = 128
PEER_HALF = PEER_KEY_DIM // 2
PEER_TOPK = 16
NORM_EPS = 1e-6
GN_EPS = 1e-6

LANES = 128
SUBLANES = 8
VMEM_LIMIT_BYTES = 56 * 1024 * 1024

NEG_BIG = -1e30

_IN_SPLITS = (ATT_WIDTH, KV_WIDTH, KV_WIDTH, IDX_WIDTH, IDX_DIM, IDX_HEADS,
              RET_WIDTH, RET_WIDTH, RET_WIDTH, RET_WIDTH)
_KIWI_WIDTH = LANES
_PAD_OFFS = {}
_off = 0
for _name, _w in (("qa", ATT_WIDTH), ("ka", KV_WIDTH), ("va", KV_WIDTH), ("qi", IDX_WIDTH),
                  ("kiwi", _KIWI_WIDTH), ("qr", RET_WIDTH), ("kr", RET_WIDTH),
                  ("vr", RET_WIDTH), ("gr", RET_WIDTH)):
    _PAD_OFFS[_name] = (_off, _w)
    _off += _w
IN_WIDTH_PADDED = _off


def _cparams(sem):
    return pltpu.CompilerParams(dimension_semantics=sem, vmem_limit_bytes=VMEM_LIMIT_BYTES)


def _dot(a, b):
    return jnp.dot(a.astype(BF16), b.astype(BF16), preferred_element_type=F32)


def _dot_nt(a, b):
    return lax.dot_general(a.astype(BF16), b.astype(BF16), (((1,), (1,)), ((), ())),
                           preferred_element_type=F32)


def _pad_w_in(w_in_l):
    off = 0
    parts = []
    for n in _IN_SPLITS:
        parts.append(w_in_l[:, off:off + n])
        off += n
    qa, ka, va, qi, ki, wi, qr, kr, vr, gr = parts
    kiwi = jnp.concatenate(
        [ki, wi, jnp.zeros((w_in_l.shape[0], _KIWI_WIDTH - IDX_DIM - IDX_HEADS), w_in_l.dtype)], axis=1)
    cols = [qa, ka, va, qi, kiwi, qr, kr, vr, gr]
    return jnp.concatenate(cols, axis=1).astype(BF16)


def _rope_tables(pos):
    posf = pos.astype(F32)
    half = ATT_ROPE_DIMS // 2
    inv = 1.0 / (ROPE_THETA ** (jnp.arange(half, dtype=F32) / half))
    ang = posf[:, None] * inv[None, :]
    cos, sin = jnp.cos(ang), jnp.sin(ang)
    P = pos.shape[0]
    one = jnp.ones((P, ATT_HEAD_DIM - ATT_ROPE_DIMS), F32)
    zero = jnp.zeros((P, ATT_HEAD_DIM - ATT_ROPE_DIMS), F32)
    zh = jnp.zeros((P, half), F32)
    c_head = jnp.concatenate([cos, cos, one], axis=1)
    s1_head = jnp.concatenate([-sin, zh, zero], axis=1)
    s2_head = jnp.concatenate([zh, sin, zero], axis=1)
    rep = LANES // ATT_HEAD_DIM
    att = jnp.stack([jnp.tile(c_head, (1, rep)), jnp.tile(s1_head, (1, rep)), jnp.tile(s2_head, (1, rep))])
    halfr = RET_HEAD_DIM // 2
    invr = 1.0 / (RET_ROPE_THETA ** (jnp.arange(halfr, dtype=F32) / halfr))
    angr = posf[:, None] * invr[None, :]
    cr, sr = jnp.cos(angr), jnp.sin(angr)
    ret = jnp.stack([jnp.concatenate([cr, cr], axis=1), jnp.concatenate([-sr, sr], axis=1)])
    return att, ret


def _in_proj_kernel(x_ref, nw_ref, w_ref, ta_ref, tr_ref,
                    qa_ref, ka_ref, va_ref, qi_ref, kiwi_ref, qr_ref, kr_ref, vr_ref, gr_ref):
    x = x_ref[...]
    ms = jnp.mean(x * x, axis=-1, keepdims=True)
    xn = (x * lax.rsqrt(ms + NORM_EPS) * nw_ref[...]).astype(BF16)
    ca, s1a, s2a = ta_ref[0], ta_ref[1], ta_ref[2]
    cr, sr = tr_ref[0], tr_ref[1]

    def proj(name):
        off, w = _PAD_OFFS[name]
        return jnp.dot(xn, w_ref[:, off:off + w], preferred_element_type=F32)

    def rope_att_chunk(vc):
        return (vc * ca + pltpu.roll(vc, LANES - ATT_ROPE_DIMS // 2, 1) * s1a
                + pltpu.roll(vc, ATT_ROPE_DIMS // 2, 1) * s2a)

    def rope_ret_chunk(vc):
        return vc * cr + pltpu.roll(vc, RET_HEAD_DIM // 2, 1) * sr

    def per_chunk(v, fn):
        n = v.shape[1] // LANES
        return jnp.concatenate([fn(v[:, c * LANES:(c + 1) * LANES]) for c in range(n)], axis=1)

    qa_ref[...] = per_chunk(proj("qa"), rope_att_chunk)
    ka_ref[...] = per_chunk(proj("ka"), rope_att_chunk)
    va_ref[...] = proj("va")
    qi_ref[...] = per_chunk(proj("qi"), rope_att_chunk)
    kiwi = proj("kiwi")
    lane = lax.broadcasted_iota(jnp.int32, kiwi.shape, 1)
    kiwi_ref[...] = jnp.where(lane < IDX_DIM, rope_att_chunk(kiwi), kiwi * (IDX_HEADS ** -0.5))
    qr_ref[...] = per_chunk(proj("qr"), rope_ret_chunk)
    kr_ref[...] = per_chunk(proj("kr"), rope_ret_chunk) * (RET_HEAD_DIM ** -0.5)
    vr_ref[...] = proj("vr")
    gr_ref[...] = proj("gr")


def _in_proj(x, norm_w, w_pad, tab_att, tab_ret, tab_block_of_tile, tm):
    N = x.shape[0]
    nt = N // tm
    tab_idx = jnp.asarray(tab_block_of_tile, jnp.int32)
    names = ("qa", "ka", "va", "qi", "kiwi", "qr", "kr", "vr", "gr")
    out_shape = [jax.ShapeDtypeStruct((N, _PAD_OFFS[n][1]), F32) for n in names]
    out_specs = [pl.BlockSpec((tm, _PAD_OFFS[n][1]), lambda i, t: (i, 0)) for n in names]
    grid_spec = pltpu.PrefetchScalarGridSpec(
        num_scalar_prefetch=1,
        grid=(nt,),
        in_specs=[
            pl.BlockSpec((tm, D_MODEL), lambda i, t: (i, 0)),
            pl.BlockSpec((1, D_MODEL), lambda i, t: (0, 0)),
            pl.BlockSpec((D_MODEL, IN_WIDTH_PADDED), lambda i, t: (0, 0)),
            pl.BlockSpec((3, tm, LANES), lambda i, t: (0, t[i], 0)),
            pl.BlockSpec((2, tm, LANES), lambda i, t: (0, t[i], 0)),
        ],
        out_specs=out_specs,
    )

    def body(t_ref, *refs):
        _in_proj_kernel(*refs)

    outs = pl.pallas_call(
        body, grid_spec=grid_spec, out_shape=out_shape, name="in_proj",
        compiler_params=_cparams(("arbitrary",)),
    )(tab_idx, x, norm_w.reshape(1, D_MODEL), w_pad, tab_att, tab_ret)
    return dict(zip(names, outs))


def _ret_constants(c_eff):
    C = RET_CHUNK
    lg = jnp.log1p(-(2.0 ** (-5.0 - jnp.arange(RET_HEADS, dtype=F32))))
    i = jnp.arange(C, dtype=F32)
    diff = i[:, None] - i[None, :]
    dmask = jnp.where(diff >= 0, jnp.exp(lg[:, None, None] * jnp.maximum(diff, 0.0)), 0.0)
    real = (i < c_eff)
    dmask = jnp.where(real[None, :, None] & real[None, None, :], dmask, 0.0)
    q_dec = jnp.exp(lg[:, None] * (i[None, :] + 1.0))
    k_dec = jnp.where(real[None, :], jnp.exp(lg[:, None] * (c_eff - 1.0 - i[None, :])), 0.0)
    chunk_dec = jnp.exp(lg * c_eff)
    bc = lambda a: jnp.broadcast_to(a[:, :, None], (RET_HEADS, C, C))
    cdec = jnp.broadcast_to(chunk_dec[:, None, None], (RET_HEADS, C, C))
    return dmask, bc(q_dec), bc(k_dec), cdec


def _retention_kernel(q_ref, k_ref, v_ref, g_ref, gnw_ref, s0_ref, dm_ref, qd_ref, kd_ref, cd_ref,
                      o_ref, s_out_ref, state_ref, *, rows):
    c = pl.program_id(1)
    nc = pl.num_programs(1)

    @pl.when(c == 0)
    def _():
        state_ref[...] = s0_ref[0]

    def padded(ref):
        v = ref[...]
        if rows < RET_CHUNK:
            v = jnp.concatenate([v, jnp.zeros((RET_CHUNK - rows, v.shape[1]), v.dtype)], axis=0)
        return v

    q, k, v, g = padded(q_ref), padded(k_ref), padded(v_ref), g_ref[...]
    gnw = gnw_ref[...]
    outs = []
    for h in range(RET_HEADS):
        sl = slice(h * RET_HEAD_DIM, (h + 1) * RET_HEAD_DIM)
        qh, kh, vh = q[:, sl], k[:, sl], v[:, sl]
        st = state_ref[h]
        att = _dot_nt(qh, kh) * dm_ref[h]
        o = _dot(att, vh) + _dot(qh, st) * qd_ref[h]
        state_ref[h] = st * cd_ref[h] + _dot((kh * kd_ref[h]).T, vh)
        o = o[:rows]
        mu = jnp.mean(o, axis=-1, keepdims=True)
        var = jnp.mean(jnp.square(o - mu), axis=-1, keepdims=True)
        outs.append((o - mu) * lax.rsqrt(var + GN_EPS) * gnw[:, sl])
    on = jnp.concatenate(outs, axis=1)
    o_ref[...] = g * (1.0 / (1.0 + jnp.exp(-g))) * on

    @pl.when(c == nc - 1)
    def _():
        s_out_ref[0] = state_ref[...]


def _retention(q, k, v, g, gn_w, s0, consts, nb, nc, rows, row0=0):
    blk0 = row0 // rows
    in_row = pl.BlockSpec((rows, RET_WIDTH), lambda b, c: (blk0 + b * nc + c, 0))
    out_row = pl.BlockSpec((rows, RET_WIDTH), lambda b, c: (b * nc + c, 0))
    st_spec = pl.BlockSpec((1, RET_HEADS, RET_HEAD_DIM, RET_HEAD_DIM), lambda b, c: (b, 0, 0, 0))
    const_spec = pl.BlockSpec((RET_HEADS, RET_CHUNK, RET_CHUNK), lambda b, c: (0, 0, 0))
    return pl.pallas_call(
        functools.partial(_retention_kernel, rows=rows),
        grid=(nb, nc),
        in_specs=[in_row, in_row, in_row, in_row,
                  pl.BlockSpec((1, RET_WIDTH), lambda b, c: (0, 0)),
                  st_spec, const_spec, const_spec, const_spec, const_spec],
        out_specs=[out_row, st_spec],
        out_shape=[jax.ShapeDtypeStruct((nb * nc * rows, RET_WIDTH), F32),
                   jax.ShapeDtypeStruct(s0.shape, F32)],
        scratch_shapes=[pltpu.VMEM((RET_HEADS, RET_HEAD_DIM, RET_HEAD_DIM), F32)],
        name="retention",
        compiler_params=_cparams(("arbitrary", "arbitrary")),
    )(q, k, v, g, gn_w.reshape(1, RET_WIDTH), s0, *consts)


_BISECT_PLAIN_ITERS = 26
_BISECT_MAX_ITERS = 400


def _lane_tiles(x):
    return [x[:, j * LANES:(j + 1) * LANES] for j in range(x.shape[1] // LANES)]


def _rowsum_b(x):
    return jnp.broadcast_to(jnp.sum(x, axis=1, keepdims=True), x.shape)


def _select_chunk(x, lo, hi, need, tied_any, tied, prefix0):
    R, CK = x.shape
    rep = CK // LANES
    wide = lambda a: jnp.concatenate([a] * rep, axis=1)

    def plain(_):
        return jnp.where(x >= wide(lo), 1.0, 0.0), prefix0

    def with_ties(_):
        low, hiw = wide(lo), wide(hi)
        band = (x >= low) & (x < hiw)
        bandf = jnp.where(band, 1.0, 0.0)
        r_i = lax.broadcasted_iota(jnp.int32, (CK, CK), 0)
        c_i = lax.broadcasted_iota(jnp.int32, (CK, CK), 1)
        tri = jnp.where(r_i <= c_i, 1.0, 0.0).astype(BF16)
        rank = jnp.dot(bandf.astype(BF16), tri, preferred_element_type=F32) + wide(prefix0)
        take_tie = jnp.where(rank <= wide(need), bandf, 0.0)
        sel_tied = jnp.where(x >= hiw, 1.0, take_tie)
        sel = jnp.where(wide(tied) > 0.0, sel_tied, jnp.where(x >= low, 1.0, 0.0))
        return sel, prefix0 + _rowsum_b(sum(_lane_tiles(bandf)))

    return lax.cond(tied_any, with_ties, plain, 0)


_QB = 128
_CK = 512


def _col_partial(x, op):
    n_chain = 4
    g = x.shape[0] // (SUBLANES * n_chain)
    x4 = x.reshape(n_chain, g, SUBLANES, x.shape[1])
    c = [op(x4[k], axis=0) for k in range(n_chain)]
    pair = jnp.stack([c[0], c[1]]), jnp.stack([c[2], c[3]])
    return op(jnp.stack([op(pair[0], axis=0), op(pair[1], axis=0)]), axis=0)


def _col_fold(x, op):
    return op(_col_partial(x, op), axis=0, keepdims=True)


def _bisect(count_pass, kk, lo0, hi0, n_causal, active):
    def not_done(c_lo, lo, hib):
        pending = active & (c_lo != kk) & (lo != hib)
        return jnp.max(jnp.where(pending, 1.0, 0.0)) > 0.0

    def make_iter(snap, reps):
        def one(state):
            lo, hi, hib, c_lo, c_hi = state
            if snap:
                mid = 0.5 * (lo + jnp.minimum(hi, hib))
                mid = jnp.where(mid > lo, mid, jnp.minimum(hi, hib))
            else:
                mid = 0.5 * (lo + hi)
            c, amin, bmax = count_pass(mid, snap)
            ge = c >= kk
            if snap:
                lo = jnp.where(ge, amin, lo)
                hib = jnp.where(ge, hib, bmax)
            else:
                lo = jnp.where(ge, mid, lo)
            hi = jnp.where(ge, hi, mid)
            return lo, hi, hib, jnp.where(ge, c, c_lo), jnp.where(ge, c_hi, c)

        def step(carry):
            it, state = carry[0], carry[1:]
            for _ in range(reps):
                state = one(state)
            return (it + reps, *state)
        return step

    init = (jnp.int32(0), lo0, hi0, jnp.full(lo0.shape, jnp.inf, F32), n_causal, jnp.zeros(lo0.shape, F32))
    carry = lax.while_loop(
        lambda c: (c[0] < _BISECT_PLAIN_ITERS) & not_done(c[4], c[1], c[3]), make_iter(False, 2), init)
    carry = lax.while_loop(
        lambda c: (c[0] < _BISECT_MAX_ITERS) & not_done(c[4], c[1], c[3]), make_iter(True, 1), carry)
    _, lo, hi, hib, c_lo, c_hi = carry
    return lo, hi, c_hi, active & (c_lo != kk)


def _threshold_search_t(sc_ref, nck, kk, lo0, hi0, n_causal):
    Q = lo0.shape[1]

    def count_pass(mid, snap):
        def body(kc, carry):
            x = sc_ref[kc]
            cnt, amin, bmax = carry
            ge = x >= mid
            cnt = cnt + _col_partial(jnp.where(ge, 1.0, 0.0), jnp.sum)
            if snap:
                amin = jnp.minimum(amin, _col_partial(jnp.where(ge, x, jnp.inf), jnp.min))
                bmax = jnp.maximum(bmax, _col_partial(jnp.where(ge, -jnp.inf, x), jnp.max))
            return cnt, amin, bmax
        init = (jnp.zeros((SUBLANES, Q), F32), jnp.full((SUBLANES, Q), jnp.inf, F32),
                jnp.full((SUBLANES, Q), -jnp.inf, F32))
        cnt, amin, bmax = lax.fori_loop(0, nck, body, init)
        return (jnp.sum(cnt, axis=0, keepdims=True), jnp.min(amin, axis=0, keepdims=True),
                jnp.max(bmax, axis=0, keepdims=True))

    return _bisect(count_pass, kk, lo0, hi0, n_causal, lo0 == lo0)


def _threshold_search_rows(sc_ref, kk, lo0, hi0, n_causal, active):
    def count_pass(mid, snap):
        x = sc_ref[...]
        ge = x >= mid
        cnt = jnp.sum(jnp.where(ge, 1.0, 0.0), axis=1, keepdims=True)
        if not snap:
            return cnt, None, None
        return (cnt, jnp.min(jnp.where(ge, x, jnp.inf), axis=1, keepdims=True),
                jnp.max(jnp.where(ge, -jnp.inf, x), axis=1, keepdims=True))

    return _bisect(count_pass, kk, lo0, hi0, n_causal, active)


def _select_chunk_t(x, lo, hi, need, tied_any, tiedf, prefix0):
    ck = x.shape[0]

    def plain(_):
        return jnp.where(x >= lo, 1.0, 0.0), prefix0

    def with_ties(_):
        bandf = jnp.where((x >= lo) & (x < hi), 1.0, 0.0)
        r_i = lax.broadcasted_iota(jnp.int32, (ck, ck), 0)
        c_i = lax.broadcasted_iota(jnp.int32, (ck, ck), 1)
        tri = jnp.where(c_i <= r_i, 1.0, 0.0).astype(BF16)
        rank = jnp.dot(tri, bandf.astype(BF16), preferred_element_type=F32) + prefix0
        take_tie = jnp.where(rank <= need, bandf, 0.0)
        sel_tied = jnp.where(x >= hi, 1.0, take_tie)
        sel = jnp.where(tiedf > 0.0, sel_tied, jnp.where(x >= lo, 1.0, 0.0))
        return sel, prefix0 + _col_fold(bandf, jnp.sum)

    return lax.cond(tied_any, with_ties, plain, 0)


def _prompt_dsa_kernel(qi_ref, kiwiq_ref, kiwi_ref, qa_ref, ka_ref, va_ref, o_ref,
                         sc_ref, vt_ref, acc_ref, *, topk, seq):
    i = pl.program_id(1)
    t0 = i * _QB
    nck = (t0 + _QB + _CK - 1) // _CK
    Q = _QB
    HQ = ATT_HEADS * Q

    @pl.when(i == 0)
    def _():
        for j in range(seq // _CK):
            vt_ref[:, j * _CK:(j + 1) * _CK] = va_ref[j * _CK:(j + 1) * _CK, :].T

    qpos = t0 + lax.broadcasted_iota(jnp.int32, (1, Q), 1)
    key_row = lax.broadcasted_iota(jnp.int32, (_CK, Q), 0)

    qi = qi_ref[...]
    w_t = kiwiq_ref[...].T
    q_heads, w_rows = [], []
    for h in range(IDX_HEADS):
        qh = qi[:, h * IDX_DIM:(h + 1) * IDX_DIM] * (IDX_DIM ** -0.5)
        q_heads.append(jnp.concatenate([qh, jnp.zeros((Q, LANES - IDX_DIM), F32)], axis=1).astype(BF16))
        w_rows.append(w_t[IDX_DIM + h:IDX_DIM + h + 1, :])

    def score_body(kc, carry):
        mn, mx = carry
        k0 = pl.multiple_of(kc * _CK, _CK)
        kch = kiwi_ref[pl.ds(k0, _CK), :].astype(BF16)
        acc = jnp.zeros((_CK, Q), F32)
        for h in range(IDX_HEADS):
            acc = acc + w_rows[h] * jnp.maximum(_dot_nt(kch, q_heads[h]), 0.0)
        causal = (k0 + key_row) <= qpos
        sc_ref[kc] = jnp.where(causal, acc, -jnp.inf)
        mn = jnp.minimum(mn, _col_fold(jnp.where(causal, acc, jnp.inf), jnp.min))
        mx = jnp.maximum(mx, _col_fold(jnp.where(causal, acc, -jnp.inf), jnp.max))
        return mn, mx

    mn, mx = lax.fori_loop(0, nck, score_body,
                           (jnp.full((1, Q), jnp.inf, F32), jnp.full((1, Q), -jnp.inf, F32)))
    n_causal = (qpos + 1).astype(F32)
    kk = jnp.minimum(n_causal, float(topk))
    lo, hi, c_hi, tied = _threshold_search_t(sc_ref, nck, kk, mn, mx + (mx - mn) + 1.0, n_causal)
    need = kk - c_hi
    tiedf = jnp.where(tied, 1.0, 0.0)
    tied_any = jnp.max(tiedf) > 0.0

    qa = qa_ref[...]
    q_rows = []
    for h in range(ATT_HEADS):
        c = h // ATT_GROUP
        parts = [jnp.zeros((Q, ATT_HEAD_DIM), F32)] * KV_HEADS
        parts[c] = qa[:, h * ATT_HEAD_DIM:(h + 1) * ATT_HEAD_DIM] * (ATT_HEAD_DIM ** -0.5)
        q_rows.append(jnp.concatenate(parts, axis=1))
    q_all = jnp.concatenate(q_rows, axis=0).astype(BF16)
    acc_ref[...] = jnp.zeros(acc_ref.shape, F32)

    def att_body(kc, carry):
        m_old, l_old, prefix = carry
        k0 = pl.multiple_of(kc * _CK, _CK)
        kch = ka_ref[pl.ds(k0, _CK), :].astype(BF16)
        vch_t = vt_ref[:, pl.ds(k0, _CK)].astype(BF16)
        sel, prefix = _select_chunk_t(sc_ref[kc], lo, hi, need, tied_any, tiedf, prefix)
        selb = jnp.concatenate([sel] * ATT_HEADS, axis=1) > 0.0
        s = jnp.where(selb, _dot_nt(kch, q_all), NEG_BIG)
        m_new = jnp.maximum(m_old, _col_fold(s, jnp.max))
        alpha = jnp.exp(m_old - m_new)
        p = jnp.exp(s - m_new)
        l_new = l_old * alpha + _col_fold(p, jnp.sum)
        acc_ref[...] = acc_ref[...] * alpha + jnp.dot(vch_t, p.astype(BF16), preferred_element_type=F32)
        return m_new, l_new, prefix

    init = (jnp.full((1, HQ), NEG_BIG, F32), jnp.zeros((1, HQ), F32), jnp.zeros((1, Q), F32))
    _, l_fin, _ = lax.fori_loop(0, nck, att_body, init)
    o_t = acc_ref[...] / l_fin
    outs = []
    for h in range(ATT_HEADS):
        c = h // ATT_GROUP
        outs.append(o_t[:, h * Q:(h + 1) * Q].T[:, c * ATT_HEAD_DIM:(c + 1) * ATT_HEAD_DIM])
    o_ref[...] = jnp.concatenate(outs, axis=1)


def _prompt_dsa(qi, kiwi, qa, ka, va, B, S):
    nq = S // _QB
    topk = min(TOPK_MAX, S // 4)
    qblk = lambda w: pl.BlockSpec((_QB, w), lambda b, i: (b * nq + i, 0))
    allk = lambda w: pl.BlockSpec((S, w), lambda b, i: (b, 0))
    return pl.pallas_call(
        functools.partial(_prompt_dsa_kernel, topk=topk, seq=S),
        grid=(B, nq),
        in_specs=[qblk(IDX_WIDTH), qblk(_KIWI_WIDTH), allk(_KIWI_WIDTH), qblk(ATT_WIDTH),
                  allk(KV_WIDTH), allk(KV_WIDTH)],
        out_specs=qblk(ATT_WIDTH),
        out_shape=jax.ShapeDtypeStruct((B * S, ATT_WIDTH), F32),
        scratch_shapes=[pltpu.VMEM((S // _CK, _CK, _QB), F32),
                        pltpu.VMEM((KV_WIDTH, S), F32),
                        pltpu.VMEM((KV_WIDTH, ATT_HEADS * _QB), F32)],
        name="prompt_dsa",
        compiler_params=_cparams(("arbitrary", "arbitrary")),
    )(qi, kiwi, kiwi, qa, ka, va)


SROWS = SUBLANES
_CKS = 640


def _sample_dsa_kernel(pt_ref, qi_ref, kiwi_ref, qa_ref, ka_ref, va_ref, cik_hbm, ck_hbm, cv_hbm,
                       o_ref, ikbuf, kbuf, vbuf, sems, sc_ref, sel_ref, *, n_pages, t_real, topk):
    b = pl.program_id(0)
    nb = pl.num_programs(0)
    slot = b % 2
    past_len = n_pages * PAGE_SIZE
    L = past_len + PAGE_SIZE
    nck = L // _CKS
    R = SROWS

    def page_copies(bb, s, p):
        phys = pt_ref[bb, p]
        cols = pl.ds(pl.multiple_of(p * PAGE_SIZE, PAGE_SIZE), PAGE_SIZE)
        return (pltpu.make_async_copy(cik_hbm.at[phys], ikbuf.at[s, :, cols], sems.at[s, 0]),
                pltpu.make_async_copy(ck_hbm.at[phys], kbuf.at[s, :, cols], sems.at[s, 1]),
                pltpu.make_async_copy(cv_hbm.at[phys], vbuf.at[s, :, cols], sems.at[s, 2]))

    def start_fetch(bb, s):
        def body(p, _):
            for cp in page_copies(bb, s, p):
                cp.start()
            return 0
        lax.fori_loop(0, n_pages, body, 0)

    def wait_fetch(bb, s):
        def body(p, _):
            for cp in page_copies(bb, s, p):
                cp.wait()
            return 0
        lax.fori_loop(0, n_pages, body, 0)

    @pl.when(b == 0)
    def _():
        tail = pl.ds(past_len, PAGE_SIZE)
        for s in range(2):
            ikbuf[s, :, tail] = jnp.zeros((IDX_DIM, PAGE_SIZE), F32)
            kbuf[s, :, tail] = jnp.zeros((KV_WIDTH, PAGE_SIZE), F32)
            vbuf[s, :, tail] = jnp.zeros((KV_WIDTH, PAGE_SIZE), F32)
        start_fetch(0, 0)

    @pl.when(b + 1 < nb)
    def _():
        start_fetch(b + 1, 1 - slot)

    kiwi = kiwi_ref[...]

    def as_columns(rows):
        padded = jnp.concatenate([rows, jnp.zeros((LANES - R, LANES), F32)], axis=0)
        return padded.T[:, :R]

    new_cols = pl.ds(past_len, R)
    ikbuf[slot, :, new_cols] = as_columns(kiwi)[:IDX_DIM]
    kbuf[slot, :, new_cols] = as_columns(ka_ref[...])
    vbuf[slot, :, new_cols] = as_columns(va_ref[...])
    wait_fetch(b, slot)

    row = lax.broadcasted_iota(jnp.int32, (R, 1), 0)
    key_pos = lax.broadcasted_iota(jnp.int32, (R, L), 1)

    qi = qi_ref[...]
    q_all = jnp.concatenate(
        [qi[:, h * IDX_DIM:(h + 1) * IDX_DIM] * (IDX_DIM ** -0.5) for h in range(IDX_HEADS)], axis=0).astype(BF16)
    d = jnp.dot(q_all, ikbuf[slot].astype(BF16), preferred_element_type=F32)
    acc = jnp.zeros((R, L), F32)
    for h in range(IDX_HEADS):
        acc = acc + kiwi[:, IDX_DIM + h:IDX_DIM + h + 1] * jnp.maximum(d[h * R:(h + 1) * R], 0.0)
    causal = key_pos <= past_len + row
    sc_ref[...] = jnp.where(causal, acc, -jnp.inf)
    mn = jnp.min(jnp.where(causal, acc, jnp.inf), axis=1, keepdims=True)
    mx = jnp.max(jnp.where(causal, acc, -jnp.inf), axis=1, keepdims=True)
    n_causal = (past_len + row + 1).astype(F32)
    kk = jnp.minimum(n_causal, float(topk))
    lo, hi, c_hi, tied = _threshold_search_rows(sc_ref, kk, mn, mx + (mx - mn) + 1.0, n_causal, row < t_real)
    tiedf = jnp.where(tied, 1.0, 0.0)
    tied_any = jnp.max(tiedf) > 0.0
    sel_ref[...] = jnp.where(sc_ref[...] >= lo, 1.0, 0.0)

    @pl.when(tied_any)
    def _():
        wide = lambda a: jnp.broadcast_to(a, (R, LANES))

        def tie_body(kc, prefix):
            cols = pl.ds(pl.multiple_of(kc * _CKS, LANES), _CKS)
            sel, prefix = _select_chunk(sc_ref[:, cols], wide(lo), wide(hi), wide(kk - c_hi), tied_any,
                                        wide(tiedf), prefix)
            sel_ref[:, cols] = sel
            return prefix

        lax.fori_loop(0, nck, tie_body, jnp.zeros((R, LANES), F32))

    qa = qa_ref[...]
    q_rows = []
    for h in range(ATT_HEADS):
        c = h // ATT_GROUP
        z = jnp.zeros((R, ATT_HEAD_DIM), F32)
        parts = [z] * KV_HEADS
        parts[c] = qa[:, h * ATT_HEAD_DIM:(h + 1) * ATT_HEAD_DIM] * (ATT_HEAD_DIM ** -0.5)
        q_rows.append(jnp.concatenate(parts, axis=1))
    q_big = jnp.concatenate(q_rows, axis=0).astype(BF16)
    selb = jnp.concatenate([sel_ref[...]] * ATT_HEADS, axis=0) > 0.0
    s = jnp.where(selb, jnp.dot(q_big, kbuf[slot].astype(BF16), preferred_element_type=F32), NEG_BIG)
    p = jnp.exp(s - jnp.max(s, axis=1, keepdims=True))
    o = _dot_nt(p, vbuf[slot]) / jnp.sum(p, axis=1, keepdims=True)
    outs = []
    for h in range(ATT_HEADS):
        c = h // ATT_GROUP
        outs.append(o[h * R:(h + 1) * R, c * ATT_HEAD_DIM:(c + 1) * ATT_HEAD_DIM])
    o_ref[...] = jnp.concatenate(outs, axis=1)


def _sample_dsa(page_table, qi, kiwi, qa, ka, va, cache_idx_k_l, cache_k_l, cache_v_l, t_real, row0):
    Bd, n_pages = page_table.shape
    past_len = n_pages * PAGE_SIZE
    L = past_len + PAGE_SIZE
    assert L % _CKS == 0 and row0 % SROWS == 0
    topk = min(TOPK_MAX, (past_len + t_real) // 4)
    blk0 = row0 // SROWS
    blk = lambda w: pl.BlockSpec((SROWS, w), lambda b, pt: (blk0 + b, 0))
    any_spec = pl.BlockSpec(memory_space=pl.ANY)
    grid_spec = pltpu.PrefetchScalarGridSpec(
        num_scalar_prefetch=1,
        grid=(Bd,),
        in_specs=[blk(IDX_WIDTH), blk(_KIWI_WIDTH), blk(ATT_WIDTH), blk(KV_WIDTH), blk(KV_WIDTH),
                  any_spec, any_spec, any_spec],
        out_specs=pl.BlockSpec((SROWS, ATT_WIDTH), lambda b, pt: (b, 0)),
        scratch_shapes=[pltpu.VMEM((2, IDX_DIM, L), F32),
                        pltpu.VMEM((2, KV_WIDTH, L), F32),
                        pltpu.VMEM((2, KV_WIDTH, L), F32),
                        pltpu.SemaphoreType.DMA((2, 3)),
                        pltpu.VMEM((SROWS, L), F32),
                        pltpu.VMEM((SROWS, L), F32)],
    )
    return pl.pallas_call(
        functools.partial(_sample_dsa_kernel, n_pages=n_pages, t_real=t_real, topk=topk),
        grid_spec=grid_spec,
        out_shape=jax.ShapeDtypeStruct((Bd * SROWS, ATT_WIDTH), F32),
        name="sample_dsa",
        compiler_params=_cparams(("arbitrary",)),
    )(page_table, qi, kiwi, qa, ka, va, cache_idx_k_l, cache_k_l, cache_v_l)


_TMP = 256
_PAIR_GROUP = 16
PEER_HALF_EXPERTS = PEER_EXPERTS // 2


def _extract_topk(s, pos, n, k, payload=None):
    vals, idxs = [], []
    for _ in range(k):
        m = jnp.max(s, axis=0, keepdims=True)
        p = jnp.min(jnp.where(s == m, pos, n), axis=0, keepdims=True)
        hit = pos == p
        vals.append(m)
        if payload is None:
            idxs.append(p)
        else:
            idxs.append(jnp.max(jnp.where(hit, payload, -1), axis=0, keepdims=True))
        s = jnp.where(hit, -jnp.inf, s)
    return vals, idxs


def _post_mix_kernel(x_ref, attp_ref, atts_ref, retp_ref, rets_ref, wo_ref, nw_ref, wqt_ref, sk1_ref, sk2_ref,
                     h_ref, xn_ref, idx_ref, gate_ref, n0_ref, qt_ref, et_ref, gt_ref, la_ref, lg_ref,
                     *, n_prompt_tiles):
    T = x_ref.shape[0]
    is_prompt = pl.program_id(0) < n_prompt_tiles
    att = jnp.where(is_prompt, attp_ref[...], atts_ref[...])
    ret = jnp.where(is_prompt, retp_ref[...], rets_ref[...])
    h = x_ref[...] + _dot(att, wo_ref[:ATT_WIDTH, :]) + _dot(ret, wo_ref[ATT_WIDTH:, :])
    h_ref[...] = h
    ms = jnp.mean(h * h, axis=-1, keepdims=True)
    xn = h * lax.rsqrt(ms + NORM_EPS) * nw_ref[...]
    for c in range(D_MODEL // LANES):
        xn_ref[pl.ds(c, T, stride=SUBLANES), :] = xn[:, c * LANES:(c + 1) * LANES]
    qt_ref[...] = _dot_nt(wqt_ref[...], xn)

    pos_k = lax.broadcasted_iota(jnp.int32, (PEER_NKEYS, T), 0)
    n_b = [PEER_TOPK // (a + 1) for a in range(PEER_TOPK)]
    n_cand = -(-sum(n_b) // SUBLANES) * SUBLANES
    pos_c = lax.broadcasted_iota(jnp.int32, (n_cand, T), 0)

    def head_body(hd, _):
        q0 = pl.multiple_of(hd * PEER_KEY_DIM, PEER_KEY_DIM)
        q1 = qt_ref[pl.ds(q0, PEER_HALF), :]
        q2 = qt_ref[pl.ds(q0 + PEER_HALF, PEER_HALF), :]
        s1 = _dot(sk1_ref[...], q1)
        s2 = _dot(sk2_ref[...], q2)
        v1, i1 = _extract_topk(s1, pos_k, PEER_NKEYS, PEER_TOPK)
        v2, i2 = _extract_topk(s2, pos_k, PEER_NKEYS, PEER_TOPK)
        v2m = jnp.concatenate(v2, axis=0)
        i2m = jnp.concatenate(i2, axis=0)
        n_fill = n_cand - sum(n_b)
        cand = jnp.concatenate([v1[a] + v2m[:n_b[a]] for a in range(PEER_TOPK)]
                               + [jnp.full((n_fill, T), -jnp.inf, F32)], axis=0)
        cid = jnp.concatenate([i1[a] * PEER_NKEYS + i2m[:n_b[a]] for a in range(PEER_TOPK)]
                              + [jnp.full((n_fill, T), -1, jnp.int32)], axis=0)
        sv, eid = _extract_topk(cand, pos_c, n_cand, PEER_TOPK, payload=cid)
        svm = jnp.concatenate(sv, axis=0)
        g = jnp.exp(svm - sv[0])
        r0 = pl.multiple_of(hd * PEER_TOPK, PEER_TOPK)
        gt_ref[pl.ds(r0, PEER_TOPK), :] = g / jnp.sum(g, axis=0, keepdims=True)
        et_ref[pl.ds(r0, PEER_TOPK), :] = jnp.concatenate(eid, axis=0).astype(F32)
        return 0

    lax.fori_loop(0, PEER_HEADS, head_body, 0)

    e = et_ref[...]
    g = gt_ref[...]
    npair = e.shape[0]
    is0 = e < float(PEER_HALF_EXPERTS)
    r_i = lax.broadcasted_iota(jnp.int32, (npair, npair), 0)
    c_i = lax.broadcasted_iota(jnp.int32, (npair, npair), 1)
    tri = jnp.where(c_i <= r_i, 1.0, 0.0).astype(BF16)
    rank0 = jnp.dot(tri, jnp.where(is0, 1.0, 0.0).astype(BF16), preferred_element_type=F32)
    rowf = lax.broadcasted_iota(jnp.int32, (npair, T), 0).astype(F32)
    n0 = rank0[npair - 1:npair, :]
    place = jnp.where(is0, rank0 - 1.0, n0 + (rowf - rank0))
    off = jnp.where(is0, e, e - float(PEER_HALF_EXPERTS)) * float(SUBLANES)
    for p in range(npair):
        m = place == float(p)
        la_ref[p:p + 1, :] = jnp.sum(jnp.where(m, off, 0.0), axis=0, keepdims=True)
        lg_ref[p:p + 1, :] = jnp.sum(jnp.where(m, g, 0.0), axis=0, keepdims=True)
    idx_ref[...] = la_ref[...].T.astype(jnp.int32)
    gate_ref[...] = lg_ref[...].T
    n0_ref[...] = n0.astype(jnp.int32)


def _post_mix(x, att_p, att_s, ret_p, ret_s, w_out_bf, norm_w, wq_t_bf, sk1_bf, sk2_bf):
    N = x.shape[0]
    nt = N // _TMP
    ntp = att_p.shape[0] // _TMP
    nts = att_s.shape[0] // _TMP
    assert ntp + nts == nt and ntp > 0 and nts > 0
    npair = PEER_HEADS * PEER_TOPK
    row = lambda w: pl.BlockSpec((_TMP, w), lambda i: (i, 0))
    prow = lambda w: pl.BlockSpec((_TMP, w), lambda i: (jnp.minimum(i, ntp - 1), 0))
    srow = lambda w: pl.BlockSpec((_TMP, w), lambda i: (jnp.maximum(i - ntp, 0), 0))
    full = lambda a: pl.BlockSpec(a.shape, lambda i: (0,) * a.ndim)
    nw = norm_w.reshape(1, D_MODEL)
    return pl.pallas_call(
        functools.partial(_post_mix_kernel, n_prompt_tiles=ntp),
        grid=(nt,),
        in_specs=[row(D_MODEL), prow(ATT_WIDTH), srow(ATT_WIDTH), prow(RET_WIDTH), srow(RET_WIDTH),
                  full(w_out_bf), full(nw), full(wq_t_bf), full(sk1_bf), full(sk2_bf)],
        out_specs=[row(D_MODEL), pl.BlockSpec((_TMP * SUBLANES, LANES), lambda i: (i, 0)),
                   row(npair), row(npair), pl.BlockSpec((1, _TMP), lambda i: (0, i))],
        out_shape=[jax.ShapeDtypeStruct((N, D_MODEL), F32), jax.ShapeDtypeStruct((N * SUBLANES, LANES), F32),
                   jax.ShapeDtypeStruct((N, npair), jnp.int32), jax.ShapeDtypeStruct((N, npair), F32),
                   jax.ShapeDtypeStruct((1, N), jnp.int32)],
        scratch_shapes=[pltpu.VMEM((PEER_HEADS * PEER_KEY_DIM, _TMP), F32),
                        pltpu.VMEM((npair, _TMP), F32), pltpu.VMEM((npair, _TMP), F32),
                        pltpu.VMEM((npair, _TMP), F32), pltpu.VMEM((npair, _TMP), F32)],
        name="post_mix",
        compiler_params=_cparams(("arbitrary",)),
    )(x, att_p, att_s, ret_p, ret_s, w_out_bf, nw, wq_t_bf, sk1_bf, sk2_bf)


_TBP = 128
_NPAIR = PEER_HEADS * PEER_TOPK
_ROW_TILE = D_MODEL // LANES
assert _ROW_TILE == SUBLANES
_HIGH_HALF = -65536


def _bf16_bits(x):
    return lax.bitcast_convert_type(x.astype(BF16).astype(F32), jnp.int32)


def _pack_table_kernel(lo_ref, hi_ref, o_ref):
    n = lo_ref.shape[0]
    words = (lax.shift_right_logical(_bf16_bits(lo_ref[...]), 16)
             | (_bf16_bits(hi_ref[...]) & _HIGH_HALF))
    for c in range(_ROW_TILE):
        o_ref[pl.ds(c, n, stride=_ROW_TILE), :] = words[:, c * LANES:(c + 1) * LANES]


def _pack_table(tbl):
    blk = 512
    nb = PEER_HALF_EXPERTS // blk
    return pl.pallas_call(
        _pack_table_kernel,
        grid=(nb,),
        in_specs=[pl.BlockSpec((blk, D_MODEL), lambda i: (i, 0)),
                  pl.BlockSpec((blk, D_MODEL), lambda i: (i + nb, 0))],
        out_specs=pl.BlockSpec((blk * _ROW_TILE, LANES), lambda i: (i, 0)),
        out_shape=jax.ShapeDtypeStruct((PEER_HALF_EXPERTS * _ROW_TILE, LANES), jnp.int32),
        name="pack_table",
        compiler_params=_cparams(("arbitrary",)),
    )(tbl, tbl)


def _unpack(words, half):
    bits = lax.shift_left(words, 16) if half == 0 else words & _HIGH_HALF
    return lax.bitcast_convert_type(bits, F32)


def _load_table(tbl_hbm, tbuf, sem):
    cp = pltpu.make_async_copy(tbl_hbm, tbuf, sem)
    cp.start()
    cp.wait()


def _fold_pair(a, b, k, sub):
    m = (sub & k) == 0
    return jnp.where(m, a, pltpu.roll(b, k, 0)) + jnp.where(m, pltpu.roll(a, SUBLANES - k, 0), b)


def _fold8(p, sub):
    a, b, c, d, e, f, g, h = p[0], p[4], p[2], p[6], p[1], p[5], p[3], p[7]
    t1, t2, t3, t4 = (_fold_pair(a, b, 4, sub), _fold_pair(c, d, 4, sub),
                      _fold_pair(e, f, 4, sub), _fold_pair(g, h, 4, sub))
    u1, u2 = _fold_pair(t1, t2, 2, sub), _fold_pair(t3, t4, 2, sub)
    return _fold_pair(u1, u2, 1, sub)


def _flat_smem_spec(nt, per_token):
    return pl.BlockSpec((1, 1, _TBP * per_token), lambda i: (i, 0, 0), memory_space=pltpu.SMEM)


def _flat_blocks(a, nt):
    return a.reshape(nt, 1, -1)


_N_GROUPS = _NPAIR // _PAIR_GROUP


def _group_start(t, g):
    g = jnp.minimum(g, _N_GROUPS - 1)
    return pl.multiple_of(t * _NPAIR + g * _PAIR_GROUP, _PAIR_GROUP)


def _table_row(tbuf, idx_ref, k):
    off = pl.multiple_of(idx_ref[0, 0, k], _ROW_TILE)
    return tbuf[pl.ds(off, _ROW_TILE), :]


def _token_segments(n0):
    g_mid = lax.shift_right_logical(n0, _PAIR_GROUP.bit_length() - 1)
    rem = n0 & (_PAIR_GROUP - 1)
    g_hi = g_mid + jnp.where(rem != 0, 1, 0)
    return g_mid, rem, g_hi


def _mixed_unpack(words, s, rem):
    return jnp.where(s < rem, _unpack(words, 0), _unpack(words, 1))


def _peer_u_kernel(idx_ref, n0_ref, x_ref, u_hbm, o_ref, tbuf, sem, z_ref):
    @pl.when(pl.program_id(0) == 0)
    def _():
        _load_table(u_hbm, tbuf, sem)

    sub = lax.broadcasted_iota(jnp.int32, (SUBLANES, LANES), 0)

    def tok_body(t, _):
        xt = x_ref[t]
        g_mid, rem, g_hi = _token_segments(n0_ref[0, 0, t])

        def load_words(g):
            kg = _group_start(t, g)
            return tuple(_table_row(tbuf, idx_ref, kg + s) for s in range(_PAIR_GROUP))

        def fold_group(g, vals):
            kg = _group_start(t, g)
            prods = [v * xt for v in vals]
            for q in range(_PAIR_GROUP // SUBLANES):
                z_ref[pl.ds(kg + q * SUBLANES, SUBLANES), :] = _fold8(prods[q * SUBLANES:(q + 1) * SUBLANES], sub)

        def uniform_groups(ga, gb, half):
            def grp_body(g, words):
                nxt = load_words(g + 1)
                fold_group(g, [_unpack(w, half) for w in words])
                return nxt
            lax.fori_loop(ga, gb, grp_body, load_words(ga))

        uniform_groups(0, g_mid, 0)

        @pl.when(rem != 0)
        def _():
            fold_group(g_mid, [_mixed_unpack(w, s, rem) for s, w in enumerate(load_words(g_mid))])

        uniform_groups(g_hi, _N_GROUPS, 1)
        return 0

    lax.fori_loop(0, _TBP, tok_body, 0)

    ones = jnp.ones((SUBLANES, LANES), BF16)

    def chunk_body(c, _):
        r0 = pl.multiple_of(c * SUBLANES * _NPAIR, SUBLANES * _NPAIR)
        z = z_ref[pl.ds(r0, SUBLANES * _NPAIR), :]
        zh = z.astype(BF16)
        zl = (z - zh.astype(F32)).astype(BF16)
        hv = _dot_nt(ones, zh) + _dot_nt(ones, zl)
        rows = [hv[0:1, tt * _NPAIR:(tt + 1) * _NPAIR] for tt in range(SUBLANES)]
        o_ref[pl.ds(pl.multiple_of(c * SUBLANES, SUBLANES), SUBLANES), :] = jnp.concatenate(rows, axis=0)
        return 0

    lax.fori_loop(0, _TBP // SUBLANES, chunk_body, 0)


def _peer_u(idx, n0, xn_tiles, u_packed):
    N = xn_tiles.shape[0] // _ROW_TILE
    nt = N // _TBP
    return pl.pallas_call(
        _peer_u_kernel,
        grid=(nt,),
        in_specs=[_flat_smem_spec(nt, _NPAIR), _flat_smem_spec(nt, 1),
                  pl.BlockSpec((_TBP, _ROW_TILE, LANES), lambda i: (i, 0, 0)),
                  pl.BlockSpec(memory_space=pl.ANY)],
        out_specs=pl.BlockSpec((_TBP, _NPAIR), lambda i: (i, 0)),
        out_shape=jax.ShapeDtypeStruct((N, _NPAIR), F32),
        scratch_shapes=[pltpu.VMEM((PEER_HALF_EXPERTS * _ROW_TILE, LANES), jnp.int32),
                        pltpu.SemaphoreType.DMA(()),
                        pltpu.VMEM((_TBP * _NPAIR, LANES), F32)],
        name="peer_u",
        compiler_params=_cparams(("arbitrary",)),
    )(_flat_blocks(idx, nt), _flat_blocks(n0, nt), xn_tiles.reshape(N, _ROW_TILE, LANES), u_packed)


def _peer_coef_kernel(hv_ref, gate_ref, o_ref):
    hv = hv_ref[...]
    o_ref[...] = gate_ref[...] * (0.5 * hv * (1.0 + lax.erf(hv * (0.5 ** 0.5))))


def _peer_coef(hval, gate):
    N = gate.shape[0]
    tm = 1024 if N % 1024 == 0 else _TBP
    spec = pl.BlockSpec((tm, _NPAIR), lambda i: (i, 0))
    return pl.pallas_call(
        _peer_coef_kernel,
        grid=(N // tm,),
        in_specs=[spec, spec],
        out_specs=spec,
        out_shape=jax.ShapeDtypeStruct((N, _NPAIR), F32),
        name="peer_coef",
        compiler_params=_cparams(("arbitrary",)),
    )(hval, gate)


def _peer_v_kernel(idx_ref, n0_ref, coef_ref, v_hbm, o_ref, tbuf, sem):
    @pl.when(pl.program_id(0) == 0)
    def _():
        _load_table(v_hbm, tbuf, sem)

    n_acc = 4

    def accumulate(accs, prods):
        accs = list(accs)
        for s in range(_PAIR_GROUP):
            accs[s % n_acc] = accs[s % n_acc] + prods[s]
        return tuple(accs)

    def tok_body(t, _):
        g_mid, rem, g_hi = _token_segments(n0_ref[0, 0, t])

        def weighted_group(g, unpack):
            kg = _group_start(t, g)
            return tuple(coef_ref[0, 0, kg + s] * unpack(_table_row(tbuf, idx_ref, kg + s), s)
                         for s in range(_PAIR_GROUP))

        def uniform_groups(ga, gb, half, accs):
            unpack = lambda w, s: _unpack(w, half)

            def grp_body(g, carry):
                prods, accs = carry
                nxt = weighted_group(g + 1, unpack)
                return nxt, accumulate(accs, prods)

            return lax.fori_loop(ga, gb, grp_body, (weighted_group(ga, unpack), accs))[1]

        zero = jnp.zeros((_ROW_TILE, LANES), F32)
        accs = uniform_groups(0, g_mid, 0, (zero,) * n_acc)
        accs = lax.cond(
            rem != 0,
            lambda a: accumulate(a, weighted_group(g_mid, lambda w, s: _mixed_unpack(w, s, rem))),
            lambda a: a, accs)
        accs = uniform_groups(g_hi, _N_GROUPS, 1, accs)
        o_ref[t] = (accs[0] + accs[1]) + (accs[2] + accs[3])
        return 0

    lax.fori_loop(0, _TBP, tok_body, 0)


def _peer_v(idx, n0, coef, v_packed):
    N = coef.shape[0]
    nt = N // _TBP
    out = pl.pallas_call(
        _peer_v_kernel,
        grid=(nt,),
        in_specs=[_flat_smem_spec(nt, _NPAIR), _flat_smem_spec(nt, 1), _flat_smem_spec(nt, _NPAIR),
                  pl.BlockSpec(memory_space=pl.ANY)],
        out_specs=pl.BlockSpec((_TBP, _ROW_TILE, LANES), lambda i: (i, 0, 0)),
        out_shape=jax.ShapeDtypeStruct((N, _ROW_TILE, LANES), F32),
        scratch_shapes=[pltpu.VMEM((PEER_HALF_EXPERTS * _ROW_TILE, LANES), jnp.int32),
                        pltpu.SemaphoreType.DMA(())],
        name="peer_v",
        compiler_params=_cparams(("arbitrary",)),
    )(_flat_blocks(idx, nt), _flat_blocks(n0, nt), _flat_blocks(coef, nt), v_packed)
    return out.reshape(N * _ROW_TILE, LANES)


def _final_kernel(h_ref, y_ref, w_ref, o_ref, *, normalize):
    T = h_ref.shape[0]
    y = jnp.concatenate(
        [y_ref[pl.ds(c, T, stride=SUBLANES), :] for c in range(D_MODEL // LANES)], axis=1)
    h = h_ref[...] + y
    if normalize:
        ms = jnp.mean(h * h, axis=-1, keepdims=True)
        h = h * lax.rsqrt(ms + NORM_EPS) * w_ref[...]
    o_ref[...] = h


def _final(h, ypart, w):
    N = h.shape[0]
    tm = 512 if N % 512 == 0 else _TBP
    normalize = w is not None
    if not normalize:
        w = jnp.ones((D_MODEL,), F32)
    return pl.pallas_call(
        functools.partial(_final_kernel, normalize=normalize),
        grid=(N // tm,),
        in_specs=[pl.BlockSpec((tm, D_MODEL), lambda i: (i, 0)),
                  pl.BlockSpec((tm * SUBLANES, LANES), lambda i: (i, 0)),
                  pl.BlockSpec((1, D_MODEL), lambda i: (0, 0))],
        out_specs=pl.BlockSpec((tm, D_MODEL), lambda i: (i, 0)),
        out_shape=jax.ShapeDtypeStruct((N, D_MODEL), F32),
        name="final_norm",
        compiler_params=_cparams(("arbitrary",)),
    )(h, ypart, w.reshape(1, D_MODEL))


_TM_IN = 256


def _pages_feature_major(cache_l):
    n_phys, page = cache_l.shape[:2]
    return jnp.swapaxes(cache_l.reshape(n_phys, page, -1), 1, 2)


def kernel(x_prompt, x_sample, cache_k, cache_v, cache_idx_k, state_ret, page_table, norm_attn_w, w_in, ret_gn_w,
           w_out, norm_ffn_w, peer_w_q, peer_sub_keys_1, peer_sub_keys_2, peer_u, peer_v, final_norm_w):
    B, S, D = x_prompt.shape
    Bd, T, _ = x_sample.shape
    depth = w_in.shape[0]
    n_pages = page_table.shape[1]
    past_len = n_pages * PAGE_SIZE
    n_phys = cache_k.shape[1]
    Np, Ns = B * S, Bd * SROWS
    assert D == D_MODEL and T <= SROWS and S % _TM_IN == 0 and Ns % _TM_IN == 0 and S % _CK == 0

    hs_pad = jnp.pad(x_sample, ((0, 0), (0, SROWS - T), (0, 0)))
    h_all = jnp.concatenate([x_prompt.reshape(Np, D), hs_pad.reshape(Ns, D)], axis=0)

    pos_s = past_len + jnp.arange(SROWS)
    pos = jnp.concatenate([jnp.arange(S), jnp.tile(pos_s, _TM_IN // SROWS)])
    tab_att, tab_ret = _rope_tables(pos)
    tiles_per_seq = S // _TM_IN
    tab_blocks = [i % tiles_per_seq for i in range(Np // _TM_IN)] + [tiles_per_seq] * (Ns // _TM_IN)
    ret_consts_p = _ret_constants(RET_CHUNK)
    ret_consts_s = _ret_constants(T)

    outs = {n: [] for n in ("kp", "vp", "ikp", "sp", "ks", "vs", "iks", "ss")}
    for l in range(depth):
        m = _in_proj(h_all, norm_attn_w[l], _pad_w_in(w_in[l]), tab_att, tab_ret, tab_blocks, _TM_IN)
        att_p = _prompt_dsa(m["qi"], m["kiwi"], m["qa"], m["ka"], m["va"], B, S)
        att_s = _sample_dsa(page_table, m["qi"], m["kiwi"], m["qa"], m["ka"], m["va"],
                            _pages_feature_major(cache_idx_k[l]), _pages_feature_major(cache_k[l]),
                            _pages_feature_major(cache_v[l]), T, Np)
        ret_p, s_p = _retention(m["qr"], m["kr"], m["vr"], m["gr"], ret_gn_w[l],
                                jnp.zeros((B, RET_HEADS, RET_HEAD_DIM, RET_HEAD_DIM), F32),
                                ret_consts_p, B, S // RET_CHUNK, RET_CHUNK)
        ret_s, s_s = _retention(m["qr"], m["kr"], m["vr"], m["gr"], ret_gn_w[l], state_ret[l].astype(F32),
                                ret_consts_s, Bd, 1, SROWS, row0=Np)
        h_mid, xn, idx, gate, n0 = _post_mix(
            h_all, att_p, att_s, ret_p, ret_s, w_out[l].astype(BF16), norm_ffn_w[l],
            peer_w_q[l].T.astype(BF16), peer_sub_keys_1[l].astype(BF16), peer_sub_keys_2[l].astype(BF16))
        coef = _peer_coef(_peer_u(idx, n0, xn, _pack_table(peer_u[l])), gate)
        ypart = _peer_v(idx, n0, coef, _pack_table(peer_v[l]))
        last = l == depth - 1
        h_all = _final(h_mid, ypart, final_norm_w if last else None)

        sample = lambda a, w: a[Np:].reshape(Bd, SROWS, w)[:, :T]
        outs["kp"].append(m["ka"][:Np].reshape(B, S, KV_HEADS, ATT_HEAD_DIM))
        outs["vp"].append(m["va"][:Np].reshape(B, S, KV_HEADS, ATT_HEAD_DIM))
        outs["ikp"].append(m["kiwi"][:Np, :IDX_DIM].reshape(B, S, IDX_DIM))
        outs["sp"].append(s_p.astype(x_prompt.dtype))
        outs["ks"].append(sample(m["ka"], KV_WIDTH).reshape(Bd, T, KV_HEADS, ATT_HEAD_DIM))
        outs["vs"].append(sample(m["va"], KV_WIDTH).reshape(Bd, T, KV_HEADS, ATT_HEAD_DIM))
        outs["iks"].append(sample(m["kiwi"], _KIWI_WIDTH)[:, :, :IDX_DIM])
        outs["ss"].append(s_s.astype(state_ret.dtype))

    y_prompt = h_all[:Np].reshape(B, S, D)
    y_sample = h_all[Np:].reshape(Bd, SROWS, D)[:, :T]
    st = lambda n: jnp.stack(outs[n])
    return (y_prompt, y_sample, st("kp"), st("vp"), st("ikp"), st("sp"),
            st("ks"), st("vs"), st("iks"), st("ss"))
```

```python
import functools

import jax
import jax.numpy as jnp
from jax import lax
from jax.experimental import pallas as pl
from jax.experimental.pallas import tpu as pltpu

F32 = jnp.float32
BF16 = jnp.bfloat16

D_MODEL = 1024
PAGE_SIZE = 128
ATT_HEADS = 8
ATT_HEAD_DIM = 64
KV_HEADS = 2
ATT_GROUP = ATT_HEADS // KV_HEADS
ATT_WIDTH = ATT_HEADS * ATT_HEAD_DIM
KV_WIDTH = KV_HEADS * ATT_HEAD_DIM
ROPE_THETA = 500000.0
ATT_ROPE_DIMS = ATT_HEAD_DIM // 4
IDX_HEADS = 8
IDX_DIM = 64
IDX_WIDTH = IDX_HEADS * IDX_DIM
TOPK_MAX = 256
RET_HEADS = 4
RET_HEAD_DIM = 128
RET_WIDTH = RET_HEADS * RET_HEAD_DIM
RET_ROPE_THETA = 10000.0
RET_CHUNK = 128
PEER_HEADS = 8
PEER_NKEYS = 128
PEER_EXPERTS = PEER_NKEYS * PEER_NKEYS
PEER_KEY_DIM = 128
PEER_HALF = PEER_KEY_DIM // 2
PEER_TOPK = 16
NORM_EPS = 1e-6
GN_EPS = 1e-6

LANES = 128
SUBLANES = 8
VMEM_LIMIT_BYTES = 56 * 1024 * 1024

NEG_BIG = -1e30

_IN_SPLITS = (ATT_WIDTH, KV_WIDTH, KV_WIDTH, IDX_WIDTH, IDX_DIM, IDX_HEADS,
              RET_WIDTH, RET_WIDTH, RET_WIDTH, RET_WIDTH)
_KIWI_WIDTH = LANES
_PAD_OFFS = {}
_off = 0
for _name, _w in (("qa", ATT_WIDTH), ("ka", KV_WIDTH), ("va", KV_WIDTH), ("qi", IDX_WIDTH),
                  ("kiwi", _KIWI_WIDTH), ("qr", RET_WIDTH), ("kr", RET_WIDTH),
                  ("vr", RET_WIDTH), ("gr", RET_WIDTH)):
    _PAD_OFFS[_name] = (_off, _w)
    _off += _w
IN_WIDTH_PADDED = _off


def _cparams(sem):
    return pltpu.CompilerParams(dimension_semantics=sem, vmem_limit_bytes=VMEM_LIMIT_BYTES)


def _dot(a, b):
    return jnp.dot(a.astype(BF16), b.astype(BF16), preferred_element_type=F32)


def _dot_nt(a, b):
    return lax.dot_general(a.astype(BF16), b.astype(BF16), (((1,), (1,)), ((), ())),
                           preferred_element_type=F32)


def _pad_w_in(w_in_l):
    off = 0
    parts = []
    for n in _IN_SPLITS:
        parts.append(w_in_l[:, off:off + n])
        off += n
    qa, ka, va, qi, ki, wi, qr, kr, vr, gr = parts
    kiwi = jnp.concatenate(
        [ki, wi, jnp.zeros((w_in_l.shape[0], _KIWI_WIDTH - IDX_DIM - IDX_HEADS), w_in_l.dtype)], axis=1)
    cols = [qa, ka, va, qi, kiwi, qr, kr, vr, gr]
    return jnp.concatenate(cols, axis=1).astype(BF16)


def _rope_tables(pos):
    posf = pos.astype(F32)
    half = ATT_ROPE_DIMS // 2
    inv = 1.0 / (ROPE_THETA ** (jnp.arange(half, dtype=F32) / half))
    ang = posf[:, None] * inv[None, :]
    cos, sin = jnp.cos(ang), jnp.sin(ang)
    P = pos.shape[0]
    one = jnp.ones((P, ATT_HEAD_DIM - ATT_ROPE_DIMS), F32)
    zero = jnp.zeros((P, ATT_HEAD_DIM - ATT_ROPE_DIMS), F32)
    zh = jnp.zeros((P, half), F32)
    c_head = jnp.concatenate([cos, cos, one], axis=1)
    s1_head = jnp.concatenate([-sin, zh, zero], axis=1)
    s2_head = jnp.concatenate([zh, sin, zero], axis=1)
    rep = LANES // ATT_HEAD_DIM
    att = jnp.stack([jnp.tile(c_head, (1, rep)), jnp.tile(s1_head, (1, rep)), jnp.tile(s2_head, (1, rep))])
    halfr = RET_HEAD_DIM // 2
    invr = 1.0 / (RET_ROPE_THETA ** (jnp.arange(halfr, dtype=F32) / halfr))
    angr = posf[:, None] * invr[None, :]
    cr, sr = jnp.cos(angr), jnp.sin(angr)
    ret = jnp.stack([jnp.concatenate([cr, cr], axis=1), jnp.concatenate([-sr, sr], axis=1)])
    return att, ret


def _in_proj_kernel(x_ref, nw_ref, w_ref, ta_ref, tr_ref,
                    qa_ref, ka_ref, va_ref, qi_ref, kiwi_ref, qr_ref, kr_ref, vr_ref, gr_ref):
    x = x_ref[...]
    ms = jnp.mean(x * x, axis=-1, keepdims=True)
    xn = (x * lax.rsqrt(ms + NORM_EPS) * nw_ref[...]).astype(BF16)
    ca, s1a, s2a = ta_ref[0], ta_ref[1], ta_ref[2]
    cr, sr = tr_ref[0], tr_ref[1]

    def proj(name):
        off, w = _PAD_OFFS[name]
        return jnp.dot(xn, w_ref[:, off:off + w], preferred_element_type=F32)

    def rope_att_chunk(vc):
        return (vc * ca + pltpu.roll(vc, LANES - ATT_ROPE_DIMS // 2, 1) * s1a
                + pltpu.roll(vc, ATT_ROPE_DIMS // 2, 1) * s2a)

    def rope_ret_chunk(vc):
        return vc * cr + pltpu.roll(vc, RET_HEAD_DIM // 2, 1) * sr

    def per_chunk(v, fn):
        n = v.shape[1] // LANES
        return jnp.concatenate([fn(v[:, c * LANES:(c + 1) * LANES]) for c in range(n)], axis=1)

    qa_ref[...] = per_chunk(proj("qa"), rope_att_chunk)
    ka_ref[...] = per_chunk(proj("ka"), rope_att_chunk)
    va_ref[...] = proj("va")
    qi_ref[...] = per_chunk(proj("qi"), rope_att_chunk)
    kiwi = proj("kiwi")
    lane = lax.broadcasted_iota(jnp.int32, kiwi.shape, 1)
    kiwi_ref[...] = jnp.where(lane < IDX_DIM, rope_att_chunk(kiwi), kiwi * (IDX_HEADS ** -0.5))
    qr_ref[...] = per_chunk(proj("qr"), rope_ret_chunk)
    kr_ref[...] = per_chunk(proj("kr"), rope_ret_chunk) * (RET_HEAD_DIM ** -0.5)
    vr_ref[...] = proj("vr")
    gr_ref[...] = proj("gr")


def _in_proj(x, norm_w, w_pad, tab_att, tab_ret, tab_block_of_tile, tm):
    N = x.shape[0]
    nt = N // tm
    tab_idx = jnp.asarray(tab_block_of_tile, jnp.int32)
    names = ("qa", "ka", "va", "qi", "kiwi", "qr", "kr", "vr", "gr")
    out_shape = [jax.ShapeDtypeStruct((N, _PAD_OFFS[n][1]), F32) for n in names]
    out_specs = [pl.BlockSpec((tm, _PAD_OFFS[n][1]), lambda i, t: (i, 0)) for n in names]
    grid_spec = pltpu.PrefetchScalarGridSpec(
        num_scalar_prefetch=1,
        grid=(nt,),
        in_specs=[
            pl.BlockSpec((tm, D_MODEL), lambda i, t: (i, 0)),
            pl.BlockSpec((1, D_MODEL), lambda i, t: (0, 0)),
            pl.BlockSpec((D_MODEL, IN_WIDTH_PADDED), lambda i, t: (0, 0)),
            pl.BlockSpec((3, tm, LANES), lambda i, t: (0, t[i], 0)),
            pl.BlockSpec((2, tm, LANES), lambda i, t: (0, t[i], 0)),
        ],
        out_specs=out_specs,
    )

    def body(t_ref, *refs):
        _in_proj_kernel(*refs)

    outs = pl.pallas_call(
        body, grid_spec=grid_spec, out_shape=out_shape, name="in_proj",
        compiler_params=_cparams(("arbitrary",)),
    )(tab_idx, x, norm_w.reshape(1, D_MODEL), w_pad, tab_att, tab_ret)
    return dict(zip(names, outs))


def _ret_constants(c_eff):
    C = RET_CHUNK
    lg = jnp.log1p(-(2.0 ** (-5.0 - jnp.arange(RET_HEADS, dtype=F32))))
    i = jnp.arange(C, dtype=F32)
    diff = i[:, None] - i[None, :]
    dmask = jnp.where(diff >= 0, jnp.exp(lg[:, None, None] * jnp.maximum(diff, 0.0)), 0.0)
    real = (i < c_eff)
    dmask = jnp.where(real[None, :, None] & real[None, None, :], dmask, 0.0)
    q_dec = jnp.exp(lg[:, None] * (i[None, :] + 1.0))
    k_dec = jnp.where(real[None, :], jnp.exp(lg[:, None] * (c_eff - 1.0 - i[None, :])), 0.0)
    chunk_dec = jnp.exp(lg * c_eff)
    bc = lambda a: jnp.broadcast_to(a[:, :, None], (RET_HEADS, C, C))
    cdec = jnp.broadcast_to(chunk_dec[:, None, None], (RET_HEADS, C, C))
    return dmask, bc(q_dec), bc(k_dec), cdec


def _retention_kernel(q_ref, k_ref, v_ref, g_ref, gnw_ref, s0_ref, dm_ref, qd_ref, kd_ref, cd_ref,
                      o_ref, s_out_ref, state_ref, *, rows):
    c = pl.program_id(1)
    nc = pl.num_programs(1)

    @pl.when(c == 0)
    def _():
        state_ref[...] = s0_ref[0]

    def padded(ref):
        v = ref[...]
        if rows < RET_CHUNK:
            v = jnp.concatenate([v, jnp.zeros((RET_CHUNK - rows, v.shape[1]), v.dtype)], axis=0)
        return v

    q, k, v, g = padded(q_ref), padded(k_ref), padded(v_ref), g_ref[...]
    gnw = gnw_ref[...]
    outs = []
    for h in range(RET_HEADS):
        sl = slice(h * RET_HEAD_DIM, (h + 1) * RET_HEAD_DIM)
        qh, kh, vh = q[:, sl], k[:, sl], v[:, sl]
        st = state_ref[h]
        att = _dot_nt(qh, kh) * dm_ref[h]
        o = _dot(att, vh) + _dot(qh, st) * qd_ref[h]
        state_ref[h] = st * cd_ref[h] + _dot((kh * kd_ref[h]).T, vh)
        o = o[:rows]
        mu = jnp.mean(o, axis=-1, keepdims=True)
        var = jnp.mean(jnp.square(o - mu), axis=-1, keepdims=True)
        outs.append((o - mu) * lax.rsqrt(var + GN_EPS) * gnw[:, sl])
    on = jnp.concatenate(outs, axis=1)
    o_ref[...] = g * (1.0 / (1.0 + jnp.exp(-g))) * on

    @pl.when(c == nc - 1)
    def _():
        s_out_ref[0] = state_ref[...]


def _retention(q, k, v, g, gn_w, s0, consts, nb, nc, rows, row0=0):
    blk0 = row0 // rows
    in_row = pl.BlockSpec((rows, RET_WIDTH), lambda b, c: (blk0 + b * nc + c, 0))
    out_row = pl.BlockSpec((rows, RET_WIDTH), lambda b, c: (b * nc + c, 0))
    st_spec = pl.BlockSpec((1, RET_HEADS, RET_HEAD_DIM, RET_HEAD_DIM), lambda b, c: (b, 0, 0, 0))
    const_spec = pl.BlockSpec((RET_HEADS, RET_CHUNK, RET_CHUNK), lambda b, c: (0, 0, 0))
    return pl.pallas_call(
        functools.partial(_retention_kernel, rows=rows),
        grid=(nb, nc),
        in_specs=[in_row, in_row, in_row, in_row,
                  pl.BlockSpec((1, RET_WIDTH), lambda b, c: (0, 0)),
                  st_spec, const_spec, const_spec, const_spec, const_spec],
        out_specs=[out_row, st_spec],
        out_shape=[jax.ShapeDtypeStruct((nb * nc * rows, RET_WIDTH), F32),
                   jax.ShapeDtypeStruct(s0.shape, F32)],
        scratch_shapes=[pltpu.VMEM((RET_HEADS, RET_HEAD_DIM, RET_HEAD_DIM), F32)],
        name="retention",
        compiler_params=_cparams(("arbitrary", "arbitrary")),
    )(q, k, v, g, gn_w.reshape(1, RET_WIDTH), s0, *consts)


_BISECT_PLAIN_ITERS = 26
_BISECT_MAX_ITERS = 400


def _lane_tiles(x):
    return [x[:, j * LANES:(j + 1) * LANES] for j in range(x.shape[1] // LANES)]


def _rowsum_b(x):
    return jnp.broadcast_to(jnp.sum(x, axis=1, keepdims=True), x.shape)


def _select_chunk(x, lo, hi, need, tied_any, tied, prefix0):
    R, CK = x.shape
    rep = CK // LANES
    wide = lambda a: jnp.concatenate([a] * rep, axis=1)

    def plain(_):
        return jnp.where(x >= wide(lo), 1.0, 0.0), prefix0

    def with_ties(_):
        low, hiw = wide(lo), wide(hi)
        band = (x >= low) & (x < hiw)
        bandf = jnp.where(band, 1.0, 0.0)
        r_i = lax.broadcasted_iota(jnp.int32, (CK, CK), 0)
        c_i = lax.broadcasted_iota(jnp.int32, (CK, CK), 1)
        tri = jnp.where(r_i <= c_i, 1.0, 0.0).astype(BF16)
        rank = jnp.dot(bandf.astype(BF16), tri, preferred_element_type=F32) + wide(prefix0)
        take_tie = jnp.where(rank <= wide(need), bandf, 0.0)
        sel_tied = jnp.where(x >= hiw, 1.0, take_tie)
        sel = jnp.where(wide(tied) > 0.0, sel_tied, jnp.where(x >= low, 1.0, 0.0))
        return sel, prefix0 + _rowsum_b(sum(_lane_tiles(bandf)))

    return lax.cond(tied_any, with_ties, plain, 0)


_QB = 128
_CK = 512


def _col_partial(x, op):
    n_chain = 4
    g = x.shape[0] // (SUBLANES * n_chain)
    x4 = x.reshape(n_chain, g, SUBLANES, x.shape[1])
    c = [op(x4[k], axis=0) for k in range(n_chain)]
    pair = jnp.stack([c[0], c[1]]), jnp.stack([c[2], c[3]])
    return op(jnp.stack([op(pair[0], axis=0), op(pair[1], axis=0)]), axis=0)


def _col_fold(x, op):
    return op(_col_partial(x, op), axis=0, keepdims=True)


def _bisect(count_pass, kk, lo0, hi0, n_causal, active):
    def not_done(c_lo, lo, hib):
        pending = active & (c_lo != kk) & (lo != hib)
        return jnp.max(jnp.where(pending, 1.0, 0.0)) > 0.0

    def make_iter(snap, reps):
        def one(state):
            lo, hi, hib, c_lo, c_hi = state
            if snap:
                mid = 0.5 * (lo + jnp.minimum(hi, hib))
                mid = jnp.where(mid > lo, mid, jnp.minimum(hi, hib))
            else:
                mid = 0.5 * (lo + hi)
            c, amin, bmax = count_pass(mid, snap)
            ge = c >= kk
            if snap:
                lo = jnp.where(ge, amin, lo)
                hib = jnp.where(ge, hib, bmax)
            else:
                lo = jnp.where(ge, mid, lo)
            hi = jnp.where(ge, hi, mid)
            return lo, hi, hib, jnp.where(ge, c, c_lo), jnp.where(ge, c_hi, c)

        def step(carry):
            it, state = carry[0], carry[1:]
            for _ in range(reps):
                state = one(state)
            return (it + reps, *state)
        return step

    init = (jnp.int32(0), lo0, hi0, jnp.full(lo0.shape, jnp.inf, F32), n_causal, jnp.zeros(lo0.shape, F32))
    carry = lax.while_loop(
        lambda c: (c[0] < _BISECT_PLAIN_ITERS) & not_done(c[4], c[1], c[3]), make_iter(False, 2), init)
    carry = lax.while_loop(
        lambda c: (c[0] < _BISECT_MAX_ITERS) & not_done(c[4], c[1], c[3]), make_iter(True, 1), carry)
    _, lo, hi, hib, c_lo, c_hi = carry
    return lo, hi, c_hi, active & (c_lo != kk)


def _threshold_search_t(sc_ref, nck, kk, lo0, hi0, n_causal):
    Q = lo0.shape[1]

    def count_pass(mid, snap):
        def body(kc, carry):
            x = sc_ref[kc]
            cnt, amin, bmax = carry
            ge = x >= mid
            cnt = cnt + _col_partial(jnp.where(ge, 1.0, 0.0), jnp.sum)
            if snap:
                amin = jnp.minimum(amin, _col_partial(jnp.where(ge, x, jnp.inf), jnp.min))
                bmax = jnp.maximum(bmax, _col_partial(jnp.where(ge, -jnp.inf, x), jnp.max))
            return cnt, amin, bmax
        init = (jnp.zeros((SUBLANES, Q), F32), jnp.full((SUBLANES, Q), jnp.inf, F32),
                jnp.full((SUBLANES, Q), -jnp.inf, F32))
        cnt, amin, bmax = lax.fori_loop(0, nck, body, init)
        return (jnp.sum(cnt, axis=0, keepdims=True), jnp.min(amin, axis=0, keepdims=True),
                jnp.max(bmax, axis=0, keepdims=True))

    return _bisect(count_pass, kk, lo0, hi0, n_causal, lo0 == lo0)


def _threshold_search_rows(sc_ref, kk, lo0, hi0, n_causal, active):
    def count_pass(mid, snap):
        x = sc_ref[...]
        ge = x >= mid
        cnt = jnp.sum(jnp.where(ge, 1.0, 0.0), axis=1, keepdims=True)
        if not snap:
            return cnt, None, None
        return (cnt, jnp.min(jnp.where(ge, x, jnp.inf), axis=1, keepdims=True),
                jnp.max(jnp.where(ge, -jnp.inf, x), axis=1, keepdims=True))

    return _bisect(count_pass, kk, lo0, hi0, n_causal, active)


def _select_chunk_t(x, lo, hi, need, tied_any, tiedf, prefix0):
    ck = x.shape[0]

    def plain(_):
        return jnp.where(x >= lo, 1.0, 0.0), prefix0

    def with_ties(_):
        bandf = jnp.where((x >= lo) & (x < hi), 1.0, 0.0)
        r_i = lax.broadcasted_iota(jnp.int32, (ck, ck), 0)
        c_i = lax.broadcasted_iota(jnp.int32, (ck, ck), 1)
        tri = jnp.where(c_i <= r_i, 1.0, 0.0).astype(BF16)
        rank = jnp.dot(tri, bandf.astype(BF16), preferred_element_type=F32) + prefix0
        take_tie = jnp.where(rank <= need, bandf, 0.0)
        sel_tied = jnp.where(x >= hi, 1.0, take_tie)
        sel = jnp.where(tiedf > 0.0, sel_tied, jnp.where(x >= lo, 1.0, 0.0))
        return sel, prefix0 + _col_fold(bandf, jnp.sum)

    return lax.cond(tied_any, with_ties, plain, 0)


def _prompt_dsa_kernel(qi_ref, kiwiq_ref, kiwi_ref, qa_ref, ka_ref, va_ref, o_ref,
                         sc_ref, vt_ref, acc_ref, *, topk, seq):
    i = pl.program_id(1)
    t0 = i * _QB
    nck = (t0 + _QB + _CK - 1) // _CK
    Q = _QB
    HQ = ATT_HEADS * Q

    @pl.when(i == 0)
    def _():
        for j in range(seq // _CK):
            vt_ref[:, j * _CK:(j + 1) * _CK] = va_ref[j * _CK:(j + 1) * _CK, :].T

    qpos = t0 + lax.broadcasted_iota(jnp.int32, (1, Q), 1)
    key_row = lax.broadcasted_iota(jnp.int32, (_CK, Q), 0)

    qi = qi_ref[...]
    w_t = kiwiq_ref[...].T
    q_heads, w_rows = [], []
    for h in range(IDX_HEADS):
        qh = qi[:, h * IDX_DIM:(h + 1) * IDX_DIM] * (IDX_DIM ** -0.5)
        q_heads.append(jnp.concatenate([qh, jnp.zeros((Q, LANES - IDX_DIM), F32)], axis=1).astype(BF16))
        w_rows.append(w_t[IDX_DIM + h:IDX_DIM + h + 1, :])

    def score_body(kc, carry):
        mn, mx = carry
        k0 = pl.multiple_of(kc * _CK, _CK)
        kch = kiwi_ref[pl.ds(k0, _CK), :].astype(BF16)
        acc = jnp.zeros((_CK, Q), F32)
        for h in range(IDX_HEADS):
            acc = acc + w_rows[h] * jnp.maximum(_dot_nt(kch, q_heads[h]), 0.0)
        causal = (k0 + key_row) <= qpos
        sc_ref[kc] = jnp.where(causal, acc, -jnp.inf)
        mn = jnp.minimum(mn, _col_fold(jnp.where(causal, acc, jnp.inf), jnp.min))
        mx = jnp.maximum(mx, _col_fold(jnp.where(causal, acc, -jnp.inf), jnp.max))
        return mn, mx

    mn, mx = lax.fori_loop(0, nck, score_body,
                           (jnp.full((1, Q), jnp.inf, F32), jnp.full((1, Q), -jnp.inf, F32)))
    n_causal = (qpos + 1).astype(F32)
    kk = jnp.minimum(n_causal, float(topk))
    lo, hi, c_hi, tied = _threshold_search_t(sc_ref, nck, kk, mn, mx + (mx - mn) + 1.0, n_causal)
    need = kk - c_hi
    tiedf = jnp.where(tied, 1.0, 0.0)
    tied_any = jnp.max(tiedf) > 0.0

    qa = qa_ref[...]
    q_rows = []
    for h in range(ATT_HEADS):
        c = h // ATT_GROUP
        parts = [jnp.zeros((Q, ATT_HEAD_DIM), F32)] * KV_HEADS
        parts[c] = qa[:, h * ATT_HEAD_DIM:(h + 1) * ATT_HEAD_DIM] * (ATT_HEAD_DIM ** -0.5)
        q_rows.append(jnp.concatenate(parts, axis=1))
    q_all = jnp.concatenate(q_rows, axis=0).astype(BF16)
    acc_ref[...] = jnp.zeros(acc_ref.shape, F32)

    def att_body(kc, carry):
        m_old, l_old, prefix = carry
        k0 = pl.multiple_of(kc * _CK, _CK)
        kch = ka_ref[pl.ds(k0, _CK), :].astype(BF16)
        vch_t = vt_ref[:, pl.ds(k0, _CK)].astype(BF16)
        sel, prefix = _select_chunk_t(sc_ref[kc], lo, hi, need, tied_any, tiedf, prefix)
        selb = jnp.concatenate([sel] * ATT_HEADS, axis=1) > 0.0
        s = jnp.where(selb, _dot_nt(kch, q_all), NEG_BIG)
        m_new = jnp.maximum(m_old, _col_fold(s, jnp.max))
        alpha = jnp.exp(m_old - m_new)
        p = jnp.exp(s - m_new)
        l_new = l_old * alpha + _col_fold(p, jnp.sum)
        acc_ref[...] = acc_ref[...] * alpha + jnp.dot(vch_t, p.astype(BF16), preferred_element_type=F32)
        return m_new, l_new, prefix

    init = (jnp.full((1, HQ), NEG_BIG, F32), jnp.zeros((1, HQ), F32), jnp.zeros((1, Q), F32))
    _, l_fin, _ = lax.fori_loop(0, nck, att_body, init)
    o_t = acc_ref[...] / l_fin
    outs = []
    for h in range(ATT_HEADS):
        c = h // ATT_GROUP
        outs.append(o_t[:, h * Q:(h + 1) * Q].T[:, c * ATT_HEAD_DIM:(c + 1) * ATT_HEAD_DIM])
    o_ref[...] = jnp.concatenate(outs, axis=1)


def _prompt_dsa(qi, kiwi, qa, ka, va, B, S):
    nq = S // _QB
    topk = min(TOPK_MAX, S // 4)
    qblk = lambda w: pl.BlockSpec((_QB, w), lambda b, i: (b * nq + i, 0))
    allk = lambda w: pl.BlockSpec((S, w), lambda b, i: (b, 0))
    return pl.pallas_call(
        functools.partial(_prompt_dsa_kernel, topk=topk, seq=S),
        grid=(B, nq),
        in_specs=[qblk(IDX_WIDTH), qblk(_KIWI_WIDTH), allk(_KIWI_WIDTH), qblk(ATT_WIDTH),
                  allk(KV_WIDTH), allk(KV_WIDTH)],
        out_specs=qblk(ATT_WIDTH),
        out_shape=jax.ShapeDtypeStruct((B * S, ATT_WIDTH), F32),
        scratch_shapes=[pltpu.VMEM((S // _CK, _CK, _QB), F32),
                        pltpu.VMEM((KV_WIDTH, S), F32),
                        pltpu.VMEM((KV_WIDTH, ATT_HEADS * _QB), F32)],
        name="prompt_dsa",
        compiler_params=_cparams(("arbitrary", "arbitrary")),
    )(qi, kiwi, kiwi, qa, ka, va)


SROWS = SUBLANES
_CKS = 640


def _sample_dsa_kernel(pt_ref, qi_ref, kiwi_ref, qa_ref, ka_ref, va_ref, cik_hbm, ck_hbm, cv_hbm,
                       o_ref, ikbuf, kbuf, vbuf, sems, sc_ref, sel_ref, *, n_pages, t_real, topk):
    b = pl.program_id(0)
    nb = pl.num_programs(0)
    slot = b % 2
    past_len = n_pages * PAGE_SIZE
    L = past_len + PAGE_SIZE
    nck = L // _CKS
    R = SROWS

    def page_copies(bb, s, p):
        phys = pt_ref[bb, p]
        cols = pl.ds(pl.multiple_of(p * PAGE_SIZE, PAGE_SIZE), PAGE_SIZE)
        return (pltpu.make_async_copy(cik_hbm.at[phys], ikbuf.at[s, :, cols], sems.at[s, 0]),
                pltpu.make_async_copy(ck_hbm.at[phys], kbuf.at[s, :, cols], sems.at[s, 1]),
                pltpu.make_async_copy(cv_hbm.at[phys], vbuf.at[s, :, cols], sems.at[s, 2]))

    def start_fetch(bb, s):
        def body(p, _):
            for cp in page_copies(bb, s, p):
                cp.start()
            return 0
        lax.fori_loop(0, n_pages, body, 0)

    def wait_fetch(bb, s):
        def body(p, _):
            for cp in page_copies(bb, s, p):
                cp.wait()
            return 0
        lax.fori_loop(0, n_pages, body, 0)

    @pl.when(b == 0)
    def _():
        tail = pl.ds(past_len, PAGE_SIZE)
        for s in range(2):
            ikbuf[s, :, tail] = jnp.zeros((IDX_DIM, PAGE_SIZE), F32)
            kbuf[s, :, tail] = jnp.zeros((KV_WIDTH, PAGE_SIZE), F32)
            vbuf[s, :, tail] = jnp.zeros((KV_WIDTH, PAGE_SIZE), F32)
        start_fetch(0, 0)

    @pl.when(b + 1 < nb)
    def _():
        start_fetch(b + 1, 1 - slot)

    kiwi = kiwi_ref[...]

    def as_columns(rows):
        padded = jnp.concatenate([rows, jnp.zeros((LANES - R, LANES), F32)], axis=0)
        return padded.T[:, :R]

    new_cols = pl.ds(past_len, R)
    ikbuf[slot, :, new_cols] = as_columns(kiwi)[:IDX_DIM]
    kbuf[slot, :, new_cols] = as_columns(ka_ref[...])
    vbuf[slot, :, new_cols] = as_columns(va_ref[...])
    wait_fetch(b, slot)

    row = lax.broadcasted_iota(jnp.int32, (R, 1), 0)
    key_pos = lax.broadcasted_iota(jnp.int32, (R, L), 1)

    qi = qi_ref[...]
    q_all = jnp.concatenate(
        [qi[:, h * IDX_DIM:(h + 1) * IDX_DIM] * (IDX_DIM ** -0.5) for h in range(IDX_HEADS)], axis=0).astype(BF16)
    d = jnp.dot(q_all, ikbuf[slot].astype(BF16), preferred_element_type=F32)
    acc = jnp.zeros((R, L), F32)
    for h in range(IDX_HEADS):
        acc = acc + kiwi[:, IDX_DIM + h:IDX_DIM + h + 1] * jnp.maximum(d[h * R:(h + 1) * R], 0.0)
    causal = key_pos <= past_len + row
    sc_ref[...] = jnp.where(causal, acc, -jnp.inf)
    mn = jnp.min(jnp.where(causal, acc, jnp.inf), axis=1, keepdims=True)
    mx = jnp.max(jnp.where(causal, acc, -jnp.inf), axis=1, keepdims=True)
    n_causal = (past_len + row + 1).astype(F32)
    kk = jnp.minimum(n_causal, float(topk))
    lo, hi, c_hi, tied = _threshold_search_rows(sc_ref, kk, mn, mx + (mx - mn) + 1.0, n_causal, row < t_real)
    tiedf = jnp.where(tied, 1.0, 0.0)
    tied_any = jnp.max(tiedf) > 0.0
    sel_ref[...] = jnp.where(sc_ref[...] >= lo, 1.0, 0.0)

    @pl.when(tied_any)
    def _():
        wide = lambda a: jnp.broadcast_to(a, (R, LANES))

        def tie_body(kc, prefix):
            cols = pl.ds(pl.multiple_of(kc * _CKS, LANES), _CKS)
            sel, prefix = _select_chunk(sc_ref[:, cols], wide(lo), wide(hi), wide(kk - c_hi), tied_any,
                                        wide(tiedf), prefix)
            sel_ref[:, cols] = sel
            return prefix

        lax.fori_loop(0, nck, tie_body, jnp.zeros((R, LANES), F32))

    qa = qa_ref[...]
    q_rows = []
    for h in range(ATT_HEADS):
        c = h // ATT_GROUP
        z = jnp.zeros((R, ATT_HEAD_DIM), F32)
        parts = [z] * KV_HEADS
        parts[c] = qa[:, h * ATT_HEAD_DIM:(h + 1) * ATT_HEAD_DIM] * (ATT_HEAD_DIM ** -0.5)
        q_rows.append(jnp.concatenate(parts, axis=1))
    q_big = jnp.concatenate(q_rows, axis=0).astype(BF16)
    selb = jnp.concatenate([sel_ref[...]] * ATT_HEADS, axis=0) > 0.0
    s = jnp.where(selb, jnp.dot(q_big, kbuf[slot].astype(BF16), preferred_element_type=F32), NEG_BIG)
    p = jnp.exp(s - jnp.max(s, axis=1, keepdims=True))
    o = _dot_nt(p, vbuf[slot]) / jnp.sum(p, axis=1, keepdims=True)
    outs = []
    for h in range(ATT_HEADS):
        c = h // ATT_GROUP
        outs.append(o[h * R:(h + 1) * R, c * ATT_HEAD_DIM:(c + 1) * ATT_HEAD_DIM])
    o_ref[...] = jnp.concatenate(outs, axis=1)


def _sample_dsa(page_table, qi, kiwi, qa, ka, va, cache_idx_k_l, cache_k_l, cache_v_l, t_real, row0):
    Bd, n_pages = page_table.shape
    past_len = n_pages * PAGE_SIZE
    L = past_len + PAGE_SIZE
    assert L % _CKS == 0 and row0 % SROWS == 0
    topk = min(TOPK_MAX, (past_len + t_real) // 4)
    blk0 = row0 // SROWS
    blk = lambda w: pl.BlockSpec((SROWS, w), lambda b, pt: (blk0 + b, 0))
    any_spec = pl.BlockSpec(memory_space=pl.ANY)
    grid_spec = pltpu.PrefetchScalarGridSpec(
        num_scalar_prefetch=1,
        grid=(Bd,),
        in_specs=[blk(IDX_WIDTH), blk(_KIWI_WIDTH), blk(ATT_WIDTH), blk(KV_WIDTH), blk(KV_WIDTH),
                  any_spec, any_spec, any_spec],
        out_specs=pl.BlockSpec((SROWS, ATT_WIDTH), lambda b, pt: (b, 0)),
        scratch_shapes=[pltpu.VMEM((2, IDX_DIM, L), F32),
                        pltpu.VMEM((2, KV_WIDTH, L), F32),
                        pltpu.VMEM((2, KV_WIDTH, L), F32),
                        pltpu.SemaphoreType.DMA((2, 3)),
                        pltpu.VMEM((SROWS, L), F32),
                        pltpu.VMEM((SROWS, L), F32)],
    )
    return pl.pallas_call(
        functools.partial(_sample_dsa_kernel, n_pages=n_pages, t_real=t_real, topk=topk),
        grid_spec=grid_spec,
        out_shape=jax.ShapeDtypeStruct((Bd * SROWS, ATT_WIDTH), F32),
        name="sample_dsa",
        compiler_params=_cparams(("arbitrary",)),
    )(page_table, qi, kiwi, qa, ka, va, cache_idx_k_l, cache_k_l, cache_v_l)


_TMP = 256
_PAIR_GROUP = 16
PEER_HALF_EXPERTS = PEER_EXPERTS // 2


def _extract_topk(s, pos, n, k, payload=None):
    vals, idxs = [], []
    for _ in range(k):
        m = jnp.max(s, axis=0, keepdims=True)
        p = jnp.min(jnp.where(s == m, pos, n), axis=0, keepdims=True)
        hit = pos == p
        vals.append(m)
        if payload is None:
            idxs.append(p)
        else:
            idxs.append(jnp.max(jnp.where(hit, payload, -1), axis=0, keepdims=True))
        s = jnp.where(hit, -jnp.inf, s)
    return vals, idxs


def _post_mix_kernel(x_ref, attp_ref, atts_ref, retp_ref, rets_ref, wo_ref, nw_ref, wqt_ref, sk1_ref, sk2_ref,
                     h_ref, xn_ref, idx_ref, gate_ref, n0_ref, qt_ref, et_ref, gt_ref, la_ref, lg_ref,
                     *, n_prompt_tiles):
    T = x_ref.shape[0]
    is_prompt = pl.program_id(0) < n_prompt_tiles
    att = jnp.where(is_prompt, attp_ref[...], atts_ref[...])
    ret = jnp.where(is_prompt, retp_ref[...], rets_ref[...])
    h = x_ref[...] + _dot(att, wo_ref[:ATT_WIDTH, :]) + _dot(ret, wo_ref[ATT_WIDTH:, :])
    h_ref[...] = h
    ms = jnp.mean(h * h, axis=-1, keepdims=True)
    xn = h * lax.rsqrt(ms + NORM_EPS) * nw_ref[...]
    for c in range(D_MODEL // LANES):
        xn_ref[pl.ds(c, T, stride=SUBLANES), :] = xn[:, c * LANES:(c + 1) * LANES]
    qt_ref[...] = _dot_nt(wqt_ref[...], xn)

    pos_k = lax.broadcasted_iota(jnp.int32, (PEER_NKEYS, T), 0)
    n_b = [PEER_TOPK // (a + 1) for a in range(PEER_TOPK)]
    n_cand = -(-sum(n_b) // SUBLANES) * SUBLANES
    pos_c = lax.broadcasted_iota(jnp.int32, (n_cand, T), 0)

    def head_body(hd, _):
        q0 = pl.multiple_of(hd * PEER_KEY_DIM, PEER_KEY_DIM)
        q1 = qt_ref[pl.ds(q0, PEER_HALF), :]
        q2 = qt_ref[pl.ds(q0 + PEER_HALF, PEER_HALF), :]
        s1 = _dot(sk1_ref[...], q1)
        s2 = _dot(sk2_ref[...], q2)
        v1, i1 = _extract_topk(s1, pos_k, PEER_NKEYS, PEER_TOPK)
        v2, i2 = _extract_topk(s2, pos_k, PEER_NKEYS, PEER_TOPK)
        v2m = jnp.concatenate(v2, axis=0)
        i2m = jnp.concatenate(i2, axis=0)
        n_fill = n_cand - sum(n_b)
        cand = jnp.concatenate([v1[a] + v2m[:n_b[a]] for a in range(PEER_TOPK)]
                               + [jnp.full((n_fill, T), -jnp.inf, F32)], axis=0)
        cid = jnp.concatenate([i1[a] * PEER_NKEYS + i2m[:n_b[a]] for a in range(PEER_TOPK)]
                              + [jnp.full((n_fill, T), -1, jnp.int32)], axis=0)
        sv, eid = _extract_topk(cand, pos_c, n_cand, PEER_TOPK, payload=cid)
        svm = jnp.concatenate(sv, axis=0)
        g = jnp.exp(svm - sv[0])
        r0 = pl.multiple_of(hd * PEER_TOPK, PEER_TOPK)
        gt_ref[pl.ds(r0, PEER_TOPK), :] = g / jnp.sum(g, axis=0, keepdims=True)
        et_ref[pl.ds(r0, PEER_TOPK), :] = jnp.concatenate(eid, axis=0).astype(F32)
        return 0

    lax.fori_loop(0, PEER_HEADS, head_body, 0)

    e = et_ref[...]
    g = gt_ref[...]
    npair = e.shape[0]
    is0 = e < float(PEER_HALF_EXPERTS)
    r_i = lax.broadcasted_iota(jnp.int32, (npair, npair), 0)
    c_i = lax.broadcasted_iota(jnp.int32, (npair, npair), 1)
    tri = jnp.where(c_i <= r_i, 1.0, 0.0).astype(BF16)
    rank0 = jnp.dot(tri, jnp.where(is0, 1.0, 0.0).astype(BF16), preferred_element_type=F32)
    rowf = lax.broadcasted_iota(jnp.int32, (npair, T), 0).astype(F32)
    n0 = rank0[npair - 1:npair, :]
    place = jnp.where(is0, rank0 - 1.0, n0 + (rowf - rank0))
    off = jnp.where(is0, e, e - float(PEER_HALF_EXPERTS)) * float(SUBLANES)
    for p in range(npair):
        m = place == float(p)
        la_ref[p:p + 1, :] = jnp.sum(jnp.where(m, off, 0.0), axis=0, keepdims=True)
        lg_ref[p:p + 1, :] = jnp.sum(jnp.where(m, g, 0.0), axis=0, keepdims=True)
    idx_ref[...] = la_ref[...].T.astype(jnp.int32)
    gate_ref[...] = lg_ref[...].T
    n0_ref[...] = n0.astype(jnp.int32)


def _post_mix(x, att_p, att_s, ret_p, ret_s, w_out_bf, norm_w, wq_t_bf, sk1_bf, sk2_bf):
    N = x.shape[0]
    nt = N // _TMP
    ntp = att_p.shape[0] // _TMP
    nts = att_s.shape[0] // _TMP
    assert ntp + nts == nt and ntp > 0 and nts > 0
    npair = PEER_HEADS * PEER_TOPK
    row = lambda w: pl.BlockSpec((_TMP, w), lambda i: (i, 0))
    prow = lambda w: pl.BlockSpec((_TMP, w), lambda i: (jnp.minimum(i, ntp - 1), 0))
    srow = lambda w: pl.BlockSpec((_TMP, w), lambda i: (jnp.maximum(i - ntp, 0), 0))
    full = lambda a: pl.BlockSpec(a.shape, lambda i: (0,) * a.ndim)
    nw = norm_w.reshape(1, D_MODEL)
    return pl.pallas_call(
        functools.partial(_post_mix_kernel, n_prompt_tiles=ntp),
        grid=(nt,),
        in_specs=[row(D_MODEL), prow(ATT_WIDTH), srow(ATT_WIDTH), prow(RET_WIDTH), srow(RET_WIDTH),
                  full(w_out_bf), full(nw), full(wq_t_bf), full(sk1_bf), full(sk2_bf)],
        out_specs=[row(D_MODEL), pl.BlockSpec((_TMP * SUBLANES, LANES), lambda i: (i, 0)),
                   row(npair), row(npair), pl.BlockSpec((1, _TMP), lambda i: (0, i))],
        out_shape=[jax.ShapeDtypeStruct((N, D_MODEL), F32), jax.ShapeDtypeStruct((N * SUBLANES, LANES), F32),
                   jax.ShapeDtypeStruct((N, npair), jnp.int32), jax.ShapeDtypeStruct((N, npair), F32),
                   jax.ShapeDtypeStruct((1, N), jnp.int32)],
        scratch_shapes=[pltpu.VMEM((PEER_HEADS * PEER_KEY_DIM, _TMP), F32),
                        pltpu.VMEM((npair, _TMP), F32), pltpu.VMEM((npair, _TMP), F32),
                        pltpu.VMEM((npair, _TMP), F32), pltpu.VMEM((npair, _TMP), F32)],
        name="post_mix",
        compiler_params=_cparams(("arbitrary",)),
    )(x, att_p, att_s, ret_p, ret_s, w_out_bf, nw, wq_t_bf, sk1_bf, sk2_bf)


_TBP = 256
_NPAIR = PEER_HEADS * PEER_TOPK
_ROW_TILE = D_MODEL // LANES
assert _ROW_TILE == SUBLANES
_HIGH_HALF = -65536


def _bf16_bits(x):
    return lax.bitcast_convert_type(x.astype(BF16).astype(F32), jnp.int32)


def _pack_table_kernel(lo_ref, hi_ref, o_ref):
    n = lo_ref.shape[0]
    words = (lax.shift_right_logical(_bf16_bits(lo_ref[...]), 16)
             | (_bf16_bits(hi_ref[...]) & _HIGH_HALF))
    for c in range(_ROW_TILE):
        o_ref[pl.ds(c, n, stride=_ROW_TILE), :] = words[:, c * LANES:(c + 1) * LANES]


def _pack_table(tbl):
    blk = 512
    nb = PEER_HALF_EXPERTS // blk
    return pl.pallas_call(
        _pack_table_kernel,
        grid=(nb,),
        in_specs=[pl.BlockSpec((blk, D_MODEL), lambda i: (i, 0)),
                  pl.BlockSpec((blk, D_MODEL), lambda i: (i + nb, 0))],
        out_specs=pl.BlockSpec((blk * _ROW_TILE, LANES), lambda i: (i, 0)),
        out_shape=jax.ShapeDtypeStruct((PEER_HALF_EXPERTS * _ROW_TILE, LANES), jnp.int32),
        name="pack_table",
        compiler_params=_cparams(("arbitrary",)),
    )(tbl, tbl)


def _unpack(words, half):
    bits = lax.shift_left(words, 16) if half == 0 else words & _HIGH_HALF
    return lax.bitcast_convert_type(bits, F32)


def _load_table(tbl_hbm, tbuf, sem):
    cp = pltpu.make_async_copy(tbl_hbm, tbuf, sem)
    cp.start()
    cp.wait()


def _fold_pair(a, b, k, sub):
    m = (sub & k) == 0
    return jnp.where(m, a, pltpu.roll(b, k, 0)) + jnp.where(m, pltpu.roll(a, SUBLANES - k, 0), b)


def _fold8(p, sub):
    a, b, c, d, e, f, g, h = p[0], p[4], p[2], p[6], p[1], p[5], p[3], p[7]
    t1, t2, t3, t4 = (_fold_pair(a, b, 4, sub), _fold_pair(c, d, 4, sub),
                      _fold_pair(e, f, 4, sub), _fold_pair(g, h, 4, sub))
    u1, u2 = _fold_pair(t1, t2, 2, sub), _fold_pair(t3, t4, 2, sub)
    return _fold_pair(u1, u2, 1, sub)


def _flat_smem_spec(nt, per_token):
    return pl.BlockSpec((1, 1, _TBP * per_token), lambda i: (i, 0, 0), memory_space=pltpu.SMEM)


def _flat_blocks(a, nt):
    return a.reshape(nt, 1, -1)


_N_GROUPS = _NPAIR // _PAIR_GROUP


def _group_start(t, g):
    g = jnp.minimum(g, _N_GROUPS - 1)
    return pl.multiple_of(t * _NPAIR + g * _PAIR_GROUP, _PAIR_GROUP)


def _table_row(tbuf, idx_ref, k):
    off = pl.multiple_of(idx_ref[0, 0, k], _ROW_TILE)
    return tbuf[pl.ds(off, _ROW_TILE), :]


def _token_segments(n0):
    g_mid = lax.shift_right_logical(n0, _PAIR_GROUP.bit_length() - 1)
    rem = n0 & (_PAIR_GROUP - 1)
    g_hi = g_mid + jnp.where(rem != 0, 1, 0)
    return g_mid, rem, g_hi


def _mixed_unpack(words, s, rem):
    return jnp.where(s < rem, _unpack(words, 0), _unpack(words, 1))


def _peer_u_kernel(idx_ref, n0_ref, x_ref, u_hbm, o_ref, tbuf, sem, z_ref):
    @pl.when(pl.program_id(0) == 0)
    def _():
        _load_table(u_hbm, tbuf, sem)

    sub = lax.broadcasted_iota(jnp.int32, (SUBLANES, LANES), 0)

    def tok_body(t, _):
        xt = x_ref[t]
        g_mid, rem, g_hi = _token_segments(n0_ref[0, 0, t])

        def load_words(g):
            kg = _group_start(t, g)
            return tuple(_table_row(tbuf, idx_ref, kg + s) for s in range(_PAIR_GROUP))

        def fold_group(g, vals):
            kg = _group_start(t, g)
            prods = [v * xt for v in vals]
            for q in range(_PAIR_GROUP // SUBLANES):
                z_ref[pl.ds(kg + q * SUBLANES, SUBLANES), :] = _fold8(prods[q * SUBLANES:(q + 1) * SUBLANES], sub)

        def uniform_groups(ga, gb, half):
            def grp_body(g, words):
                nxt = load_words(g + 1)
                fold_group(g, [_unpack(w, half) for w in words])
                return nxt
            lax.fori_loop(ga, gb, grp_body, load_words(ga))

        uniform_groups(0, g_mid, 0)

        @pl.when(rem != 0)
        def _():
            fold_group(g_mid, [_mixed_unpack(w, s, rem) for s, w in enumerate(load_words(g_mid))])

        uniform_groups(g_hi, _N_GROUPS, 1)
        return 0

    lax.fori_loop(0, _TBP, tok_body, 0)

    ones = jnp.ones((SUBLANES, LANES), BF16)

    def chunk_body(c, _):
        r0 = pl.multiple_of(c * SUBLANES * _NPAIR, SUBLANES * _NPAIR)
        z = z_ref[pl.ds(r0, SUBLANES * _NPAIR), :]
        zh = z.astype(BF16)
        zl = (z - zh.astype(F32)).astype(BF16)
        hv = _dot_nt(ones, zh) + _dot_nt(ones, zl)
        rows = [hv[0:1, tt * _NPAIR:(tt + 1) * _NPAIR] for tt in range(SUBLANES)]
        o_ref[pl.ds(pl.multiple_of(c * SUBLANES, SUBLANES), SUBLANES), :] = jnp.concatenate(rows, axis=0)
        return 0

    lax.fori_loop(0, _TBP // SUBLANES, chunk_body, 0)


def _peer_u(idx, n0, xn_tiles, u_packed):
    N = xn_tiles.shape[0] // _ROW_TILE
    nt = N // _TBP
    return pl.pallas_call(
        _peer_u_kernel,
        grid=(nt,),
        in_specs=[_flat_smem_spec(nt, _NPAIR), _flat_smem_spec(nt, 1),
                  pl.BlockSpec((_TBP, _ROW_TILE, LANES), lambda i: (i, 0, 0)),
                  pl.BlockSpec(memory_space=pl.ANY)],
        out_specs=pl.BlockSpec((_TBP, _NPAIR), lambda i: (i, 0)),
        out_shape=jax.ShapeDtypeStruct((N, _NPAIR), F32),
        scratch_shapes=[pltpu.VMEM((PEER_HALF_EXPERTS * _ROW_TILE, LANES), jnp.int32),
                        pltpu.SemaphoreType.DMA(()),
                        pltpu.VMEM((_TBP * _NPAIR, LANES), F32)],
        name="peer_u",
        compiler_params=_cparams(("arbitrary",)),
    )(_flat_blocks(idx, nt), _flat_blocks(n0, nt), xn_tiles.reshape(N, _ROW_TILE, LANES), u_packed)


def _peer_coef_kernel(hv_ref, gate_ref, o_ref):
    hv = hv_ref[...]
    o_ref[...] = gate_ref[...] * (0.5 * hv * (1.0 + lax.erf(hv * (0.5 ** 0.5))))


def _peer_coef(hval, gate):
    N = gate.shape[0]
    tm = 1024 if N % 1024 == 0 else _TBP
    spec = pl.BlockSpec((tm, _NPAIR), lambda i: (i, 0))
    return pl.pallas_call(
        _peer_coef_kernel,
        grid=(N // tm,),
        in_specs=[spec, spec],
        out_specs=spec,
        out_shape=jax.ShapeDtypeStruct((N, _NPAIR), F32),
        name="peer_coef",
        compiler_params=_cparams(("arbitrary",)),
    )(hval, gate)


def _peer_v_kernel(idx_ref, n0_ref, coef_ref, v_hbm, o_ref, tbuf, sem):
    @pl.when(pl.program_id(0) == 0)
    def _():
        _load_table(v_hbm, tbuf, sem)

    n_acc = 4

    def accumulate(accs, prods):
        accs = list(accs)
        for s in range(_PAIR_GROUP):
            accs[s % n_acc] = accs[s % n_acc] + prods[s]
        return tuple(accs)

    def tok_body(t, _):
        g_mid, rem, g_hi = _token_segments(n0_ref[0, 0, t])

        def weighted_group(g, unpack):
            kg = _group_start(t, g)
            return tuple(coef_ref[0, 0, kg + s] * unpack(_table_row(tbuf, idx_ref, kg + s), s)
                         for s in range(_PAIR_GROUP))

        def uniform_groups(ga, gb, half, accs):
            unpack = lambda w, s: _unpack(w, half)

            def grp_body(g, carry):
                prods, accs = carry
                nxt = weighted_group(g + 1, unpack)
                return nxt, accumulate(accs, prods)

            return lax.fori_loop(ga, gb, grp_body, (weighted_group(ga, unpack), accs))[1]

        zero = jnp.zeros((_ROW_TILE, LANES), F32)
        accs = uniform_groups(0, g_mid, 0, (zero,) * n_acc)
        accs = lax.cond(
            rem != 0,
            lambda a: accumulate(a, weighted_group(g_mid, lambda w, s: _mixed_unpack(w, s, rem))),
            lambda a: a, accs)
        accs = uniform_groups(g_hi, _N_GROUPS, 1, accs)
        o_ref[t] = (accs[0] + accs[1]) + (accs[2] + accs[3])
        return 0

    lax.fori_loop(0, _TBP, tok_body, 0)


def _peer_v(idx, n0, coef, v_packed):
    N = coef.shape[0]
    nt = N // _TBP
    out = pl.pallas_call(
        _peer_v_kernel,
        grid=(nt,),
        in_specs=[_flat_smem_spec(nt, _NPAIR), _flat_smem_spec(nt, 1), _flat_smem_spec(nt, _NPAIR),
                  pl.BlockSpec(memory_space=pl.ANY)],
        out_specs=pl.BlockSpec((_TBP, _ROW_TILE, LANES), lambda i: (i, 0, 0)),
        out_shape=jax.ShapeDtypeStruct((N, _ROW_TILE, LANES), F32),
        scratch_shapes=[pltpu.VMEM((PEER_HALF_EXPERTS * _ROW_TILE, LANES), jnp.int32),
                        pltpu.SemaphoreType.DMA(())],
        name="peer_v",
        compiler_params=_cparams(("arbitrary",)),
    )(_flat_blocks(idx, nt), _flat_blocks(n0, nt), _flat_blocks(coef, nt), v_packed)
    return out.reshape(N * _ROW_TILE, LANES)


def _final_kernel(h_ref, y_ref, w_ref, o_ref, *, normalize):
    T = h_ref.shape[0]
    y = jnp.concatenate(
        [y_ref[pl.ds(c, T, stride=SUBLANES), :] for c in range(D_MODEL // LANES)], axis=1)
    h = h_ref[...] + y
    if normalize:
        ms = jnp.mean(h * h, axis=-1, keepdims=True)
        h = h * lax.rsqrt(ms + NORM_EPS) * w_ref[...]
    o_ref[...] = h


def _final(h, ypart, w):
    N = h.shape[0]
    tm = 512 if N % 512 == 0 else _TBP
    normalize = w is not None
    if not normalize:
        w = jnp.ones((D_MODEL,), F32)
    return pl.pallas_call(
        functools.partial(_final_kernel, normalize=normalize),
        grid=(N // tm,),
        in_specs=[pl.BlockSpec((tm, D_MODEL), lambda i: (i, 0)),
                  pl.BlockSpec((tm * SUBLANES, LANES), lambda i: (i, 0)),
                  pl.BlockSpec((1, D_MODEL), lambda i: (0, 0))],
        out_specs=pl.BlockSpec((tm, D_MODEL), lambda i: (i, 0)),
        out_shape=jax.ShapeDtypeStruct((N, D_MODEL), F32),
        name="final_norm",
        compiler_params=_cparams(("arbitrary",)),
    )(h, ypart, w.reshape(1, D_MODEL))


_TM_IN = 256


def _pages_feature_major(cache_l):
    n_phys, page = cache_l.shape[:2]
    return jnp.swapaxes(cache_l.reshape(n_phys, page, -1), 1, 2)


def kernel(x_prompt, x_sample, cache_k, cache_v, cache_idx_k, state_ret, page_table, norm_attn_w, w_in, ret_gn_w,
           w_out, norm_ffn_w, peer_w_q, peer_sub_keys_1, peer_sub_keys_2, peer_u, peer_v, final_norm_w):
    B, S, D = x_prompt.shape
    Bd, T, _ = x_sample.shape
    depth = w_in.shape[0]
    n_pages = page_table.shape[1]
    past_len = n_pages * PAGE_SIZE
    n_phys = cache_k.shape[1]
    Np, Ns = B * S, Bd * SROWS
    assert D == D_MODEL and T <= SROWS and S % _TM_IN == 0 and Ns % _TM_IN == 0 and S % _CK == 0

    hs_pad = jnp.pad(x_sample, ((0, 0), (0, SROWS - T), (0, 0)))
    h_all = jnp.concatenate([x_prompt.reshape(Np, D), hs_pad.reshape(Ns, D)], axis=0)

    pos_s = past_len + jnp.arange(SROWS)
    pos = jnp.concatenate([jnp.arange(S), jnp.tile(pos_s, _TM_IN // SROWS)])
    tab_att, tab_ret = _rope_tables(pos)
    tiles_per_seq = S // _TM_IN
    tab_blocks = [i % tiles_per_seq for i in range(Np // _TM_IN)] + [tiles_per_seq] * (Ns // _TM_IN)
    ret_consts_p = _ret_constants(RET_CHUNK)
    ret_consts_s = _ret_constants(T)

    outs = {n: [] for n in ("kp", "vp", "ikp", "sp", "ks", "vs", "iks", "ss")}
    for l in range(depth):
        m = _in_proj(h_all, norm_attn_w[l], _pad_w_in(w_in[l]), tab_att, tab_ret, tab_blocks, _TM_IN)
        att_p = _prompt_dsa(m["qi"], m["kiwi"], m["qa"], m["ka"], m["va"], B, S)
        att_s = _sample_dsa(page_table, m["qi"], m["kiwi"], m["qa"], m["ka"], m["va"],
                            _pages_feature_major(cache_idx_k[l]), _pages_feature_major(cache_k[l]),
                            _pages_feature_major(cache_v[l]), T, Np)
        ret_p, s_p = _retention(m["qr"], m["kr"], m["vr"], m["gr"], ret_gn_w[l],
                                jnp.zeros((B, RET_HEADS, RET_HEAD_DIM, RET_HEAD_DIM), F32),
                                ret_consts_p, B, S // RET_CHUNK, RET_CHUNK)
        ret_s, s_s = _retention(m["qr"], m["kr"], m["vr"], m["gr"], ret_gn_w[l], state_ret[l].astype(F32),
                                ret_consts_s, Bd, 1, SROWS, row0=Np)
        h_mid, xn, idx, gate, n0 = _post_mix(
            h_all, att_p, att_s, ret_p, ret_s, w_out[l].astype(BF16), norm_ffn_w[l],
            peer_w_q[l].T.astype(BF16), peer_sub_keys_1[l].astype(BF16), peer_sub_keys_2[l].astype(BF16))
        coef = _peer_coef(_peer_u(idx, n0, xn, _pack_table(peer_u[l])), gate)
        ypart = _peer_v(idx, n0, coef, _pack_table(peer_v[l]))
        last = l == depth - 1
        h_all = _final(h_mid, ypart, final_norm_w if last else None)

        sample = lambda a, w: a[Np:].reshape(Bd, SROWS, w)[:, :T]
        outs["kp"].append(m["ka"][:Np].reshape(B, S, KV_HEADS, ATT_HEAD_DIM))
        outs["vp"].append(m["va"][:Np].reshape(B, S, KV_HEADS, ATT_HEAD_DIM))
        outs["ikp"].append(m["kiwi"][:Np, :IDX_DIM].reshape(B, S, IDX_DIM))
        outs["sp"].append(s_p.astype(x_prompt.dtype))
        outs["ks"].append(sample(m["ka"], KV_WIDTH).reshape(Bd, T, KV_HEADS, ATT_HEAD_DIM))
        outs["vs"].append(sample(m["va"], KV_WIDTH).reshape(Bd, T, KV_HEADS, ATT_HEAD_DIM))
        outs["iks"].append(sample(m["kiwi"], _KIWI_WIDTH)[:, :, :IDX_DIM])
        outs["ss"].append(s_s.astype(state_ret.dtype))

    y_prompt = h_all[:Np].reshape(B, S, D)
    y_sample = h_all[Np:].reshape(Bd, SROWS, D)[:, :T]
    st = lambda n: jnp.stack(outs[n])
    return (y_prompt, y_sample, st("kp"), st("vp"), st("ikp"), st("sp"),
            st("ks"), st("vs"), st("iks"), st("ss"))
```
